```python
import functools
import jax
import jax.numpy as jnp
from jax import lax
import numpy as np

D_MODEL = 1024
BATCH = 8
SEQ = 2048
DEPTH = 1
DEC_BATCH = 128
DEC_SEQ = 4
PAST_LEN = 16384
PAGE_SIZE = 128

N_META = 16
MIX_WIDTH = D_MODEL
POOL_WIDTH = MIX_WIDTH // 2
POOL_WINDOWS = (2, 4, 8, 16)
POOL_GROUPS = len(POOL_WINDOWS)
POOL_GC = POOL_WIDTH // POOL_GROUPS
POOL_CTX = max(POOL_WINDOWS) - 1
MLSTM_WIDTH = MIX_WIDTH - POOL_WIDTH
N_HEADS = 4
HEAD_DIM = MLSTM_WIDTH // N_HEADS
CHUNK = 64
D_FF = 4 * D_MODEL
EPS = 1e-6
IN_COLS = POOL_WIDTH + 4 * MLSTM_WIDTH + 2 * N_HEADS

kernel_name = "hymba_pool_mlstm_decoder_step"


def rmsnorm(x, g):
    x32 = x.astype(jnp.float32)
    y = x32 * lax.rsqrt(jnp.mean(x32 * x32, axis=-1, keepdims=True) + EPS) * g.astype(jnp.float32)
    return y.astype(x.dtype)


def mixer_inputs(x, norm_g, w_in, b_gate):
    B, T, _ = x.shape
    p = jnp.einsum('btd,dc->btc', rmsnorm(x, norm_g), w_in)
    splits = [POOL_WIDTH, POOL_WIDTH + MLSTM_WIDTH, POOL_WIDTH + 2 * MLSTM_WIDTH,
              POOL_WIDTH + 3 * MLSTM_WIDTH, POOL_WIDTH + 4 * MLSTM_WIDTH]
    u, q, k, v, o, gates = jnp.split(p, splits, axis=-1)

    def heads(z):
        return z.reshape(B, T, N_HEADS, HEAD_DIM).transpose(0, 2, 1, 3).astype(jnp.float32)

    gates = gates.astype(jnp.float32) + b_gate.astype(jnp.float32)
    ig = gates[..., :N_HEADS].transpose(0, 2, 1)
    lf = jax.nn.log_sigmoid(gates[..., N_HEADS:]).transpose(0, 2, 1)
    return u, heads(q) * (HEAD_DIM ** -0.5), heads(k), heads(v), o, ig, lf


def pool_mix(u_ext, pos, w_pool, pool_scale):
    T = u_ext.shape[1] - POOL_CTX
    u32 = u_ext.astype(jnp.float32)
    cs = jnp.pad(jnp.cumsum(u32, axis=1), ((0, 0), (1, 0), (0, 0)))
    u_tok = u32[:, POOL_CTX:]
    outs = []
    for g, w in enumerate(POOL_WINDOWS):
        sl = slice(g * POOL_GC, (g + 1) * POOL_GC)
        wsum = cs[:, POOL_CTX + 1:, sl] - cs[:, POOL_CTX + 1 - w:POOL_CTX + 1 - w + T, sl]
        cnt = jnp.minimum(pos + 1, w).astype(jnp.float32)[None, :, None]
        outs.append(jnp.einsum('btc,cd->btd', wsum / cnt - u_tok[:, :, sl], w_pool[g].astype(jnp.float32)))
    return jnp.concatenate(outs, axis=-1) * pool_scale.astype(jnp.float32)


def mlstm_chunk(carry, inp):
    C0, n0, m0 = carry
    q, k, v, ig, lf = inp
    T = q.shape[2]
    b = jnp.cumsum(lf, axis=-1)
    g = b + m0[..., None]
    mask = jnp.tril(jnp.ones((T, T), dtype=bool))
    dmat = jnp.where(mask, b[..., :, None] - b[..., None, :] + ig[..., None, :], -jnp.inf)
    m = jnp.maximum(g, jnp.max(dmat, axis=-1))
    inter = jnp.exp(g - m)
    qk = jnp.einsum('bhtk,bhsk->bhts', q, k) * jnp.exp(dmat - m[..., None])
    num = inter[..., None] * jnp.einsum('bhtk,bhkv->bhtv', q, C0) + jnp.einsum('bhts,bhsv->bhtv', qk, v)
    den = inter * jnp.einsum('bhtk,bhk->bht', q, n0) + jnp.sum(qk, axis=-1)
    h = num / jnp.maximum(jnp.abs(den), jnp.exp(-m))[..., None]
    m_last = m[..., -1]
    decay0 = jnp.exp(b[..., -1] + m0 - m_last)
    ws = jnp.exp(b[..., -1:] - b + ig - m_last[..., None])
    C1 = decay0[..., None, None] * C0 + jnp.einsum('bhs,bhsk,bhsv->bhkv', ws, k, v)
    n1 = decay0[..., None] * n0 + jnp.einsum('bhs,bhsk->bhk', ws, k)
    return (C1, n1, m_last), h


def mlstm_prompt(q, k, v, ig, lf):
    B, H, L, K = q.shape
    carry = (jnp.zeros((B, H, K, K), jnp.float32), jnp.zeros((B, H, K), jnp.float32),
             jnp.zeros((B, H), jnp.float32))
    carry, h_meta = mlstm_chunk(carry, (q[:, :, :N_META], k[:, :, :N_META], v[:, :, :N_META],
                                        ig[:, :, :N_META], lf[:, :, :N_META]))
    nc = (L - N_META) // CHUNK

    def to_chunks(z):
        z = z[:, :, N_META:]
        return jnp.moveaxis(z.reshape((B, H, nc, CHUNK) + z.shape[3:]), 2, 0)

    carry, h_rest = lax.scan(mlstm_chunk, carry, (to_chunks(q), to_chunks(k), to_chunks(v),
                                                  to_chunks(ig), to_chunks(lf)))
    h_rest = jnp.moveaxis(h_rest, 0, 2).reshape(B, H, nc * CHUNK, K)
    return jnp.concatenate([h_meta, h_rest], axis=2), carry


def mlstm_sample(q, k, v, ig, lf, c0, n0, m0):
    carry = (c0.astype(jnp.float32), n0.astype(jnp.float32), m0.astype(jnp.float32))
    carry, h = mlstm_chunk(carry, (q, k, v, ig, lf))
    return h, carry


def mixer_output(pool_out, h, o, head_gain, w_out, dtype):
    B, H, T, V = h.shape
    h = h * lax.rsqrt(jnp.mean(h * h, axis=-1, keepdims=True) + EPS) * head_gain.astype(jnp.float32)[None, :, None, :]
    h = h.transpose(0, 2, 1, 3).reshape(B, T, MLSTM_WIDTH) * jax.nn.sigmoid(o.astype(jnp.float32))
    mix = jnp.concatenate([pool_out, h], axis=-1).astype(dtype)
    return jnp.einsum('btc,cd->btd', mix, w_out)


def channel_mixer(x, g, w_up, w_down):
    a = jnp.square(jax.nn.relu(jnp.einsum('btd,df->btf', rmsnorm(x, g), w_up)))
    return jnp.einsum('btf,fd->btd', a, w_down)


def block(x, pool_ctx, pos, run_mlstm, norm1, w_in, b_gate, w_pool, pool_scale, head_gain, w_out,
          norm2, w_up, w_down):
    u, q, k, v, o, ig, lf = mixer_inputs(x, norm1, w_in, b_gate)
    u_ext = jnp.concatenate([pool_ctx.astype(u.dtype), u], axis=1)
    pool_out = pool_mix(u_ext, pos, w_pool, pool_scale)
    h, mstate = run_mlstm(q, k, v, ig, lf)
    x = x + mixer_output(pool_out, h, o, head_gain, w_out, x.dtype)
    x = x + channel_mixer(x, norm2, w_up, w_down)
    return x, u_ext[:, -POOL_CTX:], mstate


def setup_inputs(seed: int = 0) -> dict:
    key = jax.random.key(seed)
    ks = jax.random.split(key, 20)
    f32 = jnp.float32
    nrm = lambda k, s, sc: jax.random.normal(k, s, f32) * sc
    b_forget = jnp.linspace(3.0, 6.0, N_HEADS)[None, :] + nrm(ks[9], (DEPTH, N_HEADS), 0.1)
    b_input = nrm(ks[10], (DEPTH, N_HEADS), 0.1)
    return {
        'x_prompt': nrm(ks[0], (BATCH, SEQ, D_MODEL), 1.0),
        'x_sample': nrm(ks[1], (DEC_BATCH, DEC_SEQ, D_MODEL), 1.0),
        'state_pool': nrm(ks[2], (DEPTH, DEC_BATCH, POOL_CTX, POOL_WIDTH), 1.0),
        'state_C': nrm(ks[3], (DEPTH, DEC_BATCH, N_HEADS, HEAD_DIM, HEAD_DIM), 0.3),
        'state_n': nrm(ks[4], (DEPTH, DEC_BATCH, N_HEADS, HEAD_DIM), 0.3),
        'state_m': nrm(ks[5], (DEPTH, DEC_BATCH, N_HEADS), 0.5),
        'meta_tokens': nrm(ks[6], (N_META, D_MODEL), 1.0),
        'norm1': 1.0 + nrm(ks[7], (DEPTH, D_MODEL), 0.05),
        'w_in': nrm(ks[8], (DEPTH, D_MODEL, IN_COLS), D_MODEL ** -0.5),
        'b_gate': jnp.concatenate([b_input, b_forget], axis=-1),
        'w_pool': nrm(ks[11], (DEPTH, POOL_GROUPS, POOL_GC, POOL_GC), POOL_GC ** -0.5),
        'pool_scale': 1.0 + nrm(ks[12], (DEPTH, POOL_WIDTH), 0.1),
        'head_gain': 1.0 + nrm(ks[13], (DEPTH, N_HEADS, HEAD_DIM), 0.05),
        'w_out': nrm(ks[14], (DEPTH, MIX_WIDTH, D_MODEL), MIX_WIDTH ** -0.5),
        'norm2': 1.0 + nrm(ks[15], (DEPTH, D_MODEL), 0.05),
        'w_up': nrm(ks[16], (DEPTH, D_MODEL, D_FF), D_MODEL ** -0.5),
        'w_down': nrm(ks[17], (DEPTH, D_FF, D_MODEL), D_FF ** -0.5),
        'norm_f': 1.0 + nrm(ks[18], (D_MODEL,), 0.05),
    }


def reference(x_prompt, x_sample, state_pool, state_C, state_n, state_m, meta_tokens, norm1, w_in,
              b_gate, w_pool, pool_scale, head_gain, w_out, norm2, w_up, w_down, norm_f):
    dtype = x_prompt.dtype
    B = x_prompt.shape[0]
    xp = jnp.concatenate([jnp.broadcast_to(meta_tokens.astype(dtype)[None], (B, N_META, D_MODEL)),
                          x_prompt], axis=1)
    xs = x_sample
    pos_p = jnp.arange(xp.shape[1], dtype=jnp.int32)
    pos_s = PAST_LEN + jnp.arange(xs.shape[1], dtype=jnp.int32)
    zero_ctx = jnp.zeros((B, POOL_CTX, POOL_WIDTH), dtype)
    pool_p, c_p, n_p, m_p, pool_s, c_s, n_s, m_s = [], [], [], [], [], [], [], []
    for l in range(DEPTH):
        w = (norm1[l], w_in[l], b_gate[l], w_pool[l], pool_scale[l], head_gain[l], w_out[l],
             norm2[l], w_up[l], w_down[l])
        xp, pp, (cp, np_, mp) = block(xp, zero_ctx, pos_p, mlstm_prompt, *w)
        run_s = functools.partial(mlstm_sample, c0=state_C[l], n0=state_n[l], m0=state_m[l])
        xs, ps, (cs, ns, ms) = block(xs, state_pool[l], pos_s, run_s, *w)
        pool_p.append(pp); c_p.append(cp); n_p.append(np_); m_p.append(mp)
        pool_s.append(ps); c_s.append(cs); n_s.append(ns); m_s.append(ms)
    y_prompt = rmsnorm(xp[:, N_META:], norm_f)
    y_sample = rmsnorm(xs, norm_f)
    return (y_prompt, y_sample, jnp.stack(pool_p), jnp.stack(c_p), jnp.stack(n_p), jnp.stack(m_p),
            jnp.stack(pool_s), jnp.stack(c_s), jnp.stack(n_s), jnp.stack(m_s))
```

```python
import functools

import jax
import jax.numpy as jnp
from jax import lax
from jax.experimental import pallas as pl
from jax.experimental.pallas import tpu as pltpu

D_MODEL = 1024
N_META = 16
POOL_WIDTH = 512
POOL_WINDOWS = (2, 4, 8, 16)
POOL_GC = 128
POOL_CTX = 15
MLSTM_WIDTH = 512
N_HEADS = 4
HEAD_DIM = 128
D_FF = 4096
EPS = 1e-6
DEC_SEQ = 4

LANES = 128
SUBLANES = 8
COL_Q = POOL_WIDTH
COL_K = COL_Q + MLSTM_WIDTH
COL_V = COL_K + MLSTM_WIDTH
COL_O = COL_V + MLSTM_WIDTH
COL_G = COL_O + MLSTM_WIDTH
IN_PAD = COL_G + LANES
AUG = 2 * HEAD_DIM
CTX_PAD = 16
PROMPT_TILE = 256
FFN_TILE = 512
FFN_CHUNK = 1024
SAMPLE_BLOCK = 8
NEG_BIG = -1e30
VMEM_LIMIT = 56 * 1024 * 1024

F32 = jnp.float32
BF16 = jnp.bfloat16


def _rmsnorm(x, g):
    return x * lax.rsqrt(jnp.mean(x * x, axis=-1, keepdims=True) + EPS) * g


def _log_sigmoid(x):
    return jnp.minimum(x, 0.0) - jnp.log1p(jnp.exp(-jnp.abs(x)))


def _dot(a, b):
    return jnp.dot(a, b, preferred_element_type=F32)


def _dot_nt(a, b):
    return lax.dot_general(a, b, (((1,), (1,)), ((), ())), preferred_element_type=F32)


def _lane_scan(x, op):
    n = x.shape[1]
    lane = lax.broadcasted_iota(jnp.int32, x.shape, 1)
    sh = 1
    while sh < n:
        x = jnp.where(lane >= sh, op(x, pltpu.roll(x, sh, axis=1)), x)
        sh *= 2
    return x


def _gate_rows(gcol, bg, m0, n_valid):
    R = gcol.shape[0]
    g = (gcol + bg).T[0:SUBLANES, :]
    row = lax.broadcasted_iota(jnp.int32, (SUBLANES, R), 0)
    lane = lax.broadcasted_iota(jnp.int32, (SUBLANES, R), 1)
    is_head = row < N_HEADS
    ig = jnp.where(is_head, g, 0.0)
    fg = jnp.where(is_head, pltpu.roll(g, N_HEADS, axis=0), 0.0)
    b = _lane_scan(_log_sigmoid(fg), jnp.add)
    r = ig - b
    m0b = jnp.broadcast_to(m0[:, 0:1], (SUBLANES, R))
    big_m = jnp.maximum(m0b, _lane_scan(r, jnp.maximum))
    m = b + big_m
    last = n_valid - 1
    m_top = big_m[:, last:last + 1]
    ws = jnp.exp(r - m_top)
    if n_valid < R:
        ws = jnp.where(lane < n_valid, ws, 0.0)
    return dict(r=r, big_m=big_m, inter=jnp.exp(m0b - big_m), eneg=jnp.exp(-m), ws=ws,
                decay=jnp.exp(m0[:, 0:1] - m_top), m_last=m[:, last:last + 1])


def _col_stack(rows):
    R = rows[0].shape[1]
    pad = jnp.zeros((LANES - SUBLANES * len(rows), R), F32)
    return jnp.concatenate(list(rows) + [pad], axis=0).T


def _value_aug(v):
    lane = lax.broadcasted_iota(jnp.int32, (v.shape[0], HEAD_DIM), 1)
    return jnp.concatenate([v, jnp.where(lane == 0, 1.0, 0.0)], axis=1).astype(BF16)


def _meta_kernel(x_ref, n1_ref, win_ref, bg_ref, u_ref, c_ref, m_ref):
    xn = _rmsnorm(x_ref[...], n1_ref[...]).astype(BF16)
    p = _dot(xn, win_ref[...])
    u_ref[...] = p[0:N_META, 0:POOL_WIDTH]
    gr = _gate_rows(p[:, COL_G:IN_PAD], bg_ref[...], jnp.zeros((SUBLANES, LANES), F32), N_META)
    cols = _col_stack([gr["ws"]])
    for h in range(N_HEADS):
        k = p[:, COL_K + h * HEAD_DIM:COL_K + (h + 1) * HEAD_DIM]
        v = p[:, COL_V + h * HEAD_DIM:COL_V + (h + 1) * HEAD_DIM]
        kw_t = (k * cols[:, h:h + 1]).T.astype(BF16)
        c_ref[h] = _dot(kw_t, _value_aug(v))
    m_ref[...] = jnp.broadcast_to(gr["m_last"], (SUBLANES, LANES))


def _prompt_kernel(x_ref, umeta_ref, cmeta_ref, mmeta_ref, n1_ref, win_ref, bg_ref, wpool_ref,
                   pscale_ref, gain_ref, wout_ref,
                   x1_ref, pool_ref, c_ref, m_ref,
                   uext_ref, p_ref, mix_ref):
    R = PROMPT_TILE
    t = pl.program_id(1)

    @pl.when(t == 0)
    def _():
        c_ref[0] = cmeta_ref[...]
        m_ref[0] = mmeta_ref[...]
        uext_ref[0:CTX_PAD, :] = umeta_ref[...]

    x = x_ref[0]
    p_ref[...] = _dot(_rmsnorm(x, n1_ref[...]).astype(BF16), win_ref[...])

    uext_ref[CTX_PAD:CTX_PAD + R, :] = p_ref[:, 0:POOL_WIDTH]
    for g, w in enumerate(POOL_WINDOWS):
        cs = slice(g * POOL_GC, (g + 1) * POOL_GC)
        wsum = uext_ref[CTX_PAD:CTX_PAD + R, cs]
        for j in range(1, w):
            wsum = wsum + uext_ref[CTX_PAD - j:CTX_PAD - j + R, cs]
        z = wsum * (1.0 / w) - uext_ref[CTX_PAD:CTX_PAD + R, cs]
        mix_ref[:, cs] = (_dot(z.astype(BF16), wpool_ref[g]) * pscale_ref[:, cs]).astype(BF16)
    tail = uext_ref[R:R + CTX_PAD, :]
    uext_ref[0:CTX_PAD, :] = tail
    pool_ref[0] = tail

    gr = _gate_rows(p_ref[:, COL_G:IN_PAD], bg_ref[...], m_ref[0], R)
    cols = _col_stack([gr["big_m"], gr["inter"], gr["eneg"], gr["ws"]])
    causal = (lax.broadcasted_iota(jnp.int32, (R, R), 0) >= lax.broadcasted_iota(jnp.int32, (R, R), 1))
    for h in range(N_HEADS):
        hs = slice(h * HEAD_DIM, (h + 1) * HEAD_DIM)
        q = (p_ref[:, COL_Q + h * HEAD_DIM:COL_Q + (h + 1) * HEAD_DIM] * (HEAD_DIM ** -0.5)).astype(BF16)
        k = p_ref[:, COL_K + h * HEAD_DIM:COL_K + (h + 1) * HEAD_DIM]
        vaug = _value_aug(p_ref[:, COL_V + h * HEAD_DIM:COL_V + (h + 1) * HEAD_DIM])
        big_m_c = cols[:, h:h + 1]
        inter_c = cols[:, SUBLANES + h:SUBLANES + h + 1]
        eneg_c = cols[:, 2 * SUBLANES + h:2 * SUBLANES + h + 1]
        ws_c = cols[:, 3 * SUBLANES + h:3 * SUBLANES + h + 1]
        s = _dot_nt(q, k.astype(BF16))
        e = jnp.exp(jnp.where(causal, gr["r"][h:h + 1, :] - big_m_c, NEG_BIG))
        c0 = c_ref[0, h]
        num = inter_c * _dot(q, c0.astype(BF16)) + _dot((s * e).astype(BF16), vaug)
        den = jnp.maximum(jnp.abs(num[:, HEAD_DIM:HEAD_DIM + 1]), eneg_c)
        hh = num[:, 0:HEAD_DIM] / den
        hh = hh * lax.rsqrt(jnp.mean(hh * hh, axis=-1, keepdims=True) + EPS) * gain_ref[:, hs]
        o = p_ref[:, COL_O + h * HEAD_DIM:COL_O + (h + 1) * HEAD_DIM]
        mix_ref[:, POOL_WIDTH + h * HEAD_DIM:POOL_WIDTH + (h + 1) * HEAD_DIM] = (
            hh * jax.nn.sigmoid(o)).astype(BF16)
        kw_t = (k * ws_c).T.astype(BF16)
        c_ref[0, h] = gr["decay"][h:h + 1, :] * c0 + _dot(kw_t, vaug)
    m_ref[0] = jnp.broadcast_to(gr["m_last"], (SUBLANES, LANES))

    x1_ref[0] = x + _dot(mix_ref[...], wout_ref[...])


def _sample_gates(p_ref, bg, m0):
    n = m0.shape[0]
    lane = lax.broadcasted_iota(jnp.int32, (n, LANES), 1)
    is_head = lane < N_HEADS
    b = jnp.zeros((n, LANES), F32)
    big_m = m0
    r, big_ms, ms = [], [], []
    for t in range(DEC_SEQ):
        g = p_ref[t * n:(t + 1) * n, COL_G:IN_PAD] + bg
        ig = jnp.where(is_head, g, 0.0)
        fg = jnp.where(is_head, pltpu.roll(g, LANES - N_HEADS, axis=1), 0.0)
        b = b + _log_sigmoid(fg)
        r.append(ig - b)
        big_m = jnp.maximum(big_m, r[-1])
        big_ms.append(big_m)
        ms.append(b + big_m)
    return dict(r=r, big_m=big_ms, m=ms,
                ws=[jnp.exp(r[t] - big_m) for t in range(DEC_SEQ)],
                decay=jnp.exp(m0 - big_m))


def _sample_kernel(xs_ref, ctx_ref, cin_ref, n0_ref, m0_ref, n1_ref, win_ref, bg_ref, wpool_ref,
                   pscale_ref, gain_ref, wout_ref,
                   x1_ref, pool_ref, cout_ref, nout_ref, mout_ref,
                   p_ref, qc_ref, kwt_ref, vcat_ref, dec_ref, mix_ref):
    i = pl.program_id(0)
    n_seq = m0_ref.shape[0]
    scale = HEAD_DIM ** -0.5

    @pl.when(i == 0)
    def _():
        p_ref[...] = _dot(_rmsnorm(xs_ref[...], n1_ref[...]).astype(BF16), win_ref[...])
        qc_ref[...] = jnp.zeros_like(qc_ref)

        def urow(e, cs):
            if e < POOL_CTX:
                return ctx_ref[e, :, cs]
            return p_ref[(e - POOL_CTX) * n_seq:(e - POOL_CTX + 1) * n_seq, cs]

        for g, w in enumerate(POOL_WINDOWS):
            cs = slice(g * POOL_GC, (g + 1) * POOL_GC)
            for t in range(DEC_SEQ):
                e = POOL_CTX + t
                wsum = urow(e, cs)
                for j in range(1, w):
                    wsum = wsum + urow(e - j, cs)
                z = wsum * (1.0 / w) - urow(e, cs)
                mix_ref[t * n_seq:(t + 1) * n_seq, cs] = (
                    _dot(z.astype(BF16), wpool_ref[g]) * pscale_ref[:, cs]).astype(BF16)
        for e in range(DEC_SEQ, POOL_CTX + DEC_SEQ):
            pool_ref[e - DEC_SEQ] = urow(e, slice(0, POOL_WIDTH))

        gs = _sample_gates(p_ref, bg_ref[...], m0_ref[...])
        dec_ref[...] = gs["decay"]
        mout_ref[...] = gs["m"][DEC_SEQ - 1]
        for h in range(N_HEADS):
            hs = slice(h * HEAD_DIM, (h + 1) * HEAD_DIM)
            n1 = gs["decay"][:, h:h + 1] * n0_ref[:, hs]
            for t in range(DEC_SEQ):
                rows = slice(t * n_seq, (t + 1) * n_seq)
                k = p_ref[rows, COL_K + h * HEAD_DIM:COL_K + (h + 1) * HEAD_DIM]
                kw = k * gs["ws"][t][:, h:h + 1]
                n1 = n1 + kw
                kwt_ref[h, t] = kw.T.astype(BF16)
                vcat_ref[h, rows, :] = p_ref[rows, COL_V + h * HEAD_DIM:COL_V + (h + 1) * HEAD_DIM].astype(BF16)
            nout_ref[:, hs] = n1

    lane_seq = lax.broadcasted_iota(jnp.int32, (HEAD_DIM, n_seq), 1)
    row8 = lax.broadcasted_iota(jnp.int32, (SUBLANES, HEAD_DIM), 0)

    def per_seq(j, carry):
        seq = i * SAMPLE_BLOCK + j
        base = pl.multiple_of((seq // SUBLANES) * SUBLANES, SUBLANES)
        sub = seq % SUBLANES
        dec_row = dec_ref[pl.ds(seq, 1), :]
        for h in range(N_HEADS):
            c0 = cin_ref[j, h]
            c0b = c0.astype(BF16)
            for t in range(DEC_SEQ):
                rows = pl.ds(pl.multiple_of(t * n_seq + base, SUBLANES), SUBLANES)
                qcols = slice(COL_Q + h * HEAD_DIM, COL_Q + (h + 1) * HEAD_DIM)
                q8 = (p_ref[rows, qcols] * scale).astype(BF16)
                hs = slice(h * HEAD_DIM, (h + 1) * HEAD_DIM)
                qc_ref[rows, hs] = qc_ref[rows, hs] + jnp.where(row8 == sub, _dot(q8, c0b), 0.0)
            lhs = jnp.concatenate(
                [jnp.where(lane_seq == seq, kwt_ref[h, t], jnp.zeros((), BF16)) for t in range(DEC_SEQ)], axis=1)
            cout_ref[j, h] = dec_row[:, h:h + 1] * c0 + _dot(lhs, vcat_ref[h])
        return carry

    lax.fori_loop(0, SAMPLE_BLOCK, per_seq, 0)

    @pl.when(i == pl.num_programs(0) - 1)
    def _():
        gs = _sample_gates(p_ref, bg_ref[...], m0_ref[...])
        for h in range(N_HEADS):
            hs = slice(h * HEAD_DIM, (h + 1) * HEAD_DIM)
            n0 = n0_ref[:, hs]
            qs = [p_ref[t * n_seq:(t + 1) * n_seq, COL_Q + h * HEAD_DIM:COL_Q + (h + 1) * HEAD_DIM] * scale
                  for t in range(DEC_SEQ)]
            ks = [p_ref[t * n_seq:(t + 1) * n_seq, COL_K + h * HEAD_DIM:COL_K + (h + 1) * HEAD_DIM]
                  for t in range(DEC_SEQ)]
            for t in range(DEC_SEQ):
                rows = slice(t * n_seq, (t + 1) * n_seq)
                inter = jnp.exp(m0_ref[...] - gs["big_m"][t])[:, h:h + 1]
                num = inter * qc_ref[rows, hs]
                den = inter * jnp.sum(qs[t] * n0, axis=-1, keepdims=True)
                for s in range(t + 1):
                    pts = (jnp.sum(qs[t] * ks[s], axis=-1, keepdims=True)
                           * jnp.exp(gs["r"][s] - gs["big_m"][t])[:, h:h + 1])
                    v = p_ref[s * n_seq:(s + 1) * n_seq, COL_V + h * HEAD_DIM:COL_V + (h + 1) * HEAD_DIM]
                    num = num + pts * v
                    den = den + pts
                hh = num / jnp.maximum(jnp.abs(den), jnp.exp(-gs["m"][t])[:, h:h + 1])
                hh = hh * lax.rsqrt(jnp.mean(hh * hh, axis=-1, keepdims=True) + EPS) * gain_ref[:, hs]
                o = p_ref[rows, COL_O + h * HEAD_DIM:COL_O + (h + 1) * HEAD_DIM]
                mix_ref[rows, POOL_WIDTH + h * HEAD_DIM:POOL_WIDTH + (h + 1) * HEAD_DIM] = (
                    hh * jax.nn.sigmoid(o)).astype(BF16)
        x1_ref[...] = xs_ref[...] + _dot(mix_ref[...], wout_ref[...])


def _ffn_kernel(x_ref, n2_ref, wup_ref, wdown_ref, nf_ref, y_ref):
    x = x_ref[...]
    xn = _rmsnorm(x, n2_ref[...]).astype(BF16)
    acc = x
    for c in range(D_FF // FFN_CHUNK):
        cs = slice(c * FFN_CHUNK, (c + 1) * FFN_CHUNK)
        a = jnp.square(jnp.maximum(_dot(xn, wup_ref[:, cs]), 0.0)).astype(BF16)
        acc = acc + _dot(a, wdown_ref[cs, :])
    y_ref[...] = _rmsnorm(acc, nf_ref[...])


def _full(shape):
    return pl.BlockSpec(shape, lambda *_: (0,) * len(shape))


def _ffn(x, n2, wup, wdown, nf):
    rows = x.shape[0]
    return pl.pallas_call(
        _ffn_kernel,
        grid=(rows // FFN_TILE,),
        in_specs=[pl.BlockSpec((FFN_TILE, D_MODEL), lambda i: (i, 0)),
                  _full((1, D_MODEL)), _full((D_MODEL, D_FF)), _full((D_FF, D_MODEL)), _full((1, D_MODEL))],
        out_specs=pl.BlockSpec((FFN_TILE, D_MODEL), lambda i: (i, 0)),
        out_shape=jax.ShapeDtypeStruct((rows, D_MODEL), F32),
        compiler_params=pltpu.CompilerParams(dimension_semantics=("arbitrary",), vmem_limit_bytes=VMEM_LIMIT),
        name="ffn",
    )(x, n2, wup, wdown, nf)


def kernel(x_prompt, x_sample, state_pool, state_C, state_n, state_m, meta_tokens, norm1, w_in, b_gate,
           w_pool, pool_scale, head_gain, w_out, norm2, w_up, w_down, norm_f):
    B, S, _ = x_prompt.shape
    n_seq = x_sample.shape[0]
    assert w_in.shape[0] == 1 and S % PROMPT_TILE == 0 and n_seq % SAMPLE_BLOCK == 0

    win = jnp.pad(w_in[0], ((0, 0), (0, IN_PAD - w_in.shape[2]))).astype(BF16)
    bg = jnp.pad(b_gate[0], (0, LANES - 2 * N_HEADS)).reshape(1, LANES)
    wpool = w_pool[0].astype(BF16)
    wout = w_out[0].astype(BF16)
    wup = w_up[0].astype(BF16)
    wdown = w_down[0].astype(BF16)
    n1 = norm1[0].reshape(1, D_MODEL)
    n2 = norm2[0].reshape(1, D_MODEL)
    nf = norm_f.reshape(1, D_MODEL)
    pscale = pool_scale[0].reshape(1, POOL_WIDTH)
    gain = head_gain[0].reshape(1, MLSTM_WIDTH)

    meta_x = jnp.pad(meta_tokens, ((0, LANES - N_META), (0, 0)))
    u_meta, c_meta, m_meta = pl.pallas_call(
        _meta_kernel,
        out_shape=(jax.ShapeDtypeStruct((N_META, POOL_WIDTH), F32),
                   jax.ShapeDtypeStruct((N_HEADS, HEAD_DIM, AUG), F32),
                   jax.ShapeDtypeStruct((SUBLANES, LANES), F32)),
        compiler_params=pltpu.CompilerParams(vmem_limit_bytes=VMEM_LIMIT),
        name="meta_prefix",
    )(meta_x, n1, win, bg)

    nt = S // PROMPT_TILE
    x1_p, pool_p, c_p, m_p = pl.pallas_call(
        _prompt_kernel,
        grid=(B, nt),
        in_specs=[pl.BlockSpec((1, PROMPT_TILE, D_MODEL), lambda b, t: (b, t, 0)),
                  _full((N_META, POOL_WIDTH)), _full((N_HEADS, HEAD_DIM, AUG)), _full((SUBLANES, LANES)),
                  _full((1, D_MODEL)), _full((D_MODEL, IN_PAD)), _full((1, LANES)),
                  _full((len(POOL_WINDOWS), POOL_GC, POOL_GC)), _full((1, POOL_WIDTH)),
                  _full((1, MLSTM_WIDTH)), _full((D_MODEL, D_MODEL))],
        out_specs=[pl.BlockSpec((1, PROMPT_TILE, D_MODEL), lambda b, t: (b, t, 0)),
                   pl.BlockSpec((1, CTX_PAD, POOL_WIDTH), lambda b, t: (b, 0, 0)),
                   pl.BlockSpec((1, N_HEADS, HEAD_DIM, AUG), lambda b, t: (b, 0, 0, 0)),
                   pl.BlockSpec((1, SUBLANES, LANES), lambda b, t: (b, 0, 0))],
        out_shape=(jax.ShapeDtypeStruct((B, S, D_MODEL), F32),
                   jax.ShapeDtypeStruct((B, CTX_PAD, POOL_WIDTH), F32),
                   jax.ShapeDtypeStruct((B, N_HEADS, HEAD_DIM, AUG), F32),
                   jax.ShapeDtypeStruct((B, SUBLANES, LANES), F32)),
        scratch_shapes=[pltpu.VMEM((CTX_PAD + PROMPT_TILE, POOL_WIDTH), F32),
                        pltpu.VMEM((PROMPT_TILE, IN_PAD), F32),
                        pltpu.VMEM((PROMPT_TILE, D_MODEL), BF16)],
        compiler_params=pltpu.CompilerParams(dimension_semantics=("arbitrary", "arbitrary"),
                                             vmem_limit_bytes=VMEM_LIMIT),
        name="prompt_mixer",
    )(x_prompt, u_meta, c_meta, m_meta, n1, win, bg, wpool, pscale, gain, wout)

    rows_s = DEC_SEQ * n_seq
    xs_t = x_sample.transpose(1, 0, 2).reshape(rows_s, D_MODEL)
    ctx_t = state_pool[0].transpose(1, 0, 2)
    n0 = state_n[0].reshape(n_seq, MLSTM_WIDTH)
    m0 = jnp.pad(state_m[0], ((0, 0), (0, LANES - N_HEADS)))
    state_block = pl.BlockSpec((SAMPLE_BLOCK, N_HEADS, HEAD_DIM, HEAD_DIM), lambda i: (i, 0, 0, 0))
    x1_s, pool_s, c_s, n_s, m_s = pl.pallas_call(
        _sample_kernel,
        grid=(n_seq // SAMPLE_BLOCK,),
        in_specs=[_full((rows_s, D_MODEL)), _full((POOL_CTX, n_seq, POOL_WIDTH)), state_block,
                  _full((n_seq, MLSTM_WIDTH)), _full((n_seq, LANES)),
                  _full((1, D_MODEL)), _full((D_MODEL, IN_PAD)), _full((1, LANES)),
                  _full((len(POOL_WINDOWS), POOL_GC, POOL_GC)), _full((1, POOL_WIDTH)),
                  _full((1, MLSTM_WIDTH)), _full((D_MODEL, D_MODEL))],
        out_specs=[_full((rows_s, D_MODEL)), _full((POOL_CTX, n_seq, POOL_WIDTH)), state_block,
                   _full((n_seq, MLSTM_WIDTH)), _full((n_seq, LANES))],
        out_shape=(jax.ShapeDtypeStruct((rows_s, D_MODEL), F32),
                   jax.ShapeDtypeStruct((POOL_CTX, n_seq, POOL_WIDTH), F32),
                   jax.ShapeDtypeStruct((n_seq, N_HEADS, HEAD_DIM, HEAD_DIM), F32),
                   jax.ShapeDtypeStruct((n_seq, MLSTM_WIDTH), F32),
                   jax.ShapeDtypeStruct((n_seq, LANES), F32)),
        scratch_shapes=[pltpu.VMEM((rows_s, IN_PAD), F32),
                        pltpu.VMEM((rows_s, MLSTM_WIDTH), F32),
                        pltpu.VMEM((N_HEADS, DEC_SEQ, HEAD_DIM, n_seq), BF16),
                        pltpu.VMEM((N_HEADS, rows_s, HEAD_DIM), BF16),
                        pltpu.VMEM((n_seq, LANES), F32),
                        pltpu.VMEM((rows_s, D_MODEL), BF16)],
        compiler_params=pltpu.CompilerParams(dimension_semantics=("arbitrary",), vmem_limit_bytes=VMEM_LIMIT),
        name="sample_mixer",
    )(xs_t, ctx_t, state_C[0], n0, m0, n1, win, bg, wpool, pscale, gain, wout)

    y_p = _ffn(x1_p.reshape(B * S, D_MODEL), n2, wup, wdown, nf).reshape(B, S, D_MODEL)
    y_s = _ffn(x1_s, n2, wup, wdown, nf).reshape(DEC_SEQ, n_seq, D_MODEL).transpose(1, 0, 2)

    return (y_p, y_s,
            pool_p[:, CTX_PAD - POOL_CTX:, :][None],
            c_p[..., :HEAD_DIM][None],
            c_p[..., HEAD_DIM][None],
            m_p[:, :N_HEADS, 0][None],
            pool_s.transpose(1, 0, 2)[None],
            c_s[None],
            n_s.reshape(n_seq, N_HEADS, HEAD_DIM)[None],
            m_s[:, :N_HEADS][None])
```

```python
import jax
import jax.numpy as jnp
from jax import lax
from jax.experimental import pallas as pl
from jax.experimental.pallas import tpu as pltpu

D_MODEL = 1024
N_META = 16
POOL_WIDTH = 512
POOL_WINDOWS = (2, 4, 8, 16)
POOL_GC = 128
POOL_CTX = 15
MLSTM_WIDTH = 512
N_HEADS = 4
HEAD_DIM = 128
D_FF = 4096
EPS = 1e-6
DEC_SEQ = 4

LANES = 128
SUBLANES = 8
COL_Q = POOL_WIDTH
COL_K = COL_Q + MLSTM_WIDTH
COL_V = COL_K + MLSTM_WIDTH
COL_O = COL_V + MLSTM_WIDTH
IN_MAIN = COL_O + MLSTM_WIDTH
GATE_W = 2 * LANES
AUG = 2 * HEAD_DIM
CTX_PAD = 16
PROMPT_TILE = 256
FFN_TILE = 512
FFN_CHUNK = 1024
SAMPLE_BLOCK = 8
NEG_BIG = -1e30
VMEM_LIMIT = 56 * 1024 * 1024

F32 = jnp.float32
BF16 = jnp.bfloat16


def _rmsnorm(x, g):
    return x * lax.rsqrt(jnp.mean(x * x, axis=-1, keepdims=True) + EPS) * g


def _log_sigmoid(x):
    return jnp.minimum(x, 0.0) - jnp.log1p(jnp.exp(-jnp.abs(x)))


def _dot(a, b):
    return jnp.dot(a, b, preferred_element_type=F32)


def _dot_nt(a, b):
    return lax.dot_general(a, b, (((1,), (1,)), ((), ())), preferred_element_type=F32)


def _hcols(base, h):
    return slice(base + h * HEAD_DIM, base + (h + 1) * HEAD_DIM)


def _row_scan(x, op):
    n = x.shape[0]
    row = lax.broadcasted_iota(jnp.int32, x.shape, 0)
    sh = 1
    while sh < n:
        x = jnp.where(row >= sh, op(x, pltpu.roll(x, sh, axis=0)), x)
        sh *= 2
    return x


def _gate_cols(g, m0, n_valid):
    R = g.shape[0]
    b = _row_scan(_log_sigmoid(g[:, LANES:]), jnp.add)
    r = g[:, :LANES] - b
    big_m = jnp.maximum(m0, _row_scan(r, jnp.maximum))
    m = b + big_m
    last = n_valid - 1
    m_top = big_m[last:last + 1, :]
    ws = jnp.exp(r - m_top)
    if n_valid < R:
        ws = jnp.where(lax.broadcasted_iota(jnp.int32, ws.shape, 0) < n_valid, ws, 0.0)
    return dict(r=r, big_m=big_m, inter=jnp.exp(m0 - big_m), eneg=jnp.exp(-m), ws=ws,
                decay=jnp.exp(m0 - m_top), m_last=m[last:last + 1, :])


def _value_aug(v, h):
    lane = lax.broadcasted_iota(jnp.int32, (v.shape[0], HEAD_DIM), 1)
    return jnp.concatenate([v, jnp.where(lane == h, 1.0, 0.0)], axis=1).astype(BF16)


def _meta_kernel(x_ref, n1_ref, win_ref, wg_ref, bg_ref, u_ref, c_ref, m_ref):
    xn = _rmsnorm(x_ref[...], n1_ref[...]).astype(BF16)
    gc = _gate_cols(_dot(xn, wg_ref[...]) + bg_ref[...], jnp.zeros((1, LANES), F32), N_META)
    p = _dot(xn, win_ref[...])
    u_ref[...] = p[0:N_META, 0:POOL_WIDTH]
    for h in range(N_HEADS):
        kw_t = (p[:, _hcols(COL_K, h)] * gc["ws"][:, h:h + 1]).T.astype(BF16)
        c_ref[h] = _dot(kw_t, _value_aug(p[:, _hcols(COL_V, h)], h))
    m_ref[...] = jnp.broadcast_to(gc["m_last"], (SUBLANES, LANES))


def _prompt_kernel(nt, xa_ref, xb_ref, umeta_ref, cmeta_ref, mmeta_ref, n1_ref, win_ref, wg_ref, bg_ref,
                   wpool_ref, pscale_ref, gain_ref, wout_ref,
                   x1_ref, pool_ref, c_ref, m_ref,
                   uext_ref, p_ref, g_ref, mix_ref):
    R = PROMPT_TILE
    s = pl.program_id(0)
    slot = s % 2
    prev = 1 - slot
    tile = jnp.maximum(s - 1, 0)

    @pl.when(s == 0)
    def _():
        p_ref[1] = jnp.zeros((R, IN_MAIN), F32)
        g_ref[1] = jnp.zeros((R, GATE_W), F32)

    @pl.when(tile % nt == 0)
    def _():
        c_ref[0] = cmeta_ref[...]
        m_ref[0] = mmeta_ref[...]
        uext_ref[0:CTX_PAD, :] = umeta_ref[...]

    xn = _rmsnorm(xa_ref[0], n1_ref[...]).astype(BF16)
    g_ref[slot] = _dot(xn, wg_ref[...]) + bg_ref[...]
    p_ref[slot] = _dot(xn, win_ref[...])

    pb = p_ref.at[prev]
    gc = _gate_cols(g_ref[prev], m_ref[0, 0:1, :], R)
    r_t = gc["r"].T

    uext_ref[CTX_PAD:CTX_PAD + R, :] = pb[:, 0:POOL_WIDTH]
    for gi, w in enumerate(POOL_WINDOWS):
        cs = slice(gi * POOL_GC, (gi + 1) * POOL_GC)
        acc = uext_ref[:, cs]
        sh = 1
        while sh < w:
            acc = acc + pltpu.roll(acc, sh, axis=0)
            sh *= 2
        z = acc[CTX_PAD:, :] * (1.0 / w) - uext_ref[CTX_PAD:CTX_PAD + R, cs]
        mix_ref[:, cs] = (_dot(z.astype(BF16), wpool_ref[gi]) * pscale_ref[:, cs]).astype(BF16)
    tail = uext_ref[R:R + CTX_PAD, :]
    uext_ref[0:CTX_PAD, :] = tail
    pool_ref[0] = tail

    causal = (lax.broadcasted_iota(jnp.int32, (R, R), 0) >= lax.broadcasted_iota(jnp.int32, (R, R), 1))
    for h in range(N_HEADS):
        q = (pb[:, _hcols(COL_Q, h)] * (HEAD_DIM ** -0.5)).astype(BF16)
        k = pb[:, _hcols(COL_K, h)]
        vaug = _value_aug(pb[:, _hcols(COL_V, h)], h)
        sc = _dot_nt(q, k.astype(BF16))
        e = jnp.exp(jnp.where(causal, r_t[h:h + 1, :] - gc["big_m"][:, h:h + 1], NEG_BIG))
        c0 = c_ref[0, h]
        num = gc["inter"][:, h:h + 1] * _dot(q, c0.astype(BF16)) + _dot((sc * e).astype(BF16), vaug)
        inv = 1.0 / jnp.maximum(jnp.abs(num[:, HEAD_DIM:]), gc["eneg"])
        hh = num[:, 0:HEAD_DIM] * inv[:, h:h + 1]
        hh = hh * lax.rsqrt(jnp.mean(hh * hh, axis=-1, keepdims=True) + EPS) * gain_ref[:, _hcols(0, h)]
        mix_ref[:, _hcols(POOL_WIDTH, h)] = (hh * jax.nn.sigmoid(pb[:, _hcols(COL_O, h)])).astype(BF16)
        kw_t = (k * gc["ws"][:, h:h + 1]).T.astype(BF16)
        c_ref[0, h] = gc["decay"][:, h:h + 1] * c0 + _dot(kw_t, vaug)
    m_ref[0] = jnp.broadcast_to(gc["m_last"], (SUBLANES, LANES))

    x1_ref[0] = xb_ref[0] + _dot(mix_ref[...], wout_ref[...])


def _sample_gates(g_ref, m0):
    n = m0.shape[0]
    b = jnp.zeros((n, LANES), F32)
    big_m = m0
    r, big_ms, ms = [], [], []
    for t in range(DEC_SEQ):
        g = g_ref[t * n:(t + 1) * n, :]
        b = b + _log_sigmoid(g[:, LANES:])
        r.append(g[:, :LANES] - b)
        big_m = jnp.maximum(big_m, r[-1])
        big_ms.append(big_m)
        ms.append(b + big_m)
    return dict(r=r, big_m=big_ms, m=ms,
                ws=[jnp.exp(r[t] - big_m) for t in range(DEC_SEQ)],
                decay=jnp.exp(m0 - big_m))


def _sample_kernel(xs_ref, ctx_ref, cin_ref, n0_ref, m0_ref, n1_ref, win_ref, wg_ref, bg_ref, wpool_ref,
                   pscale_ref, gain_ref, wout_ref,
                   x1_ref, pool_ref, cout_ref, nout_ref, mout_ref,
                   p_ref, g_ref, qc_ref, kwt_ref, vcat_ref, dec_ref, mix_ref):
    i = pl.program_id(0)
    n_seq = m0_ref.shape[0]
    scale = HEAD_DIM ** -0.5

    @pl.when(i == 0)
    def _():
        xn = _rmsnorm(xs_ref[...], n1_ref[...]).astype(BF16)
        g_ref[...] = _dot(xn, wg_ref[...]) + bg_ref[...]
        p_ref[...] = _dot(xn, win_ref[...])
        qc_ref[...] = jnp.zeros_like(qc_ref)

        def urow(e, cs):
            if e < POOL_CTX:
                return ctx_ref[e, :, cs]
            return p_ref[(e - POOL_CTX) * n_seq:(e - POOL_CTX + 1) * n_seq, cs]

        for gi, w in enumerate(POOL_WINDOWS):
            cs = slice(gi * POOL_GC, (gi + 1) * POOL_GC)
            for t in range(DEC_SEQ):
                e = POOL_CTX + t
                wsum = urow(e, cs)
                for j in range(1, w):
                    wsum = wsum + urow(e - j, cs)
                z = wsum * (1.0 / w) - urow(e, cs)
                mix_ref[t * n_seq:(t + 1) * n_seq, cs] = (
                    _dot(z.astype(BF16), wpool_ref[gi]) * pscale_ref[:, cs]).astype(BF16)
        for e in range(DEC_SEQ, POOL_CTX + DEC_SEQ):
            pool_ref[e - DEC_SEQ] = urow(e, slice(0, POOL_WIDTH))

        gs = _sample_gates(g_ref, m0_ref[...])
        dec_ref[...] = gs["decay"]
        mout_ref[...] = gs["m"][DEC_SEQ - 1]
        for h in range(N_HEADS):
            hs = _hcols(0, h)
            n1 = gs["decay"][:, h:h + 1] * n0_ref[:, hs]
            for t in range(DEC_SEQ):
                rows = slice(t * n_seq, (t + 1) * n_seq)
                kw = p_ref[rows, _hcols(COL_K, h)] * gs["ws"][t][:, h:h + 1]
                n1 = n1 + kw
                kwt_ref[h, t] = kw.T.astype(BF16)
                vcat_ref[h, rows, :] = p_ref[rows, _hcols(COL_V, h)].astype(BF16)
            nout_ref[:, hs] = n1

    lane_seq = lax.broadcasted_iota(jnp.int32, (HEAD_DIM, n_seq), 1)
    row8 = lax.broadcasted_iota(jnp.int32, (SUBLANES, HEAD_DIM), 0)

    def per_seq(j, carry):
        seq = i * SAMPLE_BLOCK + j
        base = pl.multiple_of((seq // SUBLANES) * SUBLANES, SUBLANES)
        sub = seq % SUBLANES
        dec_row = dec_ref[pl.ds(seq, 1), :]
        for h in range(N_HEADS):
            c0 = cin_ref[j, h]
            c0b = c0.astype(BF16)
            hs = _hcols(0, h)
            for t in range(DEC_SEQ):
                rows = pl.ds(pl.multiple_of(t * n_seq + base, SUBLANES), SUBLANES)
                q8 = (p_ref[rows, _hcols(COL_Q, h)] * scale).astype(BF16)
                qc_ref[rows, hs] = qc_ref[rows, hs] + jnp.where(row8 == sub, _dot(q8, c0b), 0.0)
            lhs = jnp.concatenate(
                [jnp.where(lane_seq == seq, kwt_ref[h, t], jnp.zeros((), BF16)) for t in range(DEC_SEQ)], axis=1)
            cout_ref[j, h] = dec_row[:, h:h + 1] * c0 + _dot(lhs, vcat_ref[h])
        return carry

    lax.fori_loop(0, SAMPLE_BLOCK, per_seq, 0)

    @pl.when(i == pl.num_programs(0) - 1)
    def _():
        gs = _sample_gates(g_ref, m0_ref[...])
        for h in range(N_HEADS):
            hs = _hcols(0, h)
            n0 = n0_ref[:, hs]
            qs = [p_ref[t * n_seq:(t + 1) * n_seq, _hcols(COL_Q, h)] * scale for t in range(DEC_SEQ)]
            ks = [p_ref[t * n_seq:(t + 1) * n_seq, _hcols(COL_K, h)] for t in range(DEC_SEQ)]
            for t in range(DEC_SEQ):
                rows = slice(t * n_seq, (t + 1) * n_seq)
                inter = jnp.exp(m0_ref[...] - gs["big_m"][t])[:, h:h + 1]
                num = inter * qc_ref[rows, hs]
                den = inter * jnp.sum(qs[t] * n0, axis=-1, keepdims=True)
                for s in range(t + 1):
                    pts = (jnp.sum(qs[t] * ks[s], axis=-1, keepdims=True)
                           * jnp.exp(gs["r"][s] - gs["big_m"][t])[:, h:h + 1])
                    num = num + pts * p_ref[s * n_seq:(s + 1) * n_seq, _hcols(COL_V, h)]
                    den = den + pts
                hh = num / jnp.maximum(jnp.abs(den), jnp.exp(-gs["m"][t])[:, h:h + 1])
                hh = hh * lax.rsqrt(jnp.mean(hh * hh, axis=-1, keepdims=True) + EPS) * gain_ref[:, hs]
                mix_ref[rows, _hcols(POOL_WIDTH, h)] = (
                    hh * jax.nn.sigmoid(p_ref[rows, _hcols(COL_O, h)])).astype(BF16)
        x1_ref[...] = xs_ref[...] + _dot(mix_ref[...], wout_ref[...])


def _ffn_kernel(x_ref, n2_ref, wup_ref, wdown_ref, nf_ref, y_ref):
    x = x_ref[...]
    xn = _rmsnorm(x, n2_ref[...]).astype(BF16)
    acc = x
    for c in range(D_FF // FFN_CHUNK):
        cs = slice(c * FFN_CHUNK, (c + 1) * FFN_CHUNK)
        a = jnp.square(jnp.maximum(_dot(xn, wup_ref[:, cs]), 0.0)).astype(BF16)
        acc = acc + _dot(a, wdown_ref[cs, :])
    y_ref[...] = _rmsnorm(acc, nf_ref[...])


def _full(shape):
    return pl.BlockSpec(shape, lambda *_: (0,) * len(shape))


def _ffn(x, n2, wup, wdown, nf):
    rows = x.shape[0]
    return pl.pallas_call(
        _ffn_kernel,
        grid=(rows // FFN_TILE,),
        in_specs=[pl.BlockSpec((FFN_TILE, D_MODEL), lambda i: (i, 0)),
                  _full((1, D_MODEL)), _full((D_MODEL, D_FF)), _full((D_FF, D_MODEL)), _full((1, D_MODEL))],
        out_specs=pl.BlockSpec((FFN_TILE, D_MODEL), lambda i: (i, 0)),
        out_shape=jax.ShapeDtypeStruct((rows, D_MODEL), F32),
        compiler_params=pltpu.CompilerParams(dimension_semantics=("arbitrary",), vmem_limit_bytes=VMEM_LIMIT),
        name="ffn",
    )(x, n2, wup, wdown, nf)


def kernel(x_prompt, x_sample, state_pool, state_C, state_n, state_m, meta_tokens, norm1, w_in, b_gate,
           w_pool, pool_scale, head_gain, w_out, norm2, w_up, w_down, norm_f):
    B, S, _ = x_prompt.shape
    n_seq = x_sample.shape[0]
    assert w_in.shape[0] == 1 and S % PROMPT_TILE == 0 and n_seq % SAMPLE_BLOCK == 0

    win = w_in[0][:, :IN_MAIN].astype(BF16)
    gpad = ((0, 0), (0, LANES - N_HEADS))
    wg = jnp.concatenate([jnp.pad(w_in[0][:, IN_MAIN:IN_MAIN + N_HEADS], gpad),
                          jnp.pad(w_in[0][:, IN_MAIN + N_HEADS:], gpad)], axis=1).astype(BF16)
    bg = jnp.concatenate([jnp.pad(b_gate[0][:N_HEADS], (0, LANES - N_HEADS)),
                          jnp.pad(b_gate[0][N_HEADS:], (0, LANES - N_HEADS))]).reshape(1, GATE_W)
    wpool = w_pool[0].astype(BF16)
    wout = w_out[0].astype(BF16)
    wup = w_up[0].astype(BF16)
    wdown = w_down[0].astype(BF16)
    n1 = norm1[0].reshape(1, D_MODEL)
    n2 = norm2[0].reshape(1, D_MODEL)
    nf = norm_f.reshape(1, D_MODEL)
    pscale = pool_scale[0].reshape(1, POOL_WIDTH)
    gain = head_gain[0].reshape(1, MLSTM_WIDTH)

    weight_specs = [_full((1, D_MODEL)), _full((D_MODEL, IN_MAIN)), _full((D_MODEL, GATE_W)), _full((1, GATE_W))]
    mixer_specs = [_full((len(POOL_WINDOWS), POOL_GC, POOL_GC)), _full((1, POOL_WIDTH)),
                   _full((1, MLSTM_WIDTH)), _full((D_MODEL, D_MODEL))]

    meta_x = jnp.pad(meta_tokens, ((0, LANES - N_META), (0, 0)))
    u_meta, c_meta, m_meta = pl.pallas_call(
        _meta_kernel,
        out_shape=(jax.ShapeDtypeStruct((N_META, POOL_WIDTH), F32),
                   jax.ShapeDtypeStruct((N_HEADS, HEAD_DIM, AUG), F32),
                   jax.ShapeDtypeStruct((SUBLANES, LANES), F32)),
        compiler_params=pltpu.CompilerParams(vmem_limit_bytes=VMEM_LIMIT),
        name="meta_prefix",
    )(meta_x, n1, win, wg, bg)

    nt = S // PROMPT_TILE
    n_tiles = B * nt

    def cur_tile(s):
        j = jnp.minimum(s, n_tiles - 1)
        return (j // nt, j % nt, 0)

    def prev_tile(s):
        j = jnp.maximum(s - 1, 0)
        return (j // nt, j % nt, 0)

    def prev_seq(ndim):
        return lambda s: (jnp.maximum(s - 1, 0) // nt,) + (0,) * (ndim - 1)

    x1_p, pool_p, c_p, m_p = pl.pallas_call(
        lambda *refs: _prompt_kernel(nt, *refs),
        grid=(n_tiles + 1,),
        in_specs=[pl.BlockSpec((1, PROMPT_TILE, D_MODEL), cur_tile),
                  pl.BlockSpec((1, PROMPT_TILE, D_MODEL), prev_tile),
                  _full((N_META, POOL_WIDTH)), _full((N_HEADS, HEAD_DIM, AUG)), _full((SUBLANES, LANES))]
        + weight_specs + mixer_specs,
        out_specs=[pl.BlockSpec((1, PROMPT_TILE, D_MODEL), prev_tile),
                   pl.BlockSpec((1, CTX_PAD, POOL_WIDTH), prev_seq(3)),
                   pl.BlockSpec((1, N_HEADS, HEAD_DIM, AUG), prev_seq(4)),
                   pl.BlockSpec((1, SUBLANES, LANES), prev_seq(3))],
        out_shape=(jax.ShapeDtypeStruct((B, S, D_MODEL), F32),
                   jax.ShapeDtypeStruct((B, CTX_PAD, POOL_WIDTH), F32),
                   jax.ShapeDtypeStruct((B, N_HEADS, HEAD_DIM, AUG), F32),
                   jax.ShapeDtypeStruct((B, SUBLANES, LANES), F32)),
        scratch_shapes=[pltpu.VMEM((CTX_PAD + PROMPT_TILE, POOL_WIDTH), F32),
                        pltpu.VMEM((2, PROMPT_TILE, IN_MAIN), F32),
                        pltpu.VMEM((2, PROMPT_TILE, GATE_W), F32),
                        pltpu.VMEM((PROMPT_TILE, D_MODEL), BF16)],
        compiler_params=pltpu.CompilerParams(dimension_semantics=("arbitrary",), vmem_limit_bytes=VMEM_LIMIT),
        name="prompt_mixer",
    )(x_prompt, x_prompt, u_meta, c_meta, m_meta, n1, win, wg, bg, wpool, pscale, gain, wout)

    rows_s = DEC_SEQ * n_seq
    xs_t = x_sample.transpose(1, 0, 2).reshape(rows_s, D_MODEL)
    ctx_t = state_pool[0].transpose(1, 0, 2)
    n0 = state_n[0].reshape(n_seq, MLSTM_WIDTH)
    m0 = jnp.pad(state_m[0], ((0, 0), (0, LANES - N_HEADS)))
    state_block = pl.BlockSpec((SAMPLE_BLOCK, N_HEADS, HEAD_DIM, HEAD_DIM), lambda i: (i, 0, 0, 0))
    x1_s, pool_s, c_s, n_s, m_s = pl.pallas_call(
        _sample_kernel,
        grid=(n_seq // SAMPLE_BLOCK,),
        in_specs=[_full((rows_s, D_MODEL)), _full((POOL_CTX, n_seq, POOL_WIDTH)), state_block,
                  _full((n_seq, MLSTM_WIDTH)), _full((n_seq, LANES))] + weight_specs + mixer_specs,
        out_specs=[_full((rows_s, D_MODEL)), _full((POOL_CTX, n_seq, POOL_WIDTH)), state_block,
                   _full((n_seq, MLSTM_WIDTH)), _full((n_seq, LANES))],
        out_shape=(jax.ShapeDtypeStruct((rows_s, D_MODEL), F32),
                   jax.ShapeDtypeStruct((POOL_CTX, n_seq, POOL_WIDTH), F32),
                   jax.ShapeDtypeStruct((n_seq, N_HEADS, HEAD_DIM, HEAD_DIM), F32),
                   jax.ShapeDtypeStruct((n_seq, MLSTM_WIDTH), F32),
                   jax.ShapeDtypeStruct((n_seq, LANES), F32)),
        scratch_shapes=[pltpu.VMEM((rows_s, IN_MAIN), F32),
                        pltpu.VMEM((rows_s, GATE_W), F32),
                        pltpu.VMEM((rows_s, MLSTM_WIDTH), F32),
                        pltpu.VMEM((N_HEADS, DEC_SEQ, HEAD_DIM, n_seq), BF16),
                        pltpu.VMEM((N_HEADS, rows_s, HEAD_DIM), BF16),
                        pltpu.VMEM((n_seq, LANES), F32),
                        pltpu.VMEM((rows_s, D_MODEL), BF16)],
        compiler_params=pltpu.CompilerParams(dimension_semantics=("arbitrary",), vmem_limit_bytes=VMEM_LIMIT),
        name="sample_mixer",
    )(xs_t, ctx_t, state_C[0], n0, m0, n1, win, wg, bg, wpool, pscale, gain, wout)

    y_p = _ffn(x1_p.reshape(B * S, D_MODEL), n2, wup, wdown, nf).reshape(B, S, D_MODEL)
    y_s = _ffn(x1_s, n2, wup, wdown, nf).reshape(DEC_SEQ, n_seq, D_MODEL).transpose(1, 0, 2)

    n_p = jnp.stack([c_p[:, h, :, HEAD_DIM + h] for h in range(N_HEADS)], axis=1)
    return (y_p, y_s,
            pool_p[:, CTX_PAD - POOL_CTX:, :][None],
            c_p[..., :HEAD_DIM][None],
            n_p[None],
            m_p[:, 0, :N_HEADS][None],
            pool_s.transpose(1, 0, 2)[None],
            c_s[None],
            n_s.reshape(n_seq, N_HEADS, HEAD_DIM)[None],
            m_s[:, :N_HEADS][None])
```

```python
import functools

import jax
import jax.numpy as jnp
from jax import lax
from jax.experimental import pallas as pl
from jax.experimental.pallas import tpu as pltpu

D_MODEL = 1024
N_META = 16
POOL_WIDTH = 512
POOL_WINDOWS = (2, 4, 8, 16)
POOL_GC = 128
POOL_CTX = 15
MLSTM_WIDTH = 512
N_HEADS = 4
HEAD_DIM = 128
D_FF = 4096
EPS = 1e-6
DEC_SEQ = 4

LANES = 128
SUBLANES = 8
COL_Q = POOL_WIDTH
COL_K = COL_Q + MLSTM_WIDTH
COL_V = COL_K + MLSTM_WIDTH
COL_O = COL_V + MLSTM_WIDTH
IN_MAIN = COL_O + MLSTM_WIDTH
GATE_W = 2 * LANES
AUG = 2 * HEAD_DIM
CTX_PAD = 16
PROMPT_TILE = 256
IN_CHUNK = 512
FFN_TILE = 512
FFN_CHUNK = 1024
SAMPLE_BLOCK = 8
NEG_BIG = -1e30
VMEM_LIMIT = 56 * 1024 * 1024

F32 = jnp.float32
BF16 = jnp.bfloat16


def _rmsnorm(x, g):
    return x * lax.rsqrt(jnp.mean(x * x, axis=-1, keepdims=True) + EPS) * g


def _log_sigmoid(x):
    return jnp.minimum(x, 0.0) - jnp.log1p(jnp.exp(-jnp.abs(x)))


def _dot(a, b):
    return jnp.dot(a, b, preferred_element_type=F32)


def _dot_nt(a, b):
    return lax.dot_general(a, b, (((1,), (1,)), ((), ())), preferred_element_type=F32)


def _hcols(base, h):
    return slice(base + h * HEAD_DIM, base + (h + 1) * HEAD_DIM)


def _row_scan(x, op):
    n = x.shape[0]
    row = lax.broadcasted_iota(jnp.int32, x.shape, 0)
    sh = 1
    while sh < n:
        x = jnp.where(row >= sh, op(x, pltpu.roll(x, sh, axis=0)), x)
        sh *= 2
    return x


def _gate_cols(g, m0, n_valid):
    R = g.shape[0]
    b = _row_scan(_log_sigmoid(g[:, LANES:]), jnp.add)
    r = g[:, :LANES] - b
    big_m = jnp.maximum(m0, _row_scan(r, jnp.maximum))
    m = b + big_m
    last = n_valid - 1
    m_top = big_m[last:last + 1, :]
    ws = jnp.exp(r - m_top)
    if n_valid < R:
        ws = jnp.where(lax.broadcasted_iota(jnp.int32, ws.shape, 0) < n_valid, ws, 0.0)
    return dict(r=r, big_m=big_m, inter=jnp.exp(m0 - big_m), eneg=jnp.exp(-m), ws=ws,
                decay=jnp.exp(m0 - m_top), m_last=m[last:last + 1, :])


def _value_aug(v, h):
    lane = lax.broadcasted_iota(jnp.int32, (v.shape[0], HEAD_DIM), 1)
    return jnp.concatenate([v, jnp.where(lane == h, 1.0, 0.0)], axis=1).astype(BF16)


def _meta_kernel(x_ref, n1_ref, win_ref, wg_ref, bg_ref, u_ref, c_ref, m_ref):
    xn = _rmsnorm(x_ref[...], n1_ref[...]).astype(BF16)
    gc = _gate_cols(_dot(xn, wg_ref[...]) + bg_ref[...], jnp.zeros((1, LANES), F32), N_META)
    p = _dot(xn, win_ref[...])
    u_ref[...] = p[0:N_META, 0:POOL_WIDTH]
    for h in range(N_HEADS):
        kw_t = (p[:, _hcols(COL_K, h)] * gc["ws"][:, h:h + 1]).T.astype(BF16)
        c_ref[h] = _dot(kw_t, _value_aug(p[:, _hcols(COL_V, h)], h))
    m_ref[...] = jnp.broadcast_to(gc["m_last"], (SUBLANES, LANES))


def _prompt_kernel(nt, xa_ref, xc_ref, umeta_ref, cmeta_ref, mmeta_ref, n1_ref, win_ref, wg_ref, bg_ref,
                   wpool_ref, pscale_ref, gain_ref, wout_ref,
                   x1_ref, pool_ref, c_ref, m_ref,
                   uext_ref, xn_ref, p0_ref, p1_ref, g0_ref, g1_ref, mix0_ref, mix1_ref):
    R = PROMPT_TILE
    s = pl.program_id(0)
    tile = jnp.maximum(s - 1, 0)

    @pl.when(s == 0)
    def _():
        p1_ref[...] = jnp.zeros((R, IN_MAIN), F32)
        g1_ref[...] = jnp.zeros((R, GATE_W), F32)
        mix1_ref[...] = jnp.zeros((R, D_MODEL), BF16)

    @pl.when(tile % nt == 0)
    def _():
        c_ref[0] = cmeta_ref[...]
        m_ref[0] = mmeta_ref[...]
        uext_ref[0:CTX_PAD, :] = umeta_ref[...]

    stage = functools.partial(_prompt_step, xa_ref, xc_ref, n1_ref, win_ref, wg_ref, bg_ref, wpool_ref,
                              pscale_ref, gain_ref, wout_ref, x1_ref, pool_ref, c_ref, m_ref, uext_ref, xn_ref)

    @pl.when(s % 2 == 0)
    def _():
        stage(p0_ref, p1_ref, g0_ref, g1_ref, mix0_ref, mix1_ref)

    @pl.when(s % 2 == 1)
    def _():
        stage(p1_ref, p0_ref, g1_ref, g0_ref, mix1_ref, mix0_ref)


def _prompt_step(xa_ref, xc_ref, n1_ref, win_ref, wg_ref, bg_ref, wpool_ref, pscale_ref, gain_ref, wout_ref,
                 x1_ref, pool_ref, c_ref, m_ref, uext_ref, xn_ref,
                 p_new, pb, g_new, g_old, mx, mix_old):
    R = PROMPT_TILE
    gc = _gate_cols(g_old[...], m_ref[0, 0:1, :], R)
    r_t = gc["r"].T

    x1_ref[0] = xc_ref[0] + _dot(mix_old[...], wout_ref[...])

    xn_ref[...] = _rmsnorm(xa_ref[0], n1_ref[...]).astype(BF16)
    g_new[...] = _dot(xn_ref[...], wg_ref[...]) + bg_ref[...]
    for c in range(IN_MAIN // IN_CHUNK):
        cs = slice(c * IN_CHUNK, (c + 1) * IN_CHUNK)
        p_new[:, cs] = _dot(xn_ref[...], win_ref[:, cs])

    uext_ref[CTX_PAD:CTX_PAD + R, :] = pb[:, 0:POOL_WIDTH]
    for gi, w in enumerate(POOL_WINDOWS):
        cs = slice(gi * POOL_GC, (gi + 1) * POOL_GC)
        acc = uext_ref[:, cs]
        sh = 1
        while sh < w:
            acc = acc + pltpu.roll(acc, sh, axis=0)
            sh *= 2
        z = acc[CTX_PAD:, :] * (1.0 / w) - uext_ref[CTX_PAD:CTX_PAD + R, cs]
        mx[:, cs] = (_dot(z.astype(BF16), wpool_ref[gi]) * pscale_ref[:, cs]).astype(BF16)
    tail = uext_ref[R:R + CTX_PAD, :]
    uext_ref[0:CTX_PAD, :] = tail
    pool_ref[0] = tail

    causal = (lax.broadcasted_iota(jnp.int32, (R, R), 0) >= lax.broadcasted_iota(jnp.int32, (R, R), 1))
    for h in range(N_HEADS):
        q = (pb[:, _hcols(COL_Q, h)] * (HEAD_DIM ** -0.5)).astype(BF16)
        k = pb[:, _hcols(COL_K, h)]
        vaug = _value_aug(pb[:, _hcols(COL_V, h)], h)
        sc = _dot_nt(q, k.astype(BF16))
        e = jnp.exp(jnp.where(causal, r_t[h:h + 1, :] - gc["big_m"][:, h:h + 1], NEG_BIG))
        c0 = c_ref[0, h]
        num = gc["inter"][:, h:h + 1] * _dot(q, c0.astype(BF16)) + _dot((sc * e).astype(BF16), vaug)
        inv = 1.0 / jnp.maximum(jnp.abs(num[:, HEAD_DIM:]), gc["eneg"])
        hh = num[:, 0:HEAD_DIM] * inv[:, h:h + 1]
        hh = hh * lax.rsqrt(jnp.mean(hh * hh, axis=-1, keepdims=True) + EPS) * gain_ref[:, _hcols(0, h)]
        mx[:, _hcols(POOL_WIDTH, h)] = (hh * jax.nn.sigmoid(pb[:, _hcols(COL_O, h)])).astype(BF16)
        kw_t = (k * gc["ws"][:, h:h + 1]).T.astype(BF16)
        c_ref[0, h] = gc["decay"][:, h:h + 1] * c0 + _dot(kw_t, vaug)
    m_ref[0] = jnp.broadcast_to(gc["m_last"], (SUBLANES, LANES))


def _sample_gates(g_ref, m0):
    n = m0.shape[0]
    b = jnp.zeros((n, LANES), F32)
    big_m = m0
    r, big_ms, ms = [], [], []
    for t in range(DEC_SEQ):
        g = g_ref[t * n:(t + 1) * n, :]
        b = b + _log_sigmoid(g[:, LANES:])
        r.append(g[:, :LANES] - b)
        big_m = jnp.maximum(big_m, r[-1])
        big_ms.append(big_m)
        ms.append(b + big_m)
    return dict(r=r, big_m=big_ms, m=ms,
                ws=[jnp.exp(r[t] - big_m) for t in range(DEC_SEQ)],
                decay=jnp.exp(m0 - big_m))


def _sample_kernel(xs_ref, ctx_ref, cin_ref, n0_ref, m0_ref, n1_ref, win_ref, wg_ref, bg_ref, wpool_ref,
                   pscale_ref, gain_ref, wout_ref,
                   x1_ref, pool_ref, cout_ref, nout_ref, mout_ref,
                   p_ref, g_ref, qc_ref, kwt_ref, vcat_ref, dec_ref, mix_ref):
    i = pl.program_id(0)
    n_seq = m0_ref.shape[0]
    scale = HEAD_DIM ** -0.5

    @pl.when(i == 0)
    def _():
        xn = _rmsnorm(xs_ref[...], n1_ref[...]).astype(BF16)
        g_ref[...] = _dot(xn, wg_ref[...]) + bg_ref[...]
        p_ref[...] = _dot(xn, win_ref[...])
        qc_ref[...] = jnp.zeros_like(qc_ref)

        def urow(e, cs):
            if e < POOL_CTX:
                return ctx_ref[e, :, cs]
            return p_ref[(e - POOL_CTX) * n_seq:(e - POOL_CTX + 1) * n_seq, cs]

        for gi, w in enumerate(POOL_WINDOWS):
            cs = slice(gi * POOL_GC, (gi + 1) * POOL_GC)
            for t in range(DEC_SEQ):
                e = POOL_CTX + t
                wsum = urow(e, cs)
                for j in range(1, w):
                    wsum = wsum + urow(e - j, cs)
                z = wsum * (1.0 / w) - urow(e, cs)
                mix_ref[t * n_seq:(t + 1) * n_seq, cs] = (
                    _dot(z.astype(BF16), wpool_ref[gi]) * pscale_ref[:, cs]).astype(BF16)
        for e in range(DEC_SEQ, POOL_CTX + DEC_SEQ):
            pool_ref[e - DEC_SEQ] = urow(e, slice(0, POOL_WIDTH))

        gs = _sample_gates(g_ref, m0_ref[...])
        dec_ref[...] = gs["decay"]
        mout_ref[...] = gs["m"][DEC_SEQ - 1]
        for h in range(N_HEADS):
            hs = _hcols(0, h)
            n1 = gs["decay"][:, h:h + 1] * n0_ref[:, hs]
            for t in range(DEC_SEQ):
                rows = slice(t * n_seq, (t + 1) * n_seq)
                kw = p_ref[rows, _hcols(COL_K, h)] * gs["ws"][t][:, h:h + 1]
                n1 = n1 + kw
                kwt_ref[h, t] = kw.T.astype(BF16)
                vcat_ref[h, rows, :] = p_ref[rows, _hcols(COL_V, h)].astype(BF16)
            nout_ref[:, hs] = n1

    lane_seq = lax.broadcasted_iota(jnp.int32, (HEAD_DIM, n_seq), 1)
    row8 = lax.broadcasted_iota(jnp.int32, (SUBLANES, HEAD_DIM), 0)

    def per_seq(j, carry):
        seq = i * SAMPLE_BLOCK + j
        base = pl.multiple_of((seq // SUBLANES) * SUBLANES, SUBLANES)
        sub = seq % SUBLANES
        dec_row = dec_ref[pl.ds(seq, 1), :]
        for h in range(N_HEADS):
            c0 = cin_ref[j, h]
            c0b = c0.astype(BF16)
            hs = _hcols(0, h)
            for t in range(DEC_SEQ):
                rows = pl.ds(pl.multiple_of(t * n_seq + base, SUBLANES), SUBLANES)
                q8 = (p_ref[rows, _hcols(COL_Q, h)] * scale).astype(BF16)
                qc_ref[rows, hs] = qc_ref[rows, hs] + jnp.where(row8 == sub, _dot(q8, c0b), 0.0)
            lhs = jnp.concatenate(
                [jnp.where(lane_seq == seq, kwt_ref[h, t], jnp.zeros((), BF16)) for t in range(DEC_SEQ)], axis=1)
            cout_ref[j, h] = dec_row[:, h:h + 1] * c0 + _dot(lhs, vcat_ref[h])
        return carry

    lax.fori_loop(0, SAMPLE_BLOCK, per_seq, 0)

    @pl.when(i == pl.num_programs(0) - 1)
    def _():
        gs = _sample_gates(g_ref, m0_ref[...])
        for h in range(N_HEADS):
            hs = _hcols(0, h)
            n0 = n0_ref[:, hs]
            qs = [p_ref[t * n_seq:(t + 1) * n_seq, _hcols(COL_Q, h)] * scale for t in range(DEC_SEQ)]
            ks = [p_ref[t * n_seq:(t + 1) * n_seq, _hcols(COL_K, h)] for t in range(DEC_SEQ)]
            for t in range(DEC_SEQ):
                rows = slice(t * n_seq, (t + 1) * n_seq)
                inter = jnp.exp(m0_ref[...] - gs["big_m"][t])[:, h:h + 1]
                num = inter * qc_ref[rows, hs]
                den = inter * jnp.sum(qs[t] * n0, axis=-1, keepdims=True)
                for s in range(t + 1):
                    pts = (jnp.sum(qs[t] * ks[s], axis=-1, keepdims=True)
                           * jnp.exp(gs["r"][s] - gs["big_m"][t])[:, h:h + 1])
                    num = num + pts * p_ref[s * n_seq:(s + 1) * n_seq, _hcols(COL_V, h)]
                    den = den + pts
                hh = num / jnp.maximum(jnp.abs(den), jnp.exp(-gs["m"][t])[:, h:h + 1])
                hh = hh * lax.rsqrt(jnp.mean(hh * hh, axis=-1, keepdims=True) + EPS) * gain_ref[:, hs]
                mix_ref[rows, _hcols(POOL_WIDTH, h)] = (
                    hh * jax.nn.sigmoid(p_ref[rows, _hcols(COL_O, h)])).astype(BF16)
        x1_ref[...] = xs_ref[...] + _dot(mix_ref[...], wout_ref[...])


def _ffn_kernel(x_ref, n2_ref, wup_ref, wdown_ref, nf_ref, y_ref):
    x = x_ref[...]
    xn = _rmsnorm(x, n2_ref[...]).astype(BF16)
    acc = x
    for c in range(D_FF // FFN_CHUNK):
        cs = slice(c * FFN_CHUNK, (c + 1) * FFN_CHUNK)
        a = jnp.square(jnp.maximum(_dot(xn, wup_ref[:, cs]), 0.0)).astype(BF16)
        acc = acc + _dot(a, wdown_ref[cs, :])
    y_ref[...] = _rmsnorm(acc, nf_ref[...])


def _full(shape):
    return pl.BlockSpec(shape, lambda *_: (0,) * len(shape))


def _ffn(x, n2, wup, wdown, nf):
    rows = x.shape[0]
    return pl.pallas_call(
        _ffn_kernel,
        grid=(rows // FFN_TILE,),
        in_specs=[pl.BlockSpec((FFN_TILE, D_MODEL), lambda i: (i, 0)),
                  _full((1, D_MODEL)), _full((D_MODEL, D_FF)), _full((D_FF, D_MODEL)), _full((1, D_MODEL))],
        out_specs=pl.BlockSpec((FFN_TILE, D_MODEL), lambda i: (i, 0)),
        out_shape=jax.ShapeDtypeStruct((rows, D_MODEL), F32),
        compiler_params=pltpu.CompilerParams(dimension_semantics=("arbitrary",), vmem_limit_bytes=VMEM_LIMIT),
        name="ffn",
    )(x, n2, wup, wdown, nf)


def kernel(x_prompt, x_sample, state_pool, state_C, state_n, state_m, meta_tokens, norm1, w_in, b_gate,
           w_pool, pool_scale, head_gain, w_out, norm2, w_up, w_down, norm_f):
    B, S, _ = x_prompt.shape
    n_seq = x_sample.shape[0]
    assert w_in.shape[0] == 1 and S % PROMPT_TILE == 0 and n_seq % SAMPLE_BLOCK == 0

    win = w_in[0][:, :IN_MAIN].astype(BF16)
    gpad = ((0, 0), (0, LANES - N_HEADS))
    wg = jnp.concatenate([jnp.pad(w_in[0][:, IN_MAIN:IN_MAIN + N_HEADS], gpad),
                          jnp.pad(w_in[0][:, IN_MAIN + N_HEADS:], gpad)], axis=1).astype(BF16)
    bg = jnp.concatenate([jnp.pad(b_gate[0][:N_HEADS], (0, LANES - N_HEADS)),
                          jnp.pad(b_gate[0][N_HEADS:], (0, LANES - N_HEADS))]).reshape(1, GATE_W)
    wpool = w_pool[0].astype(BF16)
    wout = w_out[0].astype(BF16)
    wup = w_up[0].astype(BF16)
    wdown = w_down[0].astype(BF16)
    n1 = norm1[0].reshape(1, D_MODEL)
    n2 = norm2[0].reshape(1, D_MODEL)
    nf = norm_f.reshape(1, D_MODEL)
    pscale = pool_scale[0].reshape(1, POOL_WIDTH)
    gain = head_gain[0].reshape(1, MLSTM_WIDTH)

    weight_specs = [_full((1, D_MODEL)), _full((D_MODEL, IN_MAIN)), _full((D_MODEL, GATE_W)), _full((1, GATE_W))]
    mixer_specs = [_full((len(POOL_WINDOWS), POOL_GC, POOL_GC)), _full((1, POOL_WIDTH)),
                   _full((1, MLSTM_WIDTH)), _full((D_MODEL, D_MODEL))]

    meta_x = jnp.pad(meta_tokens, ((0, LANES - N_META), (0, 0)))
    u_meta, c_meta, m_meta = pl.pallas_call(
        _meta_kernel,
        out_shape=(jax.ShapeDtypeStruct((N_META, POOL_WIDTH), F32),
                   jax.ShapeDtypeStruct((N_HEADS, HEAD_DIM, AUG), F32),
                   jax.ShapeDtypeStruct((SUBLANES, LANES), F32)),
        compiler_params=pltpu.CompilerParams(vmem_limit_bytes=VMEM_LIMIT),
        name="meta_prefix",
    )(meta_x, n1, win, wg, bg)

    nt = S // PROMPT_TILE
    n_tiles = B * nt

    def tile_at(lag):
        def index(s):
            j = jnp.clip(s - lag, 0, n_tiles - 1)
            return (j // nt, j % nt, 0)
        return index

    def mixed_seq(ndim):
        return lambda s: (jnp.maximum(s - 1, 0) // nt,) + (0,) * (ndim - 1)

    x1_p, pool_p, c_p, m_p = pl.pallas_call(
        functools.partial(_prompt_kernel, nt),
        grid=(n_tiles + 2,),
        in_specs=[pl.BlockSpec((1, PROMPT_TILE, D_MODEL), tile_at(0)),
                  pl.BlockSpec((1, PROMPT_TILE, D_MODEL), tile_at(2)),
                  _full((N_META, POOL_WIDTH)), _full((N_HEADS, HEAD_DIM, AUG)), _full((SUBLANES, LANES))]
        + weight_specs + mixer_specs,
        out_specs=[pl.BlockSpec((1, PROMPT_TILE, D_MODEL), tile_at(2)),
                   pl.BlockSpec((1, CTX_PAD, POOL_WIDTH), mixed_seq(3)),
                   pl.BlockSpec((1, N_HEADS, HEAD_DIM, AUG), mixed_seq(4)),
                   pl.BlockSpec((1, SUBLANES, LANES), mixed_seq(3))],
        out_shape=(jax.ShapeDtypeStruct((B, S, D_MODEL), F32),
                   jax.ShapeDtypeStruct((B + 1, CTX_PAD, POOL_WIDTH), F32),
                   jax.ShapeDtypeStruct((B + 1, N_HEADS, HEAD_DIM, AUG), F32),
                   jax.ShapeDtypeStruct((B + 1, SUBLANES, LANES), F32)),
        scratch_shapes=[pltpu.VMEM((CTX_PAD + PROMPT_TILE, POOL_WIDTH), F32),
                        pltpu.VMEM((PROMPT_TILE, D_MODEL), BF16),
                        pltpu.VMEM((PROMPT_TILE, IN_MAIN), F32), pltpu.VMEM((PROMPT_TILE, IN_MAIN), F32),
                        pltpu.VMEM((PROMPT_TILE, GATE_W), F32), pltpu.VMEM((PROMPT_TILE, GATE_W), F32),
                        pltpu.VMEM((PROMPT_TILE, D_MODEL), BF16), pltpu.VMEM((PROMPT_TILE, D_MODEL), BF16)],
        compiler_params=pltpu.CompilerParams(dimension_semantics=("arbitrary",), vmem_limit_bytes=VMEM_LIMIT),
        name="prompt_mixer",
    )(x_prompt, x_prompt, u_meta, c_meta, m_meta, n1, win, wg, bg, wpool, pscale, gain, wout)
    pool_p, c_p, m_p = pool_p[:B], c_p[:B], m_p[:B]

    rows_s = DEC_SEQ * n_seq
    xs_t = x_sample.transpose(1, 0, 2).reshape(rows_s, D_MODEL)
    ctx_t = state_pool[0].transpose(1, 0, 2)
    n0 = state_n[0].reshape(n_seq, MLSTM_WIDTH)
    m0 = jnp.pad(state_m[0], ((0, 0), (0, LANES - N_HEADS)))
    state_block = pl.BlockSpec((SAMPLE_BLOCK, N_HEADS, HEAD_DIM, HEAD_DIM), lambda i: (i, 0, 0, 0))
    x1_s, pool_s, c_s, n_s, m_s = pl.pallas_call(
        _sample_kernel,
        grid=(n_seq // SAMPLE_BLOCK,),
        in_specs=[_full((rows_s, D_MODEL)), _full((POOL_CTX, n_seq, POOL_WIDTH)), state_block,
                  _full((n_seq, MLSTM_WIDTH)), _full((n_seq, LANES))] + weight_specs + mixer_specs,
        out_specs=[_full((rows_s, D_MODEL)), _full((POOL_CTX, n_seq, POOL_WIDTH)), state_block,
                   _full((n_seq, MLSTM_WIDTH)), _full((n_seq, LANES))],
        out_shape=(jax.ShapeDtypeStruct((rows_s, D_MODEL), F32),
                   jax.ShapeDtypeStruct((POOL_CTX, n_seq, POOL_WIDTH), F32),
                   jax.ShapeDtypeStruct((n_seq, N_HEADS, HEAD_DIM, HEAD_DIM), F32),
                   jax.ShapeDtypeStruct((n_seq, MLSTM_WIDTH), F32),
                   jax.ShapeDtypeStruct((n_seq, LANES), F32)),
        scratch_shapes=[pltpu.VMEM((rows_s, IN_MAIN), F32),
                        pltpu.VMEM((rows_s, GATE_W), F32),
                        pltpu.VMEM((rows_s, MLSTM_WIDTH), F32),
                        pltpu.VMEM((N_HEADS, DEC_SEQ, HEAD_DIM, n_seq), BF16),
                        pltpu.VMEM((N_HEADS, rows_s, HEAD_DIM), BF16),
                        pltpu.VMEM((n_seq, LANES), F32),
                        pltpu.VMEM((rows_s, D_MODEL), BF16)],
        compiler_params=pltpu.CompilerParams(dimension_semantics=("arbitrary",), vmem_limit_bytes=VMEM_LIMIT),
        name="sample_mixer",
    )(xs_t, ctx_t, state_C[0], n0, m0, n1, win, wg, bg, wpool, pscale, gain, wout)

    y_p = _ffn(x1_p.reshape(B * S, D_MODEL), n2, wup, wdown, nf).reshape(B, S, D_MODEL)
    y_s = _ffn(x1_s, n2, wup, wdown, nf).reshape(DEC_SEQ, n_seq, D_MODEL).transpose(1, 0, 2)

    n_p = jnp.stack([c_p[:, h, :, HEAD_DIM + h] for h in range(N_HEADS)], axis=1)
    return (y_p, y_s,
            pool_p[:, CTX_PAD - POOL_CTX:, :][None],
            c_p[..., :HEAD_DIM][None],
            n_p[None],
            m_p[:, 0, :N_HEADS][None],
            pool_s.transpose(1, 0, 2)[None],
            c_s[None],
            n_s.reshape(n_seq, N_HEADS, HEAD_DIM)[None],
            m_s[:, :N_HEADS][None])
```

```python
import functools

import jax
import jax.numpy as jnp
from jax import lax
from jax.experimental import pallas as pl
from jax.experimental.pallas import tpu as pltpu

D_MODEL = 1024
N_META = 16
POOL_WIDTH = 512
POOL_WINDOWS = (2, 4, 8, 16)
POOL_GC = 128
POOL_CTX = 15
MLSTM_WIDTH = 512
N_HEADS = 4
HEAD_DIM = 128
D_FF = 4096
EPS = 1e-6
DEC_SEQ = 4

LANES = 128
SUBLANES = 8
COL_Q = POOL_WIDTH
COL_K = COL_Q + MLSTM_WIDTH
COL_V = COL_K + MLSTM_WIDTH
COL_O = COL_V + MLSTM_WIDTH
IN_MAIN = COL_O + MLSTM_WIDTH
GATE_W = 2 * LANES
AUG = 2 * HEAD_DIM
CTX_PAD = 16
PROMPT_TILE = 256
IN_CHUNK = 512
FFN_TILE = 512
FFN_CHUNK = 1024
SAMPLE_BLOCK = 8
NEG_BIG = -1e30
VMEM_LIMIT = 56 * 1024 * 1024

F32 = jnp.float32
BF16 = jnp.bfloat16


def _rmsnorm(x, g):
    return x * lax.rsqrt(jnp.mean(x * x, axis=-1, keepdims=True) + EPS) * g


def _log_sigmoid(x):
    return jnp.minimum(x, 0.0) - jnp.log1p(jnp.exp(-jnp.abs(x)))


def _dot(a, b):
    return jnp.dot(a, b, preferred_element_type=F32)


def _dot_nt(a, b):
    return lax.dot_general(a, b, (((1,), (1,)), ((), ())), preferred_element_type=F32)


def _hcols(base, h):
    return slice(base + h * HEAD_DIM, base + (h + 1) * HEAD_DIM)


def _row_scan(x, op):
    n = x.shape[0]
    row = lax.broadcasted_iota(jnp.int32, x.shape, 0)
    sh = 1
    while sh < n:
        x = jnp.where(row >= sh, op(x, pltpu.roll(x, sh, axis=0)), x)
        sh *= 2
    return x


def _gate_cols(g, m0, n_valid):
    R = g.shape[0]
    b = _row_scan(_log_sigmoid(g[:, LANES:]), jnp.add)
    r = g[:, :LANES] - b
    big_m = jnp.maximum(m0, _row_scan(r, jnp.maximum))
    m = b + big_m
    last = n_valid - 1
    m_top = big_m[last:last + 1, :]
    ws = jnp.exp(r - m_top)
    if n_valid < R:
        ws = jnp.where(lax.broadcasted_iota(jnp.int32, ws.shape, 0) < n_valid, ws, 0.0)
    return dict(r=r, big_m=big_m, inter=jnp.exp(m0 - big_m), eneg=jnp.exp(-m), ws=ws,
                decay=jnp.exp(m0 - m_top), m_last=m[last:last + 1, :])


def _value_aug(v, h):
    lane = lax.broadcasted_iota(jnp.int32, (v.shape[0], HEAD_DIM), 1)
    return jnp.concatenate([v, jnp.where(lane == h, 1.0, 0.0)], axis=1).astype(BF16)


def _meta_kernel(x_ref, n1_ref, win_ref, wg_ref, bg_ref, u_ref, c_ref, m_ref):
    xn = _rmsnorm(x_ref[...], n1_ref[...]).astype(BF16)
    gc = _gate_cols(_dot(xn, wg_ref[...]) + bg_ref[...], jnp.zeros((1, LANES), F32), N_META)
    p = _dot(xn, win_ref[...])
    u_ref[...] = p[0:N_META, 0:POOL_WIDTH]
    for h in range(N_HEADS):
        kw_t = (p[:, _hcols(COL_K, h)] * gc["ws"][:, h:h + 1]).T.astype(BF16)
        c_ref[h] = _dot(kw_t, _value_aug(p[:, _hcols(COL_V, h)], h))
    m_ref[...] = jnp.broadcast_to(gc["m_last"], (SUBLANES, LANES))


def _prompt_kernel(nt, xa_ref, xc_ref, umeta_ref, cmeta_ref, mmeta_ref, n1_ref, win_ref, wg_ref, bg_ref,
                   wpool_ref, pscale_ref, gain_ref, wout_ref,
                   x1_ref, pool_ref, c_ref, m_ref,
                   uext_ref, xn_ref, p0_ref, p1_ref, g0_ref, g1_ref, mix0_ref, mix1_ref):
    R = PROMPT_TILE
    s = pl.program_id(0)
    tile = jnp.maximum(s - 1, 0)

    @pl.when(s == 0)
    def _():
        p1_ref[...] = jnp.zeros((R, IN_MAIN), F32)
        g1_ref[...] = jnp.zeros((R, GATE_W), F32)
        mix1_ref[...] = jnp.zeros((R, D_MODEL), BF16)

    @pl.when(tile % nt == 0)
    def _():
        c_ref[0] = cmeta_ref[...]
        m_ref[0] = mmeta_ref[...]
        uext_ref[0:CTX_PAD, :] = umeta_ref[...]

    stage = functools.partial(_prompt_step, xa_ref, xc_ref, n1_ref, win_ref, wg_ref, bg_ref, wpool_ref,
                              pscale_ref, gain_ref, wout_ref, x1_ref, pool_ref, c_ref, m_ref, uext_ref, xn_ref)

    @pl.when(s % 2 == 0)
    def _():
        stage(p0_ref, p1_ref, g0_ref, g1_ref, mix0_ref, mix1_ref)

    @pl.when(s % 2 == 1)
    def _():
        stage(p1_ref, p0_ref, g1_ref, g0_ref, mix1_ref, mix0_ref)


def _prompt_step(xa_ref, xc_ref, n1_ref, win_ref, wg_ref, bg_ref, wpool_ref, pscale_ref, gain_ref, wout_ref,
                 x1_ref, pool_ref, c_ref, m_ref, uext_ref, xn_ref,
                 p_new, pb, g_new, g_old, mx, mix_old):
    R = PROMPT_TILE
    gc = _gate_cols(g_old[...], m_ref[0, 0:1, :], R)
    r_t = gc["r"].T

    x1_ref[0] = xc_ref[0] + _dot(mix_old[...], wout_ref[...])

    xn_ref[...] = _rmsnorm(xa_ref[0], n1_ref[...]).astype(BF16)
    g_new[...] = _dot(xn_ref[...], wg_ref[...]) + bg_ref[...]
    for c in range(IN_MAIN // IN_CHUNK):
        cs = slice(c * IN_CHUNK, (c + 1) * IN_CHUNK)
        p_new[:, cs] = _dot(xn_ref[...], win_ref[:, cs])

    uext_ref[CTX_PAD:CTX_PAD + R, :] = pb[:, 0:POOL_WIDTH]
    for gi, w in enumerate(POOL_WINDOWS):
        cs = slice(gi * POOL_GC, (gi + 1) * POOL_GC)
        acc = uext_ref[:, cs]
        sh = 1
        while sh < w:
            acc = acc + pltpu.roll(acc, sh, axis=0)
            sh *= 2
        z = acc[CTX_PAD:, :] * (1.0 / w) - uext_ref[CTX_PAD:CTX_PAD + R, cs]
        mx[:, cs] = (_dot(z.astype(BF16), wpool_ref[gi]) * pscale_ref[:, cs]).astype(BF16)
    tail = uext_ref[R:R + CTX_PAD, :]
    uext_ref[0:CTX_PAD, :] = tail
    pool_ref[0] = tail

    causal = (lax.broadcasted_iota(jnp.int32, (R, R), 0) >= lax.broadcasted_iota(jnp.int32, (R, R), 1))
    for h in range(N_HEADS):
        q = (pb[:, _hcols(COL_Q, h)] * (HEAD_DIM ** -0.5)).astype(BF16)
        k = pb[:, _hcols(COL_K, h)]
        vaug = _value_aug(pb[:, _hcols(COL_V, h)], h)
        sc = _dot_nt(q, k.astype(BF16))
        e = jnp.exp(jnp.where(causal, r_t[h:h + 1, :] - gc["big_m"][:, h:h + 1], NEG_BIG))
        c0 = c_ref[0, h]
        num = gc["inter"][:, h:h + 1] * _dot(q, c0.astype(BF16)) + _dot((sc * e).astype(BF16), vaug)
        inv = 1.0 / jnp.maximum(jnp.abs(num[:, HEAD_DIM:]), gc["eneg"])
        hh = num[:, 0:HEAD_DIM] * inv[:, h:h + 1]
        hh = hh * lax.rsqrt(jnp.mean(hh * hh, axis=-1, keepdims=True) + EPS) * gain_ref[:, _hcols(0, h)]
        mx[:, _hcols(POOL_WIDTH, h)] = (hh * jax.nn.sigmoid(pb[:, _hcols(COL_O, h)])).astype(BF16)
        kw_t = (k * gc["ws"][:, h:h + 1]).T.astype(BF16)
        c_ref[0, h] = gc["decay"][:, h:h + 1] * c0 + _dot(kw_t, vaug)
    m_ref[0] = jnp.broadcast_to(gc["m_last"], (SUBLANES, LANES))


def _sample_gates(g_ref, m0):
    n = m0.shape[0]
    b = jnp.zeros((n, LANES), F32)
    big_m = m0
    r, big_ms, ms = [], [], []
    for t in range(DEC_SEQ):
        g = g_ref[t * n:(t + 1) * n, :]
        b = b + _log_sigmoid(g[:, LANES:])
        r.append(g[:, :LANES] - b)
        big_m = jnp.maximum(big_m, r[-1])
        big_ms.append(big_m)
        ms.append(b + big_m)
    return dict(r=r, big_m=big_ms, m=ms,
                ws=[jnp.exp(r[t] - big_m) for t in range(DEC_SEQ)],
                decay=jnp.exp(m0 - big_m))


def _sample_kernel(xs_ref, ctx_ref, cin_ref, n0_ref, m0_ref, n1_ref, win_ref, wg_ref, bg_ref, wpool_ref,
                   pscale_ref, gain_ref, wout_ref,
                   x1_ref, pool_ref, cout_ref, nout_ref, mout_ref,
                   p_ref, g_ref, tmp_ref, qsm_ref, qcsm_ref, kwt_ref, vsm_ref, dec_ref, mix_ref):
    i = pl.program_id(0)
    n_seq = m0_ref.shape[0]
    n_grp = DEC_SEQ * n_seq // LANES
    scale = HEAD_DIM ** -0.5

    def seq_major(dst_ref, slab_of_t):
        for t in range(DEC_SEQ):
            slab = slab_of_t(t)
            for h in range(N_HEADS):
                dst_ref[h, pl.ds(t, n_seq, stride=DEC_SEQ), :] = slab[:, _hcols(0, h)]

    @pl.when(i == 0)
    def _():
        xn = _rmsnorm(xs_ref[...], n1_ref[...]).astype(BF16)
        g_ref[...] = _dot(xn, wg_ref[...]) + bg_ref[...]
        p_ref[...] = _dot(xn, win_ref[...])

        def urow(e, cs):
            if e < POOL_CTX:
                return ctx_ref[e, :, cs]
            return p_ref[(e - POOL_CTX) * n_seq:(e - POOL_CTX + 1) * n_seq, cs]

        for gi, w in enumerate(POOL_WINDOWS):
            cs = slice(gi * POOL_GC, (gi + 1) * POOL_GC)
            for t in range(DEC_SEQ):
                e = POOL_CTX + t
                wsum = urow(e, cs)
                for j in range(1, w):
                    wsum = wsum + urow(e - j, cs)
                z = wsum * (1.0 / w) - urow(e, cs)
                mix_ref[t * n_seq:(t + 1) * n_seq, cs] = (
                    _dot(z.astype(BF16), wpool_ref[gi]) * pscale_ref[:, cs]).astype(BF16)
        for e in range(DEC_SEQ, POOL_CTX + DEC_SEQ):
            pool_ref[e - DEC_SEQ] = urow(e, slice(0, POOL_WIDTH))

        gs = _sample_gates(g_ref, m0_ref[...])
        dec_ref[...] = gs["decay"]
        mout_ref[...] = gs["m"][DEC_SEQ - 1]
        def kw_slab(t):
            rows = slice(t * n_seq, (t + 1) * n_seq)
            return jnp.concatenate([p_ref[rows, _hcols(COL_K, h)] * gs["ws"][t][:, h:h + 1]
                                    for h in range(N_HEADS)], axis=1)

        kw = [kw_slab(t) for t in range(DEC_SEQ)]
        nout_ref[...] = (jnp.concatenate([jnp.broadcast_to(gs["decay"][:, h:h + 1], (n_seq, HEAD_DIM))
                                          for h in range(N_HEADS)], axis=1) * n0_ref[...]
                         + kw[0] + kw[1] + kw[2] + kw[3])
        seq_major(tmp_ref, lambda t: kw[t])
        for h in range(N_HEADS):
            for gi in range(n_grp):
                kwt_ref[h, gi] = tmp_ref[h, gi * LANES:(gi + 1) * LANES, :].T.astype(BF16)
        seq_major(tmp_ref, lambda t: p_ref[t * n_seq:(t + 1) * n_seq, COL_V:COL_O])
        for h in range(N_HEADS):
            for gi in range(n_grp):
                vsm_ref[h, gi] = tmp_ref[h, gi * LANES:(gi + 1) * LANES, :].astype(BF16)
        seq_major(qsm_ref, lambda t: p_ref[t * n_seq:(t + 1) * n_seq, COL_Q:COL_K] * scale)

    lane_sl = lax.broadcasted_iota(jnp.int32, (HEAD_DIM, LANES), 1) // DEC_SEQ
    first = lax.broadcasted_iota(jnp.int32, (SUBLANES, HEAD_DIM), 0) < DEC_SEQ
    grp = (i * SAMPLE_BLOCK) // (LANES // DEC_SEQ)
    for jp in range(SAMPLE_BLOCK // 2):
        rows = pl.ds(pl.multiple_of(i * (SAMPLE_BLOCK * DEC_SEQ) + jp * SUBLANES, SUBLANES), SUBLANES)
        for h in range(N_HEADS):
            hs = _hcols(0, h)
            q8 = qsm_ref[h, rows, :].astype(BF16)
            readout = []
            for a in range(2):
                j = 2 * jp + a
                seq = i * SAMPLE_BLOCK + j
                c0 = cin_ref[j, h]
                readout.append(_dot(q8, c0.astype(BF16)))
                lhs = jnp.where(lane_sl == seq % (LANES // DEC_SEQ), kwt_ref[h, grp], jnp.zeros((), BF16))
                cout_ref[j, h] = dec_ref[pl.ds(seq, 1), :][:, h:h + 1] * c0 + _dot(lhs, vsm_ref[h, grp])
            qcsm_ref[h, rows, :] = jnp.where(first, readout[0], readout[1])

    @pl.when(i == pl.num_programs(0) - 1)
    def _():
        gs = _sample_gates(g_ref, m0_ref[...])
        for h in range(N_HEADS):
            hs = _hcols(0, h)
            n0 = n0_ref[:, hs]
            qs = [p_ref[t * n_seq:(t + 1) * n_seq, _hcols(COL_Q, h)] * scale for t in range(DEC_SEQ)]
            ks = [p_ref[t * n_seq:(t + 1) * n_seq, _hcols(COL_K, h)] for t in range(DEC_SEQ)]
            for t in range(DEC_SEQ):
                rows = slice(t * n_seq, (t + 1) * n_seq)
                inter = jnp.exp(m0_ref[...] - gs["big_m"][t])[:, h:h + 1]
                num = inter * qcsm_ref[h, pl.ds(t, n_seq, stride=DEC_SEQ), :]
                den = inter * jnp.sum(qs[t] * n0, axis=-1, keepdims=True)
                for s in range(t + 1):
                    pts = (jnp.sum(qs[t] * ks[s], axis=-1, keepdims=True)
                           * jnp.exp(gs["r"][s] - gs["big_m"][t])[:, h:h + 1])
                    num = num + pts * p_ref[s * n_seq:(s + 1) * n_seq, _hcols(COL_V, h)]
                    den = den + pts
                hh = num / jnp.maximum(jnp.abs(den), jnp.exp(-gs["m"][t])[:, h:h + 1])
                hh = hh * lax.rsqrt(jnp.mean(hh * hh, axis=-1, keepdims=True) + EPS) * gain_ref[:, hs]
                mix_ref[rows, _hcols(POOL_WIDTH, h)] = (
                    hh * jax.nn.sigmoid(p_ref[rows, _hcols(COL_O, h)])).astype(BF16)
        x1_ref[...] = xs_ref[...] + _dot(mix_ref[...], wout_ref[...])


def _ffn_kernel(x_ref, n2_ref, wup_ref, wdown_ref, nf_ref, y_ref):
    x = x_ref[...]
    xn = _rmsnorm(x, n2_ref[...]).astype(BF16)
    acc = x
    for c in range(D_FF // FFN_CHUNK):
        cs = slice(c * FFN_CHUNK, (c + 1) * FFN_CHUNK)
        a = jnp.square(jnp.maximum(_dot(xn, wup_ref[:, cs]), 0.0)).astype(BF16)
        acc = acc + _dot(a, wdown_ref[cs, :])
    y_ref[...] = _rmsnorm(acc, nf_ref[...])


def _full(shape):
    return pl.BlockSpec(shape, lambda *_: (0,) * len(shape))


def _ffn(x, n2, wup, wdown, nf):
    rows = x.shape[0]
    return pl.pallas_call(
        _ffn_kernel,
        grid=(rows // FFN_TILE,),
        in_specs=[pl.BlockSpec((FFN_TILE, D_MODEL), lambda i: (i, 0)),
                  _full((1, D_MODEL)), _full((D_MODEL, D_FF)), _full((D_FF, D_MODEL)), _full((1, D_MODEL))],
        out_specs=pl.BlockSpec((FFN_TILE, D_MODEL), lambda i: (i, 0)),
        out_shape=jax.ShapeDtypeStruct((rows, D_MODEL), F32),
        compiler_params=pltpu.CompilerParams(dimension_semantics=("arbitrary",), vmem_limit_bytes=VMEM_LIMIT),
        name="ffn",
    )(x, n2, wup, wdown, nf)


def kernel(x_prompt, x_sample, state_pool, state_C, state_n, state_m, meta_tokens, norm1, w_in, b_gate,
           w_pool, pool_scale, head_gain, w_out, norm2, w_up, w_down, norm_f):
    B, S, _ = x_prompt.shape
    n_seq = x_sample.shape[0]
    assert w_in.shape[0] == 1 and S % PROMPT_TILE == 0 and n_seq % SAMPLE_BLOCK == 0

    win = w_in[0][:, :IN_MAIN].astype(BF16)
    gpad = ((0, 0), (0, LANES - N_HEADS))
    wg = jnp.concatenate([jnp.pad(w_in[0][:, IN_MAIN:IN_MAIN + N_HEADS], gpad),
                          jnp.pad(w_in[0][:, IN_MAIN + N_HEADS:], gpad)], axis=1).astype(BF16)
    bg = jnp.concatenate([jnp.pad(b_gate[0][:N_HEADS], (0, LANES - N_HEADS)),
                          jnp.pad(b_gate[0][N_HEADS:], (0, LANES - N_HEADS))]).reshape(1, GATE_W)
    wpool = w_pool[0].astype(BF16)
    wout = w_out[0].astype(BF16)
    wup = w_up[0].astype(BF16)
    wdown = w_down[0].astype(BF16)
    n1 = norm1[0].reshape(1, D_MODEL)
    n2 = norm2[0].reshape(1, D_MODEL)
    nf = norm_f.reshape(1, D_MODEL)
    pscale = pool_scale[0].reshape(1, POOL_WIDTH)
    gain = head_gain[0].reshape(1, MLSTM_WIDTH)

    weight_specs = [_full((1, D_MODEL)), _full((D_MODEL, IN_MAIN)), _full((D_MODEL, GATE_W)), _full((1, GATE_W))]
    mixer_specs = [_full((len(POOL_WINDOWS), POOL_GC, POOL_GC)), _full((1, POOL_WIDTH)),
                   _full((1, MLSTM_WIDTH)), _full((D_MODEL, D_MODEL))]

    meta_x = jnp.pad(meta_tokens, ((0, LANES - N_META), (0, 0)))
    u_meta, c_meta, m_meta = pl.pallas_call(
        _meta_kernel,
        out_shape=(jax.ShapeDtypeStruct((N_META, POOL_WIDTH), F32),
                   jax.ShapeDtypeStruct((N_HEADS, HEAD_DIM, AUG), F32),
                   jax.ShapeDtypeStruct((SUBLANES, LANES), F32)),
        compiler_params=pltpu.CompilerParams(vmem_limit_bytes=VMEM_LIMIT),
        name="meta_prefix",
    )(meta_x, n1, win, wg, bg)

    nt = S // PROMPT_TILE
    n_tiles = B * nt

    def tile_at(lag):
        def index(s):
            j = jnp.clip(s - lag, 0, n_tiles - 1)
            return (j // nt, j % nt, 0)
        return index

    def mixed_seq(ndim):
        return lambda s: (jnp.maximum(s - 1, 0) // nt,) + (0,) * (ndim - 1)

    x1_p, pool_p, c_p, m_p = pl.pallas_call(
        functools.partial(_prompt_kernel, nt),
        grid=(n_tiles + 2,),
        in_specs=[pl.BlockSpec((1, PROMPT_TILE, D_MODEL), tile_at(0)),
                  pl.BlockSpec((1, PROMPT_TILE, D_MODEL), tile_at(2)),
                  _full((N_META, POOL_WIDTH)), _full((N_HEADS, HEAD_DIM, AUG)), _full((SUBLANES, LANES))]
        + weight_specs + mixer_specs,
        out_specs=[pl.BlockSpec((1, PROMPT_TILE, D_MODEL), tile_at(2)),
                   pl.BlockSpec((1, CTX_PAD, POOL_WIDTH), mixed_seq(3)),
                   pl.BlockSpec((1, N_HEADS, HEAD_DIM, AUG), mixed_seq(4)),
                   pl.BlockSpec((1, SUBLANES, LANES), mixed_seq(3))],
        out_shape=(jax.ShapeDtypeStruct((B, S, D_MODEL), F32),
                   jax.ShapeDtypeStruct((B + 1, CTX_PAD, POOL_WIDTH), F32),
                   jax.ShapeDtypeStruct((B + 1, N_HEADS, HEAD_DIM, AUG), F32),
                   jax.ShapeDtypeStruct((B + 1, SUBLANES, LANES), F32)),
        scratch_shapes=[pltpu.VMEM((CTX_PAD + PROMPT_TILE, POOL_WIDTH), F32),
                        pltpu.VMEM((PROMPT_TILE, D_MODEL), BF16),
                        pltpu.VMEM((PROMPT_TILE, IN_MAIN), F32), pltpu.VMEM((PROMPT_TILE, IN_MAIN), F32),
                        pltpu.VMEM((PROMPT_TILE, GATE_W), F32), pltpu.VMEM((PROMPT_TILE, GATE_W), F32),
                        pltpu.VMEM((PROMPT_TILE, D_MODEL), BF16), pltpu.VMEM((PROMPT_TILE, D_MODEL), BF16)],
        compiler_params=pltpu.CompilerParams(dimension_semantics=("arbitrary",), vmem_limit_bytes=VMEM_LIMIT),
        name="prompt_mixer",
    )(x_prompt, x_prompt, u_meta, c_meta, m_meta, n1, win, wg, bg, wpool, pscale, gain, wout)
    pool_p, c_p, m_p = pool_p[:B], c_p[:B], m_p[:B]

    rows_s = DEC_SEQ * n_seq
    xs_t = x_sample.transpose(1, 0, 2).reshape(rows_s, D_MODEL)
    ctx_t = state_pool[0].transpose(1, 0, 2)
    n0 = state_n[0].reshape(n_seq, MLSTM_WIDTH)
    m0 = jnp.pad(state_m[0], ((0, 0), (0, LANES - N_HEADS)))
    state_block = pl.BlockSpec((SAMPLE_BLOCK, N_HEADS, HEAD_DIM, HEAD_DIM), lambda i: (i, 0, 0, 0))
    x1_s, pool_s, c_s, n_s, m_s = pl.pallas_call(
        _sample_kernel,
        grid=(n_seq // SAMPLE_BLOCK,),
        in_specs=[_full((rows_s, D_MODEL)), _full((POOL_CTX, n_seq, POOL_WIDTH)), state_block,
                  _full((n_seq, MLSTM_WIDTH)), _full((n_seq, LANES))] + weight_specs + mixer_specs,
        out_specs=[_full((rows_s, D_MODEL)), _full((POOL_CTX, n_seq, POOL_WIDTH)), state_block,
                   _full((n_seq, MLSTM_WIDTH)), _full((n_seq, LANES))],
        out_shape=(jax.ShapeDtypeStruct((rows_s, D_MODEL), F32),
                   jax.ShapeDtypeStruct((POOL_CTX, n_seq, POOL_WIDTH), F32),
                   jax.ShapeDtypeStruct((n_seq, N_HEADS, HEAD_DIM, HEAD_DIM), F32),
                   jax.ShapeDtypeStruct((n_seq, MLSTM_WIDTH), F32),
                   jax.ShapeDtypeStruct((n_seq, LANES), F32)),
        scratch_shapes=[pltpu.VMEM((rows_s, IN_MAIN), F32),
                        pltpu.VMEM((rows_s, GATE_W), F32),
                        pltpu.VMEM((N_HEADS, rows_s, HEAD_DIM), F32),
                        pltpu.VMEM((N_HEADS, rows_s, HEAD_DIM), F32),
                        pltpu.VMEM((N_HEADS, rows_s, HEAD_DIM), F32),
                        pltpu.VMEM((N_HEADS, rows_s // LANES, HEAD_DIM, LANES), BF16),
                        pltpu.VMEM((N_HEADS, rows_s // LANES, LANES, HEAD_DIM), BF16),
                        pltpu.VMEM((n_seq, LANES), F32),
                        pltpu.VMEM((rows_s, D_MODEL), BF16)],
        compiler_params=pltpu.CompilerParams(dimension_semantics=("arbitrary",), vmem_limit_bytes=VMEM_LIMIT),
        name="sample_mixer",
    )(xs_t, ctx_t, state_C[0], n0, m0, n1, win, wg, bg, wpool, pscale, gain, wout)

    y_p = _ffn(x1_p.reshape(B * S, D_MODEL), n2, wup, wdown, nf).reshape(B, S, D_MODEL)
    y_s = _ffn(x1_s, n2, wup, wdown, nf).reshape(DEC_SEQ, n_seq, D_MODEL).transpose(1, 0, 2)

    n_p = jnp.stack([c_p[:, h, :, HEAD_DIM + h] for h in range(N_HEADS)], axis=1)
    return (y_p, y_s,
            pool_p[:, CTX_PAD - POOL_CTX:, :][None],
            c_p[..., :HEAD_DIM][None],
            n_p[None],
            m_p[:, 0, :N_HEADS][None],
            pool_s.transpose(1, 0, 2)[None],
            c_s[None],
            n_s.reshape(n_seq, N_HEADS, HEAD_DIM)[None],
            m_s[:, :N_HEADS][None])
```

```python
import functools

import jax
import jax.numpy as jnp
from jax import lax
from jax.experimental import pallas as pl
from jax.experimental.pallas import tpu as pltpu

D_MODEL = 1024
N_META = 16
POOL_WIDTH = 512
POOL_WINDOWS = (2, 4, 8, 16)
POOL_GC = 128
POOL_CTX = 15
MLSTM_WIDTH = 512
N_HEADS = 4
HEAD_DIM = 128
D_FF = 4096
EPS = 1e-6
DEC_SEQ = 4

LANES = 128
SUBLANES = 8
COL_Q = POOL_WIDTH
COL_K = COL_Q + MLSTM_WIDTH
COL_V = COL_K + MLSTM_WIDTH
COL_O = COL_V + MLSTM_WIDTH
IN_MAIN = COL_O + MLSTM_WIDTH
GATE_W = 2 * LANES
AUG = 2 * HEAD_DIM
CTX_PAD = 16
PROMPT_TILE = 256
IN_CHUNK = 512
FFN_TILE = 512
FFN_CHUNK = 1024
SAMPLE_BLOCK = 8
NEG_BIG = -1e30
VMEM_LIMIT = 56 * 1024 * 1024

F32 = jnp.float32
BF16 = jnp.bfloat16


def _rmsnorm(x, g):
    return x * lax.rsqrt(jnp.mean(x * x, axis=-1, keepdims=True) + EPS) * g


def _log_sigmoid(x):
    return jnp.minimum(x, 0.0) - jnp.log1p(jnp.exp(-jnp.abs(x)))


def _dot(a, b):
    return jnp.dot(a, b, preferred_element_type=F32)


def _dot_nt(a, b):
    return lax.dot_general(a, b, (((1,), (1,)), ((), ())), preferred_element_type=F32)


def _hcols(base, h):
    return slice(base + h * HEAD_DIM, base + (h + 1) * HEAD_DIM)


def _row_scan(x, op):
    n = x.shape[0]
    row = lax.broadcasted_iota(jnp.int32, x.shape, 0)
    sh = 1
    while sh < n:
        x = jnp.where(row >= sh, op(x, pltpu.roll(x, sh, axis=0)), x)
        sh *= 2
    return x


def _gate_cols(g, m0, n_valid):
    R = g.shape[0]
    b = _row_scan(_log_sigmoid(g[:, LANES:]), jnp.add)
    r = g[:, :LANES] - b
    big_m = jnp.maximum(m0, _row_scan(r, jnp.maximum))
    m = b + big_m
    last = n_valid - 1
    m_top = big_m[last:last + 1, :]
    ws = jnp.exp(r - m_top)
    if n_valid < R:
        ws = jnp.where(lax.broadcasted_iota(jnp.int32, ws.shape, 0) < n_valid, ws, 0.0)
    return dict(r=r, big_m=big_m, inter=jnp.exp(m0 - big_m), eneg=jnp.exp(-m), ws=ws,
                decay=jnp.exp(m0 - m_top), m_last=m[last:last + 1, :])


def _value_aug(v, h):
    lane = lax.broadcasted_iota(jnp.int32, (v.shape[0], HEAD_DIM), 1)
    return jnp.concatenate([v, jnp.where(lane == h, 1.0, 0.0)], axis=1).astype(BF16)


def _meta_kernel(x_ref, n1_ref, win_ref, wg_ref, bg_ref, u_ref, c_ref, m_ref):
    xn = _rmsnorm(x_ref[...], n1_ref[...]).astype(BF16)
    gc = _gate_cols(_dot(xn, wg_ref[...]) + bg_ref[...], jnp.zeros((1, LANES), F32), N_META)
    p = _dot(xn, win_ref[:, 0:IN_MAIN])
    u_ref[...] = p[0:N_META, 0:POOL_WIDTH]
    for h in range(N_HEADS):
        kw_t = (p[:, _hcols(COL_K, h)] * gc["ws"][:, h:h + 1]).T.astype(BF16)
        c_ref[h] = _dot(kw_t, _value_aug(p[:, _hcols(COL_V, h)], h))
    m_ref[...] = jnp.broadcast_to(gc["m_last"], (SUBLANES, LANES))


def _prompt_kernel(nt, xa_ref, xc_ref, umeta_ref, cmeta_ref, mmeta_ref, n1_ref, win_ref, wg_ref, bg_ref,
                   wpool_ref, pscale_ref, gain_ref, wout_ref,
                   x1_ref, pool_ref, c_ref, m_ref,
                   uext_ref, xn_ref, p0_ref, p1_ref, g0_ref, g1_ref, mix0_ref, mix1_ref):
    s = pl.program_id(0)
    n_tiles = pl.num_programs(0) - 2

    @pl.when((jnp.maximum(s - 1, 0) % nt == 0) & (s <= n_tiles))
    def _():
        c_ref[0] = cmeta_ref[...]
        m_ref[0] = mmeta_ref[...]
        uext_ref[0:CTX_PAD, :] = umeta_ref[...]

    stage = functools.partial(_prompt_step, xa_ref, xc_ref, n1_ref, win_ref, wg_ref, bg_ref, wpool_ref,
                              pscale_ref, gain_ref, wout_ref, x1_ref, pool_ref, c_ref, m_ref, uext_ref, xn_ref)
    even = (p0_ref, p1_ref, g0_ref, g1_ref, mix0_ref, mix1_ref)
    odd = (p1_ref, p0_ref, g1_ref, g0_ref, mix1_ref, mix0_ref)
    steady = (s >= 2) & (s < n_tiles)
    pl.when(s == 0)(lambda: stage(*even, project=True, mix=False, emit=False))
    pl.when(s == 1)(lambda: stage(*odd, project=True, mix=True, emit=False))
    pl.when(steady & (s % 2 == 0))(lambda: stage(*even, project=True, mix=True, emit=True))
    pl.when(steady & (s % 2 == 1))(lambda: stage(*odd, project=True, mix=True, emit=True))
    pl.when(s == n_tiles)(lambda: stage(*even, project=False, mix=True, emit=True))
    pl.when(s == n_tiles + 1)(lambda: stage(*odd, project=False, mix=False, emit=True))


def _prompt_step(xa_ref, xc_ref, n1_ref, win_ref, wg_ref, bg_ref, wpool_ref, pscale_ref, gain_ref, wout_ref,
                 x1_ref, pool_ref, c_ref, m_ref, uext_ref, xn_ref,
                 p_new, pb, g_new, g_old, mx, mix_old, *, project, mix, emit):
    R = PROMPT_TILE
    if mix:
        gc = _gate_cols(g_old[...], m_ref[0, 0:1, :], R)
        r_t = gc["r"].T

    if emit:
        x1_ref[0] = xc_ref[0] + _dot(mix_old[...], wout_ref[...])

    if project:
        xn_ref[...] = _rmsnorm(xa_ref[0], n1_ref[...]).astype(BF16)
        g_new[...] = _dot(xn_ref[...], wg_ref[...]) + bg_ref[...]
        for c in range(IN_MAIN // IN_CHUNK):
            cs = slice(c * IN_CHUNK, (c + 1) * IN_CHUNK)
            p_new[:, cs] = _dot(xn_ref[...], win_ref[:, cs])
    if not mix:
        return

    uext_ref[CTX_PAD:CTX_PAD + R, :] = pb[:, 0:POOL_WIDTH]
    for gi, w in enumerate(POOL_WINDOWS):
        cs = slice(gi * POOL_GC, (gi + 1) * POOL_GC)
        acc = uext_ref[:, cs]
        sh = 1
        while sh < w:
            acc = acc + pltpu.roll(acc, sh, axis=0)
            sh *= 2
        z = acc[CTX_PAD:, :] * (1.0 / w) - uext_ref[CTX_PAD:CTX_PAD + R, cs]
        mx[:, cs] = (_dot(z.astype(BF16), wpool_ref[gi]) * pscale_ref[:, cs]).astype(BF16)
    tail = uext_ref[R:R + CTX_PAD, :]
    uext_ref[0:CTX_PAD, :] = tail
    pool_ref[0] = tail

    causal = (lax.broadcasted_iota(jnp.int32, (R, R), 0) >= lax.broadcasted_iota(jnp.int32, (R, R), 1))
    for h in range(N_HEADS):
        q = (pb[:, _hcols(COL_Q, h)] * (HEAD_DIM ** -0.5)).astype(BF16)
        k = pb[:, _hcols(COL_K, h)]
        vaug = _value_aug(pb[:, _hcols(COL_V, h)], h)
        sc = _dot_nt(q, k.astype(BF16))
        e = jnp.exp(jnp.where(causal, r_t[h:h + 1, :] - gc["big_m"][:, h:h + 1], NEG_BIG))
        c0 = c_ref[0, h]
        num = gc["inter"][:, h:h + 1] * _dot(q, c0.astype(BF16)) + _dot((sc * e).astype(BF16), vaug)
        inv = 1.0 / jnp.maximum(jnp.abs(num[:, HEAD_DIM:]), gc["eneg"])
        hh = num[:, 0:HEAD_DIM] * inv[:, h:h + 1]
        hh = hh * lax.rsqrt(jnp.mean(hh * hh, axis=-1, keepdims=True) + EPS) * gain_ref[:, _hcols(0, h)]
        mx[:, _hcols(POOL_WIDTH, h)] = (hh * jax.nn.sigmoid(pb[:, _hcols(COL_O, h)])).astype(BF16)
        kw_t = (k * gc["ws"][:, h:h + 1]).T.astype(BF16)
        c_ref[0, h] = gc["decay"][:, h:h + 1] * c0 + _dot(kw_t, vaug)
    m_ref[0] = jnp.broadcast_to(gc["m_last"], (SUBLANES, LANES))


def _sample_gates(g_ref, m0):
    n = m0.shape[0]
    b = jnp.zeros((n, LANES), F32)
    big_m = m0
    r, big_ms, ms = [], [], []
    for t in range(DEC_SEQ):
        g = g_ref[t * n:(t + 1) * n, :]
        b = b + _log_sigmoid(g[:, LANES:])
        r.append(g[:, :LANES] - b)
        big_m = jnp.maximum(big_m, r[-1])
        big_ms.append(big_m)
        ms.append(b + big_m)
    return dict(r=r, big_m=big_ms, m=ms,
                ws=[jnp.exp(r[t] - big_m) for t in range(DEC_SEQ)],
                decay=jnp.exp(m0 - big_m))


def _sample_kernel(xs_ref, ctx_ref, cin_ref, n0_ref, m0_ref, n1_ref, win_ref, wg_ref, bg_ref, wpool_ref,
                   pscale_ref, gain_ref, wout_ref,
                   x1_ref, pool_ref, cout_ref, nout_ref, mout_ref,
                   p_ref, g_ref, tmp_ref, qsm_ref, qcsm_ref, kwt_ref, vsm_ref, dec_ref, mix_ref):
    i = pl.program_id(0)
    n_seq = m0_ref.shape[0]
    n_grp = DEC_SEQ * n_seq // LANES
    scale = HEAD_DIM ** -0.5

    def seq_major(dst_ref, slab_of_t):
        for t in range(DEC_SEQ):
            slab = slab_of_t(t)
            for h in range(N_HEADS):
                dst_ref[h, pl.ds(t, n_seq, stride=DEC_SEQ), :] = slab[:, _hcols(0, h)]

    @pl.when(i == 0)
    def _():
        xn = _rmsnorm(xs_ref[...], n1_ref[...]).astype(BF16)
        g_ref[...] = _dot(xn, wg_ref[...]) + bg_ref[...]
        p_ref[...] = _dot(xn, win_ref[...])

        def urow(e, cs):
            if e < POOL_CTX:
                return ctx_ref[e, :, cs]
            return p_ref[(e - POOL_CTX) * n_seq:(e - POOL_CTX + 1) * n_seq, cs]

        for gi, w in enumerate(POOL_WINDOWS):
            cs = slice(gi * POOL_GC, (gi + 1) * POOL_GC)
            for t in range(DEC_SEQ):
                e = POOL_CTX + t
                wsum = urow(e, cs)
                for j in range(1, w):
                    wsum = wsum + urow(e - j, cs)
                z = wsum * (1.0 / w) - urow(e, cs)
                mix_ref[t * n_seq:(t + 1) * n_seq, cs] = (
                    _dot(z.astype(BF16), wpool_ref[gi]) * pscale_ref[:, cs]).astype(BF16)
        for e in range(DEC_SEQ, POOL_CTX + DEC_SEQ):
            pool_ref[e - DEC_SEQ] = urow(e, slice(0, POOL_WIDTH))

        gs = _sample_gates(g_ref, m0_ref[...])
        dec_ref[...] = gs["decay"]
        mout_ref[...] = gs["m"][DEC_SEQ - 1]
        def kw_slab(t):
            rows = slice(t * n_seq, (t + 1) * n_seq)
            return jnp.concatenate([p_ref[rows, _hcols(COL_K, h)] * gs["ws"][t][:, h:h + 1]
                                    for h in range(N_HEADS)], axis=1)

        kw = [kw_slab(t) for t in range(DEC_SEQ)]
        nout_ref[...] = (jnp.concatenate([jnp.broadcast_to(gs["decay"][:, h:h + 1], (n_seq, HEAD_DIM))
                                          for h in range(N_HEADS)], axis=1) * n0_ref[...]
                         + kw[0] + kw[1] + kw[2] + kw[3])
        seq_major(tmp_ref, lambda t: kw[t])
        for h in range(N_HEADS):
            for gi in range(n_grp):
                kwt_ref[h, gi] = tmp_ref[h, gi * LANES:(gi + 1) * LANES, :].T.astype(BF16)
        seq_major(tmp_ref, lambda t: p_ref[t * n_seq:(t + 1) * n_seq, COL_V:COL_O])
        for h in range(N_HEADS):
            for gi in range(n_grp):
                vsm_ref[h, gi] = tmp_ref[h, gi * LANES:(gi + 1) * LANES, :].astype(BF16)
        seq_major(qsm_ref, lambda t: p_ref[t * n_seq:(t + 1) * n_seq, COL_Q:COL_K] * scale)

    lane_sl = lax.broadcasted_iota(jnp.int32, (HEAD_DIM, LANES), 1) // DEC_SEQ
    first = lax.broadcasted_iota(jnp.int32, (SUBLANES, HEAD_DIM), 0) < DEC_SEQ
    grp = (i * SAMPLE_BLOCK) // (LANES // DEC_SEQ)
    for jp in range(SAMPLE_BLOCK // 2):
        rows = pl.ds(pl.multiple_of(i * (SAMPLE_BLOCK * DEC_SEQ) + jp * SUBLANES, SUBLANES), SUBLANES)
        for h in range(N_HEADS):
            hs = _hcols(0, h)
            q8 = qsm_ref[h, rows, :].astype(BF16)
            readout = []
            for a in range(2):
                j = 2 * jp + a
                seq = i * SAMPLE_BLOCK + j
                c0 = cin_ref[j, h]
                readout.append(_dot(q8, c0.astype(BF16)))
                lhs = jnp.where(lane_sl == seq % (LANES // DEC_SEQ), kwt_ref[h, grp], jnp.zeros((), BF16))
                cout_ref[j, h] = dec_ref[pl.ds(seq, 1), :][:, h:h + 1] * c0 + _dot(lhs, vsm_ref[h, grp])
            qcsm_ref[h, rows, :] = jnp.where(first, readout[0], readout[1])

    @pl.when(i == pl.num_programs(0) - 1)
    def _():
        gs = _sample_gates(g_ref, m0_ref[...])
        for h in range(N_HEADS):
            hs = _hcols(0, h)
            n0 = n0_ref[:, hs]
            qs = [p_ref[t * n_seq:(t + 1) * n_seq, _hcols(COL_Q, h)] * scale for t in range(DEC_SEQ)]
            ks = [p_ref[t * n_seq:(t + 1) * n_seq, _hcols(COL_K, h)] for t in range(DEC_SEQ)]
            for t in range(DEC_SEQ):
                rows = slice(t * n_seq, (t + 1) * n_seq)
                inter = jnp.exp(m0_ref[...] - gs["big_m"][t])[:, h:h + 1]
                num = inter * qcsm_ref[h, pl.ds(t, n_seq, stride=DEC_SEQ), :]
                den = inter * jnp.sum(qs[t] * n0, axis=-1, keepdims=True)
                for s in range(t + 1):
                    pts = (jnp.sum(qs[t] * ks[s], axis=-1, keepdims=True)
                           * jnp.exp(gs["r"][s] - gs["big_m"][t])[:, h:h + 1])
                    num = num + pts * p_ref[s * n_seq:(s + 1) * n_seq, _hcols(COL_V, h)]
                    den = den + pts
                hh = num / jnp.maximum(jnp.abs(den), jnp.exp(-gs["m"][t])[:, h:h + 1])
                hh = hh * lax.rsqrt(jnp.mean(hh * hh, axis=-1, keepdims=True) + EPS) * gain_ref[:, hs]
                mix_ref[rows, _hcols(POOL_WIDTH, h)] = (
                    hh * jax.nn.sigmoid(p_ref[rows, _hcols(COL_O, h)])).astype(BF16)
        x1_ref[...] = xs_ref[...] + _dot(mix_ref[...], wout_ref[...])


def _ffn_kernel(x_ref, n2_ref, wup_ref, wdown_ref, nf_ref, y_ref):
    x = x_ref[...]
    xn = _rmsnorm(x, n2_ref[...]).astype(BF16)
    acc = x
    for c in range(D_FF // FFN_CHUNK):
        cs = slice(c * FFN_CHUNK, (c + 1) * FFN_CHUNK)
        a = jnp.square(jnp.maximum(_dot(xn, wup_ref[:, cs]), 0.0)).astype(BF16)
        acc = acc + _dot(a, wdown_ref[cs, :])
    y_ref[...] = _rmsnorm(acc, nf_ref[...])


def _full(shape):
    return pl.BlockSpec(shape, lambda *_: (0,) * len(shape))


def _ffn(x, n2, wup, wdown, nf):
    rows = x.shape[0]
    return pl.pallas_call(
        _ffn_kernel,
        grid=(rows // FFN_TILE,),
        in_specs=[pl.BlockSpec((FFN_TILE, D_MODEL), lambda i: (i, 0)),
                  _full((1, D_MODEL)), _full((D_MODEL, D_FF)), _full((D_FF, D_MODEL)), _full((1, D_MODEL))],
        out_specs=pl.BlockSpec((FFN_TILE, D_MODEL), lambda i: (i, 0)),
        out_shape=jax.ShapeDtypeStruct((rows, D_MODEL), F32),
        compiler_params=pltpu.CompilerParams(dimension_semantics=("arbitrary",), vmem_limit_bytes=VMEM_LIMIT),
        name="ffn",
    )(x, n2, wup, wdown, nf)


def kernel(x_prompt, x_sample, state_pool, state_C, state_n, state_m, meta_tokens, norm1, w_in, b_gate,
           w_pool, pool_scale, head_gain, w_out, norm2, w_up, w_down, norm_f):
    B, S, _ = x_prompt.shape
    n_seq = x_sample.shape[0]
    assert w_in.shape[0] == 1 and S % PROMPT_TILE == 0 and n_seq % SAMPLE_BLOCK == 0

    win = w_in[0].astype(BF16)
    gpad = ((0, 0), (0, LANES - N_HEADS))
    wg = jnp.concatenate([jnp.pad(w_in[0][:, IN_MAIN:IN_MAIN + N_HEADS], gpad),
                          jnp.pad(w_in[0][:, IN_MAIN + N_HEADS:], gpad)], axis=1).astype(BF16)
    bg = jnp.concatenate([jnp.pad(b_gate[0][:N_HEADS], (0, LANES - N_HEADS)),
                          jnp.pad(b_gate[0][N_HEADS:], (0, LANES - N_HEADS))]).reshape(1, GATE_W)
    wpool = w_pool[0].astype(BF16)
    wout = w_out[0].astype(BF16)
    wup = w_up[0].astype(BF16)
    wdown = w_down[0].astype(BF16)
    n1 = norm1[0].reshape(1, D_MODEL)
    n2 = norm2[0].reshape(1, D_MODEL)
    nf = norm_f.reshape(1, D_MODEL)
    pscale = pool_scale[0].reshape(1, POOL_WIDTH)
    gain = head_gain[0].reshape(1, MLSTM_WIDTH)

    weight_specs = [_full((1, D_MODEL)), _full((D_MODEL, IN_MAIN)), _full((D_MODEL, GATE_W)), _full((1, GATE_W))]
    mixer_specs = [_full((len(POOL_WINDOWS), POOL_GC, POOL_GC)), _full((1, POOL_WIDTH)),
                   _full((1, MLSTM_WIDTH)), _full((D_MODEL, D_MODEL))]

    meta_x = jnp.pad(meta_tokens, ((0, LANES - N_META), (0, 0)))
    u_meta, c_meta, m_meta = pl.pallas_call(
        _meta_kernel,
        out_shape=(jax.ShapeDtypeStruct((N_META, POOL_WIDTH), F32),
                   jax.ShapeDtypeStruct((N_HEADS, HEAD_DIM, AUG), F32),
                   jax.ShapeDtypeStruct((SUBLANES, LANES), F32)),
        compiler_params=pltpu.CompilerParams(vmem_limit_bytes=VMEM_LIMIT),
        name="meta_prefix",
    )(meta_x, n1, win, wg, bg)

    nt = S // PROMPT_TILE
    n_tiles = B * nt
    assert n_tiles % 2 == 0 and n_tiles >= 2

    def tile_at(lag):
        def index(s):
            j = jnp.clip(s - lag, 0, n_tiles - 1)
            return (j // nt, j % nt, 0)
        return index

    def mixed_seq(ndim):
        return lambda s: (jnp.clip(s - 1, 0, n_tiles - 1) // nt,) + (0,) * (ndim - 1)

    x1_p, pool_p, c_p, m_p = pl.pallas_call(
        functools.partial(_prompt_kernel, nt),
        grid=(n_tiles + 2,),
        in_specs=[pl.BlockSpec((1, PROMPT_TILE, D_MODEL), tile_at(0)),
                  pl.BlockSpec((1, PROMPT_TILE, D_MODEL), tile_at(2)),
                  _full((N_META, POOL_WIDTH)), _full((N_HEADS, HEAD_DIM, AUG)), _full((SUBLANES, LANES))]
        + weight_specs + mixer_specs,
        out_specs=[pl.BlockSpec((1, PROMPT_TILE, D_MODEL), tile_at(2)),
                   pl.BlockSpec((1, CTX_PAD, POOL_WIDTH), mixed_seq(3)),
                   pl.BlockSpec((1, N_HEADS, HEAD_DIM, AUG), mixed_seq(4)),
                   pl.BlockSpec((1, SUBLANES, LANES), mixed_seq(3))],
        out_shape=(jax.ShapeDtypeStruct((B, S, D_MODEL), F32),
                   jax.ShapeDtypeStruct((B, CTX_PAD, POOL_WIDTH), F32),
                   jax.ShapeDtypeStruct((B, N_HEADS, HEAD_DIM, AUG), F32),
                   jax.ShapeDtypeStruct((B, SUBLANES, LANES), F32)),
        scratch_shapes=[pltpu.VMEM((CTX_PAD + PROMPT_TILE, POOL_WIDTH), F32),
                        pltpu.VMEM((PROMPT_TILE, D_MODEL), BF16),
                        pltpu.VMEM((PROMPT_TILE, IN_MAIN), F32), pltpu.VMEM((PROMPT_TILE, IN_MAIN), F32),
                        pltpu.VMEM((PROMPT_TILE, GATE_W), F32), pltpu.VMEM((PROMPT_TILE, GATE_W), F32),
                        pltpu.VMEM((PROMPT_TILE, D_MODEL), BF16), pltpu.VMEM((PROMPT_TILE, D_MODEL), BF16)],
        compiler_params=pltpu.CompilerParams(dimension_semantics=("arbitrary",), vmem_limit_bytes=VMEM_LIMIT),
        name="prompt_mixer",
    )(x_prompt, x_prompt, u_meta, c_meta, m_meta, n1, win, wg, bg, wpool, pscale, gain, wout)

    rows_s = DEC_SEQ * n_seq
    xs_t = x_sample.transpose(1, 0, 2).reshape(rows_s, D_MODEL)
    ctx_t = state_pool[0].transpose(1, 0, 2)
    n0 = state_n[0].reshape(n_seq, MLSTM_WIDTH)
    m0 = jnp.pad(state_m[0], ((0, 0), (0, LANES - N_HEADS)))
    state_block = pl.BlockSpec((SAMPLE_BLOCK, N_HEADS, HEAD_DIM, HEAD_DIM), lambda i: (i, 0, 0, 0))
    x1_s, pool_s, c_s, n_s, m_s = pl.pallas_call(
        _sample_kernel,
        grid=(n_seq // SAMPLE_BLOCK,),
        in_specs=[_full((rows_s, D_MODEL)), _full((POOL_CTX, n_seq, POOL_WIDTH)), state_block,
                  _full((n_seq, MLSTM_WIDTH)), _full((n_seq, LANES))] + weight_specs + mixer_specs,
        out_specs=[_full((rows_s, D_MODEL)), _full((POOL_CTX, n_seq, POOL_WIDTH)), state_block,
                   _full((n_seq, MLSTM_WIDTH)), _full((n_seq, LANES))],
        out_shape=(jax.ShapeDtypeStruct((rows_s, D_MODEL), F32),
                   jax.ShapeDtypeStruct((POOL_CTX, n_seq, POOL_WIDTH), F32),
                   jax.ShapeDtypeStruct((n_seq, N_HEADS, HEAD_DIM, HEAD_DIM), F32),
                   jax.ShapeDtypeStruct((n_seq, MLSTM_WIDTH), F32),
                   jax.ShapeDtypeStruct((n_seq, LANES), F32)),
        scratch_shapes=[pltpu.VMEM((rows_s, IN_MAIN), F32),
                        pltpu.VMEM((rows_s, GATE_W), F32),
                        pltpu.VMEM((N_HEADS, rows_s, HEAD_DIM), F32),
                        pltpu.VMEM((N_HEADS, rows_s, HEAD_DIM), F32),
                        pltpu.VMEM((N_HEADS, rows_s, HEAD_DIM), F32),
                        pltpu.VMEM((N_HEADS, rows_s // LANES, HEAD_DIM, LANES), BF16),
                        pltpu.VMEM((N_HEADS, rows_s // LANES, LANES, HEAD_DIM), BF16),
                        pltpu.VMEM((n_seq, LANES), F32),
                        pltpu.VMEM((rows_s, D_MODEL), BF16)],
        compiler_params=pltpu.CompilerParams(dimension_semantics=("arbitrary",), vmem_limit_bytes=VMEM_LIMIT),
        name="sample_mixer",
    )(xs_t, ctx_t, state_C[0], n0, m0, n1, win, wg, bg, wpool, pscale, gain, wout)

    y_p = _ffn(x1_p.reshape(B * S, D_MODEL), n2, wup, wdown, nf).reshape(B, S, D_MODEL)
    y_s = _ffn(x1_s, n2, wup, wdown, nf).reshape(DEC_SEQ, n_seq, D_MODEL).transpose(1, 0, 2)

    n_p = jnp.stack([c_p[:, h, :, HEAD_DIM + h] for h in range(N_HEADS)], axis=1)
    return (y_p, y_s,
            pool_p[:, CTX_PAD - POOL_CTX:, :][None],
            c_p[..., :HEAD_DIM][None],
            n_p[None],
            m_p[:, 0, :N_HEADS][None],
            pool_s.transpose(1, 0, 2)[None],
            c_s[None],
            n_s.reshape(n_seq, N_HEADS, HEAD_DIM)[None],
            m_s[:, :N_HEADS][None])
```

```python
import functools

import jax
import jax.numpy as jnp
from jax import lax
from jax.experimental import pallas as pl
from jax.experimental.pallas import tpu as pltpu

D_MODEL = 1024
N_META = 16
POOL_WIDTH = 512
POOL_WINDOWS = (2, 4, 8, 16)
POOL_GC = 128
POOL_CTX = 15
MLSTM_WIDTH = 512
N_HEADS = 4
HEAD_DIM = 128
D_FF = 4096
EPS = 1e-6
DEC_SEQ = 4

LANES = 128
SUBLANES = 8
COL_Q = POOL_WIDTH
COL_K = COL_Q + MLSTM_WIDTH
COL_V = COL_K + MLSTM_WIDTH
COL_O = COL_V + MLSTM_WIDTH
IN_MAIN = COL_O + MLSTM_WIDTH
GATE_W = 2 * LANES
AUG = 2 * HEAD_DIM
CTX_PAD = 16
CHUNK = 256
PROMPT_TILE = 512
IN_CHUNK = 512
FFN_TILE = 512
FFN_CHUNK = 1024
SAMPLE_BLOCK = 8
NEG_BIG = -1e30
VMEM_LIMIT = 56 * 1024 * 1024

F32 = jnp.float32
BF16 = jnp.bfloat16


def _rmsnorm(x, g):
    return x * lax.rsqrt(jnp.mean(x * x, axis=-1, keepdims=True) + EPS) * g


def _log_sigmoid(x):
    return jnp.minimum(x, 0.0) - jnp.log1p(jnp.exp(-jnp.abs(x)))


def _dot(a, b):
    return jnp.dot(a, b, preferred_element_type=F32)


def _dot_nt(a, b):
    return lax.dot_general(a, b, (((1,), (1,)), ((), ())), preferred_element_type=F32)


def _hcols(base, h):
    return slice(base + h * HEAD_DIM, base + (h + 1) * HEAD_DIM)


def _row_scan(x, op):
    n = x.shape[0]
    row = lax.broadcasted_iota(jnp.int32, x.shape, 0)
    sh = 1
    while sh < n:
        x = jnp.where(row >= sh, op(x, pltpu.roll(x, sh, axis=0)), x)
        sh *= 2
    return x


def _gate_cols(g, m0, n_valid):
    R = g.shape[0]
    b = _row_scan(_log_sigmoid(g[:, LANES:]), jnp.add)
    r = g[:, :LANES] - b
    big_m = jnp.maximum(m0, _row_scan(r, jnp.maximum))
    m = b + big_m
    last = n_valid - 1
    m_top = big_m[last:last + 1, :]
    ws = jnp.exp(r - m_top)
    if n_valid < R:
        ws = jnp.where(lax.broadcasted_iota(jnp.int32, ws.shape, 0) < n_valid, ws, 0.0)
    return dict(r=r, big_m=big_m, inter=jnp.exp(m0 - big_m), eneg=jnp.exp(-m), ws=ws,
                decay=jnp.exp(m0 - m_top), m_last=m[last:last + 1, :])


def _value_aug(v, h):
    lane = lax.broadcasted_iota(jnp.int32, (v.shape[0], HEAD_DIM), 1)
    return jnp.concatenate([v, jnp.where(lane == h, 1.0, 0.0)], axis=1).astype(BF16)


def _meta_kernel(x_ref, n1_ref, win_ref, wg_ref, bg_ref, u_ref, c_ref, m_ref):
    xn = _rmsnorm(x_ref[...], n1_ref[...]).astype(BF16)
    gc = _gate_cols(_dot(xn, wg_ref[...]) + bg_ref[...], jnp.zeros((1, LANES), F32), N_META)
    p = _dot(xn, win_ref[:, 0:IN_MAIN])
    u_ref[...] = p[0:N_META, 0:POOL_WIDTH]
    for h in range(N_HEADS):
        kw_t = (p[:, _hcols(COL_K, h)] * gc["ws"][:, h:h + 1]).T.astype(BF16)
        c_ref[h] = _dot(kw_t, _value_aug(p[:, _hcols(COL_V, h)], h))
    m_ref[...] = jnp.broadcast_to(gc["m_last"], (SUBLANES, LANES))


def _prompt_kernel(nt, xa_ref, xc_ref, umeta_ref, cmeta_ref, mmeta_ref, n1_ref, win_ref, wg_ref, bg_ref,
                   wpool_ref, pscale_ref, gain_ref, wout_ref,
                   x1_ref, pool_ref, c_ref, m_ref,
                   uext_ref, xn_ref, p0_ref, p1_ref, g0_ref, g1_ref, mix0_ref, mix1_ref):
    s = pl.program_id(0)
    n_tiles = pl.num_programs(0) - 2

    @pl.when((jnp.maximum(s - 1, 0) % nt == 0) & (s <= n_tiles))
    def _():
        c_ref[0] = cmeta_ref[...]
        m_ref[0] = mmeta_ref[...]
        uext_ref[0:CTX_PAD, :] = umeta_ref[...]

    stage = functools.partial(_prompt_step, xa_ref, xc_ref, n1_ref, win_ref, wg_ref, bg_ref, wpool_ref,
                              pscale_ref, gain_ref, wout_ref, x1_ref, pool_ref, c_ref, m_ref, uext_ref, xn_ref)
    even = (p0_ref, p1_ref, g0_ref, g1_ref, mix0_ref, mix1_ref)
    odd = (p1_ref, p0_ref, g1_ref, g0_ref, mix1_ref, mix0_ref)
    steady = (s >= 2) & (s < n_tiles)
    pl.when(s == 0)(lambda: stage(*even, project=True, mix=False, emit=False))
    pl.when(s == 1)(lambda: stage(*odd, project=True, mix=True, emit=False))
    pl.when(steady & (s % 2 == 0))(lambda: stage(*even, project=True, mix=True, emit=True))
    pl.when(steady & (s % 2 == 1))(lambda: stage(*odd, project=True, mix=True, emit=True))
    pl.when(s == n_tiles)(lambda: stage(*even, project=False, mix=True, emit=True))
    pl.when(s == n_tiles + 1)(lambda: stage(*odd, project=False, mix=False, emit=True))


def _prompt_step(xa_ref, xc_ref, n1_ref, win_ref, wg_ref, bg_ref, wpool_ref, pscale_ref, gain_ref, wout_ref,
                 x1_ref, pool_ref, c_ref, m_ref, uext_ref, xn_ref,
                 p_new, pb, g_new, g_old, mx, mix_old, *, project, mix, emit):
    if emit:
        x1_ref[0] = xc_ref[0] + _dot(mix_old[...], wout_ref[...])

    if project:
        xn_ref[...] = _rmsnorm(xa_ref[0], n1_ref[...]).astype(BF16)
        g_new[...] = _dot(xn_ref[...], wg_ref[...]) + bg_ref[...]
        for c in range(IN_MAIN // IN_CHUNK):
            cs = slice(c * IN_CHUNK, (c + 1) * IN_CHUNK)
            p_new[:, cs] = _dot(xn_ref[...], win_ref[:, cs])

    if mix:
        for ci in range(PROMPT_TILE // CHUNK):
            _mix_chunk(pb, g_old, mx, slice(ci * CHUNK, (ci + 1) * CHUNK),
                       wpool_ref, pscale_ref, gain_ref, pool_ref, c_ref, m_ref, uext_ref)


def _mix_chunk(pb, g_old, mx, rows, wpool_ref, pscale_ref, gain_ref, pool_ref, c_ref, m_ref, uext_ref):
    R = CHUNK
    gc = _gate_cols(g_old[rows, :], m_ref[0, 0:1, :], R)
    r_t = gc["r"].T

    uext_ref[CTX_PAD:CTX_PAD + R, :] = pb[rows, 0:POOL_WIDTH]
    for gi, w in enumerate(POOL_WINDOWS):
        cs = slice(gi * POOL_GC, (gi + 1) * POOL_GC)
        acc = uext_ref[:, cs]
        sh = 1
        while sh < w:
            acc = acc + pltpu.roll(acc, sh, axis=0)
            sh *= 2
        z = acc[CTX_PAD:, :] * (1.0 / w) - uext_ref[CTX_PAD:CTX_PAD + R, cs]
        mx[rows, cs] = (_dot(z.astype(BF16), wpool_ref[gi]) * pscale_ref[:, cs]).astype(BF16)
    tail = uext_ref[R:R + CTX_PAD, :]
    uext_ref[0:CTX_PAD, :] = tail
    pool_ref[0] = tail

    causal = (lax.broadcasted_iota(jnp.int32, (R, R), 0) >= lax.broadcasted_iota(jnp.int32, (R, R), 1))
    for h in range(N_HEADS):
        q = (pb[rows, _hcols(COL_Q, h)] * (HEAD_DIM ** -0.5)).astype(BF16)
        k = pb[rows, _hcols(COL_K, h)]
        vaug = _value_aug(pb[rows, _hcols(COL_V, h)], h)
        sc = _dot_nt(q, k.astype(BF16))
        e = jnp.exp(jnp.where(causal, r_t[h:h + 1, :] - gc["big_m"][:, h:h + 1], NEG_BIG))
        c0 = c_ref[0, h]
        num = gc["inter"][:, h:h + 1] * _dot(q, c0.astype(BF16)) + _dot((sc * e).astype(BF16), vaug)
        inv = 1.0 / jnp.maximum(jnp.abs(num[:, HEAD_DIM:]), gc["eneg"])
        hh = num[:, 0:HEAD_DIM] * inv[:, h:h + 1]
        hh = hh * lax.rsqrt(jnp.mean(hh * hh, axis=-1, keepdims=True) + EPS) * gain_ref[:, _hcols(0, h)]
        mx[rows, _hcols(POOL_WIDTH, h)] = (hh * jax.nn.sigmoid(pb[rows, _hcols(COL_O, h)])).astype(BF16)
        kw_t = (k * gc["ws"][:, h:h + 1]).T.astype(BF16)
        c_ref[0, h] = gc["decay"][:, h:h + 1] * c0 + _dot(kw_t, vaug)
    m_ref[0] = jnp.broadcast_to(gc["m_last"], (SUBLANES, LANES))


def _sample_gates(g_ref, m0):
    n = m0.shape[0]
    b = jnp.zeros((n, LANES), F32)
    big_m = m0
    r, big_ms, ms = [], [], []
    for t in range(DEC_SEQ):
        g = g_ref[t * n:(t + 1) * n, :]
        b = b + _log_sigmoid(g[:, LANES:])
        r.append(g[:, :LANES] - b)
        big_m = jnp.maximum(big_m, r[-1])
        big_ms.append(big_m)
        ms.append(b + big_m)
    return dict(r=r, big_m=big_ms, m=ms,
                ws=[jnp.exp(r[t] - big_m) for t in range(DEC_SEQ)],
                decay=jnp.exp(m0 - big_m))


def _sample_kernel(xs_ref, ctx_ref, cin_ref, n0_ref, m0_ref, n1_ref, win_ref, wg_ref, bg_ref, wpool_ref,
                   pscale_ref, gain_ref, wout_ref,
                   x1_ref, pool_ref, cout_ref, nout_ref, mout_ref,
                   p_ref, g_ref, tmp_ref, qsm_ref, qcsm_ref, kwt_ref, vsm_ref, dec_ref, mix_ref):
    i = pl.program_id(0)
    n_seq = m0_ref.shape[0]
    n_grp = DEC_SEQ * n_seq // LANES
    scale = HEAD_DIM ** -0.5

    def seq_major(dst_ref, slab_of_t):
        for t in range(DEC_SEQ):
            slab = slab_of_t(t)
            for h in range(N_HEADS):
                dst_ref[h, pl.ds(t, n_seq, stride=DEC_SEQ), :] = slab[:, _hcols(0, h)]

    @pl.when(i == 0)
    def _():
        xn = _rmsnorm(xs_ref[...], n1_ref[...]).astype(BF16)
        g_ref[...] = _dot(xn, wg_ref[...]) + bg_ref[...]
        p_ref[...] = _dot(xn, win_ref[...])

        def urow(e, cs):
            if e < POOL_CTX:
                return ctx_ref[e, :, cs]
            return p_ref[(e - POOL_CTX) * n_seq:(e - POOL_CTX + 1) * n_seq, cs]

        for gi, w in enumerate(POOL_WINDOWS):
            cs = slice(gi * POOL_GC, (gi + 1) * POOL_GC)
            for t in range(DEC_SEQ):
                e = POOL_CTX + t
                wsum = urow(e, cs)
                for j in range(1, w):
                    wsum = wsum + urow(e - j, cs)
                z = wsum * (1.0 / w) - urow(e, cs)
                mix_ref[t * n_seq:(t + 1) * n_seq, cs] = (
                    _dot(z.astype(BF16), wpool_ref[gi]) * pscale_ref[:, cs]).astype(BF16)
        for e in range(DEC_SEQ, POOL_CTX + DEC_SEQ):
            pool_ref[e - DEC_SEQ] = urow(e, slice(0, POOL_WIDTH))

        gs = _sample_gates(g_ref, m0_ref[...])
        dec_ref[...] = gs["decay"]
        mout_ref[...] = gs["m"][DEC_SEQ - 1]
        def kw_slab(t):
            rows = slice(t * n_seq, (t + 1) * n_seq)
            return jnp.concatenate([p_ref[rows, _hcols(COL_K, h)] * gs["ws"][t][:, h:h + 1]
                                    for h in range(N_HEADS)], axis=1)

        kw = [kw_slab(t) for t in range(DEC_SEQ)]
        nout_ref[...] = (jnp.concatenate([jnp.broadcast_to(gs["decay"][:, h:h + 1], (n_seq, HEAD_DIM))
                                          for h in range(N_HEADS)], axis=1) * n0_ref[...]
                         + kw[0] + kw[1] + kw[2] + kw[3])
        seq_major(tmp_ref, lambda t: kw[t])
        for h in range(N_HEADS):
            for gi in range(n_grp):
                kwt_ref[h, gi] = tmp_ref[h, gi * LANES:(gi + 1) * LANES, :].T.astype(BF16)
        seq_major(tmp_ref, lambda t: p_ref[t * n_seq:(t + 1) * n_seq, COL_V:COL_O])
        for h in range(N_HEADS):
            for gi in range(n_grp):
                vsm_ref[h, gi] = tmp_ref[h, gi * LANES:(gi + 1) * LANES, :].astype(BF16)
        seq_major(qsm_ref, lambda t: p_ref[t * n_seq:(t + 1) * n_seq, COL_Q:COL_K] * scale)

    lane_sl = lax.broadcasted_iota(jnp.int32, (HEAD_DIM, LANES), 1) // DEC_SEQ
    first = lax.broadcasted_iota(jnp.int32, (SUBLANES, HEAD_DIM), 0) < DEC_SEQ
    grp = (i * SAMPLE_BLOCK) // (LANES // DEC_SEQ)
    for jp in range(SAMPLE_BLOCK // 2):
        rows = pl.ds(pl.multiple_of(i * (SAMPLE_BLOCK * DEC_SEQ) + jp * SUBLANES, SUBLANES), SUBLANES)
        for h in range(N_HEADS):
            hs = _hcols(0, h)
            q8 = qsm_ref[h, rows, :].astype(BF16)
            readout = []
            for a in range(2):
                j = 2 * jp + a
                seq = i * SAMPLE_BLOCK + j
                c0 = cin_ref[j, h]
                readout.append(_dot(q8, c0.astype(BF16)))
                lhs = jnp.where(lane_sl == seq % (LANES // DEC_SEQ), kwt_ref[h, grp], jnp.zeros((), BF16))
                cout_ref[j, h] = dec_ref[pl.ds(seq, 1), :][:, h:h + 1] * c0 + _dot(lhs, vsm_ref[h, grp])
            qcsm_ref[h, rows, :] = jnp.where(first, readout[0], readout[1])

    @pl.when(i == pl.num_programs(0) - 1)
    def _():
        gs = _sample_gates(g_ref, m0_ref[...])
        for h in range(N_HEADS):
            hs = _hcols(0, h)
            n0 = n0_ref[:, hs]
            qs = [p_ref[t * n_seq:(t + 1) * n_seq, _hcols(COL_Q, h)] * scale for t in range(DEC_SEQ)]
            ks = [p_ref[t * n_seq:(t + 1) * n_seq, _hcols(COL_K, h)] for t in range(DEC_SEQ)]
            for t in range(DEC_SEQ):
                rows = slice(t * n_seq, (t + 1) * n_seq)
                inter = jnp.exp(m0_ref[...] - gs["big_m"][t])[:, h:h + 1]
                num = inter * qcsm_ref[h, pl.ds(t, n_seq, stride=DEC_SEQ), :]
                den = inter * jnp.sum(qs[t] * n0, axis=-1, keepdims=True)
                for s in range(t + 1):
                    pts = (jnp.sum(qs[t] * ks[s], axis=-1, keepdims=True)
                           * jnp.exp(gs["r"][s] - gs["big_m"][t])[:, h:h + 1])
                    num = num + pts * p_ref[s * n_seq:(s + 1) * n_seq, _hcols(COL_V, h)]
                    den = den + pts
                hh = num / jnp.maximum(jnp.abs(den), jnp.exp(-gs["m"][t])[:, h:h + 1])
                hh = hh * lax.rsqrt(jnp.mean(hh * hh, axis=-1, keepdims=True) + EPS) * gain_ref[:, hs]
                mix_ref[rows, _hcols(POOL_WIDTH, h)] = (
                    hh * jax.nn.sigmoid(p_ref[rows, _hcols(COL_O, h)])).astype(BF16)
        x1_ref[...] = xs_ref[...] + _dot(mix_ref[...], wout_ref[...])


def _ffn_kernel(x_ref, n2_ref, wup_ref, wdown_ref, nf_ref, y_ref):
    x = x_ref[...]
    xn = _rmsnorm(x, n2_ref[...]).astype(BF16)
    acc = x
    for c in range(D_FF // FFN_CHUNK):
        cs = slice(c * FFN_CHUNK, (c + 1) * FFN_CHUNK)
        a = jnp.square(jnp.maximum(_dot(xn, wup_ref[:, cs]), 0.0)).astype(BF16)
        acc = acc + _dot(a, wdown_ref[cs, :])
    y_ref[...] = _rmsnorm(acc, nf_ref[...])


def _full(shape):
    return pl.BlockSpec(shape, lambda *_: (0,) * len(shape))


def _ffn(x, n2, wup, wdown, nf):
    rows = x.shape[0]
    return pl.pallas_call(
        _ffn_kernel,
        grid=(rows // FFN_TILE,),
        in_specs=[pl.BlockSpec((FFN_TILE, D_MODEL), lambda i: (i, 0)),
                  _full((1, D_MODEL)), _full((D_MODEL, D_FF)), _full((D_FF, D_MODEL)), _full((1, D_MODEL))],
        out_specs=pl.BlockSpec((FFN_TILE, D_MODEL), lambda i: (i, 0)),
        out_shape=jax.ShapeDtypeStruct((rows, D_MODEL), F32),
        compiler_params=pltpu.CompilerParams(dimension_semantics=("arbitrary",), vmem_limit_bytes=VMEM_LIMIT),
        name="ffn",
    )(x, n2, wup, wdown, nf)


def kernel(x_prompt, x_sample, state_pool, state_C, state_n, state_m, meta_tokens, norm1, w_in, b_gate,
           w_pool, pool_scale, head_gain, w_out, norm2, w_up, w_down, norm_f):
    B, S, _ = x_prompt.shape
    n_seq = x_sample.shape[0]
    assert w_in.shape[0] == 1 and S % PROMPT_TILE == 0 and n_seq % SAMPLE_BLOCK == 0

    win = w_in[0].astype(BF16)
    gpad = ((0, 0), (0, LANES - N_HEADS))
    wg = jnp.concatenate([jnp.pad(w_in[0][:, IN_MAIN:IN_MAIN + N_HEADS], gpad),
                          jnp.pad(w_in[0][:, IN_MAIN + N_HEADS:], gpad)], axis=1).astype(BF16)
    bg = jnp.concatenate([jnp.pad(b_gate[0][:N_HEADS], (0, LANES - N_HEADS)),
                          jnp.pad(b_gate[0][N_HEADS:], (0, LANES - N_HEADS))]).reshape(1, GATE_W)
    wpool = w_pool[0].astype(BF16)
    wout = w_out[0].astype(BF16)
    wup = w_up[0].astype(BF16)
    wdown = w_down[0].astype(BF16)
    n1 = norm1[0].reshape(1, D_MODEL)
    n2 = norm2[0].reshape(1, D_MODEL)
    nf = norm_f.reshape(1, D_MODEL)
    pscale = pool_scale[0].reshape(1, POOL_WIDTH)
    gain = head_gain[0].reshape(1, MLSTM_WIDTH)

    weight_specs = [_full((1, D_MODEL)), _full((D_MODEL, IN_MAIN)), _full((D_MODEL, GATE_W)), _full((1, GATE_W))]
    mixer_specs = [_full((len(POOL_WINDOWS), POOL_GC, POOL_GC)), _full((1, POOL_WIDTH)),
                   _full((1, MLSTM_WIDTH)), _full((D_MODEL, D_MODEL))]

    meta_x = jnp.pad(meta_tokens, ((0, LANES - N_META), (0, 0)))
    u_meta, c_meta, m_meta = pl.pallas_call(
        _meta_kernel,
        out_shape=(jax.ShapeDtypeStruct((N_META, POOL_WIDTH), F32),
                   jax.ShapeDtypeStruct((N_HEADS, HEAD_DIM, AUG), F32),
                   jax.ShapeDtypeStruct((SUBLANES, LANES), F32)),
        compiler_params=pltpu.CompilerParams(vmem_limit_bytes=VMEM_LIMIT),
        name="meta_prefix",
    )(meta_x, n1, win, wg, bg)

    nt = S // PROMPT_TILE
    n_tiles = B * nt
    assert n_tiles % 2 == 0 and n_tiles >= 2

    def tile_at(lag):
        def index(s):
            j = jnp.clip(s - lag, 0, n_tiles - 1)
            return (j // nt, j % nt, 0)
        return index

    def mixed_seq(ndim):
        return lambda s: (jnp.clip(s - 1, 0, n_tiles - 1) // nt,) + (0,) * (ndim - 1)

    x1_p, pool_p, c_p, m_p = pl.pallas_call(
        functools.partial(_prompt_kernel, nt),
        grid=(n_tiles + 2,),
        in_specs=[pl.BlockSpec((1, PROMPT_TILE, D_MODEL), tile_at(0)),
                  pl.BlockSpec((1, PROMPT_TILE, D_MODEL), tile_at(2)),
                  _full((N_META, POOL_WIDTH)), _full((N_HEADS, HEAD_DIM, AUG)), _full((SUBLANES, LANES))]
        + weight_specs + mixer_specs,
        out_specs=[pl.BlockSpec((1, PROMPT_TILE, D_MODEL), tile_at(2)),
                   pl.BlockSpec((1, CTX_PAD, POOL_WIDTH), mixed_seq(3)),
                   pl.BlockSpec((1, N_HEADS, HEAD_DIM, AUG), mixed_seq(4)),
                   pl.BlockSpec((1, SUBLANES, LANES), mixed_seq(3))],
        out_shape=(jax.ShapeDtypeStruct((B, S, D_MODEL), F32),
                   jax.ShapeDtypeStruct((B, CTX_PAD, POOL_WIDTH), F32),
                   jax.ShapeDtypeStruct((B, N_HEADS, HEAD_DIM, AUG), F32),
                   jax.ShapeDtypeStruct((B, SUBLANES, LANES), F32)),
        scratch_shapes=[pltpu.VMEM((CTX_PAD + CHUNK, POOL_WIDTH), F32),
                        pltpu.VMEM((PROMPT_TILE, D_MODEL), BF16),
                        pltpu.VMEM((PROMPT_TILE, IN_MAIN), F32), pltpu.VMEM((PROMPT_TILE, IN_MAIN), F32),
                        pltpu.VMEM((PROMPT_TILE, GATE_W), F32), pltpu.VMEM((PROMPT_TILE, GATE_W), F32),
                        pltpu.VMEM((PROMPT_TILE, D_MODEL), BF16), pltpu.VMEM((PROMPT_TILE, D_MODEL), BF16)],
        compiler_params=pltpu.CompilerParams(dimension_semantics=("arbitrary",), vmem_limit_bytes=VMEM_LIMIT),
        name="prompt_mixer",
    )(x_prompt, x_prompt, u_meta, c_meta, m_meta, n1, win, wg, bg, wpool, pscale, gain, wout)

    rows_s = DEC_SEQ * n_seq
    xs_t = x_sample.transpose(1, 0, 2).reshape(rows_s, D_MODEL)
    ctx_t = state_pool[0].transpose(1, 0, 2)
    n0 = state_n[0].reshape(n_seq, MLSTM_WIDTH)
    m0 = jnp.pad(state_m[0], ((0, 0), (0, LANES - N_HEADS)))
    state_block = pl.BlockSpec((SAMPLE_BLOCK, N_HEADS, HEAD_DIM, HEAD_DIM), lambda i: (i, 0, 0, 0))
    x1_s, pool_s, c_s, n_s, m_s = pl.pallas_call(
        _sample_kernel,
        grid=(n_seq // SAMPLE_BLOCK,),
        in_specs=[_full((rows_s, D_MODEL)), _full((POOL_CTX, n_seq, POOL_WIDTH)), state_block,
                  _full((n_seq, MLSTM_WIDTH)), _full((n_seq, LANES))] + weight_specs + mixer_specs,
        out_specs=[_full((rows_s, D_MODEL)), _full((POOL_CTX, n_seq, POOL_WIDTH)), state_block,
                   _full((n_seq, MLSTM_WIDTH)), _full((n_seq, LANES))],
        out_shape=(jax.ShapeDtypeStruct((rows_s, D_MODEL), F32),
                   jax.ShapeDtypeStruct((POOL_CTX, n_seq, POOL_WIDTH), F32),
                   jax.ShapeDtypeStruct((n_seq, N_HEADS, HEAD_DIM, HEAD_DIM), F32),
                   jax.ShapeDtypeStruct((n_seq, MLSTM_WIDTH), F32),
                   jax.ShapeDtypeStruct((n_seq, LANES), F32)),
        scratch_shapes=[pltpu.VMEM((rows_s, IN_MAIN), F32),
                        pltpu.VMEM((rows_s, GATE_W), F32),
                        pltpu.VMEM((N_HEADS, rows_s, HEAD_DIM), F32),
                        pltpu.VMEM((N_HEADS, rows_s, HEAD_DIM), F32),
                        pltpu.VMEM((N_HEADS, rows_s, HEAD_DIM), F32),
                        pltpu.VMEM((N_HEADS, rows_s // LANES, HEAD_DIM, LANES), BF16),
                        pltpu.VMEM((N_HEADS, rows_s // LANES, LANES, HEAD_DIM), BF16),
                        pltpu.VMEM((n_seq, LANES), F32),
                        pltpu.VMEM((rows_s, D_MODEL), BF16)],
        compiler_params=pltpu.CompilerParams(dimension_semantics=("arbitrary",), vmem_limit_bytes=VMEM_LIMIT),
        name="sample_mixer",
    )(xs_t, ctx_t, state_C[0], n0, m0, n1, win, wg, bg, wpool, pscale, gain, wout)

    y_p = _ffn(x1_p.reshape(B * S, D_MODEL), n2, wup, wdown, nf).reshape(B, S, D_MODEL)
    y_s = _ffn(x1_s, n2, wup, wdown, nf).reshape(DEC_SEQ, n_seq, D_MODEL).transpose(1, 0, 2)

    n_p = jnp.stack([c_p[:, h, :, HEAD_DIM + h] for h in range(N_HEADS)], axis=1)
    return (y_p, y_s,
            pool_p[:, CTX_PAD - POOL_CTX:, :][None],
            c_p[..., :HEAD_DIM][None],
            n_p[None],
            m_p[:, 0, :N_HEADS][None],
            pool_s.transpose(1, 0, 2)[None],
            c_s[None],
            n_s.reshape(n_seq, N_HEADS, HEAD_DIM)[None],
            m_s[:, :N_HEADS][None])
```

```python
import functools

import jax
import jax.numpy as jnp
from jax import lax
from jax.experimental import pallas as pl
from jax.experimental.pallas import tpu as pltpu

D_MODEL = 1024
N_META = 16
POOL_WIDTH = 512
POOL_WINDOWS = (2, 4, 8, 16)
POOL_GC = 128
POOL_CTX = 15
MLSTM_WIDTH = 512
N_HEADS = 4
HEAD_DIM = 128
D_FF = 4096
EPS = 1e-6
DEC_SEQ = 4

LANES = 128
SUBLANES = 8
COL_Q = POOL_WIDTH
COL_K = COL_Q + MLSTM_WIDTH
COL_V = COL_K + MLSTM_WIDTH
COL_O = COL_V + MLSTM_WIDTH
IN_MAIN = COL_O + MLSTM_WIDTH
GATE_W = 2 * LANES
AUG = 2 * HEAD_DIM
CTX_PAD = 16
CHUNK = 256
PROMPT_TILE = 512
IN_CHUNK = 512
FFN_TILE = 512
FFN_CHUNK = 1024
SAMPLE_BLOCK = 8
NEG_BIG = -1e30
VMEM_LIMIT = 56 * 1024 * 1024

F32 = jnp.float32
BF16 = jnp.bfloat16


def _rmsnorm(x, g):
    return x * lax.rsqrt(jnp.mean(x * x, axis=-1, keepdims=True) + EPS) * g


def _log_sigmoid(x):
    return jnp.minimum(x, 0.0) - jnp.log1p(jnp.exp(-jnp.abs(x)))


def _dot(a, b):
    return jnp.dot(a, b, preferred_element_type=F32)


def _dot_nt(a, b):
    return lax.dot_general(a, b, (((1,), (1,)), ((), ())), preferred_element_type=F32)


def _hcols(base, h):
    return slice(base + h * HEAD_DIM, base + (h + 1) * HEAD_DIM)


def _row_scan(x, op):
    n = x.shape[0]
    row = lax.broadcasted_iota(jnp.int32, x.shape, 0)
    sh = 1
    while sh < n:
        x = jnp.where(row >= sh, op(x, pltpu.roll(x, sh, axis=0)), x)
        sh *= 2
    return x


def _gate_cols(g, m0, n_valid):
    R = g.shape[0]
    b = _row_scan(_log_sigmoid(g[:, LANES:]), jnp.add)
    r = g[:, :LANES] - b
    big_m = jnp.maximum(m0, _row_scan(r, jnp.maximum))
    m = b + big_m
    last = n_valid - 1
    m_top = big_m[last:last + 1, :]
    ws = jnp.exp(r - m_top)
    if n_valid < R:
        ws = jnp.where(lax.broadcasted_iota(jnp.int32, ws.shape, 0) < n_valid, ws, 0.0)
    return dict(r=r, big_m=big_m, inter=jnp.exp(m0 - big_m), eneg=jnp.exp(-m), ws=ws,
                decay=jnp.exp(m0 - m_top), m_last=m[last:last + 1, :])


def _value_aug(v, h):
    lane = lax.broadcasted_iota(jnp.int32, (v.shape[0], HEAD_DIM), 1)
    return jnp.concatenate([v, jnp.where(lane == h, 1.0, 0.0)], axis=1).astype(BF16)


def _meta_kernel(x_ref, n1_ref, win_ref, wg_ref, bg_ref, u_ref, c_ref, m_ref):
    xn = _rmsnorm(x_ref[...], n1_ref[...]).astype(BF16)
    gc = _gate_cols(_dot(xn, wg_ref[...]) + bg_ref[...], jnp.zeros((1, LANES), F32), N_META)
    p = _dot(xn, win_ref[:, 0:IN_MAIN])
    u_ref[...] = p[0:N_META, 0:POOL_WIDTH]
    for h in range(N_HEADS):
        kw_t = (p[:, _hcols(COL_K, h)] * gc["ws"][:, h:h + 1]).T.astype(BF16)
        c_ref[h] = _dot(kw_t, _value_aug(p[:, _hcols(COL_V, h)], h))
    m_ref[...] = jnp.broadcast_to(gc["m_last"], (SUBLANES, LANES))


def _prompt_kernel(nt, xa_ref, xc_ref, umeta_ref, cmeta_ref, mmeta_ref, n1_ref, win_ref, wg_ref, bg_ref,
                   wpool_ref, pscale_ref, gain_ref, wout_ref, wup_ref, wdown_ref,
                   x1_ref, pool_ref, c_ref, m_ref, wup_bf_ref, wdown_bf_ref,
                   uext_ref, xn_ref, p0_ref, p1_ref, g0_ref, g1_ref, mix0_ref, mix1_ref):
    s = pl.program_id(0)
    n_tiles = pl.num_programs(0) - 2
    wup_bf_ref[...] = wup_ref[...].astype(BF16)
    wdown_bf_ref[...] = wdown_ref[...].astype(BF16)

    @pl.when((jnp.maximum(s - 1, 0) % nt == 0) & (s <= n_tiles))
    def _():
        c_ref[0] = cmeta_ref[...]
        m_ref[0] = mmeta_ref[...]
        uext_ref[0:CTX_PAD, :] = umeta_ref[...]

    stage = functools.partial(_prompt_step, xa_ref, xc_ref, n1_ref, win_ref, wg_ref, bg_ref, wpool_ref,
                              pscale_ref, gain_ref, wout_ref, x1_ref, pool_ref, c_ref, m_ref, uext_ref, xn_ref)
    even = (p0_ref, p1_ref, g0_ref, g1_ref, mix0_ref, mix1_ref)
    odd = (p1_ref, p0_ref, g1_ref, g0_ref, mix1_ref, mix0_ref)
    steady = (s >= 2) & (s < n_tiles)
    pl.when(s == 0)(lambda: stage(*even, project=True, mix=False, emit=False))
    pl.when(s == 1)(lambda: stage(*odd, project=True, mix=True, emit=False))
    pl.when(steady & (s % 2 == 0))(lambda: stage(*even, project=True, mix=True, emit=True))
    pl.when(steady & (s % 2 == 1))(lambda: stage(*odd, project=True, mix=True, emit=True))
    pl.when(s == n_tiles)(lambda: stage(*even, project=False, mix=True, emit=True))
    pl.when(s == n_tiles + 1)(lambda: stage(*odd, project=False, mix=False, emit=True))


def _prompt_step(xa_ref, xc_ref, n1_ref, win_ref, wg_ref, bg_ref, wpool_ref, pscale_ref, gain_ref, wout_ref,
                 x1_ref, pool_ref, c_ref, m_ref, uext_ref, xn_ref,
                 p_new, pb, g_new, g_old, mx, mix_old, *, project, mix, emit):
    if emit:
        x1_ref[...] = xc_ref[0] + _dot(mix_old[...], wout_ref[...])
    elif not mix:
        x1_ref[...] = jnp.zeros(x1_ref.shape, F32)

    if project:
        xn_ref[...] = _rmsnorm(xa_ref[0], n1_ref[...]).astype(BF16)
        g_new[...] = _dot(xn_ref[...], wg_ref[...]) + bg_ref[...]
        for c in range(IN_MAIN // IN_CHUNK):
            cs = slice(c * IN_CHUNK, (c + 1) * IN_CHUNK)
            p_new[:, cs] = _dot(xn_ref[...], win_ref[:, cs])

    if mix:
        for ci in range(PROMPT_TILE // CHUNK):
            _mix_chunk(pb, g_old, mx, slice(ci * CHUNK, (ci + 1) * CHUNK),
                       wpool_ref, pscale_ref, gain_ref, pool_ref, c_ref, m_ref, uext_ref)


def _mix_chunk(pb, g_old, mx, rows, wpool_ref, pscale_ref, gain_ref, pool_ref, c_ref, m_ref, uext_ref):
    R = CHUNK
    gc = _gate_cols(g_old[rows, :], m_ref[0, 0:1, :], R)
    r_t = gc["r"].T

    uext_ref[CTX_PAD:CTX_PAD + R, :] = pb[rows, 0:POOL_WIDTH]
    for gi, w in enumerate(POOL_WINDOWS):
        cs = slice(gi * POOL_GC, (gi + 1) * POOL_GC)
        acc = uext_ref[:, cs]
        sh = 1
        while sh < w:
            acc = acc + pltpu.roll(acc, sh, axis=0)
            sh *= 2
        z = acc[CTX_PAD:, :] * (1.0 / w) - uext_ref[CTX_PAD:CTX_PAD + R, cs]
        mx[rows, cs] = (_dot(z.astype(BF16), wpool_ref[gi]) * pscale_ref[:, cs]).astype(BF16)
    tail = uext_ref[R:R + CTX_PAD, :]
    uext_ref[0:CTX_PAD, :] = tail
    pool_ref[0] = tail

    causal = (lax.broadcasted_iota(jnp.int32, (R, R), 0) >= lax.broadcasted_iota(jnp.int32, (R, R), 1))
    for h in range(N_HEADS):
        q = (pb[rows, _hcols(COL_Q, h)] * (HEAD_DIM ** -0.5)).astype(BF16)
        k = pb[rows, _hcols(COL_K, h)]
        vaug = _value_aug(pb[rows, _hcols(COL_V, h)], h)
        sc = _dot_nt(q, k.astype(BF16))
        e = jnp.exp(jnp.where(causal, r_t[h:h + 1, :] - gc["big_m"][:, h:h + 1], NEG_BIG))
        c0 = c_ref[0, h]
        num = gc["inter"][:, h:h + 1] * _dot(q, c0.astype(BF16)) + _dot((sc * e).astype(BF16), vaug)
        inv = 1.0 / jnp.maximum(jnp.abs(num[:, HEAD_DIM:]), gc["eneg"])
        hh = num[:, 0:HEAD_DIM] * inv[:, h:h + 1]
        hh = hh * lax.rsqrt(jnp.mean(hh * hh, axis=-1, keepdims=True) + EPS) * gain_ref[:, _hcols(0, h)]
        mx[rows, _hcols(POOL_WIDTH, h)] = (hh * jax.nn.sigmoid(pb[rows, _hcols(COL_O, h)])).astype(BF16)
        kw_t = (k * gc["ws"][:, h:h + 1]).T.astype(BF16)
        c_ref[0, h] = gc["decay"][:, h:h + 1] * c0 + _dot(kw_t, vaug)
    m_ref[0] = jnp.broadcast_to(gc["m_last"], (SUBLANES, LANES))


def _sample_gates(g_ref, m0):
    n = m0.shape[0]
    b = jnp.zeros((n, LANES), F32)
    big_m = m0
    r, big_ms, ms = [], [], []
    for t in range(DEC_SEQ):
        g = g_ref[t * n:(t + 1) * n, :]
        b = b + _log_sigmoid(g[:, LANES:])
        r.append(g[:, :LANES] - b)
        big_m = jnp.maximum(big_m, r[-1])
        big_ms.append(big_m)
        ms.append(b + big_m)
    return dict(r=r, big_m=big_ms, m=ms,
                ws=[jnp.exp(r[t] - big_m) for t in range(DEC_SEQ)],
                decay=jnp.exp(m0 - big_m))


def _sample_kernel(xs_ref, ctx_ref, cin_ref, n0_ref, m0_ref, n1_ref, win_ref, wg_ref, bg_ref, wpool_ref,
                   pscale_ref, gain_ref, wout_ref, x1_alias_ref,
                   x1_ref, pool_ref, cout_ref, nout_ref, mout_ref,
                   p_ref, g_ref, tmp_ref, qsm_ref, qcsm_ref, kwt_ref, vsm_ref, dec_ref, mix_ref):
    i = pl.program_id(0)
    n_seq = m0_ref.shape[0]
    n_grp = DEC_SEQ * n_seq // LANES
    scale = HEAD_DIM ** -0.5

    def seq_major(dst_ref, slab_of_t):
        for t in range(DEC_SEQ):
            slab = slab_of_t(t)
            for h in range(N_HEADS):
                dst_ref[h, pl.ds(t, n_seq, stride=DEC_SEQ), :] = slab[:, _hcols(0, h)]

    @pl.when(i == 0)
    def _():
        xn = _rmsnorm(xs_ref[...], n1_ref[...]).astype(BF16)
        g_ref[...] = _dot(xn, wg_ref[...]) + bg_ref[...]
        p_ref[...] = _dot(xn, win_ref[...])

        def urow(e, cs):
            if e < POOL_CTX:
                return ctx_ref[e, :, cs]
            return p_ref[(e - POOL_CTX) * n_seq:(e - POOL_CTX + 1) * n_seq, cs]

        for gi, w in enumerate(POOL_WINDOWS):
            cs = slice(gi * POOL_GC, (gi + 1) * POOL_GC)
            for t in range(DEC_SEQ):
                e = POOL_CTX + t
                wsum = urow(e, cs)
                for j in range(1, w):
                    wsum = wsum + urow(e - j, cs)
                z = wsum * (1.0 / w) - urow(e, cs)
                mix_ref[t * n_seq:(t + 1) * n_seq, cs] = (
                    _dot(z.astype(BF16), wpool_ref[gi]) * pscale_ref[:, cs]).astype(BF16)
        for e in range(DEC_SEQ, POOL_CTX + DEC_SEQ):
            pool_ref[e - DEC_SEQ] = urow(e, slice(0, POOL_WIDTH))

        gs = _sample_gates(g_ref, m0_ref[...])
        dec_ref[...] = gs["decay"]
        mout_ref[...] = gs["m"][DEC_SEQ - 1]
        def kw_slab(t):
            rows = slice(t * n_seq, (t + 1) * n_seq)
            return jnp.concatenate([p_ref[rows, _hcols(COL_K, h)] * gs["ws"][t][:, h:h + 1]
                                    for h in range(N_HEADS)], axis=1)

        kw = [kw_slab(t) for t in range(DEC_SEQ)]
        nout_ref[...] = (jnp.concatenate([jnp.broadcast_to(gs["decay"][:, h:h + 1], (n_seq, HEAD_DIM))
                                          for h in range(N_HEADS)], axis=1) * n0_ref[...]
                         + kw[0] + kw[1] + kw[2] + kw[3])
        seq_major(tmp_ref, lambda t: kw[t])
        for h in range(N_HEADS):
            for gi in range(n_grp):
                kwt_ref[h, gi] = tmp_ref[h, gi * LANES:(gi + 1) * LANES, :].T.astype(BF16)
        seq_major(tmp_ref, lambda t: p_ref[t * n_seq:(t + 1) * n_seq, COL_V:COL_O])
        for h in range(N_HEADS):
            for gi in range(n_grp):
                vsm_ref[h, gi] = tmp_ref[h, gi * LANES:(gi + 1) * LANES, :].astype(BF16)
        seq_major(qsm_ref, lambda t: p_ref[t * n_seq:(t + 1) * n_seq, COL_Q:COL_K] * scale)

    lane_sl = lax.broadcasted_iota(jnp.int32, (HEAD_DIM, LANES), 1) // DEC_SEQ
    first = lax.broadcasted_iota(jnp.int32, (SUBLANES, HEAD_DIM), 0) < DEC_SEQ
    grp = (i * SAMPLE_BLOCK) // (LANES // DEC_SEQ)
    for jp in range(SAMPLE_BLOCK // 2):
        rows = pl.ds(pl.multiple_of(i * (SAMPLE_BLOCK * DEC_SEQ) + jp * SUBLANES, SUBLANES), SUBLANES)
        for h in range(N_HEADS):
            hs = _hcols(0, h)
            q8 = qsm_ref[h, rows, :].astype(BF16)
            readout = []
            for a in range(2):
                j = 2 * jp + a
                seq = i * SAMPLE_BLOCK + j
                c0 = cin_ref[j, h]
                readout.append(_dot(q8, c0.astype(BF16)))
                lhs = jnp.where(lane_sl == seq % (LANES // DEC_SEQ), kwt_ref[h, grp], jnp.zeros((), BF16))
                cout_ref[j, h] = dec_ref[pl.ds(seq, 1), :][:, h:h + 1] * c0 + _dot(lhs, vsm_ref[h, grp])
            qcsm_ref[h, rows, :] = jnp.where(first, readout[0], readout[1])

    @pl.when(i == pl.num_programs(0) - 1)
    def _():
        gs = _sample_gates(g_ref, m0_ref[...])
        for h in range(N_HEADS):
            hs = _hcols(0, h)
            n0 = n0_ref[:, hs]
            qs = [p_ref[t * n_seq:(t + 1) * n_seq, _hcols(COL_Q, h)] * scale for t in range(DEC_SEQ)]
            ks = [p_ref[t * n_seq:(t + 1) * n_seq, _hcols(COL_K, h)] for t in range(DEC_SEQ)]
            for t in range(DEC_SEQ):
                rows = slice(t * n_seq, (t + 1) * n_seq)
                inter = jnp.exp(m0_ref[...] - gs["big_m"][t])[:, h:h + 1]
                num = inter * qcsm_ref[h, pl.ds(t, n_seq, stride=DEC_SEQ), :]
                den = inter * jnp.sum(qs[t] * n0, axis=-1, keepdims=True)
                for s in range(t + 1):
                    pts = (jnp.sum(qs[t] * ks[s], axis=-1, keepdims=True)
                           * jnp.exp(gs["r"][s] - gs["big_m"][t])[:, h:h + 1])
                    num = num + pts * p_ref[s * n_seq:(s + 1) * n_seq, _hcols(COL_V, h)]
                    den = den + pts
                hh = num / jnp.maximum(jnp.abs(den), jnp.exp(-gs["m"][t])[:, h:h + 1])
                hh = hh * lax.rsqrt(jnp.mean(hh * hh, axis=-1, keepdims=True) + EPS) * gain_ref[:, hs]
                mix_ref[rows, _hcols(POOL_WIDTH, h)] = (
                    hh * jax.nn.sigmoid(p_ref[rows, _hcols(COL_O, h)])).astype(BF16)
        x1_ref[...] = xs_ref[...] + _dot(mix_ref[...], wout_ref[...])


def _ffn_kernel(x_ref, n2_ref, wup_ref, wdown_ref, nf_ref, yp_ref, ys_ref):
    x = x_ref[...]
    xn = _rmsnorm(x, n2_ref[...]).astype(BF16)
    acc = x
    for c in range(D_FF // FFN_CHUNK):
        cs = slice(c * FFN_CHUNK, (c + 1) * FFN_CHUNK)
        a = jnp.square(jnp.maximum(_dot(xn, wup_ref[:, cs]), 0.0)).astype(BF16)
        acc = acc + _dot(a, wdown_ref[cs, :])
    y = _rmsnorm(acc, nf_ref[...])
    yp_ref[...] = y
    ys_ref[...] = y


def _full(shape):
    return pl.BlockSpec(shape, lambda *_: (0,) * len(shape))


def _ffn(x, n2, wup, wdown, nf):
    tiles = x.shape[0] // FFN_TILE
    return pl.pallas_call(
        _ffn_kernel,
        grid=(tiles,),
        in_specs=[pl.BlockSpec((FFN_TILE, D_MODEL), lambda i: (i, 0)),
                  _full((1, D_MODEL)), _full((D_MODEL, D_FF)), _full((D_FF, D_MODEL)), _full((1, D_MODEL))],
        out_specs=[pl.BlockSpec((FFN_TILE, D_MODEL), lambda i: (jnp.maximum(i - 1, 0), 0)),
                   pl.BlockSpec((FFN_TILE, D_MODEL), lambda i: (jnp.minimum(i, 1), 0))],
        out_shape=(jax.ShapeDtypeStruct(((tiles - 1) * FFN_TILE, D_MODEL), F32),
                   jax.ShapeDtypeStruct((2 * FFN_TILE, D_MODEL), F32)),
        compiler_params=pltpu.CompilerParams(dimension_semantics=("arbitrary",), vmem_limit_bytes=VMEM_LIMIT),
        name="ffn",
    )(x, n2, wup, wdown, nf)


def kernel(x_prompt, x_sample, state_pool, state_C, state_n, state_m, meta_tokens, norm1, w_in, b_gate,
           w_pool, pool_scale, head_gain, w_out, norm2, w_up, w_down, norm_f):
    B, S, _ = x_prompt.shape
    n_seq = x_sample.shape[0]
    rows_s = DEC_SEQ * n_seq
    assert w_in.shape[0] == 1 and S % PROMPT_TILE == 0 and n_seq % SAMPLE_BLOCK == 0
    assert rows_s == PROMPT_TILE == FFN_TILE

    win = w_in[0].astype(BF16)
    gpad = ((0, 0), (0, LANES - N_HEADS))
    wg = jnp.concatenate([jnp.pad(w_in[0][:, IN_MAIN:IN_MAIN + N_HEADS], gpad),
                          jnp.pad(w_in[0][:, IN_MAIN + N_HEADS:], gpad)], axis=1).astype(BF16)
    bg = jnp.concatenate([jnp.pad(b_gate[0][:N_HEADS], (0, LANES - N_HEADS)),
                          jnp.pad(b_gate[0][N_HEADS:], (0, LANES - N_HEADS))]).reshape(1, GATE_W)
    wpool = w_pool[0].astype(BF16)
    wout = w_out[0].astype(BF16)
    n1 = norm1[0].reshape(1, D_MODEL)
    n2 = norm2[0].reshape(1, D_MODEL)
    nf = norm_f.reshape(1, D_MODEL)
    pscale = pool_scale[0].reshape(1, POOL_WIDTH)
    gain = head_gain[0].reshape(1, MLSTM_WIDTH)

    weight_specs = [_full((1, D_MODEL)), _full((D_MODEL, IN_MAIN)), _full((D_MODEL, GATE_W)), _full((1, GATE_W))]
    mixer_specs = [_full((len(POOL_WINDOWS), POOL_GC, POOL_GC)), _full((1, POOL_WIDTH)),
                   _full((1, MLSTM_WIDTH)), _full((D_MODEL, D_MODEL))]

    meta_x = jnp.pad(meta_tokens, ((0, LANES - N_META), (0, 0)))
    u_meta, c_meta, m_meta = pl.pallas_call(
        _meta_kernel,
        out_shape=(jax.ShapeDtypeStruct((N_META, POOL_WIDTH), F32),
                   jax.ShapeDtypeStruct((N_HEADS, HEAD_DIM, AUG), F32),
                   jax.ShapeDtypeStruct((SUBLANES, LANES), F32)),
        compiler_params=pltpu.CompilerParams(vmem_limit_bytes=VMEM_LIMIT),
        name="meta_prefix",
    )(meta_x, n1, win, wg, bg)

    nt = S // PROMPT_TILE
    n_tiles = B * nt
    assert n_tiles % 2 == 0 and n_tiles >= 2
    assert D_MODEL % (n_tiles * 2 * SUBLANES) == 0

    def tile_at(lag):
        def index(s):
            j = jnp.clip(s - lag, 0, n_tiles - 1)
            return (j // nt, j % nt, 0)
        return index

    def mixed_seq(ndim):
        return lambda s: (jnp.clip(s - 1, 0, n_tiles - 1) // nt,) + (0,) * (ndim - 1)

    def slab(s):
        return (jnp.minimum(s, n_tiles - 1), 0)

    up_slab = pl.BlockSpec((D_MODEL // n_tiles, D_FF), slab)
    down_slab = pl.BlockSpec((D_FF // n_tiles, D_MODEL), slab)
    x1_p, pool_p, c_p, m_p, wup, wdown = pl.pallas_call(
        functools.partial(_prompt_kernel, nt),
        grid=(n_tiles + 2,),
        in_specs=[pl.BlockSpec((1, PROMPT_TILE, D_MODEL), tile_at(0)),
                  pl.BlockSpec((1, PROMPT_TILE, D_MODEL), tile_at(2)),
                  _full((N_META, POOL_WIDTH)), _full((N_HEADS, HEAD_DIM, AUG)), _full((SUBLANES, LANES))]
        + weight_specs + mixer_specs + [up_slab, down_slab],
        out_specs=[pl.BlockSpec((PROMPT_TILE, D_MODEL), lambda s: (jnp.clip(s - 1, 0, n_tiles), 0)),
                   pl.BlockSpec((1, CTX_PAD, POOL_WIDTH), mixed_seq(3)),
                   pl.BlockSpec((1, N_HEADS, HEAD_DIM, AUG), mixed_seq(4)),
                   pl.BlockSpec((1, SUBLANES, LANES), mixed_seq(3)),
                   up_slab, down_slab],
        out_shape=(jax.ShapeDtypeStruct((rows_s + B * S, D_MODEL), F32),
                   jax.ShapeDtypeStruct((B, CTX_PAD, POOL_WIDTH), F32),
                   jax.ShapeDtypeStruct((B, N_HEADS, HEAD_DIM, AUG), F32),
                   jax.ShapeDtypeStruct((B, SUBLANES, LANES), F32),
                   jax.ShapeDtypeStruct((D_MODEL, D_FF), BF16),
                   jax.ShapeDtypeStruct((D_FF, D_MODEL), BF16)),
        scratch_shapes=[pltpu.VMEM((CTX_PAD + CHUNK, POOL_WIDTH), F32),
                        pltpu.VMEM((PROMPT_TILE, D_MODEL), BF16),
                        pltpu.VMEM((PROMPT_TILE, IN_MAIN), F32), pltpu.VMEM((PROMPT_TILE, IN_MAIN), F32),
                        pltpu.VMEM((PROMPT_TILE, GATE_W), F32), pltpu.VMEM((PROMPT_TILE, GATE_W), F32),
                        pltpu.VMEM((PROMPT_TILE, D_MODEL), BF16), pltpu.VMEM((PROMPT_TILE, D_MODEL), BF16)],
        compiler_params=pltpu.CompilerParams(dimension_semantics=("arbitrary",), vmem_limit_bytes=VMEM_LIMIT),
        name="prompt_mixer",
    )(x_prompt, x_prompt, u_meta, c_meta, m_meta, n1, win, wg, bg, wpool, pscale, gain, wout, w_up[0], w_down[0])

    xs_t = x_sample.transpose(1, 0, 2).reshape(rows_s, D_MODEL)
    ctx_t = state_pool[0].transpose(1, 0, 2)
    n0 = state_n[0].reshape(n_seq, MLSTM_WIDTH)
    m0 = jnp.pad(state_m[0], ((0, 0), (0, LANES - N_HEADS)))
    state_block = pl.BlockSpec((SAMPLE_BLOCK, N_HEADS, HEAD_DIM, HEAD_DIM), lambda i: (i, 0, 0, 0))
    x1, pool_s, c_s, n_s, m_s = pl.pallas_call(
        _sample_kernel,
        grid=(n_seq // SAMPLE_BLOCK,),
        in_specs=[_full((rows_s, D_MODEL)), _full((POOL_CTX, n_seq, POOL_WIDTH)), state_block,
                  _full((n_seq, MLSTM_WIDTH)), _full((n_seq, LANES))] + weight_specs + mixer_specs
        + [pl.BlockSpec(memory_space=pl.ANY)],
        out_specs=[_full((rows_s, D_MODEL)), _full((POOL_CTX, n_seq, POOL_WIDTH)), state_block,
                   _full((n_seq, MLSTM_WIDTH)), _full((n_seq, LANES))],
        input_output_aliases={5 + len(weight_specs) + len(mixer_specs): 0},
        out_shape=(jax.ShapeDtypeStruct((rows_s + B * S, D_MODEL), F32),
                   jax.ShapeDtypeStruct((POOL_CTX, n_seq, POOL_WIDTH), F32),
                   jax.ShapeDtypeStruct((n_seq, N_HEADS, HEAD_DIM, HEAD_DIM), F32),
                   jax.ShapeDtypeStruct((n_seq, MLSTM_WIDTH), F32),
                   jax.ShapeDtypeStruct((n_seq, LANES), F32)),
        scratch_shapes=[pltpu.VMEM((rows_s, IN_MAIN), F32),
                        pltpu.VMEM((rows_s, GATE_W), F32),
                        pltpu.VMEM((N_HEADS, rows_s, HEAD_DIM), F32),
                        pltpu.VMEM((N_HEADS, rows_s, HEAD_DIM), F32),
                        pltpu.VMEM((N_HEADS, rows_s, HEAD_DIM), F32),
                        pltpu.VMEM((N_HEADS, rows_s // LANES, HEAD_DIM, LANES), BF16),
                        pltpu.VMEM((N_HEADS, rows_s // LANES, LANES, HEAD_DIM), BF16),
                        pltpu.VMEM((n_seq, LANES), F32),
                        pltpu.VMEM((rows_s, D_MODEL), BF16)],
        compiler_params=pltpu.CompilerParams(dimension_semantics=("arbitrary",), vmem_limit_bytes=VMEM_LIMIT),
        name="sample_mixer",
    )(xs_t, ctx_t, state_C[0], n0, m0, n1, win, wg, bg, wpool, pscale, gain, wout, x1_p)

    y_p, y_s = _ffn(x1, n2, wup, wdown, nf)
    y_p = y_p.reshape(B, S, D_MODEL)
    y_s = y_s[:rows_s].reshape(DEC_SEQ, n_seq, D_MODEL).transpose(1, 0, 2)

    n_p = jnp.stack([c_p[:, h, :, HEAD_DIM + h] for h in range(N_HEADS)], axis=1)
    return (y_p, y_s,
            pool_p[:, CTX_PAD - POOL_CTX:, :][None],
            c_p[..., :HEAD_DIM][None],
            n_p[None],
            m_p[:, 0, :N_HEADS][None],
            pool_s.transpose(1, 0, 2)[None],
            c_s[None],
            n_s.reshape(n_seq, N_HEADS, HEAD_DIM)[None],
            m_s[:, :N_HEADS][None])
```

```python
import functools

import jax
import jax.numpy as jnp
from jax import lax
from jax.experimental import pallas as pl
from jax.experimental.pallas import tpu as pltpu

D_MODEL = 1024
N_META = 16
POOL_WIDTH = 512
POOL_WINDOWS = (2, 4, 8, 16)
POOL_GC = 128
POOL_CTX = 15
MLSTM_WIDTH = 512
N_HEADS = 4
HEAD_DIM = 128
D_FF = 4096
EPS = 1e-6
DEC_SEQ = 4

LANES = 128
SUBLANES = 8
COL_Q = POOL_WIDTH
COL_K = COL_Q + MLSTM_WIDTH
COL_V = COL_K + MLSTM_WIDTH
COL_O = COL_V + MLSTM_WIDTH
IN_MAIN = COL_O + MLSTM_WIDTH
GATE_W = 2 * LANES
AUG = 2 * HEAD_DIM
CTX_PAD = 16
CHUNK = 256
PROMPT_TILE = 512
IN_CHUNK = 512
FFN_TILE = 512
FFN_CHUNK = 1024
SAMPLE_BLOCK = 8
NEG_BIG = -1e30
VMEM_LIMIT = 56 * 1024 * 1024

F32 = jnp.float32
BF16 = jnp.bfloat16


def _rmsnorm(x, g):
    return x * lax.rsqrt(jnp.mean(x * x, axis=-1, keepdims=True) + EPS) * g


def _log_sigmoid(x):
    return jnp.minimum(x, 0.0) - jnp.log1p(jnp.exp(-jnp.abs(x)))


def _dot(a, b):
    return jnp.dot(a, b, preferred_element_type=F32)


def _dot_nt(a, b):
    return lax.dot_general(a, b, (((1,), (1,)), ((), ())), preferred_element_type=F32)


def _hcols(base, h):
    return slice(base + h * HEAD_DIM, base + (h + 1) * HEAD_DIM)


def _row_scan(x, op):
    n = x.shape[0]
    row = lax.broadcasted_iota(jnp.int32, x.shape, 0)
    sh = 1
    while sh < n:
        x = jnp.where(row >= sh, op(x, pltpu.roll(x, sh, axis=0)), x)
        sh *= 2
    return x


def _gate_cols(g, m0, n_valid):
    R = g.shape[0]
    b = _row_scan(_log_sigmoid(g[:, LANES:]), jnp.add)
    r = g[:, :LANES] - b
    big_m = jnp.maximum(m0, _row_scan(r, jnp.maximum))
    m = b + big_m
    last = n_valid - 1
    m_top = big_m[last:last + 1, :]
    ws = jnp.exp(r - m_top)
    if n_valid < R:
        ws = jnp.where(lax.broadcasted_iota(jnp.int32, ws.shape, 0) < n_valid, ws, 0.0)
    return dict(r=r, big_m=big_m, inter=jnp.exp(m0 - big_m), eneg=jnp.exp(-m), ws=ws,
                decay=jnp.exp(m0 - m_top), m_last=m[last:last + 1, :])


def _value_aug(v, h):
    lane = lax.broadcasted_iota(jnp.int32, (v.shape[0], HEAD_DIM), 1)
    return jnp.concatenate([v, jnp.where(lane == h, 1.0, 0.0)], axis=1).astype(BF16)


def _meta_kernel(x_ref, n1_ref, win_ref, wg_ref, bg_ref, u_ref, c_ref, m_ref):
    xn = _rmsnorm(x_ref[...], n1_ref[...]).astype(BF16)
    gc = _gate_cols(_dot(xn, wg_ref[...]) + bg_ref[...], jnp.zeros((1, LANES), F32), N_META)
    p = _dot(xn, win_ref[:, 0:IN_MAIN])
    u_ref[...] = p[0:N_META, 0:POOL_WIDTH]
    for h in range(N_HEADS):
        kw_t = (p[:, _hcols(COL_K, h)] * gc["ws"][:, h:h + 1]).T.astype(BF16)
        c_ref[h] = _dot(kw_t, _value_aug(p[:, _hcols(COL_V, h)], h))
    m_ref[...] = jnp.broadcast_to(gc["m_last"], (SUBLANES, LANES))


def _prompt_kernel(nt, xa_ref, umeta_ref, cmeta_ref, mmeta_ref, n1_ref, win_ref, wg_ref, bg_ref,
                   wpool_ref, pscale_ref, gain_ref, wout_ref,
                   x1_ref, pool_ref, c_ref, m_ref,
                   uext_ref, xn_ref, p0_ref, p1_ref, g0_ref, g1_ref, mix0_ref, mix1_ref, xr0_ref, xr1_ref):
    s = pl.program_id(0)
    n_tiles = pl.num_programs(0) - 2

    @pl.when((jnp.maximum(s - 1, 0) % nt == 0) & (s <= n_tiles))
    def _():
        c_ref[0] = cmeta_ref[...]
        m_ref[0] = mmeta_ref[...]
        uext_ref[0:CTX_PAD, :] = umeta_ref[...]

    stage = functools.partial(_prompt_step, xa_ref, n1_ref, win_ref, wg_ref, bg_ref, wpool_ref,
                              pscale_ref, gain_ref, wout_ref, x1_ref, pool_ref, c_ref, m_ref, uext_ref, xn_ref)
    even = (p0_ref, p1_ref, g0_ref, g1_ref, mix0_ref, mix1_ref, xr0_ref)
    odd = (p1_ref, p0_ref, g1_ref, g0_ref, mix1_ref, mix0_ref, xr1_ref)
    steady = (s >= 2) & (s < n_tiles)
    pl.when(s == 0)(lambda: stage(*even, project=True, mix=False, emit=False))
    pl.when(s == 1)(lambda: stage(*odd, project=True, mix=True, emit=False))
    pl.when(steady & (s % 2 == 0))(lambda: stage(*even, project=True, mix=True, emit=True))
    pl.when(steady & (s % 2 == 1))(lambda: stage(*odd, project=True, mix=True, emit=True))
    pl.when(s == n_tiles)(lambda: stage(*even, project=False, mix=True, emit=True))
    pl.when(s == n_tiles + 1)(lambda: stage(*odd, project=False, mix=False, emit=True))


def _prompt_step(xa_ref, n1_ref, win_ref, wg_ref, bg_ref, wpool_ref, pscale_ref, gain_ref, wout_ref,
                 x1_ref, pool_ref, c_ref, m_ref, uext_ref, xn_ref,
                 p_new, pb, g_new, g_old, mx, mix_old, x_keep, *, project, mix, emit):
    if emit:
        x1_ref[...] = x_keep[...] + _dot(mix_old[...], wout_ref[...])
    elif not mix:
        x1_ref[...] = jnp.zeros(x1_ref.shape, F32)

    if project:
        x_keep[...] = xa_ref[0]
        xn_ref[...] = _rmsnorm(xa_ref[0], n1_ref[...]).astype(BF16)
        g_new[...] = _dot(xn_ref[...], wg_ref[...]) + bg_ref[...]
        for c in range(IN_MAIN // IN_CHUNK):
            cs = slice(c * IN_CHUNK, (c + 1) * IN_CHUNK)
            p_new[:, cs] = _dot(xn_ref[...], win_ref[:, cs])

    if mix:
        for ci in range(PROMPT_TILE // CHUNK):
            _mix_chunk(pb, g_old, mx, slice(ci * CHUNK, (ci + 1) * CHUNK),
                       wpool_ref, pscale_ref, gain_ref, pool_ref, c_ref, m_ref, uext_ref)


def _mix_chunk(pb, g_old, mx, rows, wpool_ref, pscale_ref, gain_ref, pool_ref, c_ref, m_ref, uext_ref):
    R = CHUNK
    gc = _gate_cols(g_old[rows, :], m_ref[0, 0:1, :], R)
    r_t = gc["r"].T

    uext_ref[CTX_PAD:CTX_PAD + R, :] = pb[rows, 0:POOL_WIDTH]
    for gi, w in enumerate(POOL_WINDOWS):
        cs = slice(gi * POOL_GC, (gi + 1) * POOL_GC)
        acc = uext_ref[:, cs]
        sh = 1
        while sh < w:
            acc = acc + pltpu.roll(acc, sh, axis=0)
            sh *= 2
        z = acc[CTX_PAD:, :] * (1.0 / w) - uext_ref[CTX_PAD:CTX_PAD + R, cs]
        mx[rows, cs] = (_dot(z.astype(BF16), wpool_ref[gi]) * pscale_ref[:, cs]).astype(BF16)
    tail = uext_ref[R:R + CTX_PAD, :]
    uext_ref[0:CTX_PAD, :] = tail
    pool_ref[0] = tail

    causal = (lax.broadcasted_iota(jnp.int32, (R, R), 0) >= lax.broadcasted_iota(jnp.int32, (R, R), 1))
    for h in range(N_HEADS):
        q = (pb[rows, _hcols(COL_Q, h)] * (HEAD_DIM ** -0.5)).astype(BF16)
        k = pb[rows, _hcols(COL_K, h)]
        vaug = _value_aug(pb[rows, _hcols(COL_V, h)], h)
        sc = _dot_nt(q, k.astype(BF16))
        e = jnp.exp(jnp.where(causal, r_t[h:h + 1, :] - gc["big_m"][:, h:h + 1], NEG_BIG))
        c0 = c_ref[0, h]
        num = gc["inter"][:, h:h + 1] * _dot(q, c0.astype(BF16)) + _dot((sc * e).astype(BF16), vaug)
        inv = 1.0 / jnp.maximum(jnp.abs(num[:, HEAD_DIM:]), gc["eneg"])
        hh = num[:, 0:HEAD_DIM] * inv[:, h:h + 1]
        hh = hh * lax.rsqrt(jnp.mean(hh * hh, axis=-1, keepdims=True) + EPS) * gain_ref[:, _hcols(0, h)]
        mx[rows, _hcols(POOL_WIDTH, h)] = (hh * jax.nn.sigmoid(pb[rows, _hcols(COL_O, h)])).astype(BF16)
        kw_t = (k * gc["ws"][:, h:h + 1]).T.astype(BF16)
        c_ref[0, h] = gc["decay"][:, h:h + 1] * c0 + _dot(kw_t, vaug)
    m_ref[0] = jnp.broadcast_to(gc["m_last"], (SUBLANES, LANES))


def _sample_gates(g_ref, m0):
    n = m0.shape[0]
    b = jnp.zeros((n, LANES), F32)
    big_m = m0
    r, big_ms, ms = [], [], []
    for t in range(DEC_SEQ):
        g = g_ref[t * n:(t + 1) * n, :]
        b = b + _log_sigmoid(g[:, LANES:])
        r.append(g[:, :LANES] - b)
        big_m = jnp.maximum(big_m, r[-1])
        big_ms.append(big_m)
        ms.append(b + big_m)
    return dict(r=r, big_m=big_ms, m=ms,
                ws=[jnp.exp(r[t] - big_m) for t in range(DEC_SEQ)],
                decay=jnp.exp(m0 - big_m))


def _sample_kernel(xs_ref, ctx_ref, cin_ref, n0_ref, m0_ref, n1_ref, win_ref, wg_ref, bg_ref, wpool_ref,
                   pscale_ref, gain_ref, wout_ref, x1_alias_ref,
                   x1_ref, pool_ref, cout_ref, nout_ref, mout_ref,
                   p_ref, g_ref, tmp_ref, qsm_ref, qcsm_ref, kwt_ref, vsm_ref, dec_ref, mix_ref):
    i = pl.program_id(0)
    n_seq = m0_ref.shape[0]
    n_grp = DEC_SEQ * n_seq // LANES
    scale = HEAD_DIM ** -0.5

    def seq_major(dst_ref, slab_of_t):
        for t in range(DEC_SEQ):
            slab = slab_of_t(t)
            for h in range(N_HEADS):
                dst_ref[h, pl.ds(t, n_seq, stride=DEC_SEQ), :] = slab[:, _hcols(0, h)]

    @pl.when(i == 0)
    def _():
        xn = _rmsnorm(xs_ref[...], n1_ref[...]).astype(BF16)
        g_ref[...] = _dot(xn, wg_ref[...]) + bg_ref[...]
        p_ref[...] = _dot(xn, win_ref[...])

        def urow(e, cs):
            if e < POOL_CTX:
                return ctx_ref[e, :, cs]
            return p_ref[(e - POOL_CTX) * n_seq:(e - POOL_CTX + 1) * n_seq, cs]

        for gi, w in enumerate(POOL_WINDOWS):
            cs = slice(gi * POOL_GC, (gi + 1) * POOL_GC)
            for t in range(DEC_SEQ):
                e = POOL_CTX + t
                wsum = urow(e, cs)
                for j in range(1, w):
                    wsum = wsum + urow(e - j, cs)
                z = wsum * (1.0 / w) - urow(e, cs)
                mix_ref[t * n_seq:(t + 1) * n_seq, cs] = (
                    _dot(z.astype(BF16), wpool_ref[gi]) * pscale_ref[:, cs]).astype(BF16)
        for e in range(DEC_SEQ, POOL_CTX + DEC_SEQ):
            pool_ref[e - DEC_SEQ] = urow(e, slice(0, POOL_WIDTH))

        gs = _sample_gates(g_ref, m0_ref[...])
        dec_ref[...] = gs["decay"]
        mout_ref[...] = gs["m"][DEC_SEQ - 1]
        def kw_slab(t):
            rows = slice(t * n_seq, (t + 1) * n_seq)
            return jnp.concatenate([p_ref[rows, _hcols(COL_K, h)] * gs["ws"][t][:, h:h + 1]
                                    for h in range(N_HEADS)], axis=1)

        kw = [kw_slab(t) for t in range(DEC_SEQ)]
        nout_ref[...] = (jnp.concatenate([jnp.broadcast_to(gs["decay"][:, h:h + 1], (n_seq, HEAD_DIM))
                                          for h in range(N_HEADS)], axis=1) * n0_ref[...]
                         + kw[0] + kw[1] + kw[2] + kw[3])
        seq_major(tmp_ref, lambda t: kw[t])
        for h in range(N_HEADS):
            for gi in range(n_grp):
                kwt_ref[h, gi] = tmp_ref[h, gi * LANES:(gi + 1) * LANES, :].T.astype(BF16)
        seq_major(tmp_ref, lambda t: p_ref[t * n_seq:(t + 1) * n_seq, COL_V:COL_O])
        for h in range(N_HEADS):
            for gi in range(n_grp):
                vsm_ref[h, gi] = tmp_ref[h, gi * LANES:(gi + 1) * LANES, :].astype(BF16)
        seq_major(qsm_ref, lambda t: p_ref[t * n_seq:(t + 1) * n_seq, COL_Q:COL_K] * scale)

    lane_sl = lax.broadcasted_iota(jnp.int32, (HEAD_DIM, LANES), 1) // DEC_SEQ
    first = lax.broadcasted_iota(jnp.int32, (SUBLANES, HEAD_DIM), 0) < DEC_SEQ
    grp = (i * SAMPLE_BLOCK) // (LANES // DEC_SEQ)
    for jp in range(SAMPLE_BLOCK // 2):
        rows = pl.ds(pl.multiple_of(i * (SAMPLE_BLOCK * DEC_SEQ) + jp * SUBLANES, SUBLANES), SUBLANES)
        for h in range(N_HEADS):
            hs = _hcols(0, h)
            q8 = qsm_ref[h, rows, :].astype(BF16)
            readout = []
            for a in range(2):
                j = 2 * jp + a
                seq = i * SAMPLE_BLOCK + j
                c0 = cin_ref[j, h]
                readout.append(_dot(q8, c0.astype(BF16)))
                lhs = jnp.where(lane_sl == seq % (LANES // DEC_SEQ), kwt_ref[h, grp], jnp.zeros((), BF16))
                cout_ref[j, h] = dec_ref[pl.ds(seq, 1), :][:, h:h + 1] * c0 + _dot(lhs, vsm_ref[h, grp])
            qcsm_ref[h, rows, :] = jnp.where(first, readout[0], readout[1])

    @pl.when(i == pl.num_programs(0) - 1)
    def _():
        gs = _sample_gates(g_ref, m0_ref[...])
        for h in range(N_HEADS):
            hs = _hcols(0, h)
            n0 = n0_ref[:, hs]
            qs = [p_ref[t * n_seq:(t + 1) * n_seq, _hcols(COL_Q, h)] * scale for t in range(DEC_SEQ)]
            ks = [p_ref[t * n_seq:(t + 1) * n_seq, _hcols(COL_K, h)] for t in range(DEC_SEQ)]
            for t in range(DEC_SEQ):
                rows = slice(t * n_seq, (t + 1) * n_seq)
                inter = jnp.exp(m0_ref[...] - gs["big_m"][t])[:, h:h + 1]
                num = inter * qcsm_ref[h, pl.ds(t, n_seq, stride=DEC_SEQ), :]
                den = inter * jnp.sum(qs[t] * n0, axis=-1, keepdims=True)
                for s in range(t + 1):
                    pts = (jnp.sum(qs[t] * ks[s], axis=-1, keepdims=True)
                           * jnp.exp(gs["r"][s] - gs["big_m"][t])[:, h:h + 1])
                    num = num + pts * p_ref[s * n_seq:(s + 1) * n_seq, _hcols(COL_V, h)]
                    den = den + pts
                hh = num / jnp.maximum(jnp.abs(den), jnp.exp(-gs["m"][t])[:, h:h + 1])
                hh = hh * lax.rsqrt(jnp.mean(hh * hh, axis=-1, keepdims=True) + EPS) * gain_ref[:, hs]
                mix_ref[rows, _hcols(POOL_WIDTH, h)] = (
                    hh * jax.nn.sigmoid(p_ref[rows, _hcols(COL_O, h)])).astype(BF16)
        x1_ref[...] = xs_ref[...] + _dot(mix_ref[...], wout_ref[...])


def _ffn_kernel(x_ref, n2_ref, wup_ref, wdown_ref, nf_ref, yp_ref, ys_ref):
    x = x_ref[...]
    xn = _rmsnorm(x, n2_ref[...]).astype(BF16)
    acc = x
    for c in range(D_FF // FFN_CHUNK):
        cs = slice(c * FFN_CHUNK, (c + 1) * FFN_CHUNK)
        a = jnp.square(jnp.maximum(_dot(xn, wup_ref[:, cs]), 0.0)).astype(BF16)
        acc = acc + _dot(a, wdown_ref[cs, :])
    y = _rmsnorm(acc, nf_ref[...])
    yp_ref[...] = y
    ys_ref[...] = y


def _full(shape):
    return pl.BlockSpec(shape, lambda *_: (0,) * len(shape))


def _ffn(x, n2, wup, wdown, nf):
    tiles = x.shape[0] // FFN_TILE
    return pl.pallas_call(
        _ffn_kernel,
        grid=(tiles,),
        in_specs=[pl.BlockSpec((FFN_TILE, D_MODEL), lambda i: (i, 0)),
                  _full((1, D_MODEL)), _full((D_MODEL, D_FF)), _full((D_FF, D_MODEL)), _full((1, D_MODEL))],
        out_specs=[pl.BlockSpec((FFN_TILE, D_MODEL), lambda i: (jnp.maximum(i - 1, 0), 0)),
                   pl.BlockSpec((FFN_TILE, D_MODEL), lambda i: (jnp.minimum(i, 1), 0))],
        out_shape=(jax.ShapeDtypeStruct(((tiles - 1) * FFN_TILE, D_MODEL), F32),
                   jax.ShapeDtypeStruct((2 * FFN_TILE, D_MODEL), F32)),
        compiler_params=pltpu.CompilerParams(dimension_semantics=("arbitrary",), vmem_limit_bytes=VMEM_LIMIT),
        name="ffn",
    )(x, n2, wup, wdown, nf)


def kernel(x_prompt, x_sample, state_pool, state_C, state_n, state_m, meta_tokens, norm1, w_in, b_gate,
           w_pool, pool_scale, head_gain, w_out, norm2, w_up, w_down, norm_f):
    B, S, _ = x_prompt.shape
    n_seq = x_sample.shape[0]
    rows_s = DEC_SEQ * n_seq
    assert w_in.shape[0] == 1 and S % PROMPT_TILE == 0 and n_seq % SAMPLE_BLOCK == 0
    assert rows_s == PROMPT_TILE == FFN_TILE

    win = w_in[0].astype(BF16)
    gpad = ((0, 0), (0, LANES - N_HEADS))
    wg = jnp.concatenate([jnp.pad(w_in[0][:, IN_MAIN:IN_MAIN + N_HEADS], gpad),
                          jnp.pad(w_in[0][:, IN_MAIN + N_HEADS:], gpad)], axis=1).astype(BF16)
    bg = jnp.concatenate([jnp.pad(b_gate[0][:N_HEADS], (0, LANES - N_HEADS)),
                          jnp.pad(b_gate[0][N_HEADS:], (0, LANES - N_HEADS))]).reshape(1, GATE_W)
    wpool = w_pool[0].astype(BF16)
    wout = w_out[0].astype(BF16)
    wup = w_up[0].astype(BF16)
    wdown = w_down[0].astype(BF16)
    n1 = norm1[0].reshape(1, D_MODEL)
    n2 = norm2[0].reshape(1, D_MODEL)
    nf = norm_f.reshape(1, D_MODEL)
    pscale = pool_scale[0].reshape(1, POOL_WIDTH)
    gain = head_gain[0].reshape(1, MLSTM_WIDTH)

    weight_specs = [_full((1, D_MODEL)), _full((D_MODEL, IN_MAIN)), _full((D_MODEL, GATE_W)), _full((1, GATE_W))]
    mixer_specs = [_full((len(POOL_WINDOWS), POOL_GC, POOL_GC)), _full((1, POOL_WIDTH)),
                   _full((1, MLSTM_WIDTH)), _full((D_MODEL, D_MODEL))]

    meta_x = jnp.pad(meta_tokens, ((0, LANES - N_META), (0, 0)))
    u_meta, c_meta, m_meta = pl.pallas_call(
        _meta_kernel,
        out_shape=(jax.ShapeDtypeStruct((N_META, POOL_WIDTH), F32),
                   jax.ShapeDtypeStruct((N_HEADS, HEAD_DIM, AUG), F32),
                   jax.ShapeDtypeStruct((SUBLANES, LANES), F32)),
        compiler_params=pltpu.CompilerParams(vmem_limit_bytes=VMEM_LIMIT),
        name="meta_prefix",
    )(meta_x, n1, win, wg, bg)

    nt = S // PROMPT_TILE
    n_tiles = B * nt
    assert n_tiles % 2 == 0 and n_tiles >= 2
    assert D_MODEL % (n_tiles * 2 * SUBLANES) == 0

    def tile_at(lag):
        def index(s):
            j = jnp.clip(s - lag, 0, n_tiles - 1)
            return (j // nt, j % nt, 0)
        return index

    def mixed_seq(ndim):
        return lambda s: (jnp.clip(s - 1, 0, n_tiles - 1) // nt,) + (0,) * (ndim - 1)

    x1_p, pool_p, c_p, m_p = pl.pallas_call(
        functools.partial(_prompt_kernel, nt),
        grid=(n_tiles + 2,),
        in_specs=[pl.BlockSpec((1, PROMPT_TILE, D_MODEL), tile_at(0)),
                  _full((N_META, POOL_WIDTH)), _full((N_HEADS, HEAD_DIM, AUG)), _full((SUBLANES, LANES))]
        + weight_specs + mixer_specs,
        out_specs=[pl.BlockSpec((PROMPT_TILE, D_MODEL), lambda s: (jnp.clip(s - 1, 0, n_tiles), 0)),
                   pl.BlockSpec((1, CTX_PAD, POOL_WIDTH), mixed_seq(3)),
                   pl.BlockSpec((1, N_HEADS, HEAD_DIM, AUG), mixed_seq(4)),
                   pl.BlockSpec((1, SUBLANES, LANES), mixed_seq(3))],
        out_shape=(jax.ShapeDtypeStruct((rows_s + B * S, D_MODEL), F32),
                   jax.ShapeDtypeStruct((B, CTX_PAD, POOL_WIDTH), F32),
                   jax.ShapeDtypeStruct((B, N_HEADS, HEAD_DIM, AUG), F32),
                   jax.ShapeDtypeStruct((B, SUBLANES, LANES), F32)),
        scratch_shapes=[pltpu.VMEM((CTX_PAD + CHUNK, POOL_WIDTH), F32),
                        pltpu.VMEM((PROMPT_TILE, D_MODEL), BF16),
                        pltpu.VMEM((PROMPT_TILE, IN_MAIN), F32), pltpu.VMEM((PROMPT_TILE, IN_MAIN), F32),
                        pltpu.VMEM((PROMPT_TILE, GATE_W), F32), pltpu.VMEM((PROMPT_TILE, GATE_W), F32),
                        pltpu.VMEM((PROMPT_TILE, D_MODEL), BF16), pltpu.VMEM((PROMPT_TILE, D_MODEL), BF16),
                        pltpu.VMEM((PROMPT_TILE, D_MODEL), F32), pltpu.VMEM((PROMPT_TILE, D_MODEL), F32)],
        compiler_params=pltpu.CompilerParams(dimension_semantics=("arbitrary",), vmem_limit_bytes=VMEM_LIMIT),
        name="prompt_mixer",
    )(x_prompt, u_meta, c_meta, m_meta, n1, win, wg, bg, wpool, pscale, gain, wout)

    xs_t = x_sample.transpose(1, 0, 2).reshape(rows_s, D_MODEL)
    ctx_t = state_pool[0].transpose(1, 0, 2)
    n0 = state_n[0].reshape(n_seq, MLSTM_WIDTH)
    m0 = jnp.pad(state_m[0], ((0, 0), (0, LANES - N_HEADS)))
    state_block = pl.BlockSpec((SAMPLE_BLOCK, N_HEADS, HEAD_DIM, HEAD_DIM), lambda i: (i, 0, 0, 0))
    x1, pool_s, c_s, n_s, m_s = pl.pallas_call(
        _sample_kernel,
        grid=(n_seq // SAMPLE_BLOCK,),
        in_specs=[_full((rows_s, D_MODEL)), _full((POOL_CTX, n_seq, POOL_WIDTH)), state_block,
                  _full((n_seq, MLSTM_WIDTH)), _full((n_seq, LANES))] + weight_specs + mixer_specs
        + [pl.BlockSpec(memory_space=pl.ANY)],
        out_specs=[_full((rows_s, D_MODEL)), _full((POOL_CTX, n_seq, POOL_WIDTH)), state_block,
                   _full((n_seq, MLSTM_WIDTH)), _full((n_seq, LANES))],
        input_output_aliases={5 + len(weight_specs) + len(mixer_specs): 0},
        out_shape=(jax.ShapeDtypeStruct((rows_s + B * S, D_MODEL), F32),
                   jax.ShapeDtypeStruct((POOL_CTX, n_seq, POOL_WIDTH), F32),
                   jax.ShapeDtypeStruct((n_seq, N_HEADS, HEAD_DIM, HEAD_DIM), F32),
                   jax.ShapeDtypeStruct((n_seq, MLSTM_WIDTH), F32),
                   jax.ShapeDtypeStruct((n_seq, LANES), F32)),
        scratch_shapes=[pltpu.VMEM((rows_s, IN_MAIN), F32),
                        pltpu.VMEM((rows_s, GATE_W), F32),
                        pltpu.VMEM((N_HEADS, rows_s, HEAD_DIM), F32),
                        pltpu.VMEM((N_HEADS, rows_s, HEAD_DIM), F32),
                        pltpu.VMEM((N_HEADS, rows_s, HEAD_DIM), F32),
                        pltpu.VMEM((N_HEADS, rows_s // LANES, HEAD_DIM, LANES), BF16),
                        pltpu.VMEM((N_HEADS, rows_s // LANES, LANES, HEAD_DIM), BF16),
                        pltpu.VMEM((n_seq, LANES), F32),
                        pltpu.VMEM((rows_s, D_MODEL), BF16)],
        compiler_params=pltpu.CompilerParams(dimension_semantics=("arbitrary",), vmem_limit_bytes=VMEM_LIMIT),
        name="sample_mixer",
    )(xs_t, ctx_t, state_C[0], n0, m0, n1, win, wg, bg, wpool, pscale, gain, wout, x1_p)

    y_p, y_s = _ffn(x1, n2, wup, wdown, nf)
    y_p = y_p.reshape(B, S, D_MODEL)
    y_s = y_s[:rows_s].reshape(DEC_SEQ, n_seq, D_MODEL).transpose(1, 0, 2)

    n_p = jnp.stack([c_p[:, h, :, HEAD_DIM + h] for h in range(N_HEADS)], axis=1)
    return (y_p, y_s,
            pool_p[:, CTX_PAD - POOL_CTX:, :][None],
            c_p[..., :HEAD_DIM][None],
            n_p[None],
            m_p[:, 0, :N_HEADS][None],
            pool_s.transpose(1, 0, 2)[None],
            c_s[None],
            n_s.reshape(n_seq, N_HEADS, HEAD_DIM)[None],
            m_s[:, :N_HEADS][None])
```

```python
import functools

import jax
import jax.numpy as jnp
from jax import lax
from jax.experimental import pallas as pl
from jax.experimental.pallas import tpu as pltpu

D_MODEL = 1024
N_META = 16
POOL_WIDTH = 512
POOL_WINDOWS = (2, 4, 8, 16)
POOL_GC = 128
POOL_CTX = 15
MLSTM_WIDTH = 512
N_HEADS = 4
HEAD_DIM = 128
D_FF = 4096
EPS = 1e-6
DEC_SEQ = 4

LANES = 128
SUBLANES = 8
COL_Q = POOL_WIDTH
COL_K = COL_Q + MLSTM_WIDTH
COL_V = COL_K + MLSTM_WIDTH
COL_O = COL_V + MLSTM_WIDTH
IN_MAIN = COL_O + MLSTM_WIDTH
GATE_W = 2 * LANES
AUG = 2 * HEAD_DIM
CTX_PAD = 16
CHUNK = 256
PROMPT_TILE = 512
IN_CHUNK = 512
FFN_TILE = 1024
FFN_SUB = 256
FFN_CHUNK = 1024
SAMPLE_BLOCK = 8
NEG_BIG = -1e30
VMEM_LIMIT = 56 * 1024 * 1024

F32 = jnp.float32
BF16 = jnp.bfloat16


def _rmsnorm(x, g):
    return x * lax.rsqrt(jnp.mean(x * x, axis=-1, keepdims=True) + EPS) * g


def _log_sigmoid(x):
    return jnp.minimum(x, 0.0) - jnp.log1p(jnp.exp(-jnp.abs(x)))


def _dot(a, b):
    return jnp.dot(a, b, preferred_element_type=F32)


def _dot_nt(a, b):
    return lax.dot_general(a, b, (((1,), (1,)), ((), ())), preferred_element_type=F32)


def _hcols(base, h):
    return slice(base + h * HEAD_DIM, base + (h + 1) * HEAD_DIM)


def _row_scan(x, op):
    n = x.shape[0]
    row = lax.broadcasted_iota(jnp.int32, x.shape, 0)
    sh = 1
    while sh < n:
        x = jnp.where(row >= sh, op(x, pltpu.roll(x, sh, axis=0)), x)
        sh *= 2
    return x


def _gate_cols(g, m0, n_valid):
    R = g.shape[0]
    b = _row_scan(_log_sigmoid(g[:, LANES:]), jnp.add)
    r = g[:, :LANES] - b
    big_m = jnp.maximum(m0, _row_scan(r, jnp.maximum))
    m = b + big_m
    last = n_valid - 1
    m_top = big_m[last:last + 1, :]
    ws = jnp.exp(r - m_top)
    if n_valid < R:
        ws = jnp.where(lax.broadcasted_iota(jnp.int32, ws.shape, 0) < n_valid, ws, 0.0)
    return dict(r=r, big_m=big_m, inter=jnp.exp(m0 - big_m), eneg=jnp.exp(-m), ws=ws,
                decay=jnp.exp(m0 - m_top), m_last=m[last:last + 1, :])


def _value_aug(v, h):
    lane = lax.broadcasted_iota(jnp.int32, (v.shape[0], HEAD_DIM), 1)
    return jnp.concatenate([v, jnp.where(lane == h, 1.0, 0.0)], axis=1).astype(BF16)


def _meta_kernel(x_ref, n1_ref, win_ref, wg_ref, bg_ref, u_ref, c_ref, m_ref):
    xn = _rmsnorm(x_ref[...], n1_ref[...]).astype(BF16)
    gc = _gate_cols(_dot(xn, wg_ref[...]) + bg_ref[...], jnp.zeros((1, LANES), F32), N_META)
    p = _dot(xn, win_ref[:, 0:IN_MAIN])
    u_ref[...] = p[0:N_META, 0:POOL_WIDTH]
    for h in range(N_HEADS):
        kw_t = (p[:, _hcols(COL_K, h)] * gc["ws"][:, h:h + 1]).T.astype(BF16)
        c_ref[h] = _dot(kw_t, _value_aug(p[:, _hcols(COL_V, h)], h))
    m_ref[...] = jnp.broadcast_to(gc["m_last"], (SUBLANES, LANES))


def _prompt_kernel(nt, xa_ref, xc_ref, umeta_ref, cmeta_ref, mmeta_ref, n1_ref, win_ref, wg_ref, bg_ref,
                   wpool_ref, pscale_ref, gain_ref, wout_ref,
                   x1_ref, pool_ref, c_ref, m_ref,
                   uext_ref, xn_ref, p0_ref, p1_ref, g0_ref, g1_ref, mix0_ref, mix1_ref):
    s = pl.program_id(0)
    n_tiles = pl.num_programs(0) - 2

    @pl.when((jnp.maximum(s - 1, 0) % nt == 0) & (s <= n_tiles))
    def _():
        c_ref[0] = cmeta_ref[...]
        m_ref[0] = mmeta_ref[...]
        uext_ref[0:CTX_PAD, :] = umeta_ref[...]

    stage = functools.partial(_prompt_step, xa_ref, xc_ref, n1_ref, win_ref, wg_ref, bg_ref, wpool_ref,
                              pscale_ref, gain_ref, wout_ref, x1_ref, pool_ref, c_ref, m_ref, uext_ref, xn_ref)
    even = (p0_ref, p1_ref, g0_ref, g1_ref, mix0_ref, mix1_ref)
    odd = (p1_ref, p0_ref, g1_ref, g0_ref, mix1_ref, mix0_ref)
    steady = (s >= 2) & (s < n_tiles)
    pl.when(s == 0)(lambda: stage(*even, project=True, mix=False, emit=False))
    pl.when(s == 1)(lambda: stage(*odd, project=True, mix=True, emit=False))
    pl.when(steady & (s % 2 == 0))(lambda: stage(*even, project=True, mix=True, emit=True))
    pl.when(steady & (s % 2 == 1))(lambda: stage(*odd, project=True, mix=True, emit=True))
    pl.when(s == n_tiles)(lambda: stage(*even, project=False, mix=True, emit=True))
    pl.when(s == n_tiles + 1)(lambda: stage(*odd, project=False, mix=False, emit=True))


def _prompt_step(xa_ref, xc_ref, n1_ref, win_ref, wg_ref, bg_ref, wpool_ref, pscale_ref, gain_ref, wout_ref,
                 x1_ref, pool_ref, c_ref, m_ref, uext_ref, xn_ref,
                 p_new, pb, g_new, g_old, mx, mix_old, *, project, mix, emit):
    dense = []
    if emit:
        def out_proj():
            x1_ref[0] = xc_ref[0] + _dot(mix_old[...], wout_ref[...])
        dense.append(out_proj)
    if project:
        def in_norm():
            xn_ref[...] = _rmsnorm(xa_ref[0], n1_ref[...]).astype(BF16)
            g_new[...] = _dot(xn_ref[...], wg_ref[...]) + bg_ref[...]

        def in_chunk(c):
            cs = slice(c * IN_CHUNK, (c + 1) * IN_CHUNK)
            p_new[:, cs] = _dot(xn_ref[...], win_ref[:, cs])
        dense.append(in_norm)
        dense += [functools.partial(in_chunk, c) for c in range(IN_MAIN // IN_CHUNK)]

    pieces = []
    if mix:
        for ci in range(PROMPT_TILE // CHUNK):
            pieces.append(_mix_chunk(pb, g_old, mx, slice(ci * CHUNK, (ci + 1) * CHUNK),
                                     wpool_ref, pscale_ref, gain_ref, pool_ref, c_ref, m_ref, uext_ref))
    n_mix = (1 + N_HEADS) * len(pieces)
    done = 0
    for k, piece in enumerate(p for chunk in pieces for p in chunk):
        while done < len(dense) and done * n_mix < k * len(dense):
            dense[done]()
            done += 1
    for job in dense[done:]:
        job()


def _mix_chunk(pb, g_old, mx, rows, wpool_ref, pscale_ref, gain_ref, pool_ref, c_ref, m_ref, uext_ref):
    R = CHUNK
    gc = _gate_cols(g_old[rows, :], m_ref[0, 0:1, :], R)
    r_t = gc["r"].T

    uext_ref[CTX_PAD:CTX_PAD + R, :] = pb[rows, 0:POOL_WIDTH]
    for gi, w in enumerate(POOL_WINDOWS):
        cs = slice(gi * POOL_GC, (gi + 1) * POOL_GC)
        acc = uext_ref[:, cs]
        sh = 1
        while sh < w:
            acc = acc + pltpu.roll(acc, sh, axis=0)
            sh *= 2
        z = acc[CTX_PAD:, :] * (1.0 / w) - uext_ref[CTX_PAD:CTX_PAD + R, cs]
        mx[rows, cs] = (_dot(z.astype(BF16), wpool_ref[gi]) * pscale_ref[:, cs]).astype(BF16)
    tail = uext_ref[R:R + CTX_PAD, :]
    uext_ref[0:CTX_PAD, :] = tail
    pool_ref[0] = tail
    m_ref[0] = jnp.broadcast_to(gc["m_last"], (SUBLANES, LANES))
    yield

    causal = (lax.broadcasted_iota(jnp.int32, (R, R), 0) >= lax.broadcasted_iota(jnp.int32, (R, R), 1))
    for h in range(N_HEADS):
        q = (pb[rows, _hcols(COL_Q, h)] * (HEAD_DIM ** -0.5)).astype(BF16)
        k = pb[rows, _hcols(COL_K, h)]
        vaug = _value_aug(pb[rows, _hcols(COL_V, h)], h)
        sc = _dot_nt(q, k.astype(BF16))
        e = jnp.exp(jnp.where(causal, r_t[h:h + 1, :] - gc["big_m"][:, h:h + 1], NEG_BIG))
        c0 = c_ref[0, h]
        num = gc["inter"][:, h:h + 1] * _dot(q, c0.astype(BF16)) + _dot((sc * e).astype(BF16), vaug)
        inv = 1.0 / jnp.maximum(jnp.abs(num[:, HEAD_DIM:]), gc["eneg"])
        hh = num[:, 0:HEAD_DIM] * inv[:, h:h + 1]
        hh = hh * lax.rsqrt(jnp.mean(hh * hh, axis=-1, keepdims=True) + EPS) * gain_ref[:, _hcols(0, h)]
        mx[rows, _hcols(POOL_WIDTH, h)] = (hh * jax.nn.sigmoid(pb[rows, _hcols(COL_O, h)])).astype(BF16)
        kw_t = (k * gc["ws"][:, h:h + 1]).T.astype(BF16)
        c_ref[0, h] = gc["decay"][:, h:h + 1] * c0 + _dot(kw_t, vaug)
        yield


def _sample_gates(g_ref, m0):
    n = m0.shape[0]
    b = jnp.zeros((n, LANES), F32)
    big_m = m0
    r, big_ms, ms = [], [], []
    for t in range(DEC_SEQ):
        g = g_ref[t * n:(t + 1) * n, :]
        b = b + _log_sigmoid(g[:, LANES:])
        r.append(g[:, :LANES] - b)
        big_m = jnp.maximum(big_m, r[-1])
        big_ms.append(big_m)
        ms.append(b + big_m)
    return dict(r=r, big_m=big_ms, m=ms,
                ws=[jnp.exp(r[t] - big_m) for t in range(DEC_SEQ)],
                decay=jnp.exp(m0 - big_m))


def _sample_kernel(xs_ref, ctx_ref, cin_ref, n0_ref, m0_ref, n1_ref, win_ref, wg_ref, bg_ref, wpool_ref,
                   pscale_ref, gain_ref, wout_ref,
                   x1_ref, pool_ref, cout_ref, nout_ref, mout_ref,
                   p_ref, g_ref, tmp_ref, qsm_ref, qcsm_ref, kwt_ref, vsm_ref, dec_ref, mix_ref):
    i = pl.program_id(0)
    n_seq = m0_ref.shape[0]
    n_grp = DEC_SEQ * n_seq // LANES
    scale = HEAD_DIM ** -0.5

    def seq_major(dst_ref, slab_of_t):
        for t in range(DEC_SEQ):
            slab = slab_of_t(t)
            for h in range(N_HEADS):
                dst_ref[h, pl.ds(t, n_seq, stride=DEC_SEQ), :] = slab[:, _hcols(0, h)]

    @pl.when(i == 0)
    def _():
        xn = _rmsnorm(xs_ref[...], n1_ref[...]).astype(BF16)
        g_ref[...] = _dot(xn, wg_ref[...]) + bg_ref[...]
        p_ref[...] = _dot(xn, win_ref[...])

        def urow(e, cs):
            if e < POOL_CTX:
                return ctx_ref[e, :, cs]
            return p_ref[(e - POOL_CTX) * n_seq:(e - POOL_CTX + 1) * n_seq, cs]

        for gi, w in enumerate(POOL_WINDOWS):
            cs = slice(gi * POOL_GC, (gi + 1) * POOL_GC)
            for t in range(DEC_SEQ):
                e = POOL_CTX + t
                wsum = urow(e, cs)
                for j in range(1, w):
                    wsum = wsum + urow(e - j, cs)
                z = wsum * (1.0 / w) - urow(e, cs)
                mix_ref[t * n_seq:(t + 1) * n_seq, cs] = (
                    _dot(z.astype(BF16), wpool_ref[gi]) * pscale_ref[:, cs]).astype(BF16)
        for e in range(DEC_SEQ, POOL_CTX + DEC_SEQ):
            pool_ref[e - DEC_SEQ] = urow(e, slice(0, POOL_WIDTH))

        gs = _sample_gates(g_ref, m0_ref[...])
        dec_ref[...] = gs["decay"]
        mout_ref[...] = gs["m"][DEC_SEQ - 1]
        def kw_slab(t):
            rows = slice(t * n_seq, (t + 1) * n_seq)
            return jnp.concatenate([p_ref[rows, _hcols(COL_K, h)] * gs["ws"][t][:, h:h + 1]
                                    for h in range(N_HEADS)], axis=1)

        kw = [kw_slab(t) for t in range(DEC_SEQ)]
        nout_ref[...] = (jnp.concatenate([jnp.broadcast_to(gs["decay"][:, h:h + 1], (n_seq, HEAD_DIM))
                                          for h in range(N_HEADS)], axis=1) * n0_ref[...]
                         + kw[0] + kw[1] + kw[2] + kw[3])
        seq_major(tmp_ref, lambda t: kw[t])
        for h in range(N_HEADS):
            for gi in range(n_grp):
                kwt_ref[h, gi] = tmp_ref[h, gi * LANES:(gi + 1) * LANES, :].T.astype(BF16)
        seq_major(tmp_ref, lambda t: p_ref[t * n_seq:(t + 1) * n_seq, COL_V:COL_O])
        for h in range(N_HEADS):
            for gi in range(n_grp):
                vsm_ref[h, gi] = tmp_ref[h, gi * LANES:(gi + 1) * LANES, :].astype(BF16)
        seq_major(qsm_ref, lambda t: p_ref[t * n_seq:(t + 1) * n_seq, COL_Q:COL_K] * scale)

    lane_sl = lax.broadcasted_iota(jnp.int32, (HEAD_DIM, LANES), 1) // DEC_SEQ
    first = lax.broadcasted_iota(jnp.int32, (SUBLANES, HEAD_DIM), 0) < DEC_SEQ
    grp = (i * SAMPLE_BLOCK) // (LANES // DEC_SEQ)
    for jp in range(SAMPLE_BLOCK // 2):
        rows = pl.ds(pl.multiple_of(i * (SAMPLE_BLOCK * DEC_SEQ) + jp * SUBLANES, SUBLANES), SUBLANES)
        for h in range(N_HEADS):
            hs = _hcols(0, h)
            q8 = qsm_ref[h, rows, :].astype(BF16)
            readout = []
            for a in range(2):
                j = 2 * jp + a
                seq = i * SAMPLE_BLOCK + j
                c0 = cin_ref[j, h]
                readout.append(_dot(q8, c0.astype(BF16)))
                lhs = jnp.where(lane_sl == seq % (LANES // DEC_SEQ), kwt_ref[h, grp], jnp.zeros((), BF16))
                cout_ref[j, h] = dec_ref[pl.ds(seq, 1), :][:, h:h + 1] * c0 + _dot(lhs, vsm_ref[h, grp])
            qcsm_ref[h, rows, :] = jnp.where(first, readout[0], readout[1])

    @pl.when(i == pl.num_programs(0) - 1)
    def _():
        gs = _sample_gates(g_ref, m0_ref[...])
        for h in range(N_HEADS):
            hs = _hcols(0, h)
            n0 = n0_ref[:, hs]
            qs = [p_ref[t * n_seq:(t + 1) * n_seq, _hcols(COL_Q, h)] * scale for t in range(DEC_SEQ)]
            ks = [p_ref[t * n_seq:(t + 1) * n_seq, _hcols(COL_K, h)] for t in range(DEC_SEQ)]
            for t in range(DEC_SEQ):
                rows = slice(t * n_seq, (t + 1) * n_seq)
                inter = jnp.exp(m0_ref[...] - gs["big_m"][t])[:, h:h + 1]
                num = inter * qcsm_ref[h, pl.ds(t, n_seq, stride=DEC_SEQ), :]
                den = inter * jnp.sum(qs[t] * n0, axis=-1, keepdims=True)
                for s in range(t + 1):
                    pts = (jnp.sum(qs[t] * ks[s], axis=-1, keepdims=True)
                           * jnp.exp(gs["r"][s] - gs["big_m"][t])[:, h:h + 1])
                    num = num + pts * p_ref[s * n_seq:(s + 1) * n_seq, _hcols(COL_V, h)]
                    den = den + pts
                hh = num / jnp.maximum(jnp.abs(den), jnp.exp(-gs["m"][t])[:, h:h + 1])
                hh = hh * lax.rsqrt(jnp.mean(hh * hh, axis=-1, keepdims=True) + EPS) * gain_ref[:, hs]
                mix_ref[rows, _hcols(POOL_WIDTH, h)] = (
                    hh * jax.nn.sigmoid(p_ref[rows, _hcols(COL_O, h)])).astype(BF16)
        x1_ref[...] = xs_ref[...] + _dot(mix_ref[...], wout_ref[...])


def _ffn_kernel(x_ref, n2_ref, wup_ref, wdown_ref, nf_ref, y_ref):
    subs = [slice(j * FFN_SUB, (j + 1) * FFN_SUB) for j in range(x_ref.shape[0] // FFN_SUB)]
    xn = [_rmsnorm(x_ref[rows, :], n2_ref[...]).astype(BF16) for rows in subs]
    acc = [x_ref[rows, :] for rows in subs]
    for c in range(D_FF // FFN_CHUNK):
        cs = slice(c * FFN_CHUNK, (c + 1) * FFN_CHUNK)
        for j in range(len(subs)):
            a = jnp.square(jnp.maximum(_dot(xn[j], wup_ref[:, cs]), 0.0)).astype(BF16)
            acc[j] = acc[j] + _dot(a, wdown_ref[cs, :])
    for j, rows in enumerate(subs):
        y_ref[rows, :] = _rmsnorm(acc[j], nf_ref[...])


def _full(shape):
    return pl.BlockSpec(shape, lambda *_: (0,) * len(shape))


def _ffn(x, n2, wup, wdown, nf):
    rows = x.shape[0]
    tile = min(FFN_TILE, rows)
    return pl.pallas_call(
        _ffn_kernel,
        grid=(rows // tile,),
        in_specs=[pl.BlockSpec((tile, D_MODEL), lambda i: (i, 0)),
                  _full((1, D_MODEL)), _full((D_MODEL, D_FF)), _full((D_FF, D_MODEL)), _full((1, D_MODEL))],
        out_specs=pl.BlockSpec((tile, D_MODEL), lambda i: (i, 0)),
        out_shape=jax.ShapeDtypeStruct((rows, D_MODEL), F32),
        compiler_params=pltpu.CompilerParams(dimension_semantics=("arbitrary",), vmem_limit_bytes=VMEM_LIMIT),
        name="ffn",
    )(x, n2, wup, wdown, nf)


def kernel(x_prompt, x_sample, state_pool, state_C, state_n, state_m, meta_tokens, norm1, w_in, b_gate,
           w_pool, pool_scale, head_gain, w_out, norm2, w_up, w_down, norm_f):
    B, S, _ = x_prompt.shape
    n_seq = x_sample.shape[0]
    rows_s = DEC_SEQ * n_seq
    assert w_in.shape[0] == 1 and S % PROMPT_TILE == 0 and n_seq % SAMPLE_BLOCK == 0
    assert (B * S) % FFN_TILE == 0 and rows_s % FFN_SUB == 0

    win = w_in[0].astype(BF16)
    gpad = ((0, 0), (0, LANES - N_HEADS))
    wg = jnp.concatenate([jnp.pad(w_in[0][:, IN_MAIN:IN_MAIN + N_HEADS], gpad),
                          jnp.pad(w_in[0][:, IN_MAIN + N_HEADS:], gpad)], axis=1).astype(BF16)
    bg = jnp.concatenate([jnp.pad(b_gate[0][:N_HEADS], (0, LANES - N_HEADS)),
                          jnp.pad(b_gate[0][N_HEADS:], (0, LANES - N_HEADS))]).reshape(1, GATE_W)
    wpool = w_pool[0].astype(BF16)
    wout = w_out[0].astype(BF16)
    wup = w_up[0].astype(BF16)
    wdown = w_down[0].astype(BF16)
    n1 = norm1[0].reshape(1, D_MODEL)
    n2 = norm2[0].reshape(1, D_MODEL)
    nf = norm_f.reshape(1, D_MODEL)
    pscale = pool_scale[0].reshape(1, POOL_WIDTH)
    gain = head_gain[0].reshape(1, MLSTM_WIDTH)

    weight_specs = [_full((1, D_MODEL)), _full((D_MODEL, IN_MAIN)), _full((D_MODEL, GATE_W)), _full((1, GATE_W))]
    mixer_specs = [_full((len(POOL_WINDOWS), POOL_GC, POOL_GC)), _full((1, POOL_WIDTH)),
                   _full((1, MLSTM_WIDTH)), _full((D_MODEL, D_MODEL))]

    meta_x = jnp.pad(meta_tokens, ((0, LANES - N_META), (0, 0)))
    u_meta, c_meta, m_meta = pl.pallas_call(
        _meta_kernel,
        out_shape=(jax.ShapeDtypeStruct((N_META, POOL_WIDTH), F32),
                   jax.ShapeDtypeStruct((N_HEADS, HEAD_DIM, AUG), F32),
                   jax.ShapeDtypeStruct((SUBLANES, LANES), F32)),
        compiler_params=pltpu.CompilerParams(vmem_limit_bytes=VMEM_LIMIT),
        name="meta_prefix",
    )(meta_x, n1, win, wg, bg)

    nt = S // PROMPT_TILE
    n_tiles = B * nt
    assert n_tiles % 2 == 0 and n_tiles >= 2
    assert D_MODEL % (n_tiles * 2 * SUBLANES) == 0

    def tile_at(lag):
        def index(s):
            j = jnp.clip(s - lag, 0, n_tiles - 1)
            return (j // nt, j % nt, 0)
        return index

    def mixed_seq(ndim):
        return lambda s: (jnp.clip(s - 1, 0, n_tiles - 1) // nt,) + (0,) * (ndim - 1)

    x1_p, pool_p, c_p, m_p = pl.pallas_call(
        functools.partial(_prompt_kernel, nt),
        grid=(n_tiles + 2,),
        in_specs=[pl.BlockSpec((1, PROMPT_TILE, D_MODEL), tile_at(0)),
                  pl.BlockSpec((1, PROMPT_TILE, D_MODEL), tile_at(2)),
                  _full((N_META, POOL_WIDTH)), _full((N_HEADS, HEAD_DIM, AUG)), _full((SUBLANES, LANES))]
        + weight_specs + mixer_specs,
        out_specs=[pl.BlockSpec((1, PROMPT_TILE, D_MODEL), tile_at(2)),
                   pl.BlockSpec((1, CTX_PAD, POOL_WIDTH), mixed_seq(3)),
                   pl.BlockSpec((1, N_HEADS, HEAD_DIM, AUG), mixed_seq(4)),
                   pl.BlockSpec((1, SUBLANES, LANES), mixed_seq(3))],
        out_shape=(jax.ShapeDtypeStruct((B, S, D_MODEL), F32),
                   jax.ShapeDtypeStruct((B, CTX_PAD, POOL_WIDTH), F32),
                   jax.ShapeDtypeStruct((B, N_HEADS, HEAD_DIM, AUG), F32),
                   jax.ShapeDtypeStruct((B, SUBLANES, LANES), F32)),
        scratch_shapes=[pltpu.VMEM((CTX_PAD + CHUNK, POOL_WIDTH), F32),
                        pltpu.VMEM((PROMPT_TILE, D_MODEL), BF16),
                        pltpu.VMEM((PROMPT_TILE, IN_MAIN), F32), pltpu.VMEM((PROMPT_TILE, IN_MAIN), F32),
                        pltpu.VMEM((PROMPT_TILE, GATE_W), F32), pltpu.VMEM((PROMPT_TILE, GATE_W), F32),
                        pltpu.VMEM((PROMPT_TILE, D_MODEL), BF16), pltpu.VMEM((PROMPT_TILE, D_MODEL), BF16)],
        compiler_params=pltpu.CompilerParams(dimension_semantics=("arbitrary",), vmem_limit_bytes=VMEM_LIMIT),
        name="prompt_mixer",
    )(x_prompt, x_prompt, u_meta, c_meta, m_meta, n1, win, wg, bg, wpool, pscale, gain, wout)

    xs_t = x_sample.transpose(1, 0, 2).reshape(rows_s, D_MODEL)
    ctx_t = state_pool[0].transpose(1, 0, 2)
    n0 = state_n[0].reshape(n_seq, MLSTM_WIDTH)
    m0 = jnp.pad(state_m[0], ((0, 0), (0, LANES - N_HEADS)))
    state_block = pl.BlockSpec((SAMPLE_BLOCK, N_HEADS, HEAD_DIM, HEAD_DIM), lambda i: (i, 0, 0, 0))
    x1_s, pool_s, c_s, n_s, m_s = pl.pallas_call(
        _sample_kernel,
        grid=(n_seq // SAMPLE_BLOCK,),
        in_specs=[_full((rows_s, D_MODEL)), _full((POOL_CTX, n_seq, POOL_WIDTH)), state_block,
                  _full((n_seq, MLSTM_WIDTH)), _full((n_seq, LANES))] + weight_specs + mixer_specs,
        out_specs=[_full((rows_s, D_MODEL)), _full((POOL_CTX, n_seq, POOL_WIDTH)), state_block,
                   _full((n_seq, MLSTM_WIDTH)), _full((n_seq, LANES))],
        out_shape=(jax.ShapeDtypeStruct((rows_s, D_MODEL), F32),
                   jax.ShapeDtypeStruct((POOL_CTX, n_seq, POOL_WIDTH), F32),
                   jax.ShapeDtypeStruct((n_seq, N_HEADS, HEAD_DIM, HEAD_DIM), F32),
                   jax.ShapeDtypeStruct((n_seq, MLSTM_WIDTH), F32),
                   jax.ShapeDtypeStruct((n_seq, LANES), F32)),
        scratch_shapes=[pltpu.VMEM((rows_s, IN_MAIN), F32),
                        pltpu.VMEM((rows_s, GATE_W), F32),
                        pltpu.VMEM((N_HEADS, rows_s, HEAD_DIM), F32),
                        pltpu.VMEM((N_HEADS, rows_s, HEAD_DIM), F32),
                        pltpu.VMEM((N_HEADS, rows_s, HEAD_DIM), F32),
                        pltpu.VMEM((N_HEADS, rows_s // LANES, HEAD_DIM, LANES), BF16),
                        pltpu.VMEM((N_HEADS, rows_s // LANES, LANES, HEAD_DIM), BF16),
                        pltpu.VMEM((n_seq, LANES), F32),
                        pltpu.VMEM((rows_s, D_MODEL), BF16)],
        compiler_params=pltpu.CompilerParams(dimension_semantics=("arbitrary",), vmem_limit_bytes=VMEM_LIMIT),
        name="sample_mixer",
    )(xs_t, ctx_t, state_C[0], n0, m0, n1, win, wg, bg, wpool, pscale, gain, wout)

    y_p = _ffn(x1_p.reshape(B * S, D_MODEL), n2, wup, wdown, nf).reshape(B, S, D_MODEL)
    y_s = _ffn(x1_s, n2, wup, wdown, nf).reshape(DEC_SEQ, n_seq, D_MODEL).transpose(1, 0, 2)

    n_p = jnp.stack([c_p[:, h, :, HEAD_DIM + h] for h in range(N_HEADS)], axis=1)
    return (y_p, y_s,
            pool_p[:, CTX_PAD - POOL_CTX:, :][None],
            c_p[..., :HEAD_DIM][None],
            n_p[None],
            m_p[:, 0, :N_HEADS][None],
            pool_s.transpose(1, 0, 2)[None],
            c_s[None],
            n_s.reshape(n_seq, N_HEADS, HEAD_DIM)[None],
            m_s[:, :N_HEADS][None])
```

```python
import functools

import jax
import jax.numpy as jnp
from jax import lax
from jax.experimental import pallas as pl
from jax.experimental.pallas import tpu as pltpu

D_MODEL = 1024
N_META = 16
POOL_WIDTH = 512
POOL_WINDOWS = (2, 4, 8, 16)
POOL_GC = 128
POOL_CTX = 15
MLSTM_WIDTH = 512
N_HEADS = 4
HEAD_DIM = 128
D_FF = 4096
EPS = 1e-6
DEC_SEQ = 4

LANES = 128
SUBLANES = 8
COL_Q = POOL_WIDTH
COL_K = COL_Q + MLSTM_WIDTH
COL_V = COL_K + MLSTM_WIDTH
COL_O = COL_V + MLSTM_WIDTH
IN_MAIN = COL_O + MLSTM_WIDTH
GATE_W = 2 * LANES
AUG = 2 * HEAD_DIM
CTX_PAD = 16
CHUNK = 256
PROMPT_TILE = 512
IN_CHUNK = 512
FFN_TILE = 512
FFN_SUB = 256
FFN_CHUNK = 1024
SAMPLE_BLOCK = 8
NEG_BIG = -1e30
VMEM_LIMIT = 56 * 1024 * 1024

F32 = jnp.float32
BF16 = jnp.bfloat16


def _rmsnorm(x, g):
    return x * lax.rsqrt(jnp.mean(x * x, axis=-1, keepdims=True) + EPS) * g


def _log_sigmoid(x):
    return jnp.minimum(x, 0.0) - jnp.log1p(jnp.exp(-jnp.abs(x)))


def _dot(a, b):
    return jnp.dot(a, b, preferred_element_type=F32)


def _dot_nt(a, b):
    return lax.dot_general(a, b, (((1,), (1,)), ((), ())), preferred_element_type=F32)


def _hcols(base, h):
    return slice(base + h * HEAD_DIM, base + (h + 1) * HEAD_DIM)


def _row_scan(x, op):
    n = x.shape[0]
    row = lax.broadcasted_iota(jnp.int32, x.shape, 0)
    sh = 1
    while sh < n:
        x = jnp.where(row >= sh, op(x, pltpu.roll(x, sh, axis=0)), x)
        sh *= 2
    return x


def _gate_cols(g, m0, n_valid):
    R = g.shape[0]
    b = _row_scan(_log_sigmoid(g[:, LANES:]), jnp.add)
    r = g[:, :LANES] - b
    big_m = jnp.maximum(m0, _row_scan(r, jnp.maximum))
    m = b + big_m
    last = n_valid - 1
    m_top = big_m[last:last + 1, :]
    ws = jnp.exp(r - m_top)
    if n_valid < R:
        ws = jnp.where(lax.broadcasted_iota(jnp.int32, ws.shape, 0) < n_valid, ws, 0.0)
    return dict(r=r, big_m=big_m, inter=jnp.exp(m0 - big_m), eneg=jnp.exp(-m), ws=ws,
                decay=jnp.exp(m0 - m_top), m_last=m[last:last + 1, :])


def _value_aug(v, h):
    lane = lax.broadcasted_iota(jnp.int32, (v.shape[0], HEAD_DIM), 1)
    return jnp.concatenate([v, jnp.where(lane == h, 1.0, 0.0)], axis=1).astype(BF16)


def _meta_kernel(x_ref, n1_ref, win_ref, wg_ref, bg_ref, u_ref, c_ref, m_ref):
    xn = _rmsnorm(x_ref[...], n1_ref[...]).astype(BF16)
    gc = _gate_cols(_dot(xn, wg_ref[...]) + bg_ref[...], jnp.zeros((1, LANES), F32), N_META)
    p = _dot(xn, win_ref[:, 0:IN_MAIN])
    u_ref[...] = p[0:N_META, 0:POOL_WIDTH]
    for h in range(N_HEADS):
        kw_t = (p[:, _hcols(COL_K, h)] * gc["ws"][:, h:h + 1]).T.astype(BF16)
        c_ref[h] = _dot(kw_t, _value_aug(p[:, _hcols(COL_V, h)], h))
    m_ref[...] = jnp.broadcast_to(gc["m_last"], (SUBLANES, LANES))


def _prompt_kernel(nt, xa_ref, xc_ref, umeta_ref, cmeta_ref, mmeta_ref, n1_ref, win_ref, wg_ref, bg_ref,
                   wpool_ref, pscale_ref, gain_ref, wout_ref,
                   x1_ref, pool_ref, c_ref, m_ref,
                   uext_ref, xn_ref, p0_ref, p1_ref, g0_ref, g1_ref, mix0_ref, mix1_ref):
    s = pl.program_id(0)
    n_tiles = pl.num_programs(0) - 2

    @pl.when((jnp.maximum(s - 1, 0) % nt == 0) & (s <= n_tiles))
    def _():
        c_ref[0] = cmeta_ref[...]
        m_ref[0] = mmeta_ref[...]
        uext_ref[0:CTX_PAD, :] = umeta_ref[...]

    stage = functools.partial(_prompt_step, xa_ref, xc_ref, n1_ref, win_ref, wg_ref, bg_ref, wpool_ref,
                              pscale_ref, gain_ref, wout_ref, x1_ref, pool_ref, c_ref, m_ref, uext_ref, xn_ref)
    even = (p0_ref, p1_ref, g0_ref, g1_ref, mix0_ref, mix1_ref)
    odd = (p1_ref, p0_ref, g1_ref, g0_ref, mix1_ref, mix0_ref)
    steady = (s >= 2) & (s < n_tiles)
    pl.when(s == 0)(lambda: stage(*even, project=True, mix=False, emit=False))
    pl.when(s == 1)(lambda: stage(*odd, project=True, mix=True, emit=False))
    pl.when(steady & (s % 2 == 0))(lambda: stage(*even, project=True, mix=True, emit=True))
    pl.when(steady & (s % 2 == 1))(lambda: stage(*odd, project=True, mix=True, emit=True))
    pl.when(s == n_tiles)(lambda: stage(*even, project=False, mix=True, emit=True))
    pl.when(s == n_tiles + 1)(lambda: stage(*odd, project=False, mix=False, emit=True))


def _prompt_step(xa_ref, xc_ref, n1_ref, win_ref, wg_ref, bg_ref, wpool_ref, pscale_ref, gain_ref, wout_ref,
                 x1_ref, pool_ref, c_ref, m_ref, uext_ref, xn_ref,
                 p_new, pb, g_new, g_old, mx, mix_old, *, project, mix, emit):
    dense = []
    if emit:
        def out_proj():
            x1_ref[0] = xc_ref[0] + _dot(mix_old[...], wout_ref[...])
        dense.append(out_proj)
    if project:
        def in_norm():
            xn_ref[...] = _rmsnorm(xa_ref[0], n1_ref[...]).astype(BF16)
            g_new[...] = _dot(xn_ref[...], wg_ref[...]) + bg_ref[...]

        def in_chunk(c):
            cs = slice(c * IN_CHUNK, (c + 1) * IN_CHUNK)
            p_new[:, cs] = _dot(xn_ref[...], win_ref[:, cs])
        dense.append(in_norm)
        dense += [functools.partial(in_chunk, c) for c in range(IN_MAIN // IN_CHUNK)]

    pieces = []
    if mix:
        for ci in range(PROMPT_TILE // CHUNK):
            pieces.append(_mix_chunk(pb, g_old, mx, slice(ci * CHUNK, (ci + 1) * CHUNK),
                                     wpool_ref, pscale_ref, gain_ref, pool_ref, c_ref, m_ref, uext_ref))
    n_mix = (1 + N_HEADS) * len(pieces)
    done = 0
    for k, piece in enumerate(p for chunk in pieces for p in chunk):
        while done < len(dense) and done * n_mix < k * len(dense):
            dense[done]()
            done += 1
    for job in dense[done:]:
        job()


def _mix_chunk(pb, g_old, mx, rows, wpool_ref, pscale_ref, gain_ref, pool_ref, c_ref, m_ref, uext_ref):
    R = CHUNK
    gc = _gate_cols(g_old[rows, :], m_ref[0, 0:1, :], R)
    r_t = gc["r"].T

    uext_ref[CTX_PAD:CTX_PAD + R, :] = pb[rows, 0:POOL_WIDTH]
    for gi, w in enumerate(POOL_WINDOWS):
        cs = slice(gi * POOL_GC, (gi + 1) * POOL_GC)
        acc = uext_ref[:, cs]
        sh = 1
        while sh < w:
            acc = acc + pltpu.roll(acc, sh, axis=0)
            sh *= 2
        z = acc[CTX_PAD:, :] * (1.0 / w) - uext_ref[CTX_PAD:CTX_PAD + R, cs]
        mx[rows, cs] = (_dot(z.astype(BF16), wpool_ref[gi]) * pscale_ref[:, cs]).astype(BF16)
    tail = uext_ref[R:R + CTX_PAD, :]
    uext_ref[0:CTX_PAD, :] = tail
    pool_ref[0] = tail
    m_ref[0] = jnp.broadcast_to(gc["m_last"], (SUBLANES, LANES))
    yield

    causal = (lax.broadcasted_iota(jnp.int32, (R, R), 0) >= lax.broadcasted_iota(jnp.int32, (R, R), 1))
    for h in range(N_HEADS):
        q = (pb[rows, _hcols(COL_Q, h)] * (HEAD_DIM ** -0.5)).astype(BF16)
        k = pb[rows, _hcols(COL_K, h)]
        vaug = _value_aug(pb[rows, _hcols(COL_V, h)], h)
        sc = _dot_nt(q, k.astype(BF16))
        e = jnp.exp(jnp.where(causal, r_t[h:h + 1, :] - gc["big_m"][:, h:h + 1], NEG_BIG))
        c0 = c_ref[0, h]
        num = gc["inter"][:, h:h + 1] * _dot(q, c0.astype(BF16)) + _dot((sc * e).astype(BF16), vaug)
        inv = 1.0 / jnp.maximum(jnp.abs(num[:, HEAD_DIM:]), gc["eneg"])
        hh = num[:, 0:HEAD_DIM] * inv[:, h:h + 1]
        hh = hh * lax.rsqrt(jnp.mean(hh * hh, axis=-1, keepdims=True) + EPS) * gain_ref[:, _hcols(0, h)]
        mx[rows, _hcols(POOL_WIDTH, h)] = (hh * jax.nn.sigmoid(pb[rows, _hcols(COL_O, h)])).astype(BF16)
        kw_t = (k * gc["ws"][:, h:h + 1]).T.astype(BF16)
        c_ref[0, h] = gc["decay"][:, h:h + 1] * c0 + _dot(kw_t, vaug)
        yield


def _sample_gates(g_ref, m0):
    n = m0.shape[0]
    b = jnp.zeros((n, LANES), F32)
    big_m = m0
    r, big_ms, ms = [], [], []
    for t in range(DEC_SEQ):
        g = g_ref[t * n:(t + 1) * n, :]
        b = b + _log_sigmoid(g[:, LANES:])
        r.append(g[:, :LANES] - b)
        big_m = jnp.maximum(big_m, r[-1])
        big_ms.append(big_m)
        ms.append(b + big_m)
    return dict(r=r, big_m=big_ms, m=ms,
                ws=[jnp.exp(r[t] - big_m) for t in range(DEC_SEQ)],
                decay=jnp.exp(m0 - big_m))


def _sample_kernel(xs_ref, ctx_ref, cin_ref, n0_ref, m0_ref, n1_ref, win_ref, wg_ref, bg_ref, wpool_ref,
                   pscale_ref, gain_ref, wout_ref,
                   x1_ref, pool_ref, cout_ref, nout_ref, mout_ref,
                   p_ref, g_ref, tmp_ref, qsm_ref, qcsm_ref, kwt_ref, vsm_ref, dec_ref, mix_ref):
    i = pl.program_id(0)
    n_seq = m0_ref.shape[0]
    n_grp = DEC_SEQ * n_seq // LANES
    scale = HEAD_DIM ** -0.5

    def seq_major(dst_ref, slab_of_t):
        for t in range(DEC_SEQ):
            slab = slab_of_t(t)
            for h in range(N_HEADS):
                dst_ref[h, pl.ds(t, n_seq, stride=DEC_SEQ), :] = slab[:, _hcols(0, h)]

    @pl.when(i == 0)
    def _():
        xn = _rmsnorm(xs_ref[...], n1_ref[...]).astype(BF16)
        g_ref[...] = _dot(xn, wg_ref[...]) + bg_ref[...]
        p_ref[...] = _dot(xn, win_ref[...])

        def urow(e, cs):
            if e < POOL_CTX:
                return ctx_ref[e, :, cs]
            return p_ref[(e - POOL_CTX) * n_seq:(e - POOL_CTX + 1) * n_seq, cs]

        for gi, w in enumerate(POOL_WINDOWS):
            cs = slice(gi * POOL_GC, (gi + 1) * POOL_GC)
            for t in range(DEC_SEQ):
                e = POOL_CTX + t
                wsum = urow(e, cs)
                for j in range(1, w):
                    wsum = wsum + urow(e - j, cs)
                z = wsum * (1.0 / w) - urow(e, cs)
                mix_ref[t * n_seq:(t + 1) * n_seq, cs] = (
                    _dot(z.astype(BF16), wpool_ref[gi]) * pscale_ref[:, cs]).astype(BF16)
        for e in range(DEC_SEQ, POOL_CTX + DEC_SEQ):
            pool_ref[e - DEC_SEQ] = urow(e, slice(0, POOL_WIDTH))

        gs = _sample_gates(g_ref, m0_ref[...])
        dec_ref[...] = gs["decay"]
        mout_ref[...] = gs["m"][DEC_SEQ - 1]
        def kw_slab(t):
            rows = slice(t * n_seq, (t + 1) * n_seq)
            return jnp.concatenate([p_ref[rows, _hcols(COL_K, h)] * gs["ws"][t][:, h:h + 1]
                                    for h in range(N_HEADS)], axis=1)

        kw = [kw_slab(t) for t in range(DEC_SEQ)]
        nout_ref[...] = (jnp.concatenate([jnp.broadcast_to(gs["decay"][:, h:h + 1], (n_seq, HEAD_DIM))
                                          for h in range(N_HEADS)], axis=1) * n0_ref[...]
                         + kw[0] + kw[1] + kw[2] + kw[3])
        seq_major(tmp_ref, lambda t: kw[t])
        for h in range(N_HEADS):
            for gi in range(n_grp):
                kwt_ref[h, gi] = tmp_ref[h, gi * LANES:(gi + 1) * LANES, :].T.astype(BF16)
        seq_major(tmp_ref, lambda t: p_ref[t * n_seq:(t + 1) * n_seq, COL_V:COL_O])
        for h in range(N_HEADS):
            for gi in range(n_grp):
                vsm_ref[h, gi] = tmp_ref[h, gi * LANES:(gi + 1) * LANES, :].astype(BF16)
        seq_major(qsm_ref, lambda t: p_ref[t * n_seq:(t + 1) * n_seq, COL_Q:COL_K] * scale)

    lane_sl = lax.broadcasted_iota(jnp.int32, (HEAD_DIM, LANES), 1) // DEC_SEQ
    first = lax.broadcasted_iota(jnp.int32, (SUBLANES, HEAD_DIM), 0) < DEC_SEQ
    grp = (i * SAMPLE_BLOCK) // (LANES // DEC_SEQ)
    for jp in range(SAMPLE_BLOCK // 2):
        rows = pl.ds(pl.multiple_of(i * (SAMPLE_BLOCK * DEC_SEQ) + jp * SUBLANES, SUBLANES), SUBLANES)
        for h in range(N_HEADS):
            hs = _hcols(0, h)
            q8 = qsm_ref[h, rows, :].astype(BF16)
            readout = []
            for a in range(2):
                j = 2 * jp + a
                seq = i * SAMPLE_BLOCK + j
                c0 = cin_ref[j, h]
                readout.append(_dot(q8, c0.astype(BF16)))
                lhs = jnp.where(lane_sl == seq % (LANES // DEC_SEQ), kwt_ref[h, grp], jnp.zeros((), BF16))
                cout_ref[j, h] = dec_ref[pl.ds(seq, 1), :][:, h:h + 1] * c0 + _dot(lhs, vsm_ref[h, grp])
            qcsm_ref[h, rows, :] = jnp.where(first, readout[0], readout[1])

    @pl.when(i == pl.num_programs(0) - 1)
    def _():
        gs = _sample_gates(g_ref, m0_ref[...])
        for h in range(N_HEADS):
            hs = _hcols(0, h)
            n0 = n0_ref[:, hs]
            qs = [p_ref[t * n_seq:(t + 1) * n_seq, _hcols(COL_Q, h)] * scale for t in range(DEC_SEQ)]
            ks = [p_ref[t * n_seq:(t + 1) * n_seq, _hcols(COL_K, h)] for t in range(DEC_SEQ)]
            for t in range(DEC_SEQ):
                rows = slice(t * n_seq, (t + 1) * n_seq)
                inter = jnp.exp(m0_ref[...] - gs["big_m"][t])[:, h:h + 1]
                num = inter * qcsm_ref[h, pl.ds(t, n_seq, stride=DEC_SEQ), :]
                den = inter * jnp.sum(qs[t] * n0, axis=-1, keepdims=True)
                for s in range(t + 1):
                    pts = (jnp.sum(qs[t] * ks[s], axis=-1, keepdims=True)
                           * jnp.exp(gs["r"][s] - gs["big_m"][t])[:, h:h + 1])
                    num = num + pts * p_ref[s * n_seq:(s + 1) * n_seq, _hcols(COL_V, h)]
                    den = den + pts
                hh = num / jnp.maximum(jnp.abs(den), jnp.exp(-gs["m"][t])[:, h:h + 1])
                hh = hh * lax.rsqrt(jnp.mean(hh * hh, axis=-1, keepdims=True) + EPS) * gain_ref[:, hs]
                mix_ref[rows, _hcols(POOL_WIDTH, h)] = (
                    hh * jax.nn.sigmoid(p_ref[rows, _hcols(COL_O, h)])).astype(BF16)
        x1_ref[...] = xs_ref[...] + _dot(mix_ref[...], wout_ref[...])


def _ffn_kernel(xs_ref, xp_ref, n2_ref, nf_ref, wup_hbm, wdown_hbm, yp_ref, ys_ref,
                wup_ref, wdown_ref, stage_ref, sem):
    i = pl.program_id(0)
    last = pl.num_programs(0) - 1

    def fetch(c, up):
        cols = pl.ds(c * FFN_CHUNK, FFN_CHUNK)
        src = wup_hbm.at[:, cols] if up else wdown_hbm.at[cols, :]
        return pltpu.make_async_copy(src, stage_ref, sem)

    def mlp(x_ref, y_ref, load_weights):
        subs = [slice(j * FFN_SUB, (j + 1) * FFN_SUB) for j in range(x_ref.shape[0] // FFN_SUB)]
        n_chunk = D_FF // FFN_CHUNK
        if load_weights:
            fetch(0, True).start()
        xn = [_rmsnorm(x_ref[rows, :], n2_ref[...]).astype(BF16) for rows in subs]
        acc = [x_ref[rows, :] for rows in subs]
        for c in range(n_chunk):
            if load_weights:
                fetch(c, True).wait()
                wup_ref[c] = stage_ref[...].astype(BF16)
                fetch(c, False).start()
                fetch(c, False).wait()
                wdown_ref[c] = stage_ref[...].astype(BF16)
                if c + 1 < n_chunk:
                    fetch(c + 1, True).start()
            for j in range(len(subs)):
                a = jnp.square(jnp.maximum(_dot(xn[j], wup_ref[c]), 0.0)).astype(BF16)
                acc[j] = acc[j] + _dot(a, wdown_ref[c])
        for j, rows in enumerate(subs):
            y_ref[rows, :] = _rmsnorm(acc[j], nf_ref[...])

    pl.when(i == 0)(lambda: mlp(xp_ref, yp_ref, True))
    pl.when((i > 0) & (i < last))(lambda: mlp(xp_ref, yp_ref, False))
    pl.when(i == last)(lambda: mlp(xs_ref, ys_ref, False))


def _full(shape):
    return pl.BlockSpec(shape, lambda *_: (0,) * len(shape))


def _ffn(xs, xp, n2, nf, w_up, w_down):
    assert D_MODEL == FFN_CHUNK
    n_prompt = xp.shape[0] // FFN_TILE
    n_chunk = D_FF // FFN_CHUNK
    prompt_tile = pl.BlockSpec((FFN_TILE, D_MODEL), lambda i: (jnp.minimum(i, n_prompt - 1), 0))
    return pl.pallas_call(
        _ffn_kernel,
        grid=(n_prompt + 1,),
        in_specs=[_full(xs.shape), prompt_tile, _full((1, D_MODEL)), _full((1, D_MODEL)),
                  pl.BlockSpec(memory_space=pl.ANY), pl.BlockSpec(memory_space=pl.ANY)],
        out_specs=[prompt_tile, _full(xs.shape)],
        out_shape=(jax.ShapeDtypeStruct(xp.shape, F32), jax.ShapeDtypeStruct(xs.shape, F32)),
        scratch_shapes=[pltpu.VMEM((n_chunk, D_MODEL, FFN_CHUNK), BF16),
                        pltpu.VMEM((n_chunk, FFN_CHUNK, D_MODEL), BF16),
                        pltpu.VMEM((D_MODEL, FFN_CHUNK), F32),
                        pltpu.SemaphoreType.DMA(())],
        compiler_params=pltpu.CompilerParams(dimension_semantics=("arbitrary",), vmem_limit_bytes=VMEM_LIMIT),
        name="ffn",
    )(xs, xp, n2, nf, w_up, w_down)


def kernel(x_prompt, x_sample, state_pool, state_C, state_n, state_m, meta_tokens, norm1, w_in, b_gate,
           w_pool, pool_scale, head_gain, w_out, norm2, w_up, w_down, norm_f):
    B, S, _ = x_prompt.shape
    n_seq = x_sample.shape[0]
    rows_s = DEC_SEQ * n_seq
    assert w_in.shape[0] == 1 and S % PROMPT_TILE == 0 and n_seq % SAMPLE_BLOCK == 0
    assert (B * S) % FFN_TILE == 0 and rows_s % FFN_SUB == 0

    win = w_in[0].astype(BF16)
    gpad = ((0, 0), (0, LANES - N_HEADS))
    wg = jnp.concatenate([jnp.pad(w_in[0][:, IN_MAIN:IN_MAIN + N_HEADS], gpad),
                          jnp.pad(w_in[0][:, IN_MAIN + N_HEADS:], gpad)], axis=1).astype(BF16)
    bg = jnp.concatenate([jnp.pad(b_gate[0][:N_HEADS], (0, LANES - N_HEADS)),
                          jnp.pad(b_gate[0][N_HEADS:], (0, LANES - N_HEADS))]).reshape(1, GATE_W)
    wpool = w_pool[0].astype(BF16)
    wout = w_out[0].astype(BF16)
    n1 = norm1[0].reshape(1, D_MODEL)
    n2 = norm2[0].reshape(1, D_MODEL)
    nf = norm_f.reshape(1, D_MODEL)
    pscale = pool_scale[0].reshape(1, POOL_WIDTH)
    gain = head_gain[0].reshape(1, MLSTM_WIDTH)

    weight_specs = [_full((1, D_MODEL)), _full((D_MODEL, IN_MAIN)), _full((D_MODEL, GATE_W)), _full((1, GATE_W))]
    mixer_specs = [_full((len(POOL_WINDOWS), POOL_GC, POOL_GC)), _full((1, POOL_WIDTH)),
                   _full((1, MLSTM_WIDTH)), _full((D_MODEL, D_MODEL))]

    meta_x = jnp.pad(meta_tokens, ((0, LANES - N_META), (0, 0)))
    u_meta, c_meta, m_meta = pl.pallas_call(
        _meta_kernel,
        out_shape=(jax.ShapeDtypeStruct((N_META, POOL_WIDTH), F32),
                   jax.ShapeDtypeStruct((N_HEADS, HEAD_DIM, AUG), F32),
                   jax.ShapeDtypeStruct((SUBLANES, LANES), F32)),
        compiler_params=pltpu.CompilerParams(vmem_limit_bytes=VMEM_LIMIT),
        name="meta_prefix",
    )(meta_x, n1, win, wg, bg)

    nt = S // PROMPT_TILE
    n_tiles = B * nt
    assert n_tiles % 2 == 0 and n_tiles >= 2
    assert D_MODEL % (n_tiles * 2 * SUBLANES) == 0

    def tile_at(lag):
        def index(s):
            j = jnp.clip(s - lag, 0, n_tiles - 1)
            return (j // nt, j % nt, 0)
        return index

    def mixed_seq(ndim):
        return lambda s: (jnp.clip(s - 1, 0, n_tiles - 1) // nt,) + (0,) * (ndim - 1)

    x1_p, pool_p, c_p, m_p = pl.pallas_call(
        functools.partial(_prompt_kernel, nt),
        grid=(n_tiles + 2,),
        in_specs=[pl.BlockSpec((1, PROMPT_TILE, D_MODEL), tile_at(0)),
                  pl.BlockSpec((1, PROMPT_TILE, D_MODEL), tile_at(2)),
                  _full((N_META, POOL_WIDTH)), _full((N_HEADS, HEAD_DIM, AUG)), _full((SUBLANES, LANES))]
        + weight_specs + mixer_specs,
        out_specs=[pl.BlockSpec((1, PROMPT_TILE, D_MODEL), tile_at(2)),
                   pl.BlockSpec((1, CTX_PAD, POOL_WIDTH), mixed_seq(3)),
                   pl.BlockSpec((1, N_HEADS, HEAD_DIM, AUG), mixed_seq(4)),
                   pl.BlockSpec((1, SUBLANES, LANES), mixed_seq(3))],
        out_shape=(jax.ShapeDtypeStruct((B, S, D_MODEL), F32),
                   jax.ShapeDtypeStruct((B, CTX_PAD, POOL_WIDTH), F32),
                   jax.ShapeDtypeStruct((B, N_HEADS, HEAD_DIM, AUG), F32),
                   jax.ShapeDtypeStruct((B, SUBLANES, LANES), F32)),
        scratch_shapes=[pltpu.VMEM((CTX_PAD + CHUNK, POOL_WIDTH), F32),
                        pltpu.VMEM((PROMPT_TILE, D_MODEL), BF16),
                        pltpu.VMEM((PROMPT_TILE, IN_MAIN), F32), pltpu.VMEM((PROMPT_TILE, IN_MAIN), F32),
                        pltpu.VMEM((PROMPT_TILE, GATE_W), F32), pltpu.VMEM((PROMPT_TILE, GATE_W), F32),
                        pltpu.VMEM((PROMPT_TILE, D_MODEL), BF16), pltpu.VMEM((PROMPT_TILE, D_MODEL), BF16)],
        compiler_params=pltpu.CompilerParams(dimension_semantics=("arbitrary",), vmem_limit_bytes=VMEM_LIMIT),
        name="prompt_mixer",
    )(x_prompt, x_prompt, u_meta, c_meta, m_meta, n1, win, wg, bg, wpool, pscale, gain, wout)

    xs_t = x_sample.transpose(1, 0, 2).reshape(rows_s, D_MODEL)
    ctx_t = state_pool[0].transpose(1, 0, 2)
    n0 = state_n[0].reshape(n_seq, MLSTM_WIDTH)
    m0 = jnp.pad(state_m[0], ((0, 0), (0, LANES - N_HEADS)))
    state_block = pl.BlockSpec((SAMPLE_BLOCK, N_HEADS, HEAD_DIM, HEAD_DIM), lambda i: (i, 0, 0, 0))
    x1_s, pool_s, c_s, n_s, m_s = pl.pallas_call(
        _sample_kernel,
        grid=(n_seq // SAMPLE_BLOCK,),
        in_specs=[_full((rows_s, D_MODEL)), _full((POOL_CTX, n_seq, POOL_WIDTH)), state_block,
                  _full((n_seq, MLSTM_WIDTH)), _full((n_seq, LANES))] + weight_specs + mixer_specs,
        out_specs=[_full((rows_s, D_MODEL)), _full((POOL_CTX, n_seq, POOL_WIDTH)), state_block,
                   _full((n_seq, MLSTM_WIDTH)), _full((n_seq, LANES))],
        out_shape=(jax.ShapeDtypeStruct((rows_s, D_MODEL), F32),
                   jax.ShapeDtypeStruct((POOL_CTX, n_seq, POOL_WIDTH), F32),
                   jax.ShapeDtypeStruct((n_seq, N_HEADS, HEAD_DIM, HEAD_DIM), F32),
                   jax.ShapeDtypeStruct((n_seq, MLSTM_WIDTH), F32),
                   jax.ShapeDtypeStruct((n_seq, LANES), F32)),
        scratch_shapes=[pltpu.VMEM((rows_s, IN_MAIN), F32),
                        pltpu.VMEM((rows_s, GATE_W), F32),
                        pltpu.VMEM((N_HEADS, rows_s, HEAD_DIM), F32),
                        pltpu.VMEM((N_HEADS, rows_s, HEAD_DIM), F32),
                        pltpu.VMEM((N_HEADS, rows_s, HEAD_DIM), F32),
                        pltpu.VMEM((N_HEADS, rows_s // LANES, HEAD_DIM, LANES), BF16),
                        pltpu.VMEM((N_HEADS, rows_s // LANES, LANES, HEAD_DIM), BF16),
                        pltpu.VMEM((n_seq, LANES), F32),
                        pltpu.VMEM((rows_s, D_MODEL), BF16)],
        compiler_params=pltpu.CompilerParams(dimension_semantics=("arbitrary",), vmem_limit_bytes=VMEM_LIMIT),
        name="sample_mixer",
    )(xs_t, ctx_t, state_C[0], n0, m0, n1, win, wg, bg, wpool, pscale, gain, wout)

    y_p, y_s = _ffn(x1_s, x1_p.reshape(B * S, D_MODEL), n2, nf, w_up[0], w_down[0])
    y_p = y_p.reshape(B, S, D_MODEL)
    y_s = y_s.reshape(DEC_SEQ, n_seq, D_MODEL).transpose(1, 0, 2)

    n_p = jnp.stack([c_p[:, h, :, HEAD_DIM + h] for h in range(N_HEADS)], axis=1)
    return (y_p, y_s,
            pool_p[:, CTX_PAD - POOL_CTX:, :][None],
            c_p[..., :HEAD_DIM][None],
            n_p[None],
            m_p[:, 0, :N_HEADS][None],
            pool_s.transpose(1, 0, 2)[None],
            c_s[None],
            n_s.reshape(n_seq, N_HEADS, HEAD_DIM)[None],
            m_s[:, :N_HEADS][None])
```

```python
import functools

import jax
import jax.numpy as jnp
from jax import lax
from jax.experimental import pallas as pl
from jax.experimental.pallas import tpu as pltpu

D_MODEL = 1024
N_META = 16
POOL_WIDTH = 512
POOL_WINDOWS = (2, 4, 8, 16)
POOL_GC = 128
POOL_CTX = 15
MLSTM_WIDTH = 512
N_HEADS = 4
HEAD_DIM = 128
D_FF = 4096
EPS = 1e-6
DEC_SEQ = 4

LANES = 128
SUBLANES = 8
COL_Q = POOL_WIDTH
COL_K = COL_Q + MLSTM_WIDTH
COL_V = COL_K + MLSTM_WIDTH
COL_O = COL_V + MLSTM_WIDTH
IN_MAIN = COL_O + MLSTM_WIDTH
GATE_W = 2 * LANES
AUG = 2 * HEAD_DIM
CTX_PAD = 16
CHUNK = 256
PROMPT_TILE = 512
IN_CHUNK = 512
FFN_TILE = 512
FFN_SUB = 256
FFN_CHUNK = 1024
SAMPLE_BLOCK = 8
NEG_BIG = -1e30
VMEM_LIMIT = 56 * 1024 * 1024

F32 = jnp.float32
BF16 = jnp.bfloat16


def _rmsnorm(x, g):
    return x * lax.rsqrt(jnp.mean(x * x, axis=-1, keepdims=True) + EPS) * g


def _log_sigmoid(x):
    return jnp.minimum(x, 0.0) - jnp.log1p(jnp.exp(-jnp.abs(x)))


def _dot(a, b):
    return jnp.dot(a, b, preferred_element_type=F32)


def _dot_nt(a, b):
    return lax.dot_general(a, b, (((1,), (1,)), ((), ())), preferred_element_type=F32)


def _hcols(base, h):
    return slice(base + h * HEAD_DIM, base + (h + 1) * HEAD_DIM)


def _row_scan(x, op):
    n = x.shape[0]
    row = lax.broadcasted_iota(jnp.int32, x.shape, 0)
    sh = 1
    while sh < n:
        x = jnp.where(row >= sh, op(x, pltpu.roll(x, sh, axis=0)), x)
        sh *= 2
    return x


def _gate_cols(g, m0, n_valid):
    R = g.shape[0]
    b = _row_scan(_log_sigmoid(g[:, LANES:]), jnp.add)
    r = g[:, :LANES] - b
    big_m = jnp.maximum(m0, _row_scan(r, jnp.maximum))
    m = b + big_m
    last = n_valid - 1
    m_top = big_m[last:last + 1, :]
    ws = jnp.exp(r - m_top)
    if n_valid < R:
        ws = jnp.where(lax.broadcasted_iota(jnp.int32, ws.shape, 0) < n_valid, ws, 0.0)
    return dict(r=r, big_m=big_m, inter=jnp.exp(m0 - big_m), eneg=jnp.exp(-m), ws=ws,
                decay=jnp.exp(m0 - m_top), m_last=m[last:last + 1, :])


def _value_aug(v, h):
    lane = lax.broadcasted_iota(jnp.int32, (v.shape[0], HEAD_DIM), 1)
    return jnp.concatenate([v, jnp.where(lane == h, 1.0, 0.0)], axis=1).astype(BF16)


def _meta_kernel(x_ref, n1_ref, win_ref, wg_ref, bg_ref, u_ref, c_ref, m_ref):
    xn = _rmsnorm(x_ref[...], n1_ref[...]).astype(BF16)
    gc = _gate_cols(_dot(xn, wg_ref[...]) + bg_ref[...], jnp.zeros((1, LANES), F32), N_META)
    p = _dot(xn, win_ref[:, 0:IN_MAIN])
    u_ref[...] = p[0:N_META, 0:POOL_WIDTH]
    for h in range(N_HEADS):
        kw_t = (p[:, _hcols(COL_K, h)] * gc["ws"][:, h:h + 1]).T.astype(BF16)
        c_ref[h] = _dot(kw_t, _value_aug(p[:, _hcols(COL_V, h)], h))
    m_ref[...] = jnp.broadcast_to(gc["m_last"], (SUBLANES, LANES))


def _prompt_kernel(nt, xa_ref, xc_ref, umeta_ref, cmeta_ref, mmeta_ref, n1_ref, win_ref, wg_ref, bg_ref,
                   wpool_ref, pscale_ref, gain_ref, wout_ref,
                   x1_ref, pool_ref, c_ref, m_ref,
                   uext_ref, xn_ref, p0_ref, p1_ref, g0_ref, g1_ref, mix0_ref, mix1_ref):
    s = pl.program_id(0)
    n_tiles = pl.num_programs(0) - 2

    @pl.when((jnp.maximum(s - 1, 0) % nt == 0) & (s <= n_tiles))
    def _():
        c_ref[0] = cmeta_ref[...]
        m_ref[0] = mmeta_ref[...]
        uext_ref[0:CTX_PAD, :] = umeta_ref[...]

    stage = functools.partial(_prompt_step, xa_ref, xc_ref, n1_ref, win_ref, wg_ref, bg_ref, wpool_ref,
                              pscale_ref, gain_ref, wout_ref, x1_ref, pool_ref, c_ref, m_ref, uext_ref, xn_ref)
    even = (p0_ref, p1_ref, g0_ref, g1_ref, mix0_ref, mix1_ref)
    odd = (p1_ref, p0_ref, g1_ref, g0_ref, mix1_ref, mix0_ref)
    steady = (s >= 2) & (s < n_tiles)
    pl.when(s == 0)(lambda: stage(*even, project=True, mix=False, emit=False))
    pl.when(s == 1)(lambda: stage(*odd, project=True, mix=True, emit=False))
    pl.when(steady & (s % 2 == 0))(lambda: stage(*even, project=True, mix=True, emit=True))
    pl.when(steady & (s % 2 == 1))(lambda: stage(*odd, project=True, mix=True, emit=True))
    pl.when(s == n_tiles)(lambda: stage(*even, project=False, mix=True, emit=True))
    pl.when(s == n_tiles + 1)(lambda: stage(*odd, project=False, mix=False, emit=True))


def _prompt_step(xa_ref, xc_ref, n1_ref, win_ref, wg_ref, bg_ref, wpool_ref, pscale_ref, gain_ref, wout_ref,
                 x1_ref, pool_ref, c_ref, m_ref, uext_ref, xn_ref,
                 p_new, pb, g_new, g_old, mx, mix_old, *, project, mix, emit):
    dense = []
    if emit:
        def out_proj():
            x1_ref[0] = xc_ref[0] + _dot(mix_old[...], wout_ref[...])
        dense.append(out_proj)
    if project:
        def in_norm():
            xn_ref[...] = _rmsnorm(xa_ref[0], n1_ref[...]).astype(BF16)
            g_new[...] = _dot(xn_ref[...], wg_ref[...]) + bg_ref[...]

        def in_chunk(c):
            cs = slice(c * IN_CHUNK, (c + 1) * IN_CHUNK)
            p_new[:, cs] = _dot(xn_ref[...], win_ref[:, cs])
        dense.append(in_norm)
        dense += [functools.partial(in_chunk, c) for c in range(IN_MAIN // IN_CHUNK)]

    pieces = []
    if mix:
        for ci in range(PROMPT_TILE // CHUNK):
            pieces.append(_mix_chunk(pb, g_old, mx, slice(ci * CHUNK, (ci + 1) * CHUNK),
                                     wpool_ref, pscale_ref, gain_ref, pool_ref, c_ref, m_ref, uext_ref))
    n_mix = (1 + N_HEADS) * len(pieces)
    done = 0
    for k, piece in enumerate(p for chunk in pieces for p in chunk):
        while done < len(dense) and done * n_mix < k * len(dense):
            dense[done]()
            done += 1
    for job in dense[done:]:
        job()


def _mix_chunk(pb, g_old, mx, rows, wpool_ref, pscale_ref, gain_ref, pool_ref, c_ref, m_ref, uext_ref):
    R = CHUNK
    gc = _gate_cols(g_old[rows, :], m_ref[0, 0:1, :], R)
    r_t = gc["r"].T

    uext_ref[CTX_PAD:CTX_PAD + R, :] = pb[rows, 0:POOL_WIDTH]
    for gi, w in enumerate(POOL_WINDOWS):
        cs = slice(gi * POOL_GC, (gi + 1) * POOL_GC)
        acc = uext_ref[:, cs]
        sh = 1
        while sh < w:
            acc = acc + pltpu.roll(acc, sh, axis=0)
            sh *= 2
        z = acc[CTX_PAD:, :] * (1.0 / w) - uext_ref[CTX_PAD:CTX_PAD + R, cs]
        mx[rows, cs] = (_dot(z.astype(BF16), wpool_ref[gi]) * pscale_ref[:, cs]).astype(BF16)
    tail = uext_ref[R:R + CTX_PAD, :]
    uext_ref[0:CTX_PAD, :] = tail
    pool_ref[0] = tail
    m_ref[0] = jnp.broadcast_to(gc["m_last"], (SUBLANES, LANES))
    yield

    causal = (lax.broadcasted_iota(jnp.int32, (R, R), 0) >= lax.broadcasted_iota(jnp.int32, (R, R), 1))
    for h in range(N_HEADS):
        q = (pb[rows, _hcols(COL_Q, h)] * (HEAD_DIM ** -0.5)).astype(BF16)
        k = pb[rows, _hcols(COL_K, h)]
        vaug = _value_aug(pb[rows, _hcols(COL_V, h)], h)
        sc = _dot_nt(q, k.astype(BF16))
        e = jnp.exp(jnp.where(causal, r_t[h:h + 1, :] - gc["big_m"][:, h:h + 1], NEG_BIG))
        c0 = c_ref[0, h]
        num = gc["inter"][:, h:h + 1] * _dot(q, c0.astype(BF16)) + _dot((sc * e).astype(BF16), vaug)
        inv = 1.0 / jnp.maximum(jnp.abs(num[:, HEAD_DIM:]), gc["eneg"])
        hh = num[:, 0:HEAD_DIM] * inv[:, h:h + 1]
        hh = hh * lax.rsqrt(jnp.mean(hh * hh, axis=-1, keepdims=True) + EPS) * gain_ref[:, _hcols(0, h)]
        mx[rows, _hcols(POOL_WIDTH, h)] = (hh * jax.nn.sigmoid(pb[rows, _hcols(COL_O, h)])).astype(BF16)
        kw_t = (k * gc["ws"][:, h:h + 1]).T.astype(BF16)
        c_ref[0, h] = gc["decay"][:, h:h + 1] * c0 + _dot(kw_t, vaug)
        yield


def _sample_gates(g_ref, m0):
    n = m0.shape[0]
    b = jnp.zeros((n, LANES), F32)
    big_m = m0
    r, big_ms, ms = [], [], []
    for t in range(DEC_SEQ):
        g = g_ref[t * n:(t + 1) * n, :]
        b = b + _log_sigmoid(g[:, LANES:])
        r.append(g[:, :LANES] - b)
        big_m = jnp.maximum(big_m, r[-1])
        big_ms.append(big_m)
        ms.append(b + big_m)
    return dict(r=r, big_m=big_ms, m=ms,
                ws=[jnp.exp(r[t] - big_m) for t in range(DEC_SEQ)],
                decay=jnp.exp(m0 - big_m))


def _sample_kernel(xs_ref, ctx_ref, cin_ref, n0_ref, m0_ref, n1_ref, win_ref, wg_ref, bg_ref, wpool_ref,
                   pscale_ref, gain_ref, wout_ref,
                   x1_ref, pool_ref, cout_ref, nout_ref, mout_ref,
                   p_ref, g_ref, tmp_ref, qsm_ref, qcsm_ref, kwt_ref, vsm_ref, dec_ref, mix_ref):
    i = pl.program_id(0)
    n_seq = m0_ref.shape[0]
    n_grp = DEC_SEQ * n_seq // LANES
    scale = HEAD_DIM ** -0.5

    def seq_major(dst_ref, slab_of_t):
        for t in range(DEC_SEQ):
            slab = slab_of_t(t)
            for h in range(N_HEADS):
                dst_ref[h, pl.ds(t, n_seq, stride=DEC_SEQ), :] = slab[:, _hcols(0, h)]

    @pl.when(i == 0)
    def _():
        xn = _rmsnorm(xs_ref[...], n1_ref[...]).astype(BF16)
        g_ref[...] = _dot(xn, wg_ref[...]) + bg_ref[...]
        p_ref[...] = _dot(xn, win_ref[...])

        def urow(e, cs):
            if e < POOL_CTX:
                return ctx_ref[e, :, cs]
            return p_ref[(e - POOL_CTX) * n_seq:(e - POOL_CTX + 1) * n_seq, cs]

        for gi, w in enumerate(POOL_WINDOWS):
            cs = slice(gi * POOL_GC, (gi + 1) * POOL_GC)
            for t in range(DEC_SEQ):
                e = POOL_CTX + t
                wsum = urow(e, cs)
                for j in range(1, w):
                    wsum = wsum + urow(e - j, cs)
                z = wsum * (1.0 / w) - urow(e, cs)
                mix_ref[t * n_seq:(t + 1) * n_seq, cs] = (
                    _dot(z.astype(BF16), wpool_ref[gi]) * pscale_ref[:, cs]).astype(BF16)
        for e in range(DEC_SEQ, POOL_CTX + DEC_SEQ):
            pool_ref[e - DEC_SEQ] = urow(e, slice(0, POOL_WIDTH))

        gs = _sample_gates(g_ref, m0_ref[...])
        dec_ref[...] = gs["decay"]
        mout_ref[...] = gs["m"][DEC_SEQ - 1]
        def kw_slab(t):
            rows = slice(t * n_seq, (t + 1) * n_seq)
            return jnp.concatenate([p_ref[rows, _hcols(COL_K, h)] * gs["ws"][t][:, h:h + 1]
                                    for h in range(N_HEADS)], axis=1)

        kw = [kw_slab(t) for t in range(DEC_SEQ)]
        nout_ref[...] = (jnp.concatenate([jnp.broadcast_to(gs["decay"][:, h:h + 1], (n_seq, HEAD_DIM))
                                          for h in range(N_HEADS)], axis=1) * n0_ref[...]
                         + kw[0] + kw[1] + kw[2] + kw[3])
        seq_major(tmp_ref, lambda t: kw[t])
        for h in range(N_HEADS):
            for gi in range(n_grp):
                kwt_ref[h, gi] = tmp_ref[h, gi * LANES:(gi + 1) * LANES, :].T.astype(BF16)
        seq_major(tmp_ref, lambda t: p_ref[t * n_seq:(t + 1) * n_seq, COL_V:COL_O])
        for h in range(N_HEADS):
            for gi in range(n_grp):
                vsm_ref[h, gi] = tmp_ref[h, gi * LANES:(gi + 1) * LANES, :].astype(BF16)
        seq_major(qsm_ref, lambda t: p_ref[t * n_seq:(t + 1) * n_seq, COL_Q:COL_K] * scale)

    lane_sl = lax.broadcasted_iota(jnp.int32, (HEAD_DIM, LANES), 1) // DEC_SEQ
    first = lax.broadcasted_iota(jnp.int32, (SUBLANES, HEAD_DIM), 0) < DEC_SEQ
    grp = (i * SAMPLE_BLOCK) // (LANES // DEC_SEQ)
    for jp in range(SAMPLE_BLOCK // 2):
        rows = pl.ds(pl.multiple_of(i * (SAMPLE_BLOCK * DEC_SEQ) + jp * SUBLANES, SUBLANES), SUBLANES)
        for h in range(N_HEADS):
            hs = _hcols(0, h)
            q8 = qsm_ref[h, rows, :].astype(BF16)
            readout = []
            for a in range(2):
                j = 2 * jp + a
                seq = i * SAMPLE_BLOCK + j
                c0 = cin_ref[j, h]
                readout.append(_dot(q8, c0.astype(BF16)))
                lhs = jnp.where(lane_sl == seq % (LANES // DEC_SEQ), kwt_ref[h, grp], jnp.zeros((), BF16))
                cout_ref[j, h] = dec_ref[pl.ds(seq, 1), :][:, h:h + 1] * c0 + _dot(lhs, vsm_ref[h, grp])
            qcsm_ref[h, rows, :] = jnp.where(first, readout[0], readout[1])

    @pl.when(i == pl.num_programs(0) - 1)
    def _():
        gs = _sample_gates(g_ref, m0_ref[...])
        for h in range(N_HEADS):
            hs = _hcols(0, h)
            n0 = n0_ref[:, hs]
            qs = [p_ref[t * n_seq:(t + 1) * n_seq, _hcols(COL_Q, h)] * scale for t in range(DEC_SEQ)]
            ks = [p_ref[t * n_seq:(t + 1) * n_seq, _hcols(COL_K, h)] for t in range(DEC_SEQ)]
            for t in range(DEC_SEQ):
                rows = slice(t * n_seq, (t + 1) * n_seq)
                inter = jnp.exp(m0_ref[...] - gs["big_m"][t])[:, h:h + 1]
                num = inter * qcsm_ref[h, pl.ds(t, n_seq, stride=DEC_SEQ), :]
                den = inter * jnp.sum(qs[t] * n0, axis=-1, keepdims=True)
                for s in range(t + 1):
                    pts = (jnp.sum(qs[t] * ks[s], axis=-1, keepdims=True)
                           * jnp.exp(gs["r"][s] - gs["big_m"][t])[:, h:h + 1])
                    num = num + pts * p_ref[s * n_seq:(s + 1) * n_seq, _hcols(COL_V, h)]
                    den = den + pts
                hh = num / jnp.maximum(jnp.abs(den), jnp.exp(-gs["m"][t])[:, h:h + 1])
                hh = hh * lax.rsqrt(jnp.mean(hh * hh, axis=-1, keepdims=True) + EPS) * gain_ref[:, hs]
                mix_ref[rows, _hcols(POOL_WIDTH, h)] = (
                    hh * jax.nn.sigmoid(p_ref[rows, _hcols(COL_O, h)])).astype(BF16)
        x1_ref[...] = xs_ref[...] + _dot(mix_ref[...], wout_ref[...])


def _ffn_kernel(xs_ref, xp_ref, n2_ref, nf_ref, wup_hbm, wdown_hbm, yp_ref, ys_ref,
                wup_ref, wdown_ref, stage_ref, sem):
    i = pl.program_id(0)
    last = pl.num_programs(0) - 1

    def fetch(c, up):
        cols = pl.ds(c * FFN_CHUNK, FFN_CHUNK)
        src = wup_hbm.at[:, cols] if up else wdown_hbm.at[cols, :]
        return pltpu.make_async_copy(src, stage_ref, sem)

    def mlp(x_ref, y_ref, load_weights):
        subs = [slice(j * FFN_SUB, (j + 1) * FFN_SUB) for j in range(x_ref.shape[0] // FFN_SUB)]
        n_chunk = D_FF // FFN_CHUNK
        if load_weights:
            fetch(0, True).start()
        xn = [_rmsnorm(x_ref[rows, :], n2_ref[...]).astype(BF16) for rows in subs]
        acc = [x_ref[rows, :] for rows in subs]
        for c in range(n_chunk):
            if load_weights:
                fetch(c, True).wait()
                wup_ref[c] = stage_ref[...].astype(BF16)
                fetch(c, False).start()
                fetch(c, False).wait()
                wdown_ref[c] = stage_ref[...].astype(BF16)
                if c + 1 < n_chunk:
                    fetch(c + 1, True).start()
            for j in range(len(subs)):
                a = jnp.square(jnp.maximum(_dot(xn[j], wup_ref[c]), 0.0)).astype(BF16)
                acc[j] = acc[j] + _dot(a, wdown_ref[c])
        for j, rows in enumerate(subs):
            y_ref[rows, :] = _rmsnorm(acc[j], nf_ref[...])

    pl.when(i == 0)(lambda: mlp(xp_ref, yp_ref, True))
    pl.when((i > 0) & (i < last))(lambda: mlp(xp_ref, yp_ref, False))
    pl.when(i == last)(lambda: mlp(xs_ref, ys_ref, False))


def _full(shape):
    return pl.BlockSpec(shape, lambda *_: (0,) * len(shape))


def _ffn(xs, xp, n2, nf, w_up, w_down):
    assert D_MODEL == FFN_CHUNK
    n_prompt = xp.shape[0] // FFN_TILE
    n_chunk = D_FF // FFN_CHUNK
    prompt_tile = pl.BlockSpec((FFN_TILE, D_MODEL), lambda i: (jnp.minimum(i, n_prompt - 1), 0))
    return pl.pallas_call(
        _ffn_kernel,
        grid=(n_prompt + 1,),
        in_specs=[_full(xs.shape), prompt_tile, _full((1, D_MODEL)), _full((1, D_MODEL)),
                  pl.BlockSpec(memory_space=pl.ANY), pl.BlockSpec(memory_space=pl.ANY)],
        out_specs=[prompt_tile, _full(xs.shape)],
        out_shape=(jax.ShapeDtypeStruct(xp.shape, F32), jax.ShapeDtypeStruct(xs.shape, F32)),
        scratch_shapes=[pltpu.VMEM((n_chunk, D_MODEL, FFN_CHUNK), BF16),
                        pltpu.VMEM((n_chunk, FFN_CHUNK, D_MODEL), BF16),
                        pltpu.VMEM((D_MODEL, FFN_CHUNK), F32),
                        pltpu.SemaphoreType.DMA(())],
        compiler_params=pltpu.CompilerParams(dimension_semantics=("arbitrary",), vmem_limit_bytes=VMEM_LIMIT),
        name="ffn",
    )(xs, xp, n2, nf, w_up, w_down)


def kernel(x_prompt, x_sample, state_pool, state_C, state_n, state_m, meta_tokens, norm1, w_in, b_gate,
           w_pool, pool_scale, head_gain, w_out, norm2, w_up, w_down, norm_f):
    B, S, _ = x_prompt.shape
    n_seq = x_sample.shape[0]
    rows_s = DEC_SEQ * n_seq
    assert w_in.shape[0] == 1 and S % PROMPT_TILE == 0 and n_seq % SAMPLE_BLOCK == 0
    assert (B * S) % FFN_TILE == 0 and rows_s % FFN_SUB == 0

    win = w_in[0].astype(BF16)
    gpad = ((0, 0), (0, LANES - N_HEADS))
    wg = jnp.concatenate([jnp.pad(w_in[0][:, IN_MAIN:IN_MAIN + N_HEADS], gpad),
                          jnp.pad(w_in[0][:, IN_MAIN + N_HEADS:], gpad)], axis=1).astype(BF16)
    bg = jnp.concatenate([jnp.pad(b_gate[0][:N_HEADS], (0, LANES - N_HEADS)),
                          jnp.pad(b_gate[0][N_HEADS:], (0, LANES - N_HEADS))]).reshape(1, GATE_W)
    wpool = w_pool[0].astype(BF16)
    wout = w_out[0].astype(BF16)
    n1 = norm1[0].reshape(1, D_MODEL)
    n2 = norm2[0].reshape(1, D_MODEL)
    nf = norm_f.reshape(1, D_MODEL)
    pscale = pool_scale[0].reshape(1, POOL_WIDTH)
    gain = head_gain[0].reshape(1, MLSTM_WIDTH)

    weight_specs = [_full((1, D_MODEL)), _full((D_MODEL, IN_MAIN)), _full((D_MODEL, GATE_W)), _full((1, GATE_W))]
    mixer_specs = [_full((len(POOL_WINDOWS), POOL_GC, POOL_GC)), _full((1, POOL_WIDTH)),
                   _full((1, MLSTM_WIDTH)), _full((D_MODEL, D_MODEL))]

    meta_x = jnp.pad(meta_tokens, ((0, LANES - N_META), (0, 0)))
    u_meta, c_meta, m_meta = pl.pallas_call(
        _meta_kernel,
        out_shape=(jax.ShapeDtypeStruct((N_META, POOL_WIDTH), F32),
                   jax.ShapeDtypeStruct((N_HEADS, HEAD_DIM, AUG), F32),
                   jax.ShapeDtypeStruct((SUBLANES, LANES), F32)),
        compiler_params=pltpu.CompilerParams(vmem_limit_bytes=VMEM_LIMIT),
        name="meta_prefix",
    )(meta_x, n1, win, wg, bg)

    nt = S // PROMPT_TILE
    n_tiles = B * nt
    assert n_tiles % 2 == 0 and n_tiles >= 2
    assert D_MODEL % (n_tiles * 2 * SUBLANES) == 0

    def tile_at(lag):
        def index(s):
            j = jnp.clip(s - lag, 0, n_tiles - 1)
            return (j // nt, j % nt, 0)
        return index

    def mixed_seq(ndim):
        return lambda s: (jnp.clip(s - 1, 0, n_tiles - 1) // nt,) + (0,) * (ndim - 1)

    x1_p, pool_p, c_p, m_p = pl.pallas_call(
        functools.partial(_prompt_kernel, nt),
        grid=(n_tiles + 2,),
        in_specs=[pl.BlockSpec((1, PROMPT_TILE, D_MODEL), tile_at(0)),
                  pl.BlockSpec((1, PROMPT_TILE, D_MODEL), tile_at(2)),
                  _full((N_META, POOL_WIDTH)), _full((N_HEADS, HEAD_DIM, AUG)), _full((SUBLANES, LANES))]
        + weight_specs + mixer_specs,
        out_specs=[pl.BlockSpec((1, PROMPT_TILE, D_MODEL), tile_at(2)),
                   pl.BlockSpec((1, CTX_PAD, POOL_WIDTH), mixed_seq(3)),
                   pl.BlockSpec((1, N_HEADS, HEAD_DIM, AUG), mixed_seq(4)),
                   pl.BlockSpec((1, SUBLANES, LANES), mixed_seq(3))],
        out_shape=(jax.ShapeDtypeStruct((B, S, D_MODEL), F32),
                   jax.ShapeDtypeStruct((B, CTX_PAD, POOL_WIDTH), F32),
                   jax.ShapeDtypeStruct((B, N_HEADS, HEAD_DIM, AUG), F32),
                   jax.ShapeDtypeStruct((B, SUBLANES, LANES), F32)),
        scratch_shapes=[pltpu.VMEM((CTX_PAD + CHUNK, POOL_WIDTH), F32),
                        pltpu.VMEM((PROMPT_TILE, D_MODEL), BF16),
                        pltpu.VMEM((PROMPT_TILE, IN_MAIN), F32), pltpu.VMEM((PROMPT_TILE, IN_MAIN), F32),
                        pltpu.VMEM((PROMPT_TILE, GATE_W), F32), pltpu.VMEM((PROMPT_TILE, GATE_W), F32),
                        pltpu.VMEM((PROMPT_TILE, D_MODEL), BF16), pltpu.VMEM((PROMPT_TILE, D_MODEL), BF16)],
        compiler_params=pltpu.CompilerParams(dimension_semantics=("arbitrary",), vmem_limit_bytes=VMEM_LIMIT),
        name="prompt_mixer",
    )(x_prompt, x_prompt, u_meta, c_meta, m_meta, n1, win, wg, bg, wpool, pscale, gain, wout)

    xs_t = x_sample.transpose(1, 0, 2).reshape(rows_s, D_MODEL)
    x1_p, xs_t = lax.optimization_barrier((x1_p, xs_t))
    ctx_t = state_pool[0].transpose(1, 0, 2)
    n0 = state_n[0].reshape(n_seq, MLSTM_WIDTH)
    m0 = jnp.pad(state_m[0], ((0, 0), (0, LANES - N_HEADS)))
    state_block = pl.BlockSpec((SAMPLE_BLOCK, N_HEADS, HEAD_DIM, HEAD_DIM), lambda i: (i, 0, 0, 0))
    x1_s, pool_s, c_s, n_s, m_s = pl.pallas_call(
        _sample_kernel,
        grid=(n_seq // SAMPLE_BLOCK,),
        in_specs=[_full((rows_s, D_MODEL)), _full((POOL_CTX, n_seq, POOL_WIDTH)), state_block,
                  _full((n_seq, MLSTM_WIDTH)), _full((n_seq, LANES))] + weight_specs + mixer_specs,
        out_specs=[_full((rows_s, D_MODEL)), _full((POOL_CTX, n_seq, POOL_WIDTH)), state_block,
                   _full((n_seq, MLSTM_WIDTH)), _full((n_seq, LANES))],
        out_shape=(jax.ShapeDtypeStruct((rows_s, D_MODEL), F32),
                   jax.ShapeDtypeStruct((POOL_CTX, n_seq, POOL_WIDTH), F32),
                   jax.ShapeDtypeStruct((n_seq, N_HEADS, HEAD_DIM, HEAD_DIM), F32),
                   jax.ShapeDtypeStruct((n_seq, MLSTM_WIDTH), F32),
                   jax.ShapeDtypeStruct((n_seq, LANES), F32)),
        scratch_shapes=[pltpu.VMEM((rows_s, IN_MAIN), F32),
                        pltpu.VMEM((rows_s, GATE_W), F32),
                        pltpu.VMEM((N_HEADS, rows_s, HEAD_DIM), F32),
                        pltpu.VMEM((N_HEADS, rows_s, HEAD_DIM), F32),
                        pltpu.VMEM((N_HEADS, rows_s, HEAD_DIM), F32),
                        pltpu.VMEM((N_HEADS, rows_s // LANES, HEAD_DIM, LANES), BF16),
                        pltpu.VMEM((N_HEADS, rows_s // LANES, LANES, HEAD_DIM), BF16),
                        pltpu.VMEM((n_seq, LANES), F32),
                        pltpu.VMEM((rows_s, D_MODEL), BF16)],
        compiler_params=pltpu.CompilerParams(dimension_semantics=("arbitrary",), vmem_limit_bytes=VMEM_LIMIT),
        name="sample_mixer",
    )(xs_t, ctx_t, state_C[0], n0, m0, n1, win, wg, bg, wpool, pscale, gain, wout)

    y_p, y_s = _ffn(x1_s, x1_p.reshape(B * S, D_MODEL), n2, nf, w_up[0], w_down[0])
    y_p = y_p.reshape(B, S, D_MODEL)
    y_s = y_s.reshape(DEC_SEQ, n_seq, D_MODEL).transpose(1, 0, 2)

    n_p = jnp.stack([c_p[:, h, :, HEAD_DIM + h] for h in range(N_HEADS)], axis=1)
    return (y_p, y_s,
            pool_p[:, CTX_PAD - POOL_CTX:, :][None],
            c_p[..., :HEAD_DIM][None],
            n_p[None],
            m_p[:, 0, :N_HEADS][None],
            pool_s.transpose(1, 0, 2)[None],
            c_s[None],
            n_s.reshape(n_seq, N_HEADS, HEAD_DIM)[None],
            m_s[:, :N_HEADS][None])
```

```python
import functools

import jax
import jax.numpy as jnp
from jax import lax
from jax.experimental import pallas as pl
from jax.experimental.pallas import tpu as pltpu

D_MODEL = 1024
N_META = 16
POOL_WIDTH = 512
POOL_WINDOWS = (2, 4, 8, 16)
POOL_GC = 128
POOL_CTX = 15
MLSTM_WIDTH = 512
N_HEADS = 4
HEAD_DIM = 128
D_FF = 4096
EPS = 1e-6
DEC_SEQ = 4

LANES = 128
SUBLANES = 8
COL_Q = POOL_WIDTH
COL_K = COL_Q + MLSTM_WIDTH
COL_V = COL_K + MLSTM_WIDTH
COL_O = COL_V + MLSTM_WIDTH
IN_MAIN = COL_O + MLSTM_WIDTH
GATE_W = 2 * LANES
AUG = 2 * HEAD_DIM
CTX_PAD = 16
CHUNK = 256
PROMPT_TILE = 512
IN_CHUNK = 512
FFN_TILE = 512
FFN_SUB = 256
FFN_CHUNK = 1024
SAMPLE_BLOCK = 16
NEG_BIG = -1e30
VMEM_LIMIT = 56 * 1024 * 1024

F32 = jnp.float32
BF16 = jnp.bfloat16


def _rmsnorm(x, g):
    return x * lax.rsqrt(jnp.mean(x * x, axis=-1, keepdims=True) + EPS) * g


def _log_sigmoid(x):
    return jnp.minimum(x, 0.0) - jnp.log1p(jnp.exp(-jnp.abs(x)))


def _dot(a, b):
    return jnp.dot(a, b, preferred_element_type=F32)


def _dot_nt(a, b):
    return lax.dot_general(a, b, (((1,), (1,)), ((), ())), preferred_element_type=F32)


def _hcols(base, h):
    return slice(base + h * HEAD_DIM, base + (h + 1) * HEAD_DIM)


def _row_scan(x, op):
    n = x.shape[0]
    row = lax.broadcasted_iota(jnp.int32, x.shape, 0)
    sh = 1
    while sh < n:
        x = jnp.where(row >= sh, op(x, pltpu.roll(x, sh, axis=0)), x)
        sh *= 2
    return x


def _gate_cols(g, m0, n_valid):
    R = g.shape[0]
    b = _row_scan(_log_sigmoid(g[:, LANES:]), jnp.add)
    r = g[:, :LANES] - b
    big_m = jnp.maximum(m0, _row_scan(r, jnp.maximum))
    m = b + big_m
    last = n_valid - 1
    m_top = big_m[last:last + 1, :]
    ws = jnp.exp(r - m_top)
    if n_valid < R:
        ws = jnp.where(lax.broadcasted_iota(jnp.int32, ws.shape, 0) < n_valid, ws, 0.0)
    return dict(r=r, big_m=big_m, inter=jnp.exp(m0 - big_m), eneg=jnp.exp(-m), ws=ws,
                decay=jnp.exp(m0 - m_top), m_last=m[last:last + 1, :])


def _value_aug(v, h):
    lane = lax.broadcasted_iota(jnp.int32, (v.shape[0], HEAD_DIM), 1)
    return jnp.concatenate([v, jnp.where(lane == h, 1.0, 0.0)], axis=1).astype(BF16)


def _meta_kernel(x_ref, n1_ref, win_ref, wg_ref, bg_ref, u_ref, c_ref, m_ref):
    xn = _rmsnorm(x_ref[...], n1_ref[...]).astype(BF16)
    gc = _gate_cols(_dot(xn, wg_ref[...]) + bg_ref[...], jnp.zeros((1, LANES), F32), N_META)
    p = _dot(xn, win_ref[:, 0:IN_MAIN])
    u_ref[...] = p[0:N_META, 0:POOL_WIDTH]
    for h in range(N_HEADS):
        kw_t = (p[:, _hcols(COL_K, h)] * gc["ws"][:, h:h + 1]).T.astype(BF16)
        c_ref[h] = _dot(kw_t, _value_aug(p[:, _hcols(COL_V, h)], h))
    m_ref[...] = jnp.broadcast_to(gc["m_last"], (SUBLANES, LANES))


def _prompt_kernel(nt, xa_ref, xc_ref, umeta_ref, cmeta_ref, mmeta_ref, n1_ref, win_ref, wg_ref, bg_ref,
                   wpool_ref, pscale_ref, gain_ref, wout_ref,
                   x1_ref, pool_ref, c_ref, m_ref,
                   uext_ref, xn_ref, p0_ref, p1_ref, g0_ref, g1_ref, mix0_ref, mix1_ref):
    s = pl.program_id(0)
    n_tiles = pl.num_programs(0) - 2

    @pl.when((jnp.maximum(s - 1, 0) % nt == 0) & (s <= n_tiles))
    def _():
        c_ref[0] = cmeta_ref[...]
        m_ref[0] = mmeta_ref[...]
        uext_ref[0:CTX_PAD, :] = umeta_ref[...]

    stage = functools.partial(_prompt_step, xa_ref, xc_ref, n1_ref, win_ref, wg_ref, bg_ref, wpool_ref,
                              pscale_ref, gain_ref, wout_ref, x1_ref, pool_ref, c_ref, m_ref, uext_ref, xn_ref)
    even = (p0_ref, p1_ref, g0_ref, g1_ref, mix0_ref, mix1_ref)
    odd = (p1_ref, p0_ref, g1_ref, g0_ref, mix1_ref, mix0_ref)
    steady = (s >= 2) & (s < n_tiles)
    pl.when(s == 0)(lambda: stage(*even, project=True, mix=False, emit=False))
    pl.when(s == 1)(lambda: stage(*odd, project=True, mix=True, emit=False))
    pl.when(steady & (s % 2 == 0))(lambda: stage(*even, project=True, mix=True, emit=True))
    pl.when(steady & (s % 2 == 1))(lambda: stage(*odd, project=True, mix=True, emit=True))
    pl.when(s == n_tiles)(lambda: stage(*even, project=False, mix=True, emit=True))
    pl.when(s == n_tiles + 1)(lambda: stage(*odd, project=False, mix=False, emit=True))


def _prompt_step(xa_ref, xc_ref, n1_ref, win_ref, wg_ref, bg_ref, wpool_ref, pscale_ref, gain_ref, wout_ref,
                 x1_ref, pool_ref, c_ref, m_ref, uext_ref, xn_ref,
                 p_new, pb, g_new, g_old, mx, mix_old, *, project, mix, emit):
    dense = []
    if emit:
        def out_proj():
            x1_ref[0] = xc_ref[0] + _dot(mix_old[...], wout_ref[...])
        dense.append(out_proj)
    if project:
        def in_norm():
            xn_ref[...] = _rmsnorm(xa_ref[0], n1_ref[...]).astype(BF16)
            g_new[...] = _dot(xn_ref[...], wg_ref[...]) + bg_ref[...]

        def in_chunk(c):
            cs = slice(c * IN_CHUNK, (c + 1) * IN_CHUNK)
            p_new[:, cs] = _dot(xn_ref[...], win_ref[:, cs])
        dense.append(in_norm)
        dense += [functools.partial(in_chunk, c) for c in range(IN_MAIN // IN_CHUNK)]

    pieces = []
    if mix:
        for ci in range(PROMPT_TILE // CHUNK):
            pieces.append(_mix_chunk(pb, g_old, mx, slice(ci * CHUNK, (ci + 1) * CHUNK),
                                     wpool_ref, pscale_ref, gain_ref, pool_ref, c_ref, m_ref, uext_ref))
    n_mix = (1 + N_HEADS) * len(pieces)
    done = 0
    for k, piece in enumerate(p for chunk in pieces for p in chunk):
        while done < len(dense) and done * n_mix < k * len(dense):
            dense[done]()
            done += 1
    for job in dense[done:]:
        job()


def _mix_chunk(pb, g_old, mx, rows, wpool_ref, pscale_ref, gain_ref, pool_ref, c_ref, m_ref, uext_ref):
    R = CHUNK
    gc = _gate_cols(g_old[rows, :], m_ref[0, 0:1, :], R)
    r_t = gc["r"].T

    uext_ref[CTX_PAD:CTX_PAD + R, :] = pb[rows, 0:POOL_WIDTH]
    for gi, w in enumerate(POOL_WINDOWS):
        cs = slice(gi * POOL_GC, (gi + 1) * POOL_GC)
        acc = uext_ref[:, cs]
        sh = 1
        while sh < w:
            acc = acc + pltpu.roll(acc, sh, axis=0)
            sh *= 2
        z = acc[CTX_PAD:, :] * (1.0 / w) - uext_ref[CTX_PAD:CTX_PAD + R, cs]
        mx[rows, cs] = (_dot(z.astype(BF16), wpool_ref[gi]) * pscale_ref[:, cs]).astype(BF16)
    tail = uext_ref[R:R + CTX_PAD, :]
    uext_ref[0:CTX_PAD, :] = tail
    pool_ref[0] = tail
    m_ref[0] = jnp.broadcast_to(gc["m_last"], (SUBLANES, LANES))
    yield

    causal = (lax.broadcasted_iota(jnp.int32, (R, R), 0) >= lax.broadcasted_iota(jnp.int32, (R, R), 1))
    for h in range(N_HEADS):
        q = (pb[rows, _hcols(COL_Q, h)] * (HEAD_DIM ** -0.5)).astype(BF16)
        k = pb[rows, _hcols(COL_K, h)]
        vaug = _value_aug(pb[rows, _hcols(COL_V, h)], h)
        sc = _dot_nt(q, k.astype(BF16))
        e = jnp.exp(jnp.where(causal, r_t[h:h + 1, :] - gc["big_m"][:, h:h + 1], NEG_BIG))
        c0 = c_ref[0, h]
        num = gc["inter"][:, h:h + 1] * _dot(q, c0.astype(BF16)) + _dot((sc * e).astype(BF16), vaug)
        inv = 1.0 / jnp.maximum(jnp.abs(num[:, HEAD_DIM:]), gc["eneg"])
        hh = num[:, 0:HEAD_DIM] * inv[:, h:h + 1]
        hh = hh * lax.rsqrt(jnp.mean(hh * hh, axis=-1, keepdims=True) + EPS) * gain_ref[:, _hcols(0, h)]
        mx[rows, _hcols(POOL_WIDTH, h)] = (hh * jax.nn.sigmoid(pb[rows, _hcols(COL_O, h)])).astype(BF16)
        kw_t = (k * gc["ws"][:, h:h + 1]).T.astype(BF16)
        c_ref[0, h] = gc["decay"][:, h:h + 1] * c0 + _dot(kw_t, vaug)
        yield


def _sample_gates(g_ref, m0):
    n = m0.shape[0]
    b = jnp.zeros((n, LANES), F32)
    big_m = m0
    r, big_ms, ms = [], [], []
    for t in range(DEC_SEQ):
        g = g_ref[t * n:(t + 1) * n, :]
        b = b + _log_sigmoid(g[:, LANES:])
        r.append(g[:, :LANES] - b)
        big_m = jnp.maximum(big_m, r[-1])
        big_ms.append(big_m)
        ms.append(b + big_m)
    return dict(r=r, big_m=big_ms, m=ms,
                ws=[jnp.exp(r[t] - big_m) for t in range(DEC_SEQ)],
                decay=jnp.exp(m0 - big_m))


def _sample_kernel(xs_ref, ctx_ref, cin_ref, n0_ref, m0_ref, n1_ref, win_ref, wg_ref, bg_ref, wpool_ref,
                   pscale_ref, gain_ref, wout_ref,
                   x1_ref, pool_ref, cout_ref, nout_ref, mout_ref,
                   p_ref, g_ref, tmp_ref, qsm_ref, qcsm_ref, kwt_ref, vsm_ref, dec_ref, mix_ref):
    i = pl.program_id(0)
    n_seq = m0_ref.shape[0]
    n_grp = DEC_SEQ * n_seq // LANES
    scale = HEAD_DIM ** -0.5

    def seq_major(dst_ref, slab_of_t):
        for t in range(DEC_SEQ):
            slab = slab_of_t(t)
            for h in range(N_HEADS):
                dst_ref[h, pl.ds(t, n_seq, stride=DEC_SEQ), :] = slab[:, _hcols(0, h)]

    @pl.when(i == 0)
    def _():
        xn = _rmsnorm(xs_ref[...], n1_ref[...]).astype(BF16)
        g_ref[...] = _dot(xn, wg_ref[...]) + bg_ref[...]
        p_ref[...] = _dot(xn, win_ref[...])

        def urow(e, cs):
            if e < POOL_CTX:
                return ctx_ref[e, :, cs]
            return p_ref[(e - POOL_CTX) * n_seq:(e - POOL_CTX + 1) * n_seq, cs]

        for gi, w in enumerate(POOL_WINDOWS):
            cs = slice(gi * POOL_GC, (gi + 1) * POOL_GC)
            for t in range(DEC_SEQ):
                e = POOL_CTX + t
                wsum = urow(e, cs)
                for j in range(1, w):
                    wsum = wsum + urow(e - j, cs)
                z = wsum * (1.0 / w) - urow(e, cs)
                mix_ref[t * n_seq:(t + 1) * n_seq, cs] = (
                    _dot(z.astype(BF16), wpool_ref[gi]) * pscale_ref[:, cs]).astype(BF16)
        for e in range(DEC_SEQ, POOL_CTX + DEC_SEQ):
            pool_ref[e - DEC_SEQ] = urow(e, slice(0, POOL_WIDTH))

        gs = _sample_gates(g_ref, m0_ref[...])
        dec_ref[...] = gs["decay"]
        mout_ref[...] = gs["m"][DEC_SEQ - 1]
        def kw_slab(t):
            rows = slice(t * n_seq, (t + 1) * n_seq)
            return jnp.concatenate([p_ref[rows, _hcols(COL_K, h)] * gs["ws"][t][:, h:h + 1]
                                    for h in range(N_HEADS)], axis=1)

        kw = [kw_slab(t) for t in range(DEC_SEQ)]
        nout_ref[...] = (jnp.concatenate([jnp.broadcast_to(gs["decay"][:, h:h + 1], (n_seq, HEAD_DIM))
                                          for h in range(N_HEADS)], axis=1) * n0_ref[...]
                         + kw[0] + kw[1] + kw[2] + kw[3])
        seq_major(tmp_ref, lambda t: kw[t])
        for h in range(N_HEADS):
            for gi in range(n_grp):
                kwt_ref[h, gi] = tmp_ref[h, gi * LANES:(gi + 1) * LANES, :].T.astype(BF16)
        seq_major(tmp_ref, lambda t: p_ref[t * n_seq:(t + 1) * n_seq, COL_V:COL_O])
        for h in range(N_HEADS):
            for gi in range(n_grp):
                vsm_ref[h, gi] = tmp_ref[h, gi * LANES:(gi + 1) * LANES, :].astype(BF16)
        seq_major(qsm_ref, lambda t: p_ref[t * n_seq:(t + 1) * n_seq, COL_Q:COL_K] * scale)

    lane_sl = lax.broadcasted_iota(jnp.int32, (HEAD_DIM, LANES), 1) // DEC_SEQ
    first = lax.broadcasted_iota(jnp.int32, (SUBLANES, HEAD_DIM), 0) < DEC_SEQ
    grp = (i * SAMPLE_BLOCK) // (LANES // DEC_SEQ)
    for jp in range(SAMPLE_BLOCK // 2):
        rows = pl.ds(pl.multiple_of(i * (SAMPLE_BLOCK * DEC_SEQ) + jp * SUBLANES, SUBLANES), SUBLANES)
        for h in range(N_HEADS):
            hs = _hcols(0, h)
            q8 = qsm_ref[h, rows, :].astype(BF16)
            readout = []
            for a in range(2):
                j = 2 * jp + a
                seq = i * SAMPLE_BLOCK + j
                c0 = cin_ref[j, h]
                readout.append(_dot(q8, c0.astype(BF16)))
                lhs = jnp.where(lane_sl == seq % (LANES // DEC_SEQ), kwt_ref[h, grp], jnp.zeros((), BF16))
                cout_ref[j, h] = dec_ref[pl.ds(seq, 1), :][:, h:h + 1] * c0 + _dot(lhs, vsm_ref[h, grp])
            qcsm_ref[h, rows, :] = jnp.where(first, readout[0], readout[1])

    @pl.when(i == pl.num_programs(0) - 1)
    def _():
        gs = _sample_gates(g_ref, m0_ref[...])
        for h in range(N_HEADS):
            hs = _hcols(0, h)
            n0 = n0_ref[:, hs]
            qs = [p_ref[t * n_seq:(t + 1) * n_seq, _hcols(COL_Q, h)] * scale for t in range(DEC_SEQ)]
            ks = [p_ref[t * n_seq:(t + 1) * n_seq, _hcols(COL_K, h)] for t in range(DEC_SEQ)]
            for t in range(DEC_SEQ):
                rows = slice(t * n_seq, (t + 1) * n_seq)
                inter = jnp.exp(m0_ref[...] - gs["big_m"][t])[:, h:h + 1]
                num = inter * qcsm_ref[h, pl.ds(t, n_seq, stride=DEC_SEQ), :]
                den = inter * jnp.sum(qs[t] * n0, axis=-1, keepdims=True)
                for s in range(t + 1):
                    pts = (jnp.sum(qs[t] * ks[s], axis=-1, keepdims=True)
                           * jnp.exp(gs["r"][s] - gs["big_m"][t])[:, h:h + 1])
                    num = num + pts * p_ref[s * n_seq:(s + 1) * n_seq, _hcols(COL_V, h)]
                    den = den + pts
                hh = num / jnp.maximum(jnp.abs(den), jnp.exp(-gs["m"][t])[:, h:h + 1])
                hh = hh * lax.rsqrt(jnp.mean(hh * hh, axis=-1, keepdims=True) + EPS) * gain_ref[:, hs]
                mix_ref[rows, _hcols(POOL_WIDTH, h)] = (
                    hh * jax.nn.sigmoid(p_ref[rows, _hcols(COL_O, h)])).astype(BF16)
        x1_ref[...] = xs_ref[...] + _dot(mix_ref[...], wout_ref[...])


def _ffn_kernel(xs_ref, xp_ref, n2_ref, nf_ref, wup_hbm, wdown_hbm, yp_ref, ys_ref,
                wup_ref, wdown_ref, stage_ref, sem):
    i = pl.program_id(0)
    last = pl.num_programs(0) - 1

    def fetch(c, up):
        cols = pl.ds(c * FFN_CHUNK, FFN_CHUNK)
        src = wup_hbm.at[:, cols] if up else wdown_hbm.at[cols, :]
        slot = 0 if up else 1
        return pltpu.make_async_copy(src, stage_ref.at[slot], sem.at[slot])

    def mlp(x_ref, y_ref, load_weights):
        subs = [slice(j * FFN_SUB, (j + 1) * FFN_SUB) for j in range(x_ref.shape[0] // FFN_SUB)]
        n_chunk = D_FF // FFN_CHUNK
        if load_weights:
            fetch(0, True).start()
            fetch(0, False).start()
        xn = [_rmsnorm(x_ref[rows, :], n2_ref[...]).astype(BF16) for rows in subs]
        acc = [x_ref[rows, :] for rows in subs]
        for c in range(n_chunk):
            if load_weights:
                fetch(c, True).wait()
                wup_ref[c] = stage_ref[0].astype(BF16)
                if c + 1 < n_chunk:
                    fetch(c + 1, True).start()
                fetch(c, False).wait()
                wdown_ref[c] = stage_ref[1].astype(BF16)
                if c + 1 < n_chunk:
                    fetch(c + 1, False).start()
            for j in range(len(subs)):
                a = jnp.square(jnp.maximum(_dot(xn[j], wup_ref[c]), 0.0)).astype(BF16)
                acc[j] = acc[j] + _dot(a, wdown_ref[c])
        for j, rows in enumerate(subs):
            y_ref[rows, :] = _rmsnorm(acc[j], nf_ref[...])

    pl.when(i == 0)(lambda: mlp(xp_ref, yp_ref, True))
    pl.when((i > 0) & (i < last))(lambda: mlp(xp_ref, yp_ref, False))
    pl.when(i == last)(lambda: mlp(xs_ref, ys_ref, False))


def _full(shape):
    return pl.BlockSpec(shape, lambda *_: (0,) * len(shape))


def _ffn(xs, xp, n2, nf, w_up, w_down):
    assert D_MODEL == FFN_CHUNK
    n_prompt = xp.shape[0] // FFN_TILE
    n_chunk = D_FF // FFN_CHUNK
    prompt_tile = pl.BlockSpec((FFN_TILE, D_MODEL), lambda i: (jnp.minimum(i, n_prompt - 1), 0))
    return pl.pallas_call(
        _ffn_kernel,
        grid=(n_prompt + 1,),
        in_specs=[_full(xs.shape), prompt_tile, _full((1, D_MODEL)), _full((1, D_MODEL)),
                  pl.BlockSpec(memory_space=pl.ANY), pl.BlockSpec(memory_space=pl.ANY)],
        out_specs=[prompt_tile, _full(xs.shape)],
        out_shape=(jax.ShapeDtypeStruct(xp.shape, F32), jax.ShapeDtypeStruct(xs.shape, F32)),
        scratch_shapes=[pltpu.VMEM((n_chunk, D_MODEL, FFN_CHUNK), BF16),
                        pltpu.VMEM((n_chunk, FFN_CHUNK, D_MODEL), BF16),
                        pltpu.VMEM((2, D_MODEL, FFN_CHUNK), F32),
                        pltpu.SemaphoreType.DMA((2,))],
        compiler_params=pltpu.CompilerParams(dimension_semantics=("arbitrary",), vmem_limit_bytes=VMEM_LIMIT),
        name="ffn",
    )(xs, xp, n2, nf, w_up, w_down)


def kernel(x_prompt, x_sample, state_pool, state_C, state_n, state_m, meta_tokens, norm1, w_in, b_gate,
           w_pool, pool_scale, head_gain, w_out, norm2, w_up, w_down, norm_f):
    B, S, _ = x_prompt.shape
    n_seq = x_sample.shape[0]
    rows_s = DEC_SEQ * n_seq
    assert w_in.shape[0] == 1 and S % PROMPT_TILE == 0 and n_seq % SAMPLE_BLOCK == 0
    assert (B * S) % FFN_TILE == 0 and rows_s % FFN_SUB == 0

    win = w_in[0].astype(BF16)
    gpad = ((0, 0), (0, LANES - N_HEADS))
    wg = jnp.concatenate([jnp.pad(w_in[0][:, IN_MAIN:IN_MAIN + N_HEADS], gpad),
                          jnp.pad(w_in[0][:, IN_MAIN + N_HEADS:], gpad)], axis=1).astype(BF16)
    bg = jnp.concatenate([jnp.pad(b_gate[0][:N_HEADS], (0, LANES - N_HEADS)),
                          jnp.pad(b_gate[0][N_HEADS:], (0, LANES - N_HEADS))]).reshape(1, GATE_W)
    wpool = w_pool[0].astype(BF16)
    wout = w_out[0].astype(BF16)
    n1 = norm1[0].reshape(1, D_MODEL)
    n2 = norm2[0].reshape(1, D_MODEL)
    nf = norm_f.reshape(1, D_MODEL)
    pscale = pool_scale[0].reshape(1, POOL_WIDTH)
    gain = head_gain[0].reshape(1, MLSTM_WIDTH)

    weight_specs = [_full((1, D_MODEL)), _full((D_MODEL, IN_MAIN)), _full((D_MODEL, GATE_W)), _full((1, GATE_W))]
    mixer_specs = [_full((len(POOL_WINDOWS), POOL_GC, POOL_GC)), _full((1, POOL_WIDTH)),
                   _full((1, MLSTM_WIDTH)), _full((D_MODEL, D_MODEL))]

    meta_x = jnp.pad(meta_tokens, ((0, LANES - N_META), (0, 0)))
    u_meta, c_meta, m_meta = pl.pallas_call(
        _meta_kernel,
        out_shape=(jax.ShapeDtypeStruct((N_META, POOL_WIDTH), F32),
                   jax.ShapeDtypeStruct((N_HEADS, HEAD_DIM, AUG), F32),
                   jax.ShapeDtypeStruct((SUBLANES, LANES), F32)),
        compiler_params=pltpu.CompilerParams(vmem_limit_bytes=VMEM_LIMIT),
        name="meta_prefix",
    )(meta_x, n1, win, wg, bg)

    nt = S // PROMPT_TILE
    n_tiles = B * nt
    assert n_tiles % 2 == 0 and n_tiles >= 2
    assert D_MODEL % (n_tiles * 2 * SUBLANES) == 0

    def tile_at(lag):
        def index(s):
            j = jnp.clip(s - lag, 0, n_tiles - 1)
            return (j // nt, j % nt, 0)
        return index

    def mixed_seq(ndim):
        return lambda s: (jnp.clip(s - 1, 0, n_tiles - 1) // nt,) + (0,) * (ndim - 1)

    x1_p, pool_p, c_p, m_p = pl.pallas_call(
        functools.partial(_prompt_kernel, nt),
        grid=(n_tiles + 2,),
        in_specs=[pl.BlockSpec((1, PROMPT_TILE, D_MODEL), tile_at(0)),
                  pl.BlockSpec((1, PROMPT_TILE, D_MODEL), tile_at(2)),
                  _full((N_META, POOL_WIDTH)), _full((N_HEADS, HEAD_DIM, AUG)), _full((SUBLANES, LANES))]
        + weight_specs + mixer_specs,
        out_specs=[pl.BlockSpec((1, PROMPT_TILE, D_MODEL), tile_at(2)),
                   pl.BlockSpec((1, CTX_PAD, POOL_WIDTH), mixed_seq(3)),
                   pl.BlockSpec((1, N_HEADS, HEAD_DIM, AUG), mixed_seq(4)),
                   pl.BlockSpec((1, SUBLANES, LANES), mixed_seq(3))],
        out_shape=(jax.ShapeDtypeStruct((B, S, D_MODEL), F32),
                   jax.ShapeDtypeStruct((B, CTX_PAD, POOL_WIDTH), F32),
                   jax.ShapeDtypeStruct((B, N_HEADS, HEAD_DIM, AUG), F32),
                   jax.ShapeDtypeStruct((B, SUBLANES, LANES), F32)),
        scratch_shapes=[pltpu.VMEM((CTX_PAD + CHUNK, POOL_WIDTH), F32),
                        pltpu.VMEM((PROMPT_TILE, D_MODEL), BF16),
                        pltpu.VMEM((PROMPT_TILE, IN_MAIN), F32), pltpu.VMEM((PROMPT_TILE, IN_MAIN), F32),
                        pltpu.VMEM((PROMPT_TILE, GATE_W), F32), pltpu.VMEM((PROMPT_TILE, GATE_W), F32),
                        pltpu.VMEM((PROMPT_TILE, D_MODEL), BF16), pltpu.VMEM((PROMPT_TILE, D_MODEL), BF16)],
        compiler_params=pltpu.CompilerParams(dimension_semantics=("arbitrary",), vmem_limit_bytes=VMEM_LIMIT),
        name="prompt_mixer",
    )(x_prompt, x_prompt, u_meta, c_meta, m_meta, n1, win, wg, bg, wpool, pscale, gain, wout)

    xs_t = x_sample.transpose(1, 0, 2).reshape(rows_s, D_MODEL)
    x1_p, xs_t = lax.optimization_barrier((x1_p, xs_t))
    ctx_t = state_pool[0].transpose(1, 0, 2)
    n0 = state_n[0].reshape(n_seq, MLSTM_WIDTH)
    m0 = jnp.pad(state_m[0], ((0, 0), (0, LANES - N_HEADS)))
    state_block = pl.BlockSpec((SAMPLE_BLOCK, N_HEADS, HEAD_DIM, HEAD_DIM), lambda i: (i, 0, 0, 0))
    x1_s, pool_s, c_s, n_s, m_s = pl.pallas_call(
        _sample_kernel,
        grid=(n_seq // SAMPLE_BLOCK,),
        in_specs=[_full((rows_s, D_MODEL)), _full((POOL_CTX, n_seq, POOL_WIDTH)), state_block,
                  _full((n_seq, MLSTM_WIDTH)), _full((n_seq, LANES))] + weight_specs + mixer_specs,
        out_specs=[_full((rows_s, D_MODEL)), _full((POOL_CTX, n_seq, POOL_WIDTH)), state_block,
                   _full((n_seq, MLSTM_WIDTH)), _full((n_seq, LANES))],
        out_shape=(jax.ShapeDtypeStruct((rows_s, D_MODEL), F32),
                   jax.ShapeDtypeStruct((POOL_CTX, n_seq, POOL_WIDTH), F32),
                   jax.ShapeDtypeStruct((n_seq, N_HEADS, HEAD_DIM, HEAD_DIM), F32),
                   jax.ShapeDtypeStruct((n_seq, MLSTM_WIDTH), F32),
                   jax.ShapeDtypeStruct((n_seq, LANES), F32)),
        scratch_shapes=[pltpu.VMEM((rows_s, IN_MAIN), F32),
                        pltpu.VMEM((rows_s, GATE_W), F32),
                        pltpu.VMEM((N_HEADS, rows_s, HEAD_DIM), F32),
                        pltpu.VMEM((N_HEADS, rows_s, HEAD_DIM), F32),
                        pltpu.VMEM((N_HEADS, rows_s, HEAD_DIM), F32),
                        pltpu.VMEM((N_HEADS, rows_s // LANES, HEAD_DIM, LANES), BF16),
                        pltpu.VMEM((N_HEADS, rows_s // LANES, LANES, HEAD_DIM), BF16),
                        pltpu.VMEM((n_seq, LANES), F32),
                        pltpu.VMEM((rows_s, D_MODEL), BF16)],
        compiler_params=pltpu.CompilerParams(dimension_semantics=("arbitrary",), vmem_limit_bytes=VMEM_LIMIT),
        name="sample_mixer",
    )(xs_t, ctx_t, state_C[0], n0, m0, n1, win, wg, bg, wpool, pscale, gain, wout)

    y_p, y_s = _ffn(x1_s, x1_p.reshape(B * S, D_MODEL), n2, nf, w_up[0], w_down[0])
    y_p = y_p.reshape(B, S, D_MODEL)
    y_s = y_s.reshape(DEC_SEQ, n_seq, D_MODEL).transpose(1, 0, 2)

    n_p = jnp.stack([c_p[:, h, :, HEAD_DIM + h] for h in range(N_HEADS)], axis=1)
    return (y_p, y_s,
            pool_p[:, CTX_PAD - POOL_CTX:, :][None],
            c_p[..., :HEAD_DIM][None],
            n_p[None],
            m_p[:, 0, :N_HEADS][None],
            pool_s.transpose(1, 0, 2)[None],
            c_s[None],
            n_s.reshape(n_seq, N_HEADS, HEAD_DIM)[None],
            m_s[:, :N_HEADS][None])
```

```python
import functools

import jax
import jax.numpy as jnp
from jax import lax
from jax.experimental import pallas as pl
from jax.experimental.pallas import tpu as pltpu

D_MODEL = 1024
N_META = 16
POOL_WIDTH = 512
POOL_WINDOWS = (2, 4, 8, 16)
POOL_GC = 128
POOL_CTX = 15
MLSTM_WIDTH = 512
N_HEADS = 4
HEAD_DIM = 128
D_FF = 4096
EPS = 1e-6
DEC_SEQ = 4

LANES = 128
SUBLANES = 8
COL_Q = POOL_WIDTH
COL_K = COL_Q + MLSTM_WIDTH
COL_V = COL_K + MLSTM_WIDTH
COL_O = COL_V + MLSTM_WIDTH
IN_MAIN = COL_O + MLSTM_WIDTH
GATE_W = 2 * LANES
AUG = 2 * HEAD_DIM
CTX_PAD = 16
CHUNK = 256
PROMPT_TILE = 512
IN_CHUNK = 512
FFN_TILE = 512
FFN_SUB = 256
FFN_CHUNK = 1024
SAMPLE_BLOCK = 16
NEG_BIG = -1e30
VMEM_LIMIT = 56 * 1024 * 1024

F32 = jnp.float32
BF16 = jnp.bfloat16


def _rmsnorm(x, g):
    return x * lax.rsqrt(jnp.mean(x * x, axis=-1, keepdims=True) + EPS) * g


def _log_sigmoid(x):
    return jnp.minimum(x, 0.0) - jnp.log1p(jnp.exp(-jnp.abs(x)))


def _dot(a, b):
    return jnp.dot(a, b, preferred_element_type=F32)


def _dot_nt(a, b):
    return lax.dot_general(a, b, (((1,), (1,)), ((), ())), preferred_element_type=F32)


def _hcols(base, h):
    return slice(base + h * HEAD_DIM, base + (h + 1) * HEAD_DIM)


def _row_scan(x, op):
    n = x.shape[0]
    row = lax.broadcasted_iota(jnp.int32, x.shape, 0)
    sh = 1
    while sh < n:
        x = jnp.where(row >= sh, op(x, pltpu.roll(x, sh, axis=0)), x)
        sh *= 2
    return x


def _gate_cols(g, m0, n_valid):
    R = g.shape[0]
    b = _row_scan(_log_sigmoid(g[:, LANES:]), jnp.add)
    r = g[:, :LANES] - b
    big_m = jnp.maximum(m0, _row_scan(r, jnp.maximum))
    m = b + big_m
    last = n_valid - 1
    m_top = big_m[last:last + 1, :]
    ws = jnp.exp(r - m_top)
    if n_valid < R:
        ws = jnp.where(lax.broadcasted_iota(jnp.int32, ws.shape, 0) < n_valid, ws, 0.0)
    return dict(r=r, big_m=big_m, inter=jnp.exp(m0 - big_m), eneg=jnp.exp(-m), ws=ws,
                decay=jnp.exp(m0 - m_top), m_last=m[last:last + 1, :])


def _value_aug(v, h):
    lane = lax.broadcasted_iota(jnp.int32, (v.shape[0], HEAD_DIM), 1)
    return jnp.concatenate([v, jnp.where(lane == h, 1.0, 0.0)], axis=1).astype(BF16)


def _meta_kernel(x_ref, n1_ref, win_ref, wg_ref, bg_ref, wout_ref, wpool_ref,
                 u_ref, c_ref, m_ref, wout_bf_ref, wpool_bf_ref):
    wout_bf_ref[...] = wout_ref[...].astype(BF16)
    wpool_bf_ref[...] = wpool_ref[...].astype(BF16)
    x = jnp.concatenate([x_ref[...], jnp.zeros((LANES - N_META, D_MODEL), F32)], axis=0)
    xn = _rmsnorm(x, n1_ref[...]).astype(BF16)
    gc = _gate_cols(_dot(xn, wg_ref[...]) + bg_ref[...], jnp.zeros((1, LANES), F32), N_META)
    p = _dot(xn, win_ref[:, 0:IN_MAIN])
    u_ref[...] = p[0:N_META, 0:POOL_WIDTH]
    for h in range(N_HEADS):
        kw_t = (p[:, _hcols(COL_K, h)] * gc["ws"][:, h:h + 1]).T.astype(BF16)
        c_ref[h] = _dot(kw_t, _value_aug(p[:, _hcols(COL_V, h)], h))
    m_ref[...] = jnp.broadcast_to(gc["m_last"], (SUBLANES, LANES))


def _prompt_kernel(nt, xa_ref, xc_ref, umeta_ref, cmeta_ref, mmeta_ref, n1_ref, win_ref, wg_ref, bg_ref,
                   wpool_ref, pscale_ref, gain_ref, wout_ref,
                   x1_ref, pool_ref, c_ref, m_ref, cfin_ref, nfin_ref,
                   uext_ref, xn_ref, p0_ref, p1_ref, g0_ref, g1_ref, mix0_ref, mix1_ref):
    s = pl.program_id(0)
    n_tiles = pl.num_programs(0) - 2

    @pl.when((jnp.maximum(s - 1, 0) % nt == 0) & (s <= n_tiles))
    def _():
        c_ref[0] = cmeta_ref[...]
        m_ref[0] = mmeta_ref[...]
        uext_ref[0:CTX_PAD, :] = umeta_ref[...]

    stage = functools.partial(_prompt_step, xa_ref, xc_ref, n1_ref, win_ref, wg_ref, bg_ref, wpool_ref,
                              pscale_ref, gain_ref, wout_ref, x1_ref, pool_ref, c_ref, m_ref, uext_ref, xn_ref)
    even = (p0_ref, p1_ref, g0_ref, g1_ref, mix0_ref, mix1_ref)
    odd = (p1_ref, p0_ref, g1_ref, g0_ref, mix1_ref, mix0_ref)
    steady = (s >= 2) & (s < n_tiles)
    pl.when(s == 0)(lambda: stage(*even, project=True, mix=False, emit=False))
    pl.when(s == 1)(lambda: stage(*odd, project=True, mix=True, emit=False))
    pl.when(steady & (s % 2 == 0))(lambda: stage(*even, project=True, mix=True, emit=True))
    pl.when(steady & (s % 2 == 1))(lambda: stage(*odd, project=True, mix=True, emit=True))
    pl.when(s == n_tiles)(lambda: stage(*even, project=False, mix=True, emit=True))
    pl.when(s == n_tiles + 1)(lambda: stage(*odd, project=False, mix=False, emit=True))

    @pl.when((s >= 1) & (s <= n_tiles) & ((s - 1) % nt == nt - 1))
    def _():
        for h in range(N_HEADS):
            cfin_ref[0, h] = c_ref[0, h, :, 0:HEAD_DIM]
            nfin_ref[0, h:h + 1, :] = c_ref[0, h, :, HEAD_DIM:].T[h:h + 1, :]


def _prompt_step(xa_ref, xc_ref, n1_ref, win_ref, wg_ref, bg_ref, wpool_ref, pscale_ref, gain_ref, wout_ref,
                 x1_ref, pool_ref, c_ref, m_ref, uext_ref, xn_ref,
                 p_new, pb, g_new, g_old, mx, mix_old, *, project, mix, emit):
    dense = []
    if emit:
        def out_proj():
            x1_ref[0] = xc_ref[0] + _dot(mix_old[...], wout_ref[...])
        dense.append(out_proj)
    if project:
        def in_norm():
            xn_ref[...] = _rmsnorm(xa_ref[0], n1_ref[...]).astype(BF16)
            g_new[...] = _dot(xn_ref[...], wg_ref[...]) + bg_ref[...]

        def in_chunk(c):
            cs = slice(c * IN_CHUNK, (c + 1) * IN_CHUNK)
            p_new[:, cs] = _dot(xn_ref[...], win_ref[:, cs])
        dense.append(in_norm)
        dense += [functools.partial(in_chunk, c) for c in range(IN_MAIN // IN_CHUNK)]

    pieces = []
    if mix:
        for ci in range(PROMPT_TILE // CHUNK):
            pieces.append(_mix_chunk(pb, g_old, mx, slice(ci * CHUNK, (ci + 1) * CHUNK),
                                     wpool_ref, pscale_ref, gain_ref, pool_ref, c_ref, m_ref, uext_ref))
    n_mix = (1 + N_HEADS) * len(pieces)
    done = 0
    for k, piece in enumerate(p for chunk in pieces for p in chunk):
        while done < len(dense) and done * n_mix < k * len(dense):
            dense[done]()
            done += 1
    for job in dense[done:]:
        job()


def _mix_chunk(pb, g_old, mx, rows, wpool_ref, pscale_ref, gain_ref, pool_ref, c_ref, m_ref, uext_ref):
    R = CHUNK
    gc = _gate_cols(g_old[rows, :], m_ref[0, 0:1, :], R)
    r_t = gc["r"].T

    uext_ref[CTX_PAD:CTX_PAD + R, :] = pb[rows, 0:POOL_WIDTH]
    for gi, w in enumerate(POOL_WINDOWS):
        cs = slice(gi * POOL_GC, (gi + 1) * POOL_GC)
        acc = uext_ref[:, cs]
        sh = 1
        while sh < w:
            acc = acc + pltpu.roll(acc, sh, axis=0)
            sh *= 2
        z = acc[CTX_PAD:, :] * (1.0 / w) - uext_ref[CTX_PAD:CTX_PAD + R, cs]
        mx[rows, cs] = (_dot(z.astype(BF16), wpool_ref[gi]) * pscale_ref[:, cs]).astype(BF16)
    tail = uext_ref[R:R + CTX_PAD, :]
    uext_ref[0:CTX_PAD, :] = tail
    pool_ref[0] = tail
    m_ref[0] = jnp.broadcast_to(gc["m_last"], (SUBLANES, LANES))
    yield

    causal = (lax.broadcasted_iota(jnp.int32, (R, R), 0) >= lax.broadcasted_iota(jnp.int32, (R, R), 1))
    for h in range(N_HEADS):
        q = (pb[rows, _hcols(COL_Q, h)] * (HEAD_DIM ** -0.5)).astype(BF16)
        k = pb[rows, _hcols(COL_K, h)]
        vaug = _value_aug(pb[rows, _hcols(COL_V, h)], h)
        sc = _dot_nt(q, k.astype(BF16))
        e = jnp.exp(jnp.where(causal, r_t[h:h + 1, :] - gc["big_m"][:, h:h + 1], NEG_BIG))
        c0 = c_ref[0, h]
        num = gc["inter"][:, h:h + 1] * _dot(q, c0.astype(BF16)) + _dot((sc * e).astype(BF16), vaug)
        inv = 1.0 / jnp.maximum(jnp.abs(num[:, HEAD_DIM:]), gc["eneg"])
        hh = num[:, 0:HEAD_DIM] * inv[:, h:h + 1]
        hh = hh * lax.rsqrt(jnp.mean(hh * hh, axis=-1, keepdims=True) + EPS) * gain_ref[:, _hcols(0, h)]
        mx[rows, _hcols(POOL_WIDTH, h)] = (hh * jax.nn.sigmoid(pb[rows, _hcols(COL_O, h)])).astype(BF16)
        kw_t = (k * gc["ws"][:, h:h + 1]).T.astype(BF16)
        c_ref[0, h] = gc["decay"][:, h:h + 1] * c0 + _dot(kw_t, vaug)
        yield


def _sample_gates(g_ref, m0):
    n = m0.shape[0]
    b = jnp.zeros((n, LANES), F32)
    big_m = m0
    r, big_ms, ms = [], [], []
    for t in range(DEC_SEQ):
        g = g_ref[t * n:(t + 1) * n, :]
        b = b + _log_sigmoid(g[:, LANES:])
        r.append(g[:, :LANES] - b)
        big_m = jnp.maximum(big_m, r[-1])
        big_ms.append(big_m)
        ms.append(b + big_m)
    return dict(r=r, big_m=big_ms, m=ms,
                ws=[jnp.exp(r[t] - big_m) for t in range(DEC_SEQ)],
                decay=jnp.exp(m0 - big_m))


def _sample_kernel(xs_ref, ctx_ref, cin_ref, n0_ref, m0_ref, n1_ref, win_ref, wg_ref, bg_ref, wpool_ref,
                   pscale_ref, gain_ref, wout_ref,
                   x1_ref, pool_ref, cout_ref, nout_ref, mout_ref,
                   p_ref, g_ref, tmp_ref, qsm_ref, qcsm_ref, kwt_ref, vsm_ref, dec_ref, mix_ref):
    i = pl.program_id(0)
    n_seq = m0_ref.shape[0]
    n_grp = DEC_SEQ * n_seq // LANES
    scale = HEAD_DIM ** -0.5

    def seq_major(dst_ref, slab_of_t):
        for t in range(DEC_SEQ):
            slab = slab_of_t(t)
            for h in range(N_HEADS):
                dst_ref[h, pl.ds(t, n_seq, stride=DEC_SEQ), :] = slab[:, _hcols(0, h)]

    @pl.when(i == 0)
    def _():
        xn = _rmsnorm(xs_ref[...], n1_ref[...]).astype(BF16)
        g_ref[...] = _dot(xn, wg_ref[...]) + bg_ref[...]
        p_ref[...] = _dot(xn, win_ref[...])

        def urow(e, cs):
            if e < POOL_CTX:
                return ctx_ref[e, :, cs]
            return p_ref[(e - POOL_CTX) * n_seq:(e - POOL_CTX + 1) * n_seq, cs]

        for gi, w in enumerate(POOL_WINDOWS):
            cs = slice(gi * POOL_GC, (gi + 1) * POOL_GC)
            for t in range(DEC_SEQ):
                e = POOL_CTX + t
                wsum = urow(e, cs)
                for j in range(1, w):
                    wsum = wsum + urow(e - j, cs)
                z = wsum * (1.0 / w) - urow(e, cs)
                mix_ref[t * n_seq:(t + 1) * n_seq, cs] = (
                    _dot(z.astype(BF16), wpool_ref[gi]) * pscale_ref[:, cs]).astype(BF16)
        for e in range(DEC_SEQ, POOL_CTX + DEC_SEQ):
            pool_ref[e - DEC_SEQ] = urow(e, slice(0, POOL_WIDTH))

        gs = _sample_gates(g_ref, m0_ref[...])
        dec_ref[...] = gs["decay"]
        mout_ref[...] = gs["m"][DEC_SEQ - 1]
        def kw_slab(t):
            rows = slice(t * n_seq, (t + 1) * n_seq)
            return jnp.concatenate([p_ref[rows, _hcols(COL_K, h)] * gs["ws"][t][:, h:h + 1]
                                    for h in range(N_HEADS)], axis=1)

        kw = [kw_slab(t) for t in range(DEC_SEQ)]
        nout_ref[...] = (jnp.concatenate([jnp.broadcast_to(gs["decay"][:, h:h + 1], (n_seq, HEAD_DIM))
                                          for h in range(N_HEADS)], axis=1) * n0_ref[...]
                         + kw[0] + kw[1] + kw[2] + kw[3])
        seq_major(tmp_ref, lambda t: kw[t])
        for h in range(N_HEADS):
            for gi in range(n_grp):
                kwt_ref[h, gi] = tmp_ref[h, gi * LANES:(gi + 1) * LANES, :].T.astype(BF16)
        seq_major(tmp_ref, lambda t: p_ref[t * n_seq:(t + 1) * n_seq, COL_V:COL_O])
        for h in range(N_HEADS):
            for gi in range(n_grp):
                vsm_ref[h, gi] = tmp_ref[h, gi * LANES:(gi + 1) * LANES, :].astype(BF16)
        seq_major(qsm_ref, lambda t: p_ref[t * n_seq:(t + 1) * n_seq, COL_Q:COL_K] * scale)

    lane_sl = lax.broadcasted_iota(jnp.int32, (HEAD_DIM, LANES), 1) // DEC_SEQ
    first = lax.broadcasted_iota(jnp.int32, (SUBLANES, HEAD_DIM), 0) < DEC_SEQ
    grp = (i * SAMPLE_BLOCK) // (LANES // DEC_SEQ)
    for jp in range(SAMPLE_BLOCK // 2):
        rows = pl.ds(pl.multiple_of(i * (SAMPLE_BLOCK * DEC_SEQ) + jp * SUBLANES, SUBLANES), SUBLANES)
        for h in range(N_HEADS):
            hs = _hcols(0, h)
            q8 = qsm_ref[h, rows, :].astype(BF16)
            readout = []
            for a in range(2):
                j = 2 * jp + a
                seq = i * SAMPLE_BLOCK + j
                c0 = cin_ref[j, h]
                readout.append(_dot(q8, c0.astype(BF16)))
                lhs = jnp.where(lane_sl == seq % (LANES // DEC_SEQ), kwt_ref[h, grp], jnp.zeros((), BF16))
                cout_ref[j, h] = dec_ref[pl.ds(seq, 1), :][:, h:h + 1] * c0 + _dot(lhs, vsm_ref[h, grp])
            qcsm_ref[h, rows, :] = jnp.where(first, readout[0], readout[1])

    @pl.when(i == pl.num_programs(0) - 1)
    def _():
        gs = _sample_gates(g_ref, m0_ref[...])
        for h in range(N_HEADS):
            hs = _hcols(0, h)
            n0 = n0_ref[:, hs]
            qs = [p_ref[t * n_seq:(t + 1) * n_seq, _hcols(COL_Q, h)] * scale for t in range(DEC_SEQ)]
            ks = [p_ref[t * n_seq:(t + 1) * n_seq, _hcols(COL_K, h)] for t in range(DEC_SEQ)]
            for t in range(DEC_SEQ):
                rows = slice(t * n_seq, (t + 1) * n_seq)
                inter = jnp.exp(m0_ref[...] - gs["big_m"][t])[:, h:h + 1]
                num = inter * qcsm_ref[h, pl.ds(t, n_seq, stride=DEC_SEQ), :]
                den = inter * jnp.sum(qs[t] * n0, axis=-1, keepdims=True)
                for s in range(t + 1):
                    pts = (jnp.sum(qs[t] * ks[s], axis=-1, keepdims=True)
                           * jnp.exp(gs["r"][s] - gs["big_m"][t])[:, h:h + 1])
                    num = num + pts * p_ref[s * n_seq:(s + 1) * n_seq, _hcols(COL_V, h)]
                    den = den + pts
                hh = num / jnp.maximum(jnp.abs(den), jnp.exp(-gs["m"][t])[:, h:h + 1])
                hh = hh * lax.rsqrt(jnp.mean(hh * hh, axis=-1, keepdims=True) + EPS) * gain_ref[:, hs]
                mix_ref[rows, _hcols(POOL_WIDTH, h)] = (
                    hh * jax.nn.sigmoid(p_ref[rows, _hcols(COL_O, h)])).astype(BF16)
        x1_ref[...] = xs_ref[...] + _dot(mix_ref[...], wout_ref[...])


def _ffn_kernel(xs_ref, xp_ref, n2_ref, nf_ref, wup_hbm, wdown_hbm, yp_ref, ys_ref,
                wup_ref, wdown_ref, stage_ref, sem):
    i = pl.program_id(0)
    last = pl.num_programs(0) - 1

    def fetch(c, up):
        cols = pl.ds(c * FFN_CHUNK, FFN_CHUNK)
        src = wup_hbm.at[:, cols] if up else wdown_hbm.at[cols, :]
        slot = 0 if up else 1
        return pltpu.make_async_copy(src, stage_ref.at[slot], sem.at[slot])

    def mlp(x_ref, y_ref, load_weights):
        subs = [slice(j * FFN_SUB, (j + 1) * FFN_SUB) for j in range(x_ref.shape[0] // FFN_SUB)]
        n_chunk = D_FF // FFN_CHUNK
        if load_weights:
            fetch(0, True).start()
            fetch(0, False).start()
        xn = [_rmsnorm(x_ref[rows, :], n2_ref[...]).astype(BF16) for rows in subs]
        acc = [x_ref[rows, :] for rows in subs]
        for c in range(n_chunk):
            if load_weights:
                fetch(c, True).wait()
                wup_ref[c] = stage_ref[0].astype(BF16)
                if c + 1 < n_chunk:
                    fetch(c + 1, True).start()
                fetch(c, False).wait()
                wdown_ref[c] = stage_ref[1].astype(BF16)
                if c + 1 < n_chunk:
                    fetch(c + 1, False).start()
            for j in range(len(subs)):
                a = jnp.square(jnp.maximum(_dot(xn[j], wup_ref[c]), 0.0)).astype(BF16)
                acc[j] = acc[j] + _dot(a, wdown_ref[c])
        for j, rows in enumerate(subs):
            y_ref[rows, :] = _rmsnorm(acc[j], nf_ref[...])

    pl.when(i == 0)(lambda: mlp(xp_ref, yp_ref, True))
    pl.when((i > 0) & (i < last))(lambda: mlp(xp_ref, yp_ref, False))
    pl.when(i == last)(lambda: mlp(xs_ref, ys_ref, False))


def _full(shape):
    return pl.BlockSpec(shape, lambda *_: (0,) * len(shape))


def _ffn(xs, xp, n2, nf, w_up, w_down):
    assert D_MODEL == FFN_CHUNK
    n_prompt = xp.shape[0] // FFN_TILE
    n_chunk = D_FF // FFN_CHUNK
    prompt_tile = pl.BlockSpec((FFN_TILE, D_MODEL), lambda i: (jnp.minimum(i, n_prompt - 1), 0))
    return pl.pallas_call(
        _ffn_kernel,
        grid=(n_prompt + 1,),
        in_specs=[_full(xs.shape), prompt_tile, _full((1, D_MODEL)), _full((1, D_MODEL)),
                  pl.BlockSpec(memory_space=pl.ANY), pl.BlockSpec(memory_space=pl.ANY)],
        out_specs=[prompt_tile, _full(xs.shape)],
        out_shape=(jax.ShapeDtypeStruct(xp.shape, F32), jax.ShapeDtypeStruct(xs.shape, F32)),
        scratch_shapes=[pltpu.VMEM((n_chunk, D_MODEL, FFN_CHUNK), BF16),
                        pltpu.VMEM((n_chunk, FFN_CHUNK, D_MODEL), BF16),
                        pltpu.VMEM((2, D_MODEL, FFN_CHUNK), F32),
                        pltpu.SemaphoreType.DMA((2,))],
        compiler_params=pltpu.CompilerParams(dimension_semantics=("arbitrary",), vmem_limit_bytes=VMEM_LIMIT),
        name="ffn",
    )(xs, xp, n2, nf, w_up, w_down)


def kernel(x_prompt, x_sample, state_pool, state_C, state_n, state_m, meta_tokens, norm1, w_in, b_gate,
           w_pool, pool_scale, head_gain, w_out, norm2, w_up, w_down, norm_f):
    B, S, _ = x_prompt.shape
    n_seq = x_sample.shape[0]
    rows_s = DEC_SEQ * n_seq
    assert w_in.shape[0] == 1 and S % PROMPT_TILE == 0 and n_seq % SAMPLE_BLOCK == 0
    assert (B * S) % FFN_TILE == 0 and rows_s % FFN_SUB == 0

    win = w_in[0].astype(BF16)
    gpad = ((0, 0), (0, LANES - N_HEADS))
    wg = jnp.concatenate([jnp.pad(w_in[0][:, IN_MAIN:IN_MAIN + N_HEADS], gpad),
                          jnp.pad(w_in[0][:, IN_MAIN + N_HEADS:], gpad)], axis=1).astype(BF16)
    bg = jnp.concatenate([jnp.pad(b_gate[0][:N_HEADS], (0, LANES - N_HEADS)),
                          jnp.pad(b_gate[0][N_HEADS:], (0, LANES - N_HEADS))]).reshape(1, GATE_W)
    n1 = norm1[0].reshape(1, D_MODEL)
    n2 = norm2[0].reshape(1, D_MODEL)
    nf = norm_f.reshape(1, D_MODEL)
    pscale = pool_scale[0].reshape(1, POOL_WIDTH)
    gain = head_gain[0].reshape(1, MLSTM_WIDTH)

    weight_specs = [_full((1, D_MODEL)), _full((D_MODEL, IN_MAIN)), _full((D_MODEL, GATE_W)), _full((1, GATE_W))]
    mixer_specs = [_full((len(POOL_WINDOWS), POOL_GC, POOL_GC)), _full((1, POOL_WIDTH)),
                   _full((1, MLSTM_WIDTH)), _full((D_MODEL, D_MODEL))]

    u_meta, c_meta, m_meta, wout, wpool = pl.pallas_call(
        _meta_kernel,
        out_shape=(jax.ShapeDtypeStruct((N_META, POOL_WIDTH), F32),
                   jax.ShapeDtypeStruct((N_HEADS, HEAD_DIM, AUG), F32),
                   jax.ShapeDtypeStruct((SUBLANES, LANES), F32),
                   jax.ShapeDtypeStruct((D_MODEL, D_MODEL), BF16),
                   jax.ShapeDtypeStruct((len(POOL_WINDOWS), POOL_GC, POOL_GC), BF16)),
        compiler_params=pltpu.CompilerParams(vmem_limit_bytes=VMEM_LIMIT),
        name="meta_prefix",
    )(meta_tokens, n1, win, wg, bg, w_out[0], w_pool[0])

    nt = S // PROMPT_TILE
    n_tiles = B * nt
    assert n_tiles % 2 == 0 and n_tiles >= 2
    assert D_MODEL % (n_tiles * 2 * SUBLANES) == 0

    def tile_at(lag):
        def index(s):
            j = jnp.clip(s - lag, 0, n_tiles - 1)
            return (j // nt, j % nt, 0)
        return index

    def mixed_seq(ndim):
        return lambda s: (jnp.clip(s - 1, 0, n_tiles - 1) // nt,) + (0,) * (ndim - 1)

    x1_p, pool_p, _, m_p, c_p, n_p = pl.pallas_call(
        functools.partial(_prompt_kernel, nt),
        grid=(n_tiles + 2,),
        in_specs=[pl.BlockSpec((1, PROMPT_TILE, D_MODEL), tile_at(0)),
                  pl.BlockSpec((1, PROMPT_TILE, D_MODEL), tile_at(2)),
                  _full((N_META, POOL_WIDTH)), _full((N_HEADS, HEAD_DIM, AUG)), _full((SUBLANES, LANES))]
        + weight_specs + mixer_specs,
        out_specs=[pl.BlockSpec((1, PROMPT_TILE, D_MODEL), tile_at(2)),
                   pl.BlockSpec((1, CTX_PAD, POOL_WIDTH), mixed_seq(3)),
                   pl.BlockSpec((1, N_HEADS, HEAD_DIM, AUG), mixed_seq(4)),
                   pl.BlockSpec((1, SUBLANES, LANES), mixed_seq(3)),
                   pl.BlockSpec((1, N_HEADS, HEAD_DIM, HEAD_DIM), mixed_seq(4)),
                   pl.BlockSpec((1, N_HEADS, HEAD_DIM), mixed_seq(3))],
        out_shape=(jax.ShapeDtypeStruct((B, S, D_MODEL), F32),
                   jax.ShapeDtypeStruct((B, CTX_PAD, POOL_WIDTH), F32),
                   jax.ShapeDtypeStruct((B, N_HEADS, HEAD_DIM, AUG), F32),
                   jax.ShapeDtypeStruct((B, SUBLANES, LANES), F32),
                   jax.ShapeDtypeStruct((B, N_HEADS, HEAD_DIM, HEAD_DIM), F32),
                   jax.ShapeDtypeStruct((B, N_HEADS, HEAD_DIM), F32)),
        scratch_shapes=[pltpu.VMEM((CTX_PAD + CHUNK, POOL_WIDTH), F32),
                        pltpu.VMEM((PROMPT_TILE, D_MODEL), BF16),
                        pltpu.VMEM((PROMPT_TILE, IN_MAIN), F32), pltpu.VMEM((PROMPT_TILE, IN_MAIN), F32),
                        pltpu.VMEM((PROMPT_TILE, GATE_W), F32), pltpu.VMEM((PROMPT_TILE, GATE_W), F32),
                        pltpu.VMEM((PROMPT_TILE, D_MODEL), BF16), pltpu.VMEM((PROMPT_TILE, D_MODEL), BF16)],
        compiler_params=pltpu.CompilerParams(dimension_semantics=("arbitrary",), vmem_limit_bytes=VMEM_LIMIT),
        name="prompt_mixer",
    )(x_prompt, x_prompt, u_meta, c_meta, m_meta, n1, win, wg, bg, wpool, pscale, gain, wout)

    xs_t = x_sample.transpose(1, 0, 2).reshape(rows_s, D_MODEL)
    x1_p, xs_t = lax.optimization_barrier((x1_p, xs_t))
    ctx_t = state_pool[0].transpose(1, 0, 2)
    n0 = state_n[0].reshape(n_seq, MLSTM_WIDTH)
    m0 = jnp.pad(state_m[0], ((0, 0), (0, LANES - N_HEADS)))
    state_block = pl.BlockSpec((SAMPLE_BLOCK, N_HEADS, HEAD_DIM, HEAD_DIM), lambda i: (i, 0, 0, 0))
    x1_s, pool_s, c_s, n_s, m_s = pl.pallas_call(
        _sample_kernel,
        grid=(n_seq // SAMPLE_BLOCK,),
        in_specs=[_full((rows_s, D_MODEL)), _full((POOL_CTX, n_seq, POOL_WIDTH)), state_block,
                  _full((n_seq, MLSTM_WIDTH)), _full((n_seq, LANES))] + weight_specs + mixer_specs,
        out_specs=[_full((rows_s, D_MODEL)), _full((POOL_CTX, n_seq, POOL_WIDTH)), state_block,
                   _full((n_seq, MLSTM_WIDTH)), _full((n_seq, LANES))],
        out_shape=(jax.ShapeDtypeStruct((rows_s, D_MODEL), F32),
                   jax.ShapeDtypeStruct((POOL_CTX, n_seq, POOL_WIDTH), F32),
                   jax.ShapeDtypeStruct((n_seq, N_HEADS, HEAD_DIM, HEAD_DIM), F32),
                   jax.ShapeDtypeStruct((n_seq, MLSTM_WIDTH), F32),
                   jax.ShapeDtypeStruct((n_seq, LANES), F32)),
        scratch_shapes=[pltpu.VMEM((rows_s, IN_MAIN), F32),
                        pltpu.VMEM((rows_s, GATE_W), F32),
                        pltpu.VMEM((N_HEADS, rows_s, HEAD_DIM), F32),
                        pltpu.VMEM((N_HEADS, rows_s, HEAD_DIM), F32),
                        pltpu.VMEM((N_HEADS, rows_s, HEAD_DIM), F32),
                        pltpu.VMEM((N_HEADS, rows_s // LANES, HEAD_DIM, LANES), BF16),
                        pltpu.VMEM((N_HEADS, rows_s // LANES, LANES, HEAD_DIM), BF16),
                        pltpu.VMEM((n_seq, LANES), F32),
                        pltpu.VMEM((rows_s, D_MODEL), BF16)],
        compiler_params=pltpu.CompilerParams(dimension_semantics=("arbitrary",), vmem_limit_bytes=VMEM_LIMIT),
        name="sample_mixer",
    )(xs_t, ctx_t, state_C[0], n0, m0, n1, win, wg, bg, wpool, pscale, gain, wout)

    y_p, y_s = _ffn(x1_s, x1_p.reshape(B * S, D_MODEL), n2, nf, w_up[0], w_down[0])
    y_p = y_p.reshape(B, S, D_MODEL)
    y_s = y_s.reshape(DEC_SEQ, n_seq, D_MODEL).transpose(1, 0, 2)

    return (y_p, y_s,
            pool_p[:, CTX_PAD - POOL_CTX:, :][None],
            c_p[None],
            n_p[None],
            m_p[:, 0, :N_HEADS][None],
            pool_s.transpose(1, 0, 2)[None],
            c_s[None],
            n_s.reshape(n_seq, N_HEADS, HEAD_DIM)[None],
            m_s[:, :N_HEADS][None])
```

```python
import functools

import jax
import jax.numpy as jnp
from jax import lax
from jax.experimental import pallas as pl
from jax.experimental.pallas import tpu as pltpu

D_MODEL = 1024
N_META = 16
POOL_WIDTH = 512
POOL_WINDOWS = (2, 4, 8, 16)
POOL_GC = 128
POOL_CTX = 15
MLSTM_WIDTH = 512
N_HEADS = 4
HEAD_DIM = 128
D_FF = 4096
EPS = 1e-6
DEC_SEQ = 4

LANES = 128
SUBLANES = 8
COL_Q = POOL_WIDTH
COL_K = COL_Q + MLSTM_WIDTH
COL_V = COL_K + MLSTM_WIDTH
COL_O = COL_V + MLSTM_WIDTH
IN_MAIN = COL_O + MLSTM_WIDTH
GATE_W = 2 * LANES
AUG = 2 * HEAD_DIM
CTX_PAD = 16
CHUNK = 256
PROMPT_TILE = 512
IN_CHUNK = 512
FFN_TILE = 512
FFN_SUB = 256
FFN_CHUNK = 1024
SAMPLE_BLOCK = 16
NEG_BIG = -1e30
VMEM_LIMIT = 56 * 1024 * 1024

F32 = jnp.float32
BF16 = jnp.bfloat16


def _rmsnorm(x, g):
    return x * lax.rsqrt(jnp.mean(x * x, axis=-1, keepdims=True) + EPS) * g


def _log_sigmoid(x):
    return jnp.minimum(x, 0.0) - jnp.log1p(jnp.exp(-jnp.abs(x)))


def _dot(a, b):
    return jnp.dot(a, b, preferred_element_type=F32)


def _dot_nt(a, b):
    return lax.dot_general(a, b, (((1,), (1,)), ((), ())), preferred_element_type=F32)


def _hcols(base, h):
    return slice(base + h * HEAD_DIM, base + (h + 1) * HEAD_DIM)


def _row_scan(x, op):
    n = x.shape[0]
    row = lax.broadcasted_iota(jnp.int32, x.shape, 0)
    sh = 1
    while sh < n:
        x = jnp.where(row >= sh, op(x, pltpu.roll(x, sh, axis=0)), x)
        sh *= 2
    return x


def _gate_cols(g, m0, n_valid):
    R = g.shape[0]
    b = _row_scan(_log_sigmoid(g[:, LANES:]), jnp.add)
    r = g[:, :LANES] - b
    big_m = jnp.maximum(m0, _row_scan(r, jnp.maximum))
    m = b + big_m
    last = n_valid - 1
    m_top = big_m[last:last + 1, :]
    ws = jnp.exp(r - m_top)
    if n_valid < R:
        ws = jnp.where(lax.broadcasted_iota(jnp.int32, ws.shape, 0) < n_valid, ws, 0.0)
    return dict(r=r, big_m=big_m, inter=jnp.exp(m0 - big_m), eneg=jnp.exp(-m), ws=ws,
                decay=jnp.exp(m0 - m_top), m_last=m[last:last + 1, :])


def _value_aug(v, h):
    lane = lax.broadcasted_iota(jnp.int32, (v.shape[0], HEAD_DIM), 1)
    return jnp.concatenate([v, jnp.where(lane == h, 1.0, 0.0)], axis=1).astype(BF16)


def _meta_kernel(x_ref, n1_ref, win_ref, wg_ref, bg_ref, wout_ref, wpool_ref,
                 u_ref, c_ref, m_ref, wout_bf_ref, wpool_bf_ref):
    wout_bf_ref[...] = wout_ref[...].astype(BF16)
    wpool_bf_ref[...] = wpool_ref[...].astype(BF16)
    x = jnp.concatenate([x_ref[...], jnp.zeros((LANES - N_META, D_MODEL), F32)], axis=0)
    xn = _rmsnorm(x, n1_ref[...]).astype(BF16)
    gc = _gate_cols(_dot(xn, wg_ref[...]) + bg_ref[...], jnp.zeros((1, LANES), F32), N_META)
    p = _dot(xn, win_ref[:, 0:IN_MAIN])
    u_ref[...] = p[0:N_META, 0:POOL_WIDTH]
    for h in range(N_HEADS):
        kw_t = (p[:, _hcols(COL_K, h)] * gc["ws"][:, h:h + 1]).T.astype(BF16)
        c_ref[h] = _dot(kw_t, _value_aug(p[:, _hcols(COL_V, h)], h))
    m_ref[...] = jnp.broadcast_to(gc["m_last"], (SUBLANES, LANES))


def _prompt_kernel(nt, xa_ref, xc_ref, umeta_ref, cmeta_ref, mmeta_ref, n1_ref, win_ref, wg_ref, bg_ref,
                   wpool_ref, pscale_ref, gain_ref, wout_ref,
                   x1_ref, pool_ref, c_ref, m_ref, cfin_ref, nfin_ref,
                   uext_ref, xn_ref, p0_ref, p1_ref, g0_ref, g1_ref, mix0_ref, mix1_ref):
    s = pl.program_id(0)
    n_tiles = pl.num_programs(0) - 2

    @pl.when((jnp.maximum(s - 1, 0) % nt == 0) & (s <= n_tiles))
    def _():
        c_ref[0] = cmeta_ref[...]
        m_ref[0] = mmeta_ref[...]
        uext_ref[0:CTX_PAD, :] = umeta_ref[...]

    stage = functools.partial(_prompt_step, xa_ref, xc_ref, n1_ref, win_ref, wg_ref, bg_ref, wpool_ref,
                              pscale_ref, gain_ref, wout_ref, x1_ref, pool_ref, c_ref, m_ref, uext_ref, xn_ref)
    even = (p0_ref, p1_ref, g0_ref, g1_ref, mix0_ref, mix1_ref)
    odd = (p1_ref, p0_ref, g1_ref, g0_ref, mix1_ref, mix0_ref)
    steady = (s >= 2) & (s < n_tiles)
    pl.when(s == 0)(lambda: stage(*even, project=True, mix=False, emit=False))
    pl.when(s == 1)(lambda: stage(*odd, project=True, mix=True, emit=False))
    pl.when(steady & (s % 2 == 0))(lambda: stage(*even, project=True, mix=True, emit=True))
    pl.when(steady & (s % 2 == 1))(lambda: stage(*odd, project=True, mix=True, emit=True))
    pl.when(s == n_tiles)(lambda: stage(*even, project=False, mix=True, emit=True))
    pl.when(s == n_tiles + 1)(lambda: stage(*odd, project=False, mix=False, emit=True))

    @pl.when((s >= 1) & (s <= n_tiles) & ((s - 1) % nt == nt - 1))
    def _():
        for h in range(N_HEADS):
            cfin_ref[0, h] = c_ref[0, h, :, 0:HEAD_DIM]
            nfin_ref[0, h:h + 1, :] = c_ref[0, h, :, HEAD_DIM:].T[h:h + 1, :]


def _prompt_step(xa_ref, xc_ref, n1_ref, win_ref, wg_ref, bg_ref, wpool_ref, pscale_ref, gain_ref, wout_ref,
                 x1_ref, pool_ref, c_ref, m_ref, uext_ref, xn_ref,
                 p_new, pb, g_new, g_old, mx, mix_old, *, project, mix, emit):
    dense = []
    if emit:
        def out_proj():
            x1_ref[0] = xc_ref[0] + _dot(mix_old[...], wout_ref[...])
        dense.append(out_proj)
    if project:
        def in_norm():
            xn_ref[...] = _rmsnorm(xa_ref[0], n1_ref[...]).astype(BF16)
            g_new[...] = _dot(xn_ref[...], wg_ref[...]) + bg_ref[...]

        def in_chunk(c):
            cs = slice(c * IN_CHUNK, (c + 1) * IN_CHUNK)
            p_new[:, cs] = _dot(xn_ref[...], win_ref[:, cs])
        dense.append(in_norm)
        dense += [functools.partial(in_chunk, c) for c in range(IN_MAIN // IN_CHUNK)]

    pieces = []
    if mix:
        for ci in range(PROMPT_TILE // CHUNK):
            pieces.append(_mix_chunk(pb, g_old, mx, slice(ci * CHUNK, (ci + 1) * CHUNK),
                                     wpool_ref, pscale_ref, gain_ref, pool_ref, c_ref, m_ref, uext_ref))
    n_mix = (1 + N_HEADS) * len(pieces)
    done = 0
    for k, piece in enumerate(p for chunk in pieces for p in chunk):
        while done < len(dense) and done * n_mix < k * len(dense):
            dense[done]()
            done += 1
    for job in dense[done:]:
        job()


def _mix_chunk(pb, g_old, mx, rows, wpool_ref, pscale_ref, gain_ref, pool_ref, c_ref, m_ref, uext_ref):
    R = CHUNK
    gc = _gate_cols(g_old[rows, :], m_ref[0, 0:1, :], R)
    r_t = gc["r"].T

    uext_ref[CTX_PAD:CTX_PAD + R, :] = pb[rows, 0:POOL_WIDTH]
    for gi, w in enumerate(POOL_WINDOWS):
        cs = slice(gi * POOL_GC, (gi + 1) * POOL_GC)
        acc = uext_ref[:, cs]
        sh = 1
        while sh < w:
            acc = acc + pltpu.roll(acc, sh, axis=0)
            sh *= 2
        z = acc[CTX_PAD:, :] * (1.0 / w) - uext_ref[CTX_PAD:CTX_PAD + R, cs]
        mx[rows, cs] = (_dot(z.astype(BF16), wpool_ref[gi]) * pscale_ref[:, cs]).astype(BF16)
    tail = uext_ref[R:R + CTX_PAD, :]
    uext_ref[0:CTX_PAD, :] = tail
    pool_ref[0] = tail
    m_ref[0] = jnp.broadcast_to(gc["m_last"], (SUBLANES, LANES))
    yield

    causal = (lax.broadcasted_iota(jnp.int32, (R, R), 0) >= lax.broadcasted_iota(jnp.int32, (R, R), 1))
    for h in range(N_HEADS):
        q = (pb[rows, _hcols(COL_Q, h)] * (HEAD_DIM ** -0.5)).astype(BF16)
        k = pb[rows, _hcols(COL_K, h)]
        vaug = _value_aug(pb[rows, _hcols(COL_V, h)], h)
        sc = _dot_nt(q, k.astype(BF16))
        e = jnp.exp(jnp.where(causal, r_t[h:h + 1, :] - gc["big_m"][:, h:h + 1], NEG_BIG))
        c0 = c_ref[0, h]
        num = gc["inter"][:, h:h + 1] * _dot(q, c0.astype(BF16)) + _dot((sc * e).astype(BF16), vaug)
        inv = 1.0 / jnp.maximum(jnp.abs(num[:, HEAD_DIM:]), gc["eneg"])
        hh = num[:, 0:HEAD_DIM] * inv[:, h:h + 1]
        hh = hh * lax.rsqrt(jnp.mean(hh * hh, axis=-1, keepdims=True) + EPS) * gain_ref[:, _hcols(0, h)]
        mx[rows, _hcols(POOL_WIDTH, h)] = (hh * jax.nn.sigmoid(pb[rows, _hcols(COL_O, h)])).astype(BF16)
        kw_t = (k * gc["ws"][:, h:h + 1]).T.astype(BF16)
        c_ref[0, h] = gc["decay"][:, h:h + 1] * c0 + _dot(kw_t, vaug)
        yield


def _sample_gates(g_ref, m0):
    n = m0.shape[0]
    b = jnp.zeros((n, LANES), F32)
    big_m = m0
    r, big_ms, ms = [], [], []
    for t in range(DEC_SEQ):
        g = g_ref[t * n:(t + 1) * n, :]
        b = b + _log_sigmoid(g[:, LANES:])
        r.append(g[:, :LANES] - b)
        big_m = jnp.maximum(big_m, r[-1])
        big_ms.append(big_m)
        ms.append(b + big_m)
    return dict(r=r, big_m=big_ms, m=ms,
                ws=[jnp.exp(r[t] - big_m) for t in range(DEC_SEQ)],
                decay=jnp.exp(m0 - big_m))


def _sample_kernel(x4_ref, ctx_ref, cin_ref, n0_ref, m0_ref, n1_ref, win_ref, wg_ref, bg_ref, wpool_ref,
                   pscale_ref, gain_ref, wout_ref,
                   x1_ref, pool_ref, cout_ref, nout_ref, mout_ref,
                   xs_ref, p_ref, g_ref, tmp_ref, qsm_ref, qcsm_ref, kwt_ref, vsm_ref, dec_ref, mix_ref):
    i = pl.program_id(0)
    n_seq = m0_ref.shape[0]
    n_grp = DEC_SEQ * n_seq // LANES
    scale = HEAD_DIM ** -0.5

    def seq_major(dst_ref, slab_of_t):
        for t in range(DEC_SEQ):
            slab = slab_of_t(t)
            for h in range(N_HEADS):
                dst_ref[h, pl.ds(t, n_seq, stride=DEC_SEQ), :] = slab[:, _hcols(0, h)]

    @pl.when(i == 0)
    def _():
        for t in range(DEC_SEQ):
            for lt in range(D_MODEL // LANES):
                xs_ref[t * n_seq:(t + 1) * n_seq, lt * LANES:(lt + 1) * LANES] = x4_ref[:, lt, t, :]
        xn = _rmsnorm(xs_ref[...], n1_ref[...]).astype(BF16)
        g_ref[...] = _dot(xn, wg_ref[...]) + bg_ref[...]
        p_ref[...] = _dot(xn, win_ref[...])

        def urow(e, cs):
            if e < POOL_CTX:
                return ctx_ref[e, :, cs]
            return p_ref[(e - POOL_CTX) * n_seq:(e - POOL_CTX + 1) * n_seq, cs]

        for gi, w in enumerate(POOL_WINDOWS):
            cs = slice(gi * POOL_GC, (gi + 1) * POOL_GC)
            for t in range(DEC_SEQ):
                e = POOL_CTX + t
                wsum = urow(e, cs)
                for j in range(1, w):
                    wsum = wsum + urow(e - j, cs)
                z = wsum * (1.0 / w) - urow(e, cs)
                mix_ref[t * n_seq:(t + 1) * n_seq, cs] = (
                    _dot(z.astype(BF16), wpool_ref[gi]) * pscale_ref[:, cs]).astype(BF16)
        for e in range(DEC_SEQ, POOL_CTX + DEC_SEQ):
            pool_ref[e - DEC_SEQ] = urow(e, slice(0, POOL_WIDTH))

        gs = _sample_gates(g_ref, m0_ref[...])
        dec_ref[...] = gs["decay"]
        mout_ref[...] = gs["m"][DEC_SEQ - 1]
        def kw_slab(t):
            rows = slice(t * n_seq, (t + 1) * n_seq)
            return jnp.concatenate([p_ref[rows, _hcols(COL_K, h)] * gs["ws"][t][:, h:h + 1]
                                    for h in range(N_HEADS)], axis=1)

        kw = [kw_slab(t) for t in range(DEC_SEQ)]
        nout_ref[...] = (jnp.concatenate([jnp.broadcast_to(gs["decay"][:, h:h + 1], (n_seq, HEAD_DIM))
                                          for h in range(N_HEADS)], axis=1) * n0_ref[...]
                         + kw[0] + kw[1] + kw[2] + kw[3])
        seq_major(tmp_ref, lambda t: kw[t])
        for h in range(N_HEADS):
            for gi in range(n_grp):
                kwt_ref[h, gi] = tmp_ref[h, gi * LANES:(gi + 1) * LANES, :].T.astype(BF16)
        seq_major(tmp_ref, lambda t: p_ref[t * n_seq:(t + 1) * n_seq, COL_V:COL_O])
        for h in range(N_HEADS):
            for gi in range(n_grp):
                vsm_ref[h, gi] = tmp_ref[h, gi * LANES:(gi + 1) * LANES, :].astype(BF16)
        seq_major(qsm_ref, lambda t: p_ref[t * n_seq:(t + 1) * n_seq, COL_Q:COL_K] * scale)

    lane_sl = lax.broadcasted_iota(jnp.int32, (HEAD_DIM, LANES), 1) // DEC_SEQ
    first = lax.broadcasted_iota(jnp.int32, (SUBLANES, HEAD_DIM), 0) < DEC_SEQ
    grp = (i * SAMPLE_BLOCK) // (LANES // DEC_SEQ)
    for jp in range(SAMPLE_BLOCK // 2):
        rows = pl.ds(pl.multiple_of(i * (SAMPLE_BLOCK * DEC_SEQ) + jp * SUBLANES, SUBLANES), SUBLANES)
        for h in range(N_HEADS):
            hs = _hcols(0, h)
            q8 = qsm_ref[h, rows, :].astype(BF16)
            readout = []
            for a in range(2):
                j = 2 * jp + a
                seq = i * SAMPLE_BLOCK + j
                c0 = cin_ref[j, h]
                readout.append(_dot(q8, c0.astype(BF16)))
                lhs = jnp.where(lane_sl == seq % (LANES // DEC_SEQ), kwt_ref[h, grp], jnp.zeros((), BF16))
                cout_ref[j, h] = dec_ref[pl.ds(seq, 1), :][:, h:h + 1] * c0 + _dot(lhs, vsm_ref[h, grp])
            qcsm_ref[h, rows, :] = jnp.where(first, readout[0], readout[1])

    @pl.when(i == pl.num_programs(0) - 1)
    def _():
        gs = _sample_gates(g_ref, m0_ref[...])
        for h in range(N_HEADS):
            hs = _hcols(0, h)
            n0 = n0_ref[:, hs]
            qs = [p_ref[t * n_seq:(t + 1) * n_seq, _hcols(COL_Q, h)] * scale for t in range(DEC_SEQ)]
            ks = [p_ref[t * n_seq:(t + 1) * n_seq, _hcols(COL_K, h)] for t in range(DEC_SEQ)]
            for t in range(DEC_SEQ):
                rows = slice(t * n_seq, (t + 1) * n_seq)
                inter = jnp.exp(m0_ref[...] - gs["big_m"][t])[:, h:h + 1]
                num = inter * qcsm_ref[h, pl.ds(t, n_seq, stride=DEC_SEQ), :]
                den = inter * jnp.sum(qs[t] * n0, axis=-1, keepdims=True)
                for s in range(t + 1):
                    pts = (jnp.sum(qs[t] * ks[s], axis=-1, keepdims=True)
                           * jnp.exp(gs["r"][s] - gs["big_m"][t])[:, h:h + 1])
                    num = num + pts * p_ref[s * n_seq:(s + 1) * n_seq, _hcols(COL_V, h)]
                    den = den + pts
                hh = num / jnp.maximum(jnp.abs(den), jnp.exp(-gs["m"][t])[:, h:h + 1])
                hh = hh * lax.rsqrt(jnp.mean(hh * hh, axis=-1, keepdims=True) + EPS) * gain_ref[:, hs]
                mix_ref[rows, _hcols(POOL_WIDTH, h)] = (
                    hh * jax.nn.sigmoid(p_ref[rows, _hcols(COL_O, h)])).astype(BF16)
        x1_ref[...] = xs_ref[...] + _dot(mix_ref[...], wout_ref[...])


def _ffn_kernel(xs_ref, xp_ref, n2_ref, nf_ref, wup_hbm, wdown_hbm, yp_ref, ys_ref,
                wup_ref, wdown_ref, stage_ref, sem):
    i = pl.program_id(0)
    last = pl.num_programs(0) - 1

    def fetch(c, up):
        cols = pl.ds(c * FFN_CHUNK, FFN_CHUNK)
        src = wup_hbm.at[:, cols] if up else wdown_hbm.at[cols, :]
        slot = 0 if up else 1
        return pltpu.make_async_copy(src, stage_ref.at[slot], sem.at[slot])

    def store_rows(rows, y):
        yp_ref[rows, :] = y

    def store_tiles(rows, y):
        n_seq = ys_ref.shape[0]
        for k in range(FFN_SUB // n_seq):
            t = rows.start // n_seq + k
            for lt in range(D_MODEL // LANES):
                ys_ref[:, lt, t, :] = y[k * n_seq:(k + 1) * n_seq, lt * LANES:(lt + 1) * LANES]

    def mlp(x_ref, store, load_weights):
        subs = [slice(j * FFN_SUB, (j + 1) * FFN_SUB) for j in range(x_ref.shape[0] // FFN_SUB)]
        n_chunk = D_FF // FFN_CHUNK
        if load_weights:
            fetch(0, True).start()
            fetch(0, False).start()
        xn = [_rmsnorm(x_ref[rows, :], n2_ref[...]).astype(BF16) for rows in subs]
        acc = [x_ref[rows, :] for rows in subs]
        for c in range(n_chunk):
            if load_weights:
                fetch(c, True).wait()
                wup_ref[c] = stage_ref[0].astype(BF16)
                if c + 1 < n_chunk:
                    fetch(c + 1, True).start()
                fetch(c, False).wait()
                wdown_ref[c] = stage_ref[1].astype(BF16)
                if c + 1 < n_chunk:
                    fetch(c + 1, False).start()
            for j in range(len(subs)):
                a = jnp.square(jnp.maximum(_dot(xn[j], wup_ref[c]), 0.0)).astype(BF16)
                acc[j] = acc[j] + _dot(a, wdown_ref[c])
        for j, rows in enumerate(subs):
            store(rows, _rmsnorm(acc[j], nf_ref[...]))

    pl.when(i == 0)(lambda: mlp(xp_ref, store_rows, True))
    pl.when((i > 0) & (i < last))(lambda: mlp(xp_ref, store_rows, False))
    pl.when(i == last)(lambda: mlp(xs_ref, store_tiles, False))


def _full(shape):
    return pl.BlockSpec(shape, lambda *_: (0,) * len(shape))


def _ffn(xs, xp, n2, nf, w_up, w_down, n_seq):
    assert D_MODEL == FFN_CHUNK
    assert FFN_SUB % n_seq == 0
    n_prompt = xp.shape[0] // FFN_TILE
    n_chunk = D_FF // FFN_CHUNK
    ys_shape = (n_seq, D_MODEL // LANES, xs.shape[0] // n_seq, LANES)
    prompt_tile = pl.BlockSpec((FFN_TILE, D_MODEL), lambda i: (jnp.minimum(i, n_prompt - 1), 0))
    return pl.pallas_call(
        _ffn_kernel,
        grid=(n_prompt + 1,),
        in_specs=[_full(xs.shape), prompt_tile, _full((1, D_MODEL)), _full((1, D_MODEL)),
                  pl.BlockSpec(memory_space=pl.ANY), pl.BlockSpec(memory_space=pl.ANY)],
        out_specs=[prompt_tile, _full(ys_shape)],
        out_shape=(jax.ShapeDtypeStruct(xp.shape, F32), jax.ShapeDtypeStruct(ys_shape, F32)),
        scratch_shapes=[pltpu.VMEM((n_chunk, D_MODEL, FFN_CHUNK), BF16),
                        pltpu.VMEM((n_chunk, FFN_CHUNK, D_MODEL), BF16),
                        pltpu.VMEM((2, D_MODEL, FFN_CHUNK), F32),
                        pltpu.SemaphoreType.DMA((2,))],
        compiler_params=pltpu.CompilerParams(dimension_semantics=("arbitrary",), vmem_limit_bytes=VMEM_LIMIT),
        name="ffn",
    )(xs, xp, n2, nf, w_up, w_down)


def kernel(x_prompt, x_sample, state_pool, state_C, state_n, state_m, meta_tokens, norm1, w_in, b_gate,
           w_pool, pool_scale, head_gain, w_out, norm2, w_up, w_down, norm_f):
    B, S, _ = x_prompt.shape
    n_seq = x_sample.shape[0]
    rows_s = DEC_SEQ * n_seq
    assert w_in.shape[0] == 1 and S % PROMPT_TILE == 0 and n_seq % SAMPLE_BLOCK == 0
    assert (B * S) % FFN_TILE == 0 and rows_s % FFN_SUB == 0

    win = w_in[0].astype(BF16)
    gpad = ((0, 0), (0, LANES - N_HEADS))
    wg = jnp.concatenate([jnp.pad(w_in[0][:, IN_MAIN:IN_MAIN + N_HEADS], gpad),
                          jnp.pad(w_in[0][:, IN_MAIN + N_HEADS:], gpad)], axis=1).astype(BF16)
    bg = jnp.concatenate([jnp.pad(b_gate[0][:N_HEADS], (0, LANES - N_HEADS)),
                          jnp.pad(b_gate[0][N_HEADS:], (0, LANES - N_HEADS))]).reshape(1, GATE_W)
    n1 = norm1[0].reshape(1, D_MODEL)
    n2 = norm2[0].reshape(1, D_MODEL)
    nf = norm_f.reshape(1, D_MODEL)
    pscale = pool_scale[0].reshape(1, POOL_WIDTH)
    gain = head_gain[0].reshape(1, MLSTM_WIDTH)

    weight_specs = [_full((1, D_MODEL)), _full((D_MODEL, IN_MAIN)), _full((D_MODEL, GATE_W)), _full((1, GATE_W))]
    mixer_specs = [_full((len(POOL_WINDOWS), POOL_GC, POOL_GC)), _full((1, POOL_WIDTH)),
                   _full((1, MLSTM_WIDTH)), _full((D_MODEL, D_MODEL))]

    u_meta, c_meta, m_meta, wout, wpool = pl.pallas_call(
        _meta_kernel,
        out_shape=(jax.ShapeDtypeStruct((N_META, POOL_WIDTH), F32),
                   jax.ShapeDtypeStruct((N_HEADS, HEAD_DIM, AUG), F32),
                   jax.ShapeDtypeStruct((SUBLANES, LANES), F32),
                   jax.ShapeDtypeStruct((D_MODEL, D_MODEL), BF16),
                   jax.ShapeDtypeStruct((len(POOL_WINDOWS), POOL_GC, POOL_GC), BF16)),
        compiler_params=pltpu.CompilerParams(vmem_limit_bytes=VMEM_LIMIT),
        name="meta_prefix",
    )(meta_tokens, n1, win, wg, bg, w_out[0], w_pool[0])

    nt = S // PROMPT_TILE
    n_tiles = B * nt
    assert n_tiles % 2 == 0 and n_tiles >= 2
    assert D_MODEL % (n_tiles * 2 * SUBLANES) == 0

    def tile_at(lag):
        def index(s):
            j = jnp.clip(s - lag, 0, n_tiles - 1)
            return (j // nt, j % nt, 0)
        return index

    def mixed_seq(ndim):
        return lambda s: (jnp.clip(s - 1, 0, n_tiles - 1) // nt,) + (0,) * (ndim - 1)

    x1_p, pool_p, _, m_p, c_p, n_p = pl.pallas_call(
        functools.partial(_prompt_kernel, nt),
        grid=(n_tiles + 2,),
        in_specs=[pl.BlockSpec((1, PROMPT_TILE, D_MODEL), tile_at(0)),
                  pl.BlockSpec((1, PROMPT_TILE, D_MODEL), tile_at(2)),
                  _full((N_META, POOL_WIDTH)), _full((N_HEADS, HEAD_DIM, AUG)), _full((SUBLANES, LANES))]
        + weight_specs + mixer_specs,
        out_specs=[pl.BlockSpec((1, PROMPT_TILE, D_MODEL), tile_at(2)),
                   pl.BlockSpec((1, CTX_PAD, POOL_WIDTH), mixed_seq(3)),
                   pl.BlockSpec((1, N_HEADS, HEAD_DIM, AUG), mixed_seq(4)),
                   pl.BlockSpec((1, SUBLANES, LANES), mixed_seq(3)),
                   pl.BlockSpec((1, N_HEADS, HEAD_DIM, HEAD_DIM), mixed_seq(4)),
                   pl.BlockSpec((1, N_HEADS, HEAD_DIM), mixed_seq(3))],
        out_shape=(jax.ShapeDtypeStruct((B, S, D_MODEL), F32),
                   jax.ShapeDtypeStruct((B, CTX_PAD, POOL_WIDTH), F32),
                   jax.ShapeDtypeStruct((B, N_HEADS, HEAD_DIM, AUG), F32),
                   jax.ShapeDtypeStruct((B, SUBLANES, LANES), F32),
                   jax.ShapeDtypeStruct((B, N_HEADS, HEAD_DIM, HEAD_DIM), F32),
                   jax.ShapeDtypeStruct((B, N_HEADS, HEAD_DIM), F32)),
        scratch_shapes=[pltpu.VMEM((CTX_PAD + CHUNK, POOL_WIDTH), F32),
                        pltpu.VMEM((PROMPT_TILE, D_MODEL), BF16),
                        pltpu.VMEM((PROMPT_TILE, IN_MAIN), F32), pltpu.VMEM((PROMPT_TILE, IN_MAIN), F32),
                        pltpu.VMEM((PROMPT_TILE, GATE_W), F32), pltpu.VMEM((PROMPT_TILE, GATE_W), F32),
                        pltpu.VMEM((PROMPT_TILE, D_MODEL), BF16), pltpu.VMEM((PROMPT_TILE, D_MODEL), BF16)],
        compiler_params=pltpu.CompilerParams(dimension_semantics=("arbitrary",), vmem_limit_bytes=VMEM_LIMIT),
        name="prompt_mixer",
    )(x_prompt, x_prompt, u_meta, c_meta, m_meta, n1, win, wg, bg, wpool, pscale, gain, wout)

    xs_tiles = x_sample.reshape(n_seq, DEC_SEQ, D_MODEL // LANES, LANES).transpose(0, 2, 1, 3)
    x1_p, xs_tiles = lax.optimization_barrier((x1_p, xs_tiles))
    ctx_t = state_pool[0].transpose(1, 0, 2)
    n0 = state_n[0].reshape(n_seq, MLSTM_WIDTH)
    m0 = jnp.pad(state_m[0], ((0, 0), (0, LANES - N_HEADS)))
    state_block = pl.BlockSpec((SAMPLE_BLOCK, N_HEADS, HEAD_DIM, HEAD_DIM), lambda i: (i, 0, 0, 0))
    x1_s, pool_s, c_s, n_s, m_s = pl.pallas_call(
        _sample_kernel,
        grid=(n_seq // SAMPLE_BLOCK,),
        in_specs=[_full(xs_tiles.shape), _full((POOL_CTX, n_seq, POOL_WIDTH)), state_block,
                  _full((n_seq, MLSTM_WIDTH)), _full((n_seq, LANES))] + weight_specs + mixer_specs,
        out_specs=[_full((rows_s, D_MODEL)), _full((POOL_CTX, n_seq, POOL_WIDTH)), state_block,
                   _full((n_seq, MLSTM_WIDTH)), _full((n_seq, LANES))],
        out_shape=(jax.ShapeDtypeStruct((rows_s, D_MODEL), F32),
                   jax.ShapeDtypeStruct((POOL_CTX, n_seq, POOL_WIDTH), F32),
                   jax.ShapeDtypeStruct((n_seq, N_HEADS, HEAD_DIM, HEAD_DIM), F32),
                   jax.ShapeDtypeStruct((n_seq, MLSTM_WIDTH), F32),
                   jax.ShapeDtypeStruct((n_seq, LANES), F32)),
        scratch_shapes=[pltpu.VMEM((rows_s, D_MODEL), F32),
                        pltpu.VMEM((rows_s, IN_MAIN), F32),
                        pltpu.VMEM((rows_s, GATE_W), F32),
                        pltpu.VMEM((N_HEADS, rows_s, HEAD_DIM), F32),
                        pltpu.VMEM((N_HEADS, rows_s, HEAD_DIM), F32),
                        pltpu.VMEM((N_HEADS, rows_s, HEAD_DIM), F32),
                        pltpu.VMEM((N_HEADS, rows_s // LANES, HEAD_DIM, LANES), BF16),
                        pltpu.VMEM((N_HEADS, rows_s // LANES, LANES, HEAD_DIM), BF16),
                        pltpu.VMEM((n_seq, LANES), F32),
                        pltpu.VMEM((rows_s, D_MODEL), BF16)],
        compiler_params=pltpu.CompilerParams(dimension_semantics=("arbitrary",), vmem_limit_bytes=VMEM_LIMIT),
        name="sample_mixer",
    )(xs_tiles, ctx_t, state_C[0], n0, m0, n1, win, wg, bg, wpool, pscale, gain, wout)

    y_p, y_s = _ffn(x1_s, x1_p.reshape(B * S, D_MODEL), n2, nf, w_up[0], w_down[0], n_seq)
    y_p = y_p.reshape(B, S, D_MODEL)
    y_s = y_s.transpose(0, 2, 1, 3).reshape(n_seq, DEC_SEQ, D_MODEL)

    return (y_p, y_s,
            pool_p[:, CTX_PAD - POOL_CTX:, :][None],
            c_p[None],
            n_p[None],
            m_p[:, 0, :N_HEADS][None],
            pool_s.transpose(1, 0, 2)[None],
            c_s[None],
            n_s.reshape(n_seq, N_HEADS, HEAD_DIM)[None],
            m_s[:, :N_HEADS][None])
```

```python
import functools

import jax
import jax.numpy as jnp
from jax import lax
from jax.experimental import pallas as pl
from jax.experimental.pallas import tpu as pltpu

D_MODEL = 1024
N_META = 16
POOL_WIDTH = 512
POOL_WINDOWS = (2, 4, 8, 16)
POOL_GC = 128
POOL_CTX = 15
MLSTM_WIDTH = 512
N_HEADS = 4
HEAD_DIM = 128
D_FF = 4096
EPS = 1e-6
DEC_SEQ = 4

LANES = 128
SUBLANES = 8
COL_Q = POOL_WIDTH
COL_K = COL_Q + MLSTM_WIDTH
COL_V = COL_K + MLSTM_WIDTH
COL_O = COL_V + MLSTM_WIDTH
IN_MAIN = COL_O + MLSTM_WIDTH
GATE_W = 2 * LANES
AUG = 2 * HEAD_DIM
CTX_PAD = 16
CHUNK = 256
PROMPT_TILE = 512
IN_CHUNK = 512
FFN_TILE = 1024
FFN_SUB = 256
FFN_CHUNK = 1024
SAMPLE_BLOCK = 16
NEG_BIG = -1e30
VMEM_LIMIT = 56 * 1024 * 1024

F32 = jnp.float32
BF16 = jnp.bfloat16


def _rmsnorm(x, g):
    return x * lax.rsqrt(jnp.mean(x * x, axis=-1, keepdims=True) + EPS) * g


def _log_sigmoid(x):
    return jnp.minimum(x, 0.0) - jnp.log1p(jnp.exp(-jnp.abs(x)))


def _dot(a, b):
    return jnp.dot(a, b, preferred_element_type=F32)


def _dot_nt(a, b):
    return lax.dot_general(a, b, (((1,), (1,)), ((), ())), preferred_element_type=F32)


def _hcols(base, h):
    return slice(base + h * HEAD_DIM, base + (h + 1) * HEAD_DIM)


def _row_scan(x, op):
    n = x.shape[0]
    row = lax.broadcasted_iota(jnp.int32, x.shape, 0)
    sh = 1
    while sh < n:
        x = jnp.where(row >= sh, op(x, pltpu.roll(x, sh, axis=0)), x)
        sh *= 2
    return x


def _gate_cols(g, m0, n_valid):
    R = g.shape[0]
    b = _row_scan(_log_sigmoid(g[:, LANES:]), jnp.add)
    r = g[:, :LANES] - b
    big_m = jnp.maximum(m0, _row_scan(r, jnp.maximum))
    m = b + big_m
    last = n_valid - 1
    m_top = big_m[last:last + 1, :]
    ws = jnp.exp(r - m_top)
    if n_valid < R:
        ws = jnp.where(lax.broadcasted_iota(jnp.int32, ws.shape, 0) < n_valid, ws, 0.0)
    return dict(r=r, big_m=big_m, inter=jnp.exp(m0 - big_m), eneg=jnp.exp(-m), ws=ws,
                decay=jnp.exp(m0 - m_top), m_last=m[last:last + 1, :])


def _value_aug(v, h):
    lane = lax.broadcasted_iota(jnp.int32, (v.shape[0], HEAD_DIM), 1)
    return jnp.concatenate([v, jnp.where(lane == h, 1.0, 0.0)], axis=1).astype(BF16)


def _meta_kernel(x_ref, n1_ref, win_ref, wg_ref, bg_ref, wout_ref, wpool_ref,
                 u_ref, c_ref, m_ref, wout_bf_ref, wpool_bf_ref):
    wout_bf_ref[...] = wout_ref[...].astype(BF16)
    wpool_bf_ref[...] = wpool_ref[...].astype(BF16)
    x = jnp.concatenate([x_ref[...], jnp.zeros((LANES - N_META, D_MODEL), F32)], axis=0)
    xn = _rmsnorm(x, n1_ref[...]).astype(BF16)
    gc = _gate_cols(_dot(xn, wg_ref[...]) + bg_ref[...], jnp.zeros((1, LANES), F32), N_META)
    p = _dot(xn, win_ref[:, 0:IN_MAIN])
    u_ref[...] = p[0:N_META, 0:POOL_WIDTH]
    for h in range(N_HEADS):
        kw_t = (p[:, _hcols(COL_K, h)] * gc["ws"][:, h:h + 1]).T.astype(BF16)
        c_ref[h] = _dot(kw_t, _value_aug(p[:, _hcols(COL_V, h)], h))
    m_ref[...] = jnp.broadcast_to(gc["m_last"], (SUBLANES, LANES))


def _prompt_kernel(nt, xa_ref, xc_ref, umeta_ref, cmeta_ref, mmeta_ref, n1_ref, win_ref, wg_ref, bg_ref,
                   wpool_ref, pscale_ref, gain_ref, wout_ref,
                   x1_ref, pool_ref, c_ref, m_ref, cfin_ref, nfin_ref,
                   uext_ref, xn_ref, p0_ref, p1_ref, g0_ref, g1_ref, mix0_ref, mix1_ref):
    s = pl.program_id(0)
    n_tiles = pl.num_programs(0) - 2

    @pl.when((jnp.maximum(s - 1, 0) % nt == 0) & (s <= n_tiles))
    def _():
        c_ref[0] = cmeta_ref[...]
        m_ref[0] = mmeta_ref[...]
        uext_ref[0:CTX_PAD, :] = umeta_ref[...]

    stage = functools.partial(_prompt_step, xa_ref, xc_ref, n1_ref, win_ref, wg_ref, bg_ref, wpool_ref,
                              pscale_ref, gain_ref, wout_ref, x1_ref, pool_ref, c_ref, m_ref, uext_ref, xn_ref)
    even = (p0_ref, p1_ref, g0_ref, g1_ref, mix0_ref, mix1_ref)
    odd = (p1_ref, p0_ref, g1_ref, g0_ref, mix1_ref, mix0_ref)
    steady = (s >= 2) & (s < n_tiles)
    pl.when(s == 0)(lambda: stage(*even, project=True, mix=False, emit=False))
    pl.when(s == 1)(lambda: stage(*odd, project=True, mix=True, emit=False))
    pl.when(steady & (s % 2 == 0))(lambda: stage(*even, project=True, mix=True, emit=True))
    pl.when(steady & (s % 2 == 1))(lambda: stage(*odd, project=True, mix=True, emit=True))
    pl.when(s == n_tiles)(lambda: stage(*even, project=False, mix=True, emit=True))
    pl.when(s == n_tiles + 1)(lambda: stage(*odd, project=False, mix=False, emit=True))

    @pl.when((s >= 1) & (s <= n_tiles) & ((s - 1) % nt == nt - 1))
    def _():
        for h in range(N_HEADS):
            cfin_ref[0, h] = c_ref[0, h, :, 0:HEAD_DIM]
            nfin_ref[0, h:h + 1, :] = c_ref[0, h, :, HEAD_DIM:].T[h:h + 1, :]


def _prompt_step(xa_ref, xc_ref, n1_ref, win_ref, wg_ref, bg_ref, wpool_ref, pscale_ref, gain_ref, wout_ref,
                 x1_ref, pool_ref, c_ref, m_ref, uext_ref, xn_ref,
                 p_new, pb, g_new, g_old, mx, mix_old, *, project, mix, emit):
    dense = []
    if emit:
        def out_proj():
            x1_ref[0] = xc_ref[0] + _dot(mix_old[...], wout_ref[...])
        dense.append(out_proj)
    if project:
        def in_norm():
            xn_ref[...] = _rmsnorm(xa_ref[0], n1_ref[...]).astype(BF16)
            g_new[...] = _dot(xn_ref[...], wg_ref[...]) + bg_ref[...]

        def in_chunk(c):
            cs = slice(c * IN_CHUNK, (c + 1) * IN_CHUNK)
            p_new[:, cs] = _dot(xn_ref[...], win_ref[:, cs])
        dense.append(in_norm)
        dense += [functools.partial(in_chunk, c) for c in range(IN_MAIN // IN_CHUNK)]

    pieces = []
    if mix:
        for ci in range(PROMPT_TILE // CHUNK):
            pieces.append(_mix_chunk(pb, g_old, mx, slice(ci * CHUNK, (ci + 1) * CHUNK),
                                     wpool_ref, pscale_ref, gain_ref, pool_ref, c_ref, m_ref, uext_ref))
    n_mix = (1 + N_HEADS) * len(pieces)
    done = 0
    for k, piece in enumerate(p for chunk in pieces for p in chunk):
        while done < len(dense) and done * n_mix < k * len(dense):
            dense[done]()
            done += 1
    for job in dense[done:]:
        job()


def _mix_chunk(pb, g_old, mx, rows, wpool_ref, pscale_ref, gain_ref, pool_ref, c_ref, m_ref, uext_ref):
    R = CHUNK
    gc = _gate_cols(g_old[rows, :], m_ref[0, 0:1, :], R)
    r_t = gc["r"].T

    uext_ref[CTX_PAD:CTX_PAD + R, :] = pb[rows, 0:POOL_WIDTH]
    for gi, w in enumerate(POOL_WINDOWS):
        cs = slice(gi * POOL_GC, (gi + 1) * POOL_GC)
        acc = uext_ref[:, cs]
        sh = 1
        while sh < w:
            acc = acc + pltpu.roll(acc, sh, axis=0)
            sh *= 2
        z = acc[CTX_PAD:, :] * (1.0 / w) - uext_ref[CTX_PAD:CTX_PAD + R, cs]
        mx[rows, cs] = (_dot(z.astype(BF16), wpool_ref[gi]) * pscale_ref[:, cs]).astype(BF16)
    tail = uext_ref[R:R + CTX_PAD, :]
    uext_ref[0:CTX_PAD, :] = tail
    pool_ref[0] = tail
    m_ref[0] = jnp.broadcast_to(gc["m_last"], (SUBLANES, LANES))
    yield

    causal = (lax.broadcasted_iota(jnp.int32, (R, R), 0) >= lax.broadcasted_iota(jnp.int32, (R, R), 1))
    for h in range(N_HEADS):
        q = (pb[rows, _hcols(COL_Q, h)] * (HEAD_DIM ** -0.5)).astype(BF16)
        k = pb[rows, _hcols(COL_K, h)]
        vaug = _value_aug(pb[rows, _hcols(COL_V, h)], h)
        sc = _dot_nt(q, k.astype(BF16))
        e = jnp.exp(jnp.where(causal, r_t[h:h + 1, :] - gc["big_m"][:, h:h + 1], NEG_BIG))
        c0 = c_ref[0, h]
        num = gc["inter"][:, h:h + 1] * _dot(q, c0.astype(BF16)) + _dot((sc * e).astype(BF16), vaug)
        inv = 1.0 / jnp.maximum(jnp.abs(num[:, HEAD_DIM:]), gc["eneg"])
        hh = num[:, 0:HEAD_DIM] * inv[:, h:h + 1]
        hh = hh * lax.rsqrt(jnp.mean(hh * hh, axis=-1, keepdims=True) + EPS) * gain_ref[:, _hcols(0, h)]
        mx[rows, _hcols(POOL_WIDTH, h)] = (hh * jax.nn.sigmoid(pb[rows, _hcols(COL_O, h)])).astype(BF16)
        kw_t = (k * gc["ws"][:, h:h + 1]).T.astype(BF16)
        c_ref[0, h] = gc["decay"][:, h:h + 1] * c0 + _dot(kw_t, vaug)
        yield


def _sample_gates(g_ref, m0):
    n = m0.shape[0]
    b = jnp.zeros((n, LANES), F32)
    big_m = m0
    r, big_ms, ms = [], [], []
    for t in range(DEC_SEQ):
        g = g_ref[t * n:(t + 1) * n, :]
        b = b + _log_sigmoid(g[:, LANES:])
        r.append(g[:, :LANES] - b)
        big_m = jnp.maximum(big_m, r[-1])
        big_ms.append(big_m)
        ms.append(b + big_m)
    return dict(r=r, big_m=big_ms, m=ms,
                ws=[jnp.exp(r[t] - big_m) for t in range(DEC_SEQ)],
                decay=jnp.exp(m0 - big_m))


def _sample_kernel(x4_ref, ctx_ref, cin_ref, n0_ref, m0_ref, n1_ref, win_ref, wg_ref, bg_ref, wpool_ref,
                   pscale_ref, gain_ref, wout_ref,
                   x1_ref, pool_ref, cout_ref, nout_ref, mout_ref,
                   xs_ref, p_ref, g_ref, tmp_ref, qsm_ref, qcsm_ref, kwt_ref, vsm_ref, dec_ref, mix_ref):
    i = pl.program_id(0)
    n_seq = m0_ref.shape[0]
    n_grp = DEC_SEQ * n_seq // LANES
    scale = HEAD_DIM ** -0.5

    def seq_major(dst_ref, slab_of_t):
        for t in range(DEC_SEQ):
            slab = slab_of_t(t)
            for h in range(N_HEADS):
                dst_ref[h, pl.ds(t, n_seq, stride=DEC_SEQ), :] = slab[:, _hcols(0, h)]

    @pl.when(i == 0)
    def _():
        for t in range(DEC_SEQ):
            for lt in range(D_MODEL // LANES):
                xs_ref[t * n_seq:(t + 1) * n_seq, lt * LANES:(lt + 1) * LANES] = x4_ref[:, lt, t, :]
        xn = _rmsnorm(xs_ref[...], n1_ref[...]).astype(BF16)
        g_ref[...] = _dot(xn, wg_ref[...]) + bg_ref[...]
        p_ref[...] = _dot(xn, win_ref[...])

        def urow(e, cs):
            if e < POOL_CTX:
                return ctx_ref[e, :, cs]
            return p_ref[(e - POOL_CTX) * n_seq:(e - POOL_CTX + 1) * n_seq, cs]

        for gi, w in enumerate(POOL_WINDOWS):
            cs = slice(gi * POOL_GC, (gi + 1) * POOL_GC)
            for t in range(DEC_SEQ):
                e = POOL_CTX + t
                wsum = urow(e, cs)
                for j in range(1, w):
                    wsum = wsum + urow(e - j, cs)
                z = wsum * (1.0 / w) - urow(e, cs)
                mix_ref[t * n_seq:(t + 1) * n_seq, cs] = (
                    _dot(z.astype(BF16), wpool_ref[gi]) * pscale_ref[:, cs]).astype(BF16)
        for e in range(DEC_SEQ, POOL_CTX + DEC_SEQ):
            pool_ref[e - DEC_SEQ] = urow(e, slice(0, POOL_WIDTH))

        gs = _sample_gates(g_ref, m0_ref[...])
        dec_ref[...] = gs["decay"]
        mout_ref[...] = gs["m"][DEC_SEQ - 1]
        def kw_slab(t):
            rows = slice(t * n_seq, (t + 1) * n_seq)
            return jnp.concatenate([p_ref[rows, _hcols(COL_K, h)] * gs["ws"][t][:, h:h + 1]
                                    for h in range(N_HEADS)], axis=1)

        kw = [kw_slab(t) for t in range(DEC_SEQ)]
        nout_ref[...] = (jnp.concatenate([jnp.broadcast_to(gs["decay"][:, h:h + 1], (n_seq, HEAD_DIM))
                                          for h in range(N_HEADS)], axis=1) * n0_ref[...]
                         + kw[0] + kw[1] + kw[2] + kw[3])
        seq_major(tmp_ref, lambda t: kw[t])
        for h in range(N_HEADS):
            for gi in range(n_grp):
                kwt_ref[h, gi] = tmp_ref[h, gi * LANES:(gi + 1) * LANES, :].T.astype(BF16)
        seq_major(tmp_ref, lambda t: p_ref[t * n_seq:(t + 1) * n_seq, COL_V:COL_O])
        for h in range(N_HEADS):
            for gi in range(n_grp):
                vsm_ref[h, gi] = tmp_ref[h, gi * LANES:(gi + 1) * LANES, :].astype(BF16)
        seq_major(qsm_ref, lambda t: p_ref[t * n_seq:(t + 1) * n_seq, COL_Q:COL_K] * scale)

    lane_sl = lax.broadcasted_iota(jnp.int32, (HEAD_DIM, LANES), 1) // DEC_SEQ
    first = lax.broadcasted_iota(jnp.int32, (SUBLANES, HEAD_DIM), 0) < DEC_SEQ
    grp = (i * SAMPLE_BLOCK) // (LANES // DEC_SEQ)
    for jp in range(SAMPLE_BLOCK // 2):
        rows = pl.ds(pl.multiple_of(i * (SAMPLE_BLOCK * DEC_SEQ) + jp * SUBLANES, SUBLANES), SUBLANES)
        for h in range(N_HEADS):
            hs = _hcols(0, h)
            q8 = qsm_ref[h, rows, :].astype(BF16)
            readout = []
            for a in range(2):
                j = 2 * jp + a
                seq = i * SAMPLE_BLOCK + j
                c0 = cin_ref[j, h]
                readout.append(_dot(q8, c0.astype(BF16)))
                lhs = jnp.where(lane_sl == seq % (LANES // DEC_SEQ), kwt_ref[h, grp], jnp.zeros((), BF16))
                cout_ref[j, h] = dec_ref[pl.ds(seq, 1), :][:, h:h + 1] * c0 + _dot(lhs, vsm_ref[h, grp])
            qcsm_ref[h, rows, :] = jnp.where(first, readout[0], readout[1])

    @pl.when(i == pl.num_programs(0) - 1)
    def _():
        gs = _sample_gates(g_ref, m0_ref[...])
        for h in range(N_HEADS):
            hs = _hcols(0, h)
            n0 = n0_ref[:, hs]
            qs = [p_ref[t * n_seq:(t + 1) * n_seq, _hcols(COL_Q, h)] * scale for t in range(DEC_SEQ)]
            ks = [p_ref[t * n_seq:(t + 1) * n_seq, _hcols(COL_K, h)] for t in range(DEC_SEQ)]
            for t in range(DEC_SEQ):
                rows = slice(t * n_seq, (t + 1) * n_seq)
                inter = jnp.exp(m0_ref[...] - gs["big_m"][t])[:, h:h + 1]
                num = inter * qcsm_ref[h, pl.ds(t, n_seq, stride=DEC_SEQ), :]
                den = inter * jnp.sum(qs[t] * n0, axis=-1, keepdims=True)
                for s in range(t + 1):
                    pts = (jnp.sum(qs[t] * ks[s], axis=-1, keepdims=True)
                           * jnp.exp(gs["r"][s] - gs["big_m"][t])[:, h:h + 1])
                    num = num + pts * p_ref[s * n_seq:(s + 1) * n_seq, _hcols(COL_V, h)]
                    den = den + pts
                hh = num / jnp.maximum(jnp.abs(den), jnp.exp(-gs["m"][t])[:, h:h + 1])
                hh = hh * lax.rsqrt(jnp.mean(hh * hh, axis=-1, keepdims=True) + EPS) * gain_ref[:, hs]
                mix_ref[rows, _hcols(POOL_WIDTH, h)] = (
                    hh * jax.nn.sigmoid(p_ref[rows, _hcols(COL_O, h)])).astype(BF16)
        x1_ref[...] = xs_ref[...] + _dot(mix_ref[...], wout_ref[...])


def _ffn_kernel(xs_ref, xp_ref, n2_ref, nf_ref, wup_hbm, wdown_hbm, yp_ref, ys_ref,
                wup_ref, wdown_ref, stage_ref, sem):
    i = pl.program_id(0)
    last = pl.num_programs(0) - 1

    def fetch(c, up):
        cols = pl.ds(c * FFN_CHUNK, FFN_CHUNK)
        src = wup_hbm.at[:, cols] if up else wdown_hbm.at[cols, :]
        slot = 0 if up else 1
        return pltpu.make_async_copy(src, stage_ref.at[slot], sem.at[slot])

    def store_rows(rows, y):
        yp_ref[rows, :] = y

    def store_tiles(rows, y):
        n_seq = ys_ref.shape[0]
        for k in range(FFN_SUB // n_seq):
            t = rows.start // n_seq + k
            for lt in range(D_MODEL // LANES):
                ys_ref[:, lt, t, :] = y[k * n_seq:(k + 1) * n_seq, lt * LANES:(lt + 1) * LANES]

    def mlp(x_ref, store, load_weights, acc_ref=None):
        subs = [slice(j * FFN_SUB, (j + 1) * FFN_SUB) for j in range(x_ref.shape[0] // FFN_SUB)]
        n_chunk = D_FF // FFN_CHUNK
        if load_weights:
            fetch(0, True).start()
            fetch(0, False).start()
        xn = [_rmsnorm(x_ref[rows, :], n2_ref[...]).astype(BF16) for rows in subs]
        if acc_ref is None:
            acc = [x_ref[rows, :] for rows in subs]
        else:
            acc_ref[...] = x_ref[...]
        for c in range(n_chunk):
            if load_weights:
                fetch(c, True).wait()
                wup_ref[c] = stage_ref[0].astype(BF16)
                if c + 1 < n_chunk:
                    fetch(c + 1, True).start()
                fetch(c, False).wait()
                wdown_ref[c] = stage_ref[1].astype(BF16)
                if c + 1 < n_chunk:
                    fetch(c + 1, False).start()
            for j, rows in enumerate(subs):
                a = jnp.square(jnp.maximum(_dot(xn[j], wup_ref[c]), 0.0)).astype(BF16)
                if acc_ref is None:
                    acc[j] = acc[j] + _dot(a, wdown_ref[c])
                else:
                    acc_ref[rows, :] += _dot(a, wdown_ref[c])
        for j, rows in enumerate(subs):
            store(rows, _rmsnorm(acc[j] if acc_ref is None else acc_ref[rows, :], nf_ref[...]))

    pl.when(i == 0)(lambda: mlp(xp_ref, store_rows, True, yp_ref))
    pl.when((i > 0) & (i < last))(lambda: mlp(xp_ref, store_rows, False, yp_ref))
    pl.when(i == last)(lambda: mlp(xs_ref, store_tiles, False))


def _full(shape):
    return pl.BlockSpec(shape, lambda *_: (0,) * len(shape))


def _ffn(xs, xp, n2, nf, w_up, w_down, n_seq):
    assert D_MODEL == FFN_CHUNK
    assert FFN_SUB % n_seq == 0
    n_prompt = xp.shape[0] // FFN_TILE
    n_chunk = D_FF // FFN_CHUNK
    ys_shape = (n_seq, D_MODEL // LANES, xs.shape[0] // n_seq, LANES)
    prompt_tile = pl.BlockSpec((FFN_TILE, D_MODEL), lambda i: (jnp.minimum(i, n_prompt - 1), 0))
    return pl.pallas_call(
        _ffn_kernel,
        grid=(n_prompt + 1,),
        in_specs=[_full(xs.shape), prompt_tile, _full((1, D_MODEL)), _full((1, D_MODEL)),
                  pl.BlockSpec(memory_space=pl.ANY), pl.BlockSpec(memory_space=pl.ANY)],
        out_specs=[prompt_tile, _full(ys_shape)],
        out_shape=(jax.ShapeDtypeStruct(xp.shape, F32), jax.ShapeDtypeStruct(ys_shape, F32)),
        scratch_shapes=[pltpu.VMEM((n_chunk, D_MODEL, FFN_CHUNK), BF16),
                        pltpu.VMEM((n_chunk, FFN_CHUNK, D_MODEL), BF16),
                        pltpu.VMEM((2, D_MODEL, FFN_CHUNK), F32),
                        pltpu.SemaphoreType.DMA((2,))],
        compiler_params=pltpu.CompilerParams(dimension_semantics=("arbitrary",), vmem_limit_bytes=VMEM_LIMIT),
        name="ffn",
    )(xs, xp, n2, nf, w_up, w_down)


def kernel(x_prompt, x_sample, state_pool, state_C, state_n, state_m, meta_tokens, norm1, w_in, b_gate,
           w_pool, pool_scale, head_gain, w_out, norm2, w_up, w_down, norm_f):
    B, S, _ = x_prompt.shape
    n_seq = x_sample.shape[0]
    rows_s = DEC_SEQ * n_seq
    assert w_in.shape[0] == 1 and S % PROMPT_TILE == 0 and n_seq % SAMPLE_BLOCK == 0
    assert (B * S) % FFN_TILE == 0 and rows_s % FFN_SUB == 0

    win = w_in[0].astype(BF16)
    gpad = ((0, 0), (0, LANES - N_HEADS))
    wg = jnp.concatenate([jnp.pad(w_in[0][:, IN_MAIN:IN_MAIN + N_HEADS], gpad),
                          jnp.pad(w_in[0][:, IN_MAIN + N_HEADS:], gpad)], axis=1).astype(BF16)
    bg = jnp.concatenate([jnp.pad(b_gate[0][:N_HEADS], (0, LANES - N_HEADS)),
                          jnp.pad(b_gate[0][N_HEADS:], (0, LANES - N_HEADS))]).reshape(1, GATE_W)
    n1 = norm1[0].reshape(1, D_MODEL)
    n2 = norm2[0].reshape(1, D_MODEL)
    nf = norm_f.reshape(1, D_MODEL)
    pscale = pool_scale[0].reshape(1, POOL_WIDTH)
    gain = head_gain[0].reshape(1, MLSTM_WIDTH)

    weight_specs = [_full((1, D_MODEL)), _full((D_MODEL, IN_MAIN)), _full((D_MODEL, GATE_W)), _full((1, GATE_W))]
    mixer_specs = [_full((len(POOL_WINDOWS), POOL_GC, POOL_GC)), _full((1, POOL_WIDTH)),
                   _full((1, MLSTM_WIDTH)), _full((D_MODEL, D_MODEL))]

    u_meta, c_meta, m_meta, wout, wpool = pl.pallas_call(
        _meta_kernel,
        out_shape=(jax.ShapeDtypeStruct((N_META, POOL_WIDTH), F32),
                   jax.ShapeDtypeStruct((N_HEADS, HEAD_DIM, AUG), F32),
                   jax.ShapeDtypeStruct((SUBLANES, LANES), F32),
                   jax.ShapeDtypeStruct((D_MODEL, D_MODEL), BF16),
                   jax.ShapeDtypeStruct((len(POOL_WINDOWS), POOL_GC, POOL_GC), BF16)),
        compiler_params=pltpu.CompilerParams(vmem_limit_bytes=VMEM_LIMIT),
        name="meta_prefix",
    )(meta_tokens, n1, win, wg, bg, w_out[0], w_pool[0])

    nt = S // PROMPT_TILE
    n_tiles = B * nt
    assert n_tiles % 2 == 0 and n_tiles >= 2
    assert D_MODEL % (n_tiles * 2 * SUBLANES) == 0

    def tile_at(lag):
        def index(s):
            j = jnp.clip(s - lag, 0, n_tiles - 1)
            return (j // nt, j % nt, 0)
        return index

    def mixed_seq(ndim):
        return lambda s: (jnp.clip(s - 1, 0, n_tiles - 1) // nt,) + (0,) * (ndim - 1)

    x1_p, pool_p, _, m_p, c_p, n_p = pl.pallas_call(
        functools.partial(_prompt_kernel, nt),
        grid=(n_tiles + 2,),
        in_specs=[pl.BlockSpec((1, PROMPT_TILE, D_MODEL), tile_at(0)),
                  pl.BlockSpec((1, PROMPT_TILE, D_MODEL), tile_at(2)),
                  _full((N_META, POOL_WIDTH)), _full((N_HEADS, HEAD_DIM, AUG)), _full((SUBLANES, LANES))]
        + weight_specs + mixer_specs,
        out_specs=[pl.BlockSpec((1, PROMPT_TILE, D_MODEL), tile_at(2)),
                   pl.BlockSpec((1, CTX_PAD, POOL_WIDTH), mixed_seq(3)),
                   pl.BlockSpec((1, N_HEADS, HEAD_DIM, AUG), mixed_seq(4)),
                   pl.BlockSpec((1, SUBLANES, LANES), mixed_seq(3)),
                   pl.BlockSpec((1, N_HEADS, HEAD_DIM, HEAD_DIM), mixed_seq(4)),
                   pl.BlockSpec((1, N_HEADS, HEAD_DIM), mixed_seq(3))],
        out_shape=(jax.ShapeDtypeStruct((B, S, D_MODEL), F32),
                   jax.ShapeDtypeStruct((B, CTX_PAD, POOL_WIDTH), F32),
                   jax.ShapeDtypeStruct((B, N_HEADS, HEAD_DIM, AUG), F32),
                   jax.ShapeDtypeStruct((B, SUBLANES, LANES), F32),
                   jax.ShapeDtypeStruct((B, N_HEADS, HEAD_DIM, HEAD_DIM), F32),
                   jax.ShapeDtypeStruct((B, N_HEADS, HEAD_DIM), F32)),
        scratch_shapes=[pltpu.VMEM((CTX_PAD + CHUNK, POOL_WIDTH), F32),
                        pltpu.VMEM((PROMPT_TILE, D_MODEL), BF16),
                        pltpu.VMEM((PROMPT_TILE, IN_MAIN), F32), pltpu.VMEM((PROMPT_TILE, IN_MAIN), F32),
                        pltpu.VMEM((PROMPT_TILE, GATE_W), F32), pltpu.VMEM((PROMPT_TILE, GATE_W), F32),
                        pltpu.VMEM((PROMPT_TILE, D_MODEL), BF16), pltpu.VMEM((PROMPT_TILE, D_MODEL), BF16)],
        compiler_params=pltpu.CompilerParams(dimension_semantics=("arbitrary",), vmem_limit_bytes=VMEM_LIMIT),
        name="prompt_mixer",
    )(x_prompt, x_prompt, u_meta, c_meta, m_meta, n1, win, wg, bg, wpool, pscale, gain, wout)

    xs_tiles = x_sample.reshape(n_seq, DEC_SEQ, D_MODEL // LANES, LANES).transpose(0, 2, 1, 3)
    x1_p, xs_tiles = lax.optimization_barrier((x1_p, xs_tiles))
    ctx_t = state_pool[0].transpose(1, 0, 2)
    n0 = state_n[0].reshape(n_seq, MLSTM_WIDTH)
    m0 = jnp.pad(state_m[0], ((0, 0), (0, LANES - N_HEADS)))
    state_block = pl.BlockSpec((SAMPLE_BLOCK, N_HEADS, HEAD_DIM, HEAD_DIM), lambda i: (i, 0, 0, 0))
    x1_s, pool_s, c_s, n_s, m_s = pl.pallas_call(
        _sample_kernel,
        grid=(n_seq // SAMPLE_BLOCK,),
        in_specs=[_full(xs_tiles.shape), _full((POOL_CTX, n_seq, POOL_WIDTH)), state_block,
                  _full((n_seq, MLSTM_WIDTH)), _full((n_seq, LANES))] + weight_specs + mixer_specs,
        out_specs=[_full((rows_s, D_MODEL)), _full((POOL_CTX, n_seq, POOL_WIDTH)), state_block,
                   _full((n_seq, MLSTM_WIDTH)), _full((n_seq, LANES))],
        out_shape=(jax.ShapeDtypeStruct((rows_s, D_MODEL), F32),
                   jax.ShapeDtypeStruct((POOL_CTX, n_seq, POOL_WIDTH), F32),
                   jax.ShapeDtypeStruct((n_seq, N_HEADS, HEAD_DIM, HEAD_DIM), F32),
                   jax.ShapeDtypeStruct((n_seq, MLSTM_WIDTH), F32),
                   jax.ShapeDtypeStruct((n_seq, LANES), F32)),
        scratch_shapes=[pltpu.VMEM((rows_s, D_MODEL), F32),
                        pltpu.VMEM((rows_s, IN_MAIN), F32),
                        pltpu.VMEM((rows_s, GATE_W), F32),
                        pltpu.VMEM((N_HEADS, rows_s, HEAD_DIM), F32),
                        pltpu.VMEM((N_HEADS, rows_s, HEAD_DIM), F32),
                        pltpu.VMEM((N_HEADS, rows_s, HEAD_DIM), F32),
                        pltpu.VMEM((N_HEADS, rows_s // LANES, HEAD_DIM, LANES), BF16),
                        pltpu.VMEM((N_HEADS, rows_s // LANES, LANES, HEAD_DIM), BF16),
                        pltpu.VMEM((n_seq, LANES), F32),
                        pltpu.VMEM((rows_s, D_MODEL), BF16)],
        compiler_params=pltpu.CompilerParams(dimension_semantics=("arbitrary",), vmem_limit_bytes=VMEM_LIMIT),
        name="sample_mixer",
    )(xs_tiles, ctx_t, state_C[0], n0, m0, n1, win, wg, bg, wpool, pscale, gain, wout)

    y_p, y_s = _ffn(x1_s, x1_p.reshape(B * S, D_MODEL), n2, nf, w_up[0], w_down[0], n_seq)
    y_p = y_p.reshape(B, S, D_MODEL)
    y_s = y_s.transpose(0, 2, 1, 3).reshape(n_seq, DEC_SEQ, D_MODEL)

    return (y_p, y_s,
            pool_p[:, CTX_PAD - POOL_CTX:, :][None],
            c_p[None],
            n_p[None],
            m_p[:, 0, :N_HEADS][None],
            pool_s.transpose(1, 0, 2)[None],
            c_s[None],
            n_s.reshape(n_seq, N_HEADS, HEAD_DIM)[None],
            m_s[:, :N_HEADS][None])
```

```python
import functools

import jax
import jax.numpy as jnp
from jax import lax
from jax.experimental import pallas as pl
from jax.experimental.pallas import tpu as pltpu

D_MODEL = 1024
N_META = 16
POOL_WIDTH = 512
POOL_WINDOWS = (2, 4, 8, 16)
POOL_GC = 128
POOL_CTX = 15
MLSTM_WIDTH = 512
N_HEADS = 4
HEAD_DIM = 128
D_FF = 4096
EPS = 1e-6
DEC_SEQ = 4

LANES = 128
SUBLANES = 8
COL_Q = POOL_WIDTH
COL_K = COL_Q + MLSTM_WIDTH
COL_V = COL_K + MLSTM_WIDTH
COL_O = COL_V + MLSTM_WIDTH
IN_MAIN = COL_O + MLSTM_WIDTH
GATE_W = 2 * LANES
AUG = 2 * HEAD_DIM
CTX_PAD = 16
CHUNK = 256
PROMPT_TILE = 512
IN_CHUNK = 512
FFN_TILE = 512
FFN_SUB = 256
FFN_CHUNK = 1024
SAMPLE_BLOCK = 16
NEG_BIG = -1e30
VMEM_LIMIT = 56 * 1024 * 1024

F32 = jnp.float32
BF16 = jnp.bfloat16


def _rmsnorm(x, g):
    return x * lax.rsqrt(jnp.mean(x * x, axis=-1, keepdims=True) + EPS) * g


def _log_sigmoid(x):
    return jnp.minimum(x, 0.0) - jnp.log1p(jnp.exp(-jnp.abs(x)))


def _dot(a, b):
    return jnp.dot(a, b, preferred_element_type=F32)


def _dot_nt(a, b):
    return lax.dot_general(a, b, (((1,), (1,)), ((), ())), preferred_element_type=F32)


def _hcols(base, h):
    return slice(base + h * HEAD_DIM, base + (h + 1) * HEAD_DIM)


def _row_scan(x, op):
    n = x.shape[0]
    row = lax.broadcasted_iota(jnp.int32, x.shape, 0)
    sh = 1
    while sh < n:
        x = jnp.where(row >= sh, op(x, pltpu.roll(x, sh, axis=0)), x)
        sh *= 2
    return x


def _gate_cols(g, m0, n_valid):
    R = g.shape[0]
    b = _row_scan(_log_sigmoid(g[:, LANES:]), jnp.add)
    r = g[:, :LANES] - b
    big_m = jnp.maximum(m0, _row_scan(r, jnp.maximum))
    m = b + big_m
    last = n_valid - 1
    m_top = big_m[last:last + 1, :]
    ws = jnp.exp(r - m_top)
    if n_valid < R:
        ws = jnp.where(lax.broadcasted_iota(jnp.int32, ws.shape, 0) < n_valid, ws, 0.0)
    return dict(r=r, big_m=big_m, inter=jnp.exp(m0 - big_m), eneg=jnp.exp(-m), ws=ws,
                decay=jnp.exp(m0 - m_top), m_last=m[last:last + 1, :])


def _value_aug(v, h):
    lane = lax.broadcasted_iota(jnp.int32, (v.shape[0], HEAD_DIM), 1)
    return jnp.concatenate([v, jnp.where(lane == h, 1.0, 0.0)], axis=1).astype(BF16)


def _meta_kernel(x_ref, n1_ref, win_ref, wg_ref, bg_ref, wout_ref, wpool_ref,
                 u_ref, c_ref, m_ref, wout_bf_ref, wpool_bf_ref):
    wout_bf_ref[...] = wout_ref[...].astype(BF16)
    wpool_bf_ref[...] = wpool_ref[...].astype(BF16)
    x = jnp.concatenate([x_ref[...], jnp.zeros((LANES - N_META, D_MODEL), F32)], axis=0)
    xn = _rmsnorm(x, n1_ref[...]).astype(BF16)
    gc = _gate_cols(_dot(xn, wg_ref[...]) + bg_ref[...], jnp.zeros((1, LANES), F32), N_META)
    p = _dot(xn, win_ref[:, 0:IN_MAIN])
    u_ref[...] = p[0:N_META, 0:POOL_WIDTH]
    for h in range(N_HEADS):
        kw_t = (p[:, _hcols(COL_K, h)] * gc["ws"][:, h:h + 1]).T.astype(BF16)
        c_ref[h] = _dot(kw_t, _value_aug(p[:, _hcols(COL_V, h)], h))
    m_ref[...] = jnp.broadcast_to(gc["m_last"], (SUBLANES, LANES))


def _prompt_kernel(nt, xa_ref, xc_ref, umeta_ref, cmeta_ref, mmeta_ref, n1_ref, win_ref, wg_ref, bg_ref,
                   wpool_ref, pscale_ref, gain_ref, wout_ref,
                   x1_ref, pool_ref, mfin_ref, cfin_ref, nfin_ref,
                   uext_ref, xn_ref, p0_ref, p1_ref, g0_ref, g1_ref, mix0_ref, mix1_ref, c_ref, m_ref):
    s = pl.program_id(0)
    n_tiles = pl.num_programs(0) - 2

    @pl.when((jnp.maximum(s - 1, 0) % nt == 0) & (s <= n_tiles))
    def _():
        c_ref[...] = cmeta_ref[...]
        m_ref[...] = mmeta_ref[...]
        uext_ref[0:CTX_PAD, :] = umeta_ref[...]

    stage = functools.partial(_prompt_step, xa_ref, xc_ref, n1_ref, win_ref, wg_ref, bg_ref, wpool_ref,
                              pscale_ref, gain_ref, wout_ref, x1_ref, c_ref, m_ref, uext_ref, xn_ref)
    even = (p0_ref, p1_ref, g0_ref, g1_ref, mix0_ref, mix1_ref)
    odd = (p1_ref, p0_ref, g1_ref, g0_ref, mix1_ref, mix0_ref)
    steady = (s >= 2) & (s < n_tiles)
    pl.when(s == 0)(lambda: stage(*even, project=True, mix=False, emit=False))
    pl.when(s == 1)(lambda: stage(*odd, project=True, mix=True, emit=False))
    pl.when(steady & (s % 2 == 0))(lambda: stage(*even, project=True, mix=True, emit=True))
    pl.when(steady & (s % 2 == 1))(lambda: stage(*odd, project=True, mix=True, emit=True))
    pl.when(s == n_tiles)(lambda: stage(*even, project=False, mix=True, emit=True))
    pl.when(s == n_tiles + 1)(lambda: stage(*odd, project=False, mix=False, emit=True))

    @pl.when((s >= 1) & (s <= n_tiles) & ((s - 1) % nt == nt - 1))
    def _():
        for h in range(N_HEADS):
            cfin_ref[0, h] = c_ref[h, :, 0:HEAD_DIM]
            nfin_ref[0, h:h + 1, :] = c_ref[h, :, HEAD_DIM:].T[h:h + 1, :]
        mfin_ref[0] = m_ref[...]
        pool_ref[0] = uext_ref[0:CTX_PAD, :]


def _prompt_step(xa_ref, xc_ref, n1_ref, win_ref, wg_ref, bg_ref, wpool_ref, pscale_ref, gain_ref, wout_ref,
                 x1_ref, c_ref, m_ref, uext_ref, xn_ref,
                 p_new, pb, g_new, g_old, mx, mix_old, *, project, mix, emit):
    dense = []
    if emit:
        def out_proj():
            x1_ref[0] = xc_ref[0] + _dot(mix_old[...], wout_ref[...])
        dense.append(out_proj)
    if project:
        def in_norm():
            xn_ref[...] = _rmsnorm(xa_ref[0], n1_ref[...]).astype(BF16)
            g_new[...] = _dot(xn_ref[...], wg_ref[...]) + bg_ref[...]

        def in_chunk(c):
            cs = slice(c * IN_CHUNK, (c + 1) * IN_CHUNK)
            p_new[:, cs] = _dot(xn_ref[...], win_ref[:, cs])
        dense.append(in_norm)
        dense += [functools.partial(in_chunk, c) for c in range(IN_MAIN // IN_CHUNK)]

    pieces = []
    if mix:
        for ci in range(PROMPT_TILE // CHUNK):
            pieces.append(_mix_chunk(pb, g_old, mx, slice(ci * CHUNK, (ci + 1) * CHUNK),
                                     wpool_ref, pscale_ref, gain_ref, c_ref, m_ref, uext_ref))
    n_mix = (1 + N_HEADS) * len(pieces)
    done = 0
    for k, piece in enumerate(p for chunk in pieces for p in chunk):
        while done < len(dense) and done * n_mix < k * len(dense):
            dense[done]()
            done += 1
    for job in dense[done:]:
        job()


def _mix_chunk(pb, g_old, mx, rows, wpool_ref, pscale_ref, gain_ref, c_ref, m_ref, uext_ref):
    R = CHUNK
    gc = _gate_cols(g_old[rows, :], m_ref[0:1, :], R)
    r_t = gc["r"].T

    uext_ref[CTX_PAD:CTX_PAD + R, :] = pb[rows, 0:POOL_WIDTH]
    for gi, w in enumerate(POOL_WINDOWS):
        cs = slice(gi * POOL_GC, (gi + 1) * POOL_GC)
        acc = uext_ref[:, cs]
        sh = 1
        while sh < w:
            acc = acc + pltpu.roll(acc, sh, axis=0)
            sh *= 2
        z = acc[CTX_PAD:, :] * (1.0 / w) - uext_ref[CTX_PAD:CTX_PAD + R, cs]
        mx[rows, cs] = (_dot(z.astype(BF16), wpool_ref[gi]) * pscale_ref[:, cs]).astype(BF16)
    tail = uext_ref[R:R + CTX_PAD, :]
    uext_ref[0:CTX_PAD, :] = tail
    m_ref[...] = jnp.broadcast_to(gc["m_last"], (SUBLANES, LANES))
    yield

    causal = (lax.broadcasted_iota(jnp.int32, (R, R), 0) >= lax.broadcasted_iota(jnp.int32, (R, R), 1))
    for h in range(N_HEADS):
        q = (pb[rows, _hcols(COL_Q, h)] * (HEAD_DIM ** -0.5)).astype(BF16)
        k = pb[rows, _hcols(COL_K, h)]
        vaug = _value_aug(pb[rows, _hcols(COL_V, h)], h)
        sc = _dot_nt(q, k.astype(BF16))
        e = jnp.exp(jnp.where(causal, r_t[h:h + 1, :] - gc["big_m"][:, h:h + 1], NEG_BIG))
        c0 = c_ref[h]
        num = gc["inter"][:, h:h + 1] * _dot(q, c0.astype(BF16)) + _dot((sc * e).astype(BF16), vaug)
        inv = 1.0 / jnp.maximum(jnp.abs(num[:, HEAD_DIM:]), gc["eneg"])
        hh = num[:, 0:HEAD_DIM] * inv[:, h:h + 1]
        hh = hh * lax.rsqrt(jnp.mean(hh * hh, axis=-1, keepdims=True) + EPS) * gain_ref[:, _hcols(0, h)]
        mx[rows, _hcols(POOL_WIDTH, h)] = (hh * jax.nn.sigmoid(pb[rows, _hcols(COL_O, h)])).astype(BF16)
        kw_t = (k * gc["ws"][:, h:h + 1]).T.astype(BF16)
        c_ref[h] = gc["decay"][:, h:h + 1] * c0 + _dot(kw_t, vaug)
        yield


def _sample_gates(g_ref, m0):
    n = m0.shape[0]
    b = jnp.zeros((n, LANES), F32)
    big_m = m0
    r, big_ms, ms = [], [], []
    for t in range(DEC_SEQ):
        g = g_ref[t * n:(t + 1) * n, :]
        b = b + _log_sigmoid(g[:, LANES:])
        r.append(g[:, :LANES] - b)
        big_m = jnp.maximum(big_m, r[-1])
        big_ms.append(big_m)
        ms.append(b + big_m)
    return dict(r=r, big_m=big_ms, m=ms,
                ws=[jnp.exp(r[t] - big_m) for t in range(DEC_SEQ)],
                decay=jnp.exp(m0 - big_m))


def _sample_kernel(x4_ref, ctx_ref, cin_ref, n0_ref, m0_ref, n1_ref, win_ref, wg_ref, bg_ref, wpool_ref,
                   pscale_ref, gain_ref, wout_ref,
                   x1_ref, pool_ref, cout_ref, nout_ref, mout_ref,
                   xs_ref, p_ref, g_ref, tmp_ref, qsm_ref, qcsm_ref, kwt_ref, vsm_ref, dec_ref, mix_ref):
    i = pl.program_id(0)
    n_seq = m0_ref.shape[0]
    n_grp = DEC_SEQ * n_seq // LANES
    scale = HEAD_DIM ** -0.5

    def seq_major(dst_ref, slab_of_t):
        for t in range(DEC_SEQ):
            slab = slab_of_t(t)
            for h in range(N_HEADS):
                dst_ref[h, pl.ds(t, n_seq, stride=DEC_SEQ), :] = slab[:, _hcols(0, h)]

    @pl.when(i == 0)
    def _():
        for t in range(DEC_SEQ):
            for lt in range(D_MODEL // LANES):
                xs_ref[t * n_seq:(t + 1) * n_seq, lt * LANES:(lt + 1) * LANES] = x4_ref[:, lt, t, :]
        xn = _rmsnorm(xs_ref[...], n1_ref[...]).astype(BF16)
        g_ref[...] = _dot(xn, wg_ref[...]) + bg_ref[...]
        p_ref[...] = _dot(xn, win_ref[...])

        def urow(e, cs):
            if e < POOL_CTX:
                return ctx_ref[e, :, cs]
            return p_ref[(e - POOL_CTX) * n_seq:(e - POOL_CTX + 1) * n_seq, cs]

        for gi, w in enumerate(POOL_WINDOWS):
            cs = slice(gi * POOL_GC, (gi + 1) * POOL_GC)
            for t in range(DEC_SEQ):
                e = POOL_CTX + t
                wsum = urow(e, cs)
                for j in range(1, w):
                    wsum = wsum + urow(e - j, cs)
                z = wsum * (1.0 / w) - urow(e, cs)
                mix_ref[t * n_seq:(t + 1) * n_seq, cs] = (
                    _dot(z.astype(BF16), wpool_ref[gi]) * pscale_ref[:, cs]).astype(BF16)
        for e in range(DEC_SEQ, POOL_CTX + DEC_SEQ):
            pool_ref[e - DEC_SEQ] = urow(e, slice(0, POOL_WIDTH))

        gs = _sample_gates(g_ref, m0_ref[...])
        dec_ref[...] = gs["decay"]
        mout_ref[...] = gs["m"][DEC_SEQ - 1]
        def kw_slab(t):
            rows = slice(t * n_seq, (t + 1) * n_seq)
            return jnp.concatenate([p_ref[rows, _hcols(COL_K, h)] * gs["ws"][t][:, h:h + 1]
                                    for h in range(N_HEADS)], axis=1)

        kw = [kw_slab(t) for t in range(DEC_SEQ)]
        nout_ref[...] = (jnp.concatenate([jnp.broadcast_to(gs["decay"][:, h:h + 1], (n_seq, HEAD_DIM))
                                          for h in range(N_HEADS)], axis=1) * n0_ref[...]
                         + kw[0] + kw[1] + kw[2] + kw[3])
        seq_major(tmp_ref, lambda t: kw[t])
        for h in range(N_HEADS):
            for gi in range(n_grp):
                kwt_ref[h, gi] = tmp_ref[h, gi * LANES:(gi + 1) * LANES, :].T.astype(BF16)
        seq_major(tmp_ref, lambda t: p_ref[t * n_seq:(t + 1) * n_seq, COL_V:COL_O])
        for h in range(N_HEADS):
            for gi in range(n_grp):
                vsm_ref[h, gi] = tmp_ref[h, gi * LANES:(gi + 1) * LANES, :].astype(BF16)
        seq_major(qsm_ref, lambda t: p_ref[t * n_seq:(t + 1) * n_seq, COL_Q:COL_K] * scale)

    lane_sl = lax.broadcasted_iota(jnp.int32, (HEAD_DIM, LANES), 1) // DEC_SEQ
    first = lax.broadcasted_iota(jnp.int32, (SUBLANES, HEAD_DIM), 0) < DEC_SEQ
    grp = (i * SAMPLE_BLOCK) // (LANES // DEC_SEQ)
    for jp in range(SAMPLE_BLOCK // 2):
        rows = pl.ds(pl.multiple_of(i * (SAMPLE_BLOCK * DEC_SEQ) + jp * SUBLANES, SUBLANES), SUBLANES)
        for h in range(N_HEADS):
            hs = _hcols(0, h)
            q8 = qsm_ref[h, rows, :].astype(BF16)
            readout = []
            for a in range(2):
                j = 2 * jp + a
                seq = i * SAMPLE_BLOCK + j
                c0 = cin_ref[j, h]
                readout.append(_dot(q8, c0.astype(BF16)))
                lhs = jnp.where(lane_sl == seq % (LANES // DEC_SEQ), kwt_ref[h, grp], jnp.zeros((), BF16))
                cout_ref[j, h] = dec_ref[pl.ds(seq, 1), :][:, h:h + 1] * c0 + _dot(lhs, vsm_ref[h, grp])
            qcsm_ref[h, rows, :] = jnp.where(first, readout[0], readout[1])

    @pl.when(i == pl.num_programs(0) - 1)
    def _():
        gs = _sample_gates(g_ref, m0_ref[...])
        for h in range(N_HEADS):
            hs = _hcols(0, h)
            n0 = n0_ref[:, hs]
            qs = [p_ref[t * n_seq:(t + 1) * n_seq, _hcols(COL_Q, h)] * scale for t in range(DEC_SEQ)]
            ks = [p_ref[t * n_seq:(t + 1) * n_seq, _hcols(COL_K, h)] for t in range(DEC_SEQ)]
            for t in range(DEC_SEQ):
                rows = slice(t * n_seq, (t + 1) * n_seq)
                inter = jnp.exp(m0_ref[...] - gs["big_m"][t])[:, h:h + 1]
                num = inter * qcsm_ref[h, pl.ds(t, n_seq, stride=DEC_SEQ), :]
                den = inter * jnp.sum(qs[t] * n0, axis=-1, keepdims=True)
                for s in range(t + 1):
                    pts = (jnp.sum(qs[t] * ks[s], axis=-1, keepdims=True)
                           * jnp.exp(gs["r"][s] - gs["big_m"][t])[:, h:h + 1])
                    num = num + pts * p_ref[s * n_seq:(s + 1) * n_seq, _hcols(COL_V, h)]
                    den = den + pts
                hh = num / jnp.maximum(jnp.abs(den), jnp.exp(-gs["m"][t])[:, h:h + 1])
                hh = hh * lax.rsqrt(jnp.mean(hh * hh, axis=-1, keepdims=True) + EPS) * gain_ref[:, hs]
                mix_ref[rows, _hcols(POOL_WIDTH, h)] = (
                    hh * jax.nn.sigmoid(p_ref[rows, _hcols(COL_O, h)])).astype(BF16)
        x1_ref[...] = xs_ref[...] + _dot(mix_ref[...], wout_ref[...])


def _ffn_kernel(xs_ref, xp_ref, n2_ref, nf_ref, wup_hbm, wdown_hbm, yp_ref, ys_ref,
                wup_ref, wdown_ref, stage_ref, sem):
    i = pl.program_id(0)
    last = pl.num_programs(0) - 1

    def fetch(c, up):
        cols = pl.ds(c * FFN_CHUNK, FFN_CHUNK)
        src = wup_hbm.at[:, cols] if up else wdown_hbm.at[cols, :]
        slot = 0 if up else 1
        return pltpu.make_async_copy(src, stage_ref.at[slot], sem.at[slot])

    def store_rows(rows, y):
        yp_ref[rows, :] = y

    def store_tiles(rows, y):
        n_seq = ys_ref.shape[0]
        for k in range(FFN_SUB // n_seq):
            t = rows.start // n_seq + k
            for lt in range(D_MODEL // LANES):
                ys_ref[:, lt, t, :] = y[k * n_seq:(k + 1) * n_seq, lt * LANES:(lt + 1) * LANES]

    def mlp(x_ref, store, load_weights):
        subs = [slice(j * FFN_SUB, (j + 1) * FFN_SUB) for j in range(x_ref.shape[0] // FFN_SUB)]
        n_chunk = D_FF // FFN_CHUNK
        if load_weights:
            fetch(0, True).start()
            fetch(0, False).start()
        xn = [_rmsnorm(x_ref[rows, :], n2_ref[...]).astype(BF16) for rows in subs]
        acc = [x_ref[rows, :] for rows in subs]
        for c in range(n_chunk):
            if load_weights:
                fetch(c, True).wait()
                wup_ref[c] = stage_ref[0].astype(BF16)
                if c + 1 < n_chunk:
                    fetch(c + 1, True).start()
                fetch(c, False).wait()
                wdown_ref[c] = stage_ref[1].astype(BF16)
                if c + 1 < n_chunk:
                    fetch(c + 1, False).start()
            for j in range(len(subs)):
                a = jnp.square(jnp.maximum(_dot(xn[j], wup_ref[c]), 0.0)).astype(BF16)
                acc[j] = acc[j] + _dot(a, wdown_ref[c])
        for j, rows in enumerate(subs):
            store(rows, _rmsnorm(acc[j], nf_ref[...]))

    pl.when(i == 0)(lambda: mlp(xp_ref, store_rows, True))
    pl.when((i > 0) & (i < last))(lambda: mlp(xp_ref, store_rows, False))
    pl.when(i == last)(lambda: mlp(xs_ref, store_tiles, False))


def _full(shape):
    return pl.BlockSpec(shape, lambda *_: (0,) * len(shape))


def _ffn(xs, xp, n2, nf, w_up, w_down, n_seq):
    assert D_MODEL == FFN_CHUNK
    assert FFN_SUB % n_seq == 0
    n_prompt = xp.shape[0] // FFN_TILE
    n_chunk = D_FF // FFN_CHUNK
    ys_shape = (n_seq, D_MODEL // LANES, xs.shape[0] // n_seq, LANES)
    prompt_tile = pl.BlockSpec((FFN_TILE, D_MODEL), lambda i: (jnp.minimum(i, n_prompt - 1), 0))
    return pl.pallas_call(
        _ffn_kernel,
        grid=(n_prompt + 1,),
        in_specs=[_full(xs.shape), prompt_tile, _full((1, D_MODEL)), _full((1, D_MODEL)),
                  pl.BlockSpec(memory_space=pl.ANY), pl.BlockSpec(memory_space=pl.ANY)],
        out_specs=[prompt_tile, _full(ys_shape)],
        out_shape=(jax.ShapeDtypeStruct(xp.shape, F32), jax.ShapeDtypeStruct(ys_shape, F32)),
        scratch_shapes=[pltpu.VMEM((n_chunk, D_MODEL, FFN_CHUNK), BF16),
                        pltpu.VMEM((n_chunk, FFN_CHUNK, D_MODEL), BF16),
                        pltpu.VMEM((2, D_MODEL, FFN_CHUNK), F32),
                        pltpu.SemaphoreType.DMA((2,))],
        compiler_params=pltpu.CompilerParams(dimension_semantics=("arbitrary",), vmem_limit_bytes=VMEM_LIMIT),
        name="ffn",
    )(xs, xp, n2, nf, w_up, w_down)


def kernel(x_prompt, x_sample, state_pool, state_C, state_n, state_m, meta_tokens, norm1, w_in, b_gate,
           w_pool, pool_scale, head_gain, w_out, norm2, w_up, w_down, norm_f):
    B, S, _ = x_prompt.shape
    n_seq = x_sample.shape[0]
    rows_s = DEC_SEQ * n_seq
    assert w_in.shape[0] == 1 and S % PROMPT_TILE == 0 and n_seq % SAMPLE_BLOCK == 0
    assert (B * S) % FFN_TILE == 0 and rows_s % FFN_SUB == 0

    win = w_in[0].astype(BF16)
    gpad = ((0, 0), (0, LANES - N_HEADS))
    wg = jnp.concatenate([jnp.pad(w_in[0][:, IN_MAIN:IN_MAIN + N_HEADS], gpad),
                          jnp.pad(w_in[0][:, IN_MAIN + N_HEADS:], gpad)], axis=1).astype(BF16)
    bg = jnp.concatenate([jnp.pad(b_gate[0][:N_HEADS], (0, LANES - N_HEADS)),
                          jnp.pad(b_gate[0][N_HEADS:], (0, LANES - N_HEADS))]).reshape(1, GATE_W)
    n1 = norm1[0].reshape(1, D_MODEL)
    n2 = norm2[0].reshape(1, D_MODEL)
    nf = norm_f.reshape(1, D_MODEL)
    pscale = pool_scale[0].reshape(1, POOL_WIDTH)
    gain = head_gain[0].reshape(1, MLSTM_WIDTH)

    weight_specs = [_full((1, D_MODEL)), _full((D_MODEL, IN_MAIN)), _full((D_MODEL, GATE_W)), _full((1, GATE_W))]
    mixer_specs = [_full((len(POOL_WINDOWS), POOL_GC, POOL_GC)), _full((1, POOL_WIDTH)),
                   _full((1, MLSTM_WIDTH)), _full((D_MODEL, D_MODEL))]

    u_meta, c_meta, m_meta, wout, wpool = pl.pallas_call(
        _meta_kernel,
        out_shape=(jax.ShapeDtypeStruct((N_META, POOL_WIDTH), F32),
                   jax.ShapeDtypeStruct((N_HEADS, HEAD_DIM, AUG), F32),
                   jax.ShapeDtypeStruct((SUBLANES, LANES), F32),
                   jax.ShapeDtypeStruct((D_MODEL, D_MODEL), BF16),
                   jax.ShapeDtypeStruct((len(POOL_WINDOWS), POOL_GC, POOL_GC), BF16)),
        compiler_params=pltpu.CompilerParams(vmem_limit_bytes=VMEM_LIMIT),
        name="meta_prefix",
    )(meta_tokens, n1, win, wg, bg, w_out[0], w_pool[0])

    nt = S // PROMPT_TILE
    n_tiles = B * nt
    assert n_tiles % 2 == 0 and n_tiles >= 2
    assert D_MODEL % (n_tiles * 2 * SUBLANES) == 0

    def tile_at(lag):
        def index(s):
            j = jnp.clip(s - lag, 0, n_tiles - 1)
            return (j // nt, j % nt, 0)
        return index

    def mixed_seq(ndim):
        return lambda s: (jnp.clip(s - 1, 0, n_tiles - 1) // nt,) + (0,) * (ndim - 1)

    x1_p, pool_p, m_p, c_p, n_p = pl.pallas_call(
        functools.partial(_prompt_kernel, nt),
        grid=(n_tiles + 2,),
        in_specs=[pl.BlockSpec((1, PROMPT_TILE, D_MODEL), tile_at(0)),
                  pl.BlockSpec((1, PROMPT_TILE, D_MODEL), tile_at(2)),
                  _full((N_META, POOL_WIDTH)), _full((N_HEADS, HEAD_DIM, AUG)), _full((SUBLANES, LANES))]
        + weight_specs + mixer_specs,
        out_specs=[pl.BlockSpec((1, PROMPT_TILE, D_MODEL), tile_at(2)),
                   pl.BlockSpec((1, CTX_PAD, POOL_WIDTH), mixed_seq(3)),
                   pl.BlockSpec((1, SUBLANES, LANES), mixed_seq(3)),
                   pl.BlockSpec((1, N_HEADS, HEAD_DIM, HEAD_DIM), mixed_seq(4)),
                   pl.BlockSpec((1, N_HEADS, HEAD_DIM), mixed_seq(3))],
        out_shape=(jax.ShapeDtypeStruct((B, S, D_MODEL), F32),
                   jax.ShapeDtypeStruct((B, CTX_PAD, POOL_WIDTH), F32),
                   jax.ShapeDtypeStruct((B, SUBLANES, LANES), F32),
                   jax.ShapeDtypeStruct((B, N_HEADS, HEAD_DIM, HEAD_DIM), F32),
                   jax.ShapeDtypeStruct((B, N_HEADS, HEAD_DIM), F32)),
        scratch_shapes=[pltpu.VMEM((CTX_PAD + CHUNK, POOL_WIDTH), F32),
                        pltpu.VMEM((PROMPT_TILE, D_MODEL), BF16),
                        pltpu.VMEM((PROMPT_TILE, IN_MAIN), F32), pltpu.VMEM((PROMPT_TILE, IN_MAIN), F32),
                        pltpu.VMEM((PROMPT_TILE, GATE_W), F32), pltpu.VMEM((PROMPT_TILE, GATE_W), F32),
                        pltpu.VMEM((PROMPT_TILE, D_MODEL), BF16), pltpu.VMEM((PROMPT_TILE, D_MODEL), BF16),
                        pltpu.VMEM((N_HEADS, HEAD_DIM, AUG), F32), pltpu.VMEM((SUBLANES, LANES), F32)],
        compiler_params=pltpu.CompilerParams(dimension_semantics=("arbitrary",), vmem_limit_bytes=VMEM_LIMIT),
        name="prompt_mixer",
    )(x_prompt, x_prompt, u_meta, c_meta, m_meta, n1, win, wg, bg, wpool, pscale, gain, wout)

    xs_tiles = x_sample.reshape(n_seq, DEC_SEQ, D_MODEL // LANES, LANES).transpose(0, 2, 1, 3)
    x1_p, xs_tiles = lax.optimization_barrier((x1_p, xs_tiles))
    ctx_t = state_pool[0].transpose(1, 0, 2)
    n0 = state_n[0].reshape(n_seq, MLSTM_WIDTH)
    m0 = jnp.pad(state_m[0], ((0, 0), (0, LANES - N_HEADS)))
    state_block = pl.BlockSpec((SAMPLE_BLOCK, N_HEADS, HEAD_DIM, HEAD_DIM), lambda i: (i, 0, 0, 0))
    x1_s, pool_s, c_s, n_s, m_s = pl.pallas_call(
        _sample_kernel,
        grid=(n_seq // SAMPLE_BLOCK,),
        in_specs=[_full(xs_tiles.shape), _full((POOL_CTX, n_seq, POOL_WIDTH)), state_block,
                  _full((n_seq, MLSTM_WIDTH)), _full((n_seq, LANES))] + weight_specs + mixer_specs,
        out_specs=[_full((rows_s, D_MODEL)), _full((POOL_CTX, n_seq, POOL_WIDTH)), state_block,
                   _full((n_seq, MLSTM_WIDTH)), _full((n_seq, LANES))],
        out_shape=(jax.ShapeDtypeStruct((rows_s, D_MODEL), F32),
                   jax.ShapeDtypeStruct((POOL_CTX, n_seq, POOL_WIDTH), F32),
                   jax.ShapeDtypeStruct((n_seq, N_HEADS, HEAD_DIM, HEAD_DIM), F32),
                   jax.ShapeDtypeStruct((n_seq, MLSTM_WIDTH), F32),
                   jax.ShapeDtypeStruct((n_seq, LANES), F32)),
        scratch_shapes=[pltpu.VMEM((rows_s, D_MODEL), F32),
                        pltpu.VMEM((rows_s, IN_MAIN), F32),
                        pltpu.VMEM((rows_s, GATE_W), F32),
                        pltpu.VMEM((N_HEADS, rows_s, HEAD_DIM), F32),
                        pltpu.VMEM((N_HEADS, rows_s, HEAD_DIM), F32),
                        pltpu.VMEM((N_HEADS, rows_s, HEAD_DIM), F32),
                        pltpu.VMEM((N_HEADS, rows_s // LANES, HEAD_DIM, LANES), BF16),
                        pltpu.VMEM((N_HEADS, rows_s // LANES, LANES, HEAD_DIM), BF16),
                        pltpu.VMEM((n_seq, LANES), F32),
                        pltpu.VMEM((rows_s, D_MODEL), BF16)],
        compiler_params=pltpu.CompilerParams(dimension_semantics=("arbitrary",), vmem_limit_bytes=VMEM_LIMIT),
        name="sample_mixer",
    )(xs_tiles, ctx_t, state_C[0], n0, m0, n1, win, wg, bg, wpool, pscale, gain, wout)

    y_p, y_s = _ffn(x1_s, x1_p.reshape(B * S, D_MODEL), n2, nf, w_up[0], w_down[0], n_seq)
    y_p = y_p.reshape(B, S, D_MODEL)
    y_s = y_s.transpose(0, 2, 1, 3).reshape(n_seq, DEC_SEQ, D_MODEL)

    return (y_p, y_s,
            pool_p[:, CTX_PAD - POOL_CTX:, :][None],
            c_p[None],
            n_p[None],
            m_p[:, 0, :N_HEADS][None],
            pool_s.transpose(1, 0, 2)[None],
            c_s[None],
            n_s.reshape(n_seq, N_HEADS, HEAD_DIM)[None],
            m_s[:, :N_HEADS][None])
```

```python
import functools

import jax
import jax.numpy as jnp
from jax import lax
from jax.experimental import pallas as pl
from jax.experimental.pallas import tpu as pltpu

D_MODEL = 1024
N_META = 16
POOL_WIDTH = 512
POOL_WINDOWS = (2, 4, 8, 16)
POOL_GC = 128
POOL_CTX = 15
MLSTM_WIDTH = 512
N_HEADS = 4
HEAD_DIM = 128
D_FF = 4096
EPS = 1e-6
DEC_SEQ = 4

LANES = 128
SUBLANES = 8
COL_Q = POOL_WIDTH
COL_K = COL_Q + MLSTM_WIDTH
COL_V = COL_K + MLSTM_WIDTH
COL_O = COL_V + MLSTM_WIDTH
IN_MAIN = COL_O + MLSTM_WIDTH
GATE_W = 2 * LANES
AUG = 2 * HEAD_DIM
CTX_PAD = 16
CHUNK = 256
PROMPT_TILE = 512
IN_CHUNK = 512
MIX_PIECES = 1 + N_HEADS // 2 + N_HEADS
FFN_TILE = 512
FFN_SUB = 256
FFN_CHUNK = 1024
SAMPLE_BLOCK = 16
NEG_BIG = -1e30
VMEM_LIMIT = 56 * 1024 * 1024

F32 = jnp.float32
BF16 = jnp.bfloat16


def _rmsnorm(x, g):
    return x * lax.rsqrt(jnp.mean(x * x, axis=-1, keepdims=True) + EPS) * g


def _log_sigmoid(x):
    return jnp.minimum(x, 0.0) - jnp.log1p(jnp.exp(-jnp.abs(x)))


def _dot(a, b):
    return jnp.dot(a, b, preferred_element_type=F32)


def _dot_nt(a, b):
    return lax.dot_general(a, b, (((1,), (1,)), ((), ())), preferred_element_type=F32)


def _hcols(base, h):
    return slice(base + h * HEAD_DIM, base + (h + 1) * HEAD_DIM)


def _row_scan(x, op):
    n = x.shape[0]
    row = lax.broadcasted_iota(jnp.int32, x.shape, 0)
    sh = 1
    while sh < n:
        x = jnp.where(row >= sh, op(x, pltpu.roll(x, sh, axis=0)), x)
        sh *= 2
    return x


def _gate_cols(g, m0, n_valid):
    R = g.shape[0]
    b = _row_scan(_log_sigmoid(g[:, LANES:]), jnp.add)
    r = g[:, :LANES] - b
    big_m = jnp.maximum(m0, _row_scan(r, jnp.maximum))
    m = b + big_m
    last = n_valid - 1
    m_top = big_m[last:last + 1, :]
    ws = jnp.exp(r - m_top)
    if n_valid < R:
        ws = jnp.where(lax.broadcasted_iota(jnp.int32, ws.shape, 0) < n_valid, ws, 0.0)
    return dict(r=r, big_m=big_m, inter=jnp.exp(m0 - big_m), eneg=jnp.exp(-m), ws=ws,
                decay=jnp.exp(m0 - m_top), m_last=m[last:last + 1, :])


def _value_aug(v, h):
    lane = lax.broadcasted_iota(jnp.int32, (v.shape[0], HEAD_DIM), 1)
    return jnp.concatenate([v, jnp.where(lane == h, 1.0, 0.0)], axis=1).astype(BF16)


def _meta_kernel(x_ref, n1_ref, win_ref, wg_ref, bg_ref, wout_ref, wpool_ref,
                 u_ref, c_ref, m_ref, wout_bf_ref, wpool_bf_ref):
    wout_bf_ref[...] = wout_ref[...].astype(BF16)
    wpool_bf_ref[...] = wpool_ref[...].astype(BF16)
    x = jnp.concatenate([x_ref[...], jnp.zeros((LANES - N_META, D_MODEL), F32)], axis=0)
    xn = _rmsnorm(x, n1_ref[...]).astype(BF16)
    gc = _gate_cols(_dot(xn, wg_ref[...]) + bg_ref[...], jnp.zeros((1, LANES), F32), N_META)
    p = _dot(xn, win_ref[:, 0:IN_MAIN])
    u_ref[...] = p[0:N_META, 0:POOL_WIDTH]
    for h in range(N_HEADS):
        kw_t = (p[:, _hcols(COL_K, h)] * gc["ws"][:, h:h + 1]).T.astype(BF16)
        c_ref[h] = _dot(kw_t, _value_aug(p[:, _hcols(COL_V, h)], h))
    m_ref[...] = jnp.broadcast_to(gc["m_last"], (SUBLANES, LANES))


def _prompt_kernel(nt, xa_ref, xc_ref, umeta_ref, cmeta_ref, mmeta_ref, n1_ref, win_ref, wg_ref, bg_ref,
                   wpool_ref, pscale_ref, gain_ref, wout_ref,
                   x1_ref, pool_ref, mfin_ref, cfin_ref, nfin_ref,
                   uext_ref, xn_ref, p0_ref, p1_ref, g0_ref, g1_ref, mix0_ref, mix1_ref, c_ref, m_ref):
    s = pl.program_id(0)
    n_tiles = pl.num_programs(0) - 2

    @pl.when((jnp.maximum(s - 1, 0) % nt == 0) & (s <= n_tiles))
    def _():
        c_ref[...] = cmeta_ref[...]
        m_ref[...] = mmeta_ref[...]
        uext_ref[0:CTX_PAD, :] = umeta_ref[...]

    stage = functools.partial(_prompt_step, xa_ref, xc_ref, n1_ref, win_ref, wg_ref, bg_ref, wpool_ref,
                              pscale_ref, gain_ref, wout_ref, x1_ref, c_ref, m_ref, uext_ref, xn_ref)
    even = (p0_ref, p1_ref, g0_ref, g1_ref, mix0_ref, mix1_ref)
    odd = (p1_ref, p0_ref, g1_ref, g0_ref, mix1_ref, mix0_ref)
    steady = (s >= 2) & (s < n_tiles)
    pl.when(s == 0)(lambda: stage(*even, project=True, mix=False, emit=False))
    pl.when(s == 1)(lambda: stage(*odd, project=True, mix=True, emit=False))
    pl.when(steady & (s % 2 == 0))(lambda: stage(*even, project=True, mix=True, emit=True))
    pl.when(steady & (s % 2 == 1))(lambda: stage(*odd, project=True, mix=True, emit=True))
    pl.when(s == n_tiles)(lambda: stage(*even, project=False, mix=True, emit=True))
    pl.when(s == n_tiles + 1)(lambda: stage(*odd, project=False, mix=False, emit=True))

    @pl.when((s >= 1) & (s <= n_tiles) & ((s - 1) % nt == nt - 1))
    def _():
        for h in range(N_HEADS):
            cfin_ref[0, h] = c_ref[h, :, 0:HEAD_DIM]
            nfin_ref[0, h:h + 1, :] = c_ref[h, :, HEAD_DIM:].T[h:h + 1, :]
        mfin_ref[0] = m_ref[...]
        pool_ref[0] = uext_ref[0:CTX_PAD, :]


def _prompt_step(xa_ref, xc_ref, n1_ref, win_ref, wg_ref, bg_ref, wpool_ref, pscale_ref, gain_ref, wout_ref,
                 x1_ref, c_ref, m_ref, uext_ref, xn_ref,
                 p_new, pb, g_new, g_old, mx, mix_old, *, project, mix, emit):
    dense = []
    if emit:
        def out_proj():
            x1_ref[0] = xc_ref[0] + _dot(mix_old[...], wout_ref[...])
        dense.append(out_proj)
    if project:
        def in_norm():
            xn_ref[...] = _rmsnorm(xa_ref[0], n1_ref[...]).astype(BF16)
            g_new[...] = _dot(xn_ref[...], wg_ref[...]) + bg_ref[...]

        def in_chunk(c):
            cs = slice(c * IN_CHUNK, (c + 1) * IN_CHUNK)
            p_new[:, cs] = _dot(xn_ref[...], win_ref[:, cs])
        dense.append(in_norm)
        dense += [functools.partial(in_chunk, c) for c in range(IN_MAIN // IN_CHUNK)]

    pieces = []
    if mix:
        for ci in range(PROMPT_TILE // CHUNK):
            pieces.append(_mix_chunk(pb, g_old, mx, slice(ci * CHUNK, (ci + 1) * CHUNK),
                                     wpool_ref, pscale_ref, gain_ref, c_ref, m_ref, uext_ref))
    n_mix = MIX_PIECES * len(pieces)
    done = 0
    for k, piece in enumerate(p for chunk in pieces for p in chunk):
        while done < len(dense) and done * n_mix < k * len(dense):
            dense[done]()
            done += 1
    for job in dense[done:]:
        job()


def _mix_chunk(pb, g_old, mx, rows, wpool_ref, pscale_ref, gain_ref, c_ref, m_ref, uext_ref):
    R = CHUNK
    gc = _gate_cols(g_old[rows, :], m_ref[0:1, :], R)
    r_t = gc["r"].T

    uext_ref[CTX_PAD:CTX_PAD + R, :] = pb[rows, 0:POOL_WIDTH]
    for gi, w in enumerate(POOL_WINDOWS):
        cs = slice(gi * POOL_GC, (gi + 1) * POOL_GC)
        acc = uext_ref[:, cs]
        sh = 1
        while sh < w:
            acc = acc + pltpu.roll(acc, sh, axis=0)
            sh *= 2
        z = acc[CTX_PAD:, :] * (1.0 / w) - uext_ref[CTX_PAD:CTX_PAD + R, cs]
        mx[rows, cs] = (_dot(z.astype(BF16), wpool_ref[gi]) * pscale_ref[:, cs]).astype(BF16)
    tail = uext_ref[R:R + CTX_PAD, :]
    uext_ref[0:CTX_PAD, :] = tail
    m_ref[...] = jnp.broadcast_to(gc["m_last"], (SUBLANES, LANES))
    yield

    causal = (lax.broadcasted_iota(jnp.int32, (R, R), 0) >= lax.broadcasted_iota(jnp.int32, (R, R), 1))
    front = []
    for h in range(N_HEADS):
        q = (pb[rows, _hcols(COL_Q, h)] * (HEAD_DIM ** -0.5)).astype(BF16)
        k = pb[rows, _hcols(COL_K, h)]
        vaug = _value_aug(pb[rows, _hcols(COL_V, h)], h)
        sc = _dot_nt(q, k.astype(BF16))
        c0 = c_ref[h]
        qc = _dot(q, c0.astype(BF16))
        kw_t = (k * gc["ws"][:, h:h + 1]).T.astype(BF16)
        c_ref[h] = gc["decay"][:, h:h + 1] * c0 + _dot(kw_t, vaug)
        front.append((sc, qc, vaug))
        if h % 2 == 1:
            yield
    for h in range(N_HEADS):
        sc, qc, vaug = front[h]
        e = jnp.exp(jnp.where(causal, r_t[h:h + 1, :] - gc["big_m"][:, h:h + 1], NEG_BIG))
        num = gc["inter"][:, h:h + 1] * qc + _dot((sc * e).astype(BF16), vaug)
        inv = 1.0 / jnp.maximum(jnp.abs(num[:, HEAD_DIM:]), gc["eneg"])
        hh = num[:, 0:HEAD_DIM] * inv[:, h:h + 1]
        hh = hh * lax.rsqrt(jnp.mean(hh * hh, axis=-1, keepdims=True) + EPS) * gain_ref[:, _hcols(0, h)]
        mx[rows, _hcols(POOL_WIDTH, h)] = (hh * jax.nn.sigmoid(pb[rows, _hcols(COL_O, h)])).astype(BF16)
        yield


def _sample_gates(g_ref, m0):
    n = m0.shape[0]
    b = jnp.zeros((n, LANES), F32)
    big_m = m0
    r, big_ms, ms = [], [], []
    for t in range(DEC_SEQ):
        g = g_ref[t * n:(t + 1) * n, :]
        b = b + _log_sigmoid(g[:, LANES:])
        r.append(g[:, :LANES] - b)
        big_m = jnp.maximum(big_m, r[-1])
        big_ms.append(big_m)
        ms.append(b + big_m)
    return dict(r=r, big_m=big_ms, m=ms,
                ws=[jnp.exp(r[t] - big_m) for t in range(DEC_SEQ)],
                decay=jnp.exp(m0 - big_m))


def _sample_kernel(x4_ref, ctx_ref, cin_ref, n0_ref, m0_ref, n1_ref, win_ref, wg_ref, bg_ref, wpool_ref,
                   pscale_ref, gain_ref, wout_ref,
                   x1_ref, pool_ref, cout_ref, nout_ref, mout_ref,
                   xs_ref, p_ref, g_ref, tmp_ref, qsm_ref, qcsm_ref, kwt_ref, vsm_ref, dec_ref, mix_ref):
    i = pl.program_id(0)
    n_seq = m0_ref.shape[0]
    n_grp = DEC_SEQ * n_seq // LANES
    scale = HEAD_DIM ** -0.5

    def seq_major(dst_ref, slab_of_t):
        for t in range(DEC_SEQ):
            slab = slab_of_t(t)
            for h in range(N_HEADS):
                dst_ref[h, pl.ds(t, n_seq, stride=DEC_SEQ), :] = slab[:, _hcols(0, h)]

    @pl.when(i == 0)
    def _():
        for t in range(DEC_SEQ):
            for lt in range(D_MODEL // LANES):
                xs_ref[t * n_seq:(t + 1) * n_seq, lt * LANES:(lt + 1) * LANES] = x4_ref[:, lt, t, :]
        xn = _rmsnorm(xs_ref[...], n1_ref[...]).astype(BF16)
        g_ref[...] = _dot(xn, wg_ref[...]) + bg_ref[...]
        p_ref[...] = _dot(xn, win_ref[...])

        def urow(e, cs):
            if e < POOL_CTX:
                return ctx_ref[e, :, cs]
            return p_ref[(e - POOL_CTX) * n_seq:(e - POOL_CTX + 1) * n_seq, cs]

        for gi, w in enumerate(POOL_WINDOWS):
            cs = slice(gi * POOL_GC, (gi + 1) * POOL_GC)
            for t in range(DEC_SEQ):
                e = POOL_CTX + t
                wsum = urow(e, cs)
                for j in range(1, w):
                    wsum = wsum + urow(e - j, cs)
                z = wsum * (1.0 / w) - urow(e, cs)
                mix_ref[t * n_seq:(t + 1) * n_seq, cs] = (
                    _dot(z.astype(BF16), wpool_ref[gi]) * pscale_ref[:, cs]).astype(BF16)
        for e in range(DEC_SEQ, POOL_CTX + DEC_SEQ):
            pool_ref[e - DEC_SEQ] = urow(e, slice(0, POOL_WIDTH))

        gs = _sample_gates(g_ref, m0_ref[...])
        dec_ref[...] = gs["decay"]
        mout_ref[...] = gs["m"][DEC_SEQ - 1]
        def kw_slab(t):
            rows = slice(t * n_seq, (t + 1) * n_seq)
            return jnp.concatenate([p_ref[rows, _hcols(COL_K, h)] * gs["ws"][t][:, h:h + 1]
                                    for h in range(N_HEADS)], axis=1)

        kw = [kw_slab(t) for t in range(DEC_SEQ)]
        nout_ref[...] = (jnp.concatenate([jnp.broadcast_to(gs["decay"][:, h:h + 1], (n_seq, HEAD_DIM))
                                          for h in range(N_HEADS)], axis=1) * n0_ref[...]
                         + kw[0] + kw[1] + kw[2] + kw[3])
        seq_major(tmp_ref, lambda t: kw[t])
        for h in range(N_HEADS):
            for gi in range(n_grp):
                kwt_ref[h, gi] = tmp_ref[h, gi * LANES:(gi + 1) * LANES, :].T.astype(BF16)
        seq_major(tmp_ref, lambda t: p_ref[t * n_seq:(t + 1) * n_seq, COL_V:COL_O])
        for h in range(N_HEADS):
            for gi in range(n_grp):
                vsm_ref[h, gi] = tmp_ref[h, gi * LANES:(gi + 1) * LANES, :].astype(BF16)
        seq_major(qsm_ref, lambda t: p_ref[t * n_seq:(t + 1) * n_seq, COL_Q:COL_K] * scale)

    lane_sl = lax.broadcasted_iota(jnp.int32, (HEAD_DIM, LANES), 1) // DEC_SEQ
    first = lax.broadcasted_iota(jnp.int32, (SUBLANES, HEAD_DIM), 0) < DEC_SEQ
    grp = (i * SAMPLE_BLOCK) // (LANES // DEC_SEQ)
    for jp in range(SAMPLE_BLOCK // 2):
        rows = pl.ds(pl.multiple_of(i * (SAMPLE_BLOCK * DEC_SEQ) + jp * SUBLANES, SUBLANES), SUBLANES)
        for h in range(N_HEADS):
            hs = _hcols(0, h)
            q8 = qsm_ref[h, rows, :].astype(BF16)
            readout = []
            for a in range(2):
                j = 2 * jp + a
                seq = i * SAMPLE_BLOCK + j
                c0 = cin_ref[j, h]
                readout.append(_dot(q8, c0.astype(BF16)))
                lhs = jnp.where(lane_sl == seq % (LANES // DEC_SEQ), kwt_ref[h, grp], jnp.zeros((), BF16))
                cout_ref[j, h] = dec_ref[pl.ds(seq, 1), :][:, h:h + 1] * c0 + _dot(lhs, vsm_ref[h, grp])
            qcsm_ref[h, rows, :] = jnp.where(first, readout[0], readout[1])

    @pl.when(i == pl.num_programs(0) - 1)
    def _():
        gs = _sample_gates(g_ref, m0_ref[...])
        for h in range(N_HEADS):
            hs = _hcols(0, h)
            n0 = n0_ref[:, hs]
            qs = [p_ref[t * n_seq:(t + 1) * n_seq, _hcols(COL_Q, h)] * scale for t in range(DEC_SEQ)]
            ks = [p_ref[t * n_seq:(t + 1) * n_seq, _hcols(COL_K, h)] for t in range(DEC_SEQ)]
            for t in range(DEC_SEQ):
                rows = slice(t * n_seq, (t + 1) * n_seq)
                inter = jnp.exp(m0_ref[...] - gs["big_m"][t])[:, h:h + 1]
                num = inter * qcsm_ref[h, pl.ds(t, n_seq, stride=DEC_SEQ), :]
                den = inter * jnp.sum(qs[t] * n0, axis=-1, keepdims=True)
                for s in range(t + 1):
                    pts = (jnp.sum(qs[t] * ks[s], axis=-1, keepdims=True)
                           * jnp.exp(gs["r"][s] - gs["big_m"][t])[:, h:h + 1])
                    num = num + pts * p_ref[s * n_seq:(s + 1) * n_seq, _hcols(COL_V, h)]
                    den = den + pts
                hh = num / jnp.maximum(jnp.abs(den), jnp.exp(-gs["m"][t])[:, h:h + 1])
                hh = hh * lax.rsqrt(jnp.mean(hh * hh, axis=-1, keepdims=True) + EPS) * gain_ref[:, hs]
                mix_ref[rows, _hcols(POOL_WIDTH, h)] = (
                    hh * jax.nn.sigmoid(p_ref[rows, _hcols(COL_O, h)])).astype(BF16)
        x1_ref[...] = xs_ref[...] + _dot(mix_ref[...], wout_ref[...])


def _ffn_kernel(xs_ref, xp_ref, n2_ref, nf_ref, wup_hbm, wdown_hbm, yp_ref, ys_ref,
                wup_ref, wdown_ref, stage_ref, sem):
    i = pl.program_id(0)
    last = pl.num_programs(0) - 1

    def fetch(c, up):
        cols = pl.ds(c * FFN_CHUNK, FFN_CHUNK)
        src = wup_hbm.at[:, cols] if up else wdown_hbm.at[cols, :]
        slot = 0 if up else 1
        return pltpu.make_async_copy(src, stage_ref.at[slot], sem.at[slot])

    def store_rows(rows, y):
        yp_ref[rows, :] = y

    def store_tiles(rows, y):
        n_seq = ys_ref.shape[0]
        for k in range(FFN_SUB // n_seq):
            t = rows.start // n_seq + k
            for lt in range(D_MODEL // LANES):
                ys_ref[:, lt, t, :] = y[k * n_seq:(k + 1) * n_seq, lt * LANES:(lt + 1) * LANES]

    def mlp(x_ref, store, load_weights):
        subs = [slice(j * FFN_SUB, (j + 1) * FFN_SUB) for j in range(x_ref.shape[0] // FFN_SUB)]
        n_chunk = D_FF // FFN_CHUNK
        if load_weights:
            fetch(0, True).start()
            fetch(0, False).start()
        xn = [_rmsnorm(x_ref[rows, :], n2_ref[...]).astype(BF16) for rows in subs]
        acc = [x_ref[rows, :] for rows in subs]
        for c in range(n_chunk):
            if load_weights:
                fetch(c, True).wait()
                wup_ref[c] = stage_ref[0].astype(BF16)
                if c + 1 < n_chunk:
                    fetch(c + 1, True).start()
                fetch(c, False).wait()
                wdown_ref[c] = stage_ref[1].astype(BF16)
                if c + 1 < n_chunk:
                    fetch(c + 1, False).start()
            for j in range(len(subs)):
                a = jnp.square(jnp.maximum(_dot(xn[j], wup_ref[c]), 0.0)).astype(BF16)
                acc[j] = acc[j] + _dot(a, wdown_ref[c])
        for j, rows in enumerate(subs):
            store(rows, _rmsnorm(acc[j], nf_ref[...]))

    pl.when(i == 0)(lambda: mlp(xp_ref, store_rows, True))
    pl.when((i > 0) & (i < last))(lambda: mlp(xp_ref, store_rows, False))
    pl.when(i == last)(lambda: mlp(xs_ref, store_tiles, False))


def _full(shape):
    return pl.BlockSpec(shape, lambda *_: (0,) * len(shape))


def _ffn(xs, xp, n2, nf, w_up, w_down, n_seq):
    assert D_MODEL == FFN_CHUNK
    assert FFN_SUB % n_seq == 0
    n_prompt = xp.shape[0] // FFN_TILE
    n_chunk = D_FF // FFN_CHUNK
    ys_shape = (n_seq, D_MODEL // LANES, xs.shape[0] // n_seq, LANES)
    prompt_tile = pl.BlockSpec((FFN_TILE, D_MODEL), lambda i: (jnp.minimum(i, n_prompt - 1), 0))
    return pl.pallas_call(
        _ffn_kernel,
        grid=(n_prompt + 1,),
        in_specs=[_full(xs.shape), prompt_tile, _full((1, D_MODEL)), _full((1, D_MODEL)),
                  pl.BlockSpec(memory_space=pl.ANY), pl.BlockSpec(memory_space=pl.ANY)],
        out_specs=[prompt_tile, _full(ys_shape)],
        out_shape=(jax.ShapeDtypeStruct(xp.shape, F32), jax.ShapeDtypeStruct(ys_shape, F32)),
        scratch_shapes=[pltpu.VMEM((n_chunk, D_MODEL, FFN_CHUNK), BF16),
                        pltpu.VMEM((n_chunk, FFN_CHUNK, D_MODEL), BF16),
                        pltpu.VMEM((2, D_MODEL, FFN_CHUNK), F32),
                        pltpu.SemaphoreType.DMA((2,))],
        compiler_params=pltpu.CompilerParams(dimension_semantics=("arbitrary",), vmem_limit_bytes=VMEM_LIMIT),
        name="ffn",
    )(xs, xp, n2, nf, w_up, w_down)


def kernel(x_prompt, x_sample, state_pool, state_C, state_n, state_m, meta_tokens, norm1, w_in, b_gate,
           w_pool, pool_scale, head_gain, w_out, norm2, w_up, w_down, norm_f):
    B, S, _ = x_prompt.shape
    n_seq = x_sample.shape[0]
    rows_s = DEC_SEQ * n_seq
    assert w_in.shape[0] == 1 and S % PROMPT_TILE == 0 and n_seq % SAMPLE_BLOCK == 0
    assert (B * S) % FFN_TILE == 0 and rows_s % FFN_SUB == 0

    win = w_in[0].astype(BF16)
    gpad = ((0, 0), (0, LANES - N_HEADS))
    wg = jnp.concatenate([jnp.pad(w_in[0][:, IN_MAIN:IN_MAIN + N_HEADS], gpad),
                          jnp.pad(w_in[0][:, IN_MAIN + N_HEADS:], gpad)], axis=1).astype(BF16)
    bg = jnp.concatenate([jnp.pad(b_gate[0][:N_HEADS], (0, LANES - N_HEADS)),
                          jnp.pad(b_gate[0][N_HEADS:], (0, LANES - N_HEADS))]).reshape(1, GATE_W)
    n1 = norm1[0].reshape(1, D_MODEL)
    n2 = norm2[0].reshape(1, D_MODEL)
    nf = norm_f.reshape(1, D_MODEL)
    pscale = pool_scale[0].reshape(1, POOL_WIDTH)
    gain = head_gain[0].reshape(1, MLSTM_WIDTH)

    weight_specs = [_full((1, D_MODEL)), _full((D_MODEL, IN_MAIN)), _full((D_MODEL, GATE_W)), _full((1, GATE_W))]
    mixer_specs = [_full((len(POOL_WINDOWS), POOL_GC, POOL_GC)), _full((1, POOL_WIDTH)),
                   _full((1, MLSTM_WIDTH)), _full((D_MODEL, D_MODEL))]

    u_meta, c_meta, m_meta, wout, wpool = pl.pallas_call(
        _meta_kernel,
        out_shape=(jax.ShapeDtypeStruct((N_META, POOL_WIDTH), F32),
                   jax.ShapeDtypeStruct((N_HEADS, HEAD_DIM, AUG), F32),
                   jax.ShapeDtypeStruct((SUBLANES, LANES), F32),
                   jax.ShapeDtypeStruct((D_MODEL, D_MODEL), BF16),
                   jax.ShapeDtypeStruct((len(POOL_WINDOWS), POOL_GC, POOL_GC), BF16)),
        compiler_params=pltpu.CompilerParams(vmem_limit_bytes=VMEM_LIMIT),
        name="meta_prefix",
    )(meta_tokens, n1, win, wg, bg, w_out[0], w_pool[0])

    nt = S // PROMPT_TILE
    n_tiles = B * nt
    assert n_tiles % 2 == 0 and n_tiles >= 2
    assert D_MODEL % (n_tiles * 2 * SUBLANES) == 0

    def tile_at(lag):
        def index(s):
            j = jnp.clip(s - lag, 0, n_tiles - 1)
            return (j // nt, j % nt, 0)
        return index

    def mixed_seq(ndim):
        return lambda s: (jnp.clip(s - 1, 0, n_tiles - 1) // nt,) + (0,) * (ndim - 1)

    x1_p, pool_p, m_p, c_p, n_p = pl.pallas_call(
        functools.partial(_prompt_kernel, nt),
        grid=(n_tiles + 2,),
        in_specs=[pl.BlockSpec((1, PROMPT_TILE, D_MODEL), tile_at(0)),
                  pl.BlockSpec((1, PROMPT_TILE, D_MODEL), tile_at(2)),
                  _full((N_META, POOL_WIDTH)), _full((N_HEADS, HEAD_DIM, AUG)), _full((SUBLANES, LANES))]
        + weight_specs + mixer_specs,
        out_specs=[pl.BlockSpec((1, PROMPT_TILE, D_MODEL), tile_at(2)),
                   pl.BlockSpec((1, CTX_PAD, POOL_WIDTH), mixed_seq(3)),
                   pl.BlockSpec((1, SUBLANES, LANES), mixed_seq(3)),
                   pl.BlockSpec((1, N_HEADS, HEAD_DIM, HEAD_DIM), mixed_seq(4)),
                   pl.BlockSpec((1, N_HEADS, HEAD_DIM), mixed_seq(3))],
        out_shape=(jax.ShapeDtypeStruct((B, S, D_MODEL), F32),
                   jax.ShapeDtypeStruct((B, CTX_PAD, POOL_WIDTH), F32),
                   jax.ShapeDtypeStruct((B, SUBLANES, LANES), F32),
                   jax.ShapeDtypeStruct((B, N_HEADS, HEAD_DIM, HEAD_DIM), F32),
                   jax.ShapeDtypeStruct((B, N_HEADS, HEAD_DIM), F32)),
        scratch_shapes=[pltpu.VMEM((CTX_PAD + CHUNK, POOL_WIDTH), F32),
                        pltpu.VMEM((PROMPT_TILE, D_MODEL), BF16),
                        pltpu.VMEM((PROMPT_TILE, IN_MAIN), F32), pltpu.VMEM((PROMPT_TILE, IN_MAIN), F32),
                        pltpu.VMEM((PROMPT_TILE, GATE_W), F32), pltpu.VMEM((PROMPT_TILE, GATE_W), F32),
                        pltpu.VMEM((PROMPT_TILE, D_MODEL), BF16), pltpu.VMEM((PROMPT_TILE, D_MODEL), BF16),
                        pltpu.VMEM((N_HEADS, HEAD_DIM, AUG), F32), pltpu.VMEM((SUBLANES, LANES), F32)],
        compiler_params=pltpu.CompilerParams(dimension_semantics=("arbitrary",), vmem_limit_bytes=VMEM_LIMIT),
        name="prompt_mixer",
    )(x_prompt, x_prompt, u_meta, c_meta, m_meta, n1, win, wg, bg, wpool, pscale, gain, wout)

    xs_tiles = x_sample.reshape(n_seq, DEC_SEQ, D_MODEL // LANES, LANES).transpose(0, 2, 1, 3)
    x1_p, xs_tiles = lax.optimization_barrier((x1_p, xs_tiles))
    ctx_t = state_pool[0].transpose(1, 0, 2)
    n0 = state_n[0].reshape(n_seq, MLSTM_WIDTH)
    m0 = jnp.pad(state_m[0], ((0, 0), (0, LANES - N_HEADS)))
    state_block = pl.BlockSpec((SAMPLE_BLOCK, N_HEADS, HEAD_DIM, HEAD_DIM), lambda i: (i, 0, 0, 0))
    x1_s, pool_s, c_s, n_s, m_s = pl.pallas_call(
        _sample_kernel,
        grid=(n_seq // SAMPLE_BLOCK,),
        in_specs=[_full(xs_tiles.shape), _full((POOL_CTX, n_seq, POOL_WIDTH)), state_block,
                  _full((n_seq, MLSTM_WIDTH)), _full((n_seq, LANES))] + weight_specs + mixer_specs,
        out_specs=[_full((rows_s, D_MODEL)), _full((POOL_CTX, n_seq, POOL_WIDTH)), state_block,
                   _full((n_seq, MLSTM_WIDTH)), _full((n_seq, LANES))],
        out_shape=(jax.ShapeDtypeStruct((rows_s, D_MODEL), F32),
                   jax.ShapeDtypeStruct((POOL_CTX, n_seq, POOL_WIDTH), F32),
                   jax.ShapeDtypeStruct((n_seq, N_HEADS, HEAD_DIM, HEAD_DIM), F32),
                   jax.ShapeDtypeStruct((n_seq, MLSTM_WIDTH), F32),
                   jax.ShapeDtypeStruct((n_seq, LANES), F32)),
        scratch_shapes=[pltpu.VMEM((rows_s, D_MODEL), F32),
                        pltpu.VMEM((rows_s, IN_MAIN), F32),
                        pltpu.VMEM((rows_s, GATE_W), F32),
                        pltpu.VMEM((N_HEADS, rows_s, HEAD_DIM), F32),
                        pltpu.VMEM((N_HEADS, rows_s, HEAD_DIM), F32),
                        pltpu.VMEM((N_HEADS, rows_s, HEAD_DIM), F32),
                        pltpu.VMEM((N_HEADS, rows_s // LANES, HEAD_DIM, LANES), BF16),
                        pltpu.VMEM((N_HEADS, rows_s // LANES, LANES, HEAD_DIM), BF16),
                        pltpu.VMEM((n_seq, LANES), F32),
                        pltpu.VMEM((rows_s, D_MODEL), BF16)],
        compiler_params=pltpu.CompilerParams(dimension_semantics=("arbitrary",), vmem_limit_bytes=VMEM_LIMIT),
        name="sample_mixer",
    )(xs_tiles, ctx_t, state_C[0], n0, m0, n1, win, wg, bg, wpool, pscale, gain, wout)

    y_p, y_s = _ffn(x1_s, x1_p.reshape(B * S, D_MODEL), n2, nf, w_up[0], w_down[0], n_seq)
    y_p = y_p.reshape(B, S, D_MODEL)
    y_s = y_s.transpose(0, 2, 1, 3).reshape(n_seq, DEC_SEQ, D_MODEL)

    return (y_p, y_s,
            pool_p[:, CTX_PAD - POOL_CTX:, :][None],
            c_p[None],
            n_p[None],
            m_p[:, 0, :N_HEADS][None],
            pool_s.transpose(1, 0, 2)[None],
            c_s[None],
            n_s.reshape(n_seq, N_HEADS, HEAD_DIM)[None],
            m_s[:, :N_HEADS][None])
```

```python
import functools

import jax
import jax.numpy as jnp
from jax import lax
from jax.experimental import pallas as pl
from jax.experimental.pallas import tpu as pltpu

D_MODEL = 1024
N_META = 16
POOL_WIDTH = 512
POOL_WINDOWS = (2, 4, 8, 16)
POOL_GC = 128
POOL_CTX = 15
MLSTM_WIDTH = 512
N_HEADS = 4
HEAD_DIM = 128
D_FF = 4096
EPS = 1e-6
DEC_SEQ = 4

LANES = 128
SUBLANES = 8
COL_Q = POOL_WIDTH
COL_K = COL_Q + MLSTM_WIDTH
COL_V = COL_K + MLSTM_WIDTH
COL_O = COL_V + MLSTM_WIDTH
IN_MAIN = COL_O + MLSTM_WIDTH
GATE_W = 2 * LANES
AUG = 2 * HEAD_DIM
CTX_PAD = 16
CHUNK = 256
PROMPT_TILE = 512
IN_CHUNK = 512
MIX_PIECES = 1 + N_HEADS // 2 + N_HEADS
FFN_TILE = 512
FFN_SUB = 256
FFN_CHUNK = 1024
SAMPLE_BLOCK = 16
NEG_BIG = -1e30
VMEM_LIMIT = 56 * 1024 * 1024

F32 = jnp.float32
BF16 = jnp.bfloat16


def _rmsnorm(x, g):
    return x * lax.rsqrt(jnp.mean(x * x, axis=-1, keepdims=True) + EPS) * g


def _log_sigmoid(x):
    return jnp.minimum(x, 0.0) - jnp.log1p(jnp.exp(-jnp.abs(x)))


def _dot(a, b):
    return jnp.dot(a, b, preferred_element_type=F32)


def _dot_nt(a, b):
    return lax.dot_general(a, b, (((1,), (1,)), ((), ())), preferred_element_type=F32)


def _hcols(base, h):
    return slice(base + h * HEAD_DIM, base + (h + 1) * HEAD_DIM)


def _row_scan(x, op):
    n = x.shape[0]
    row = lax.broadcasted_iota(jnp.int32, x.shape, 0)
    sh = 1
    while sh < n:
        x = jnp.where(row >= sh, op(x, pltpu.roll(x, sh, axis=0)), x)
        sh *= 2
    return x


def _gate_cols(g, m0, n_valid):
    R = g.shape[0]
    b = _row_scan(_log_sigmoid(g[:, LANES:]), jnp.add)
    r = g[:, :LANES] - b
    big_m = jnp.maximum(m0, _row_scan(r, jnp.maximum))
    m = b + big_m
    last = n_valid - 1
    m_top = big_m[last:last + 1, :]
    ws = jnp.exp(r - m_top)
    if n_valid < R:
        ws = jnp.where(lax.broadcasted_iota(jnp.int32, ws.shape, 0) < n_valid, ws, 0.0)
    return dict(r=r, big_m=big_m, inter=jnp.exp(m0 - big_m), eneg=jnp.exp(-m), ws=ws,
                decay=jnp.exp(m0 - m_top), m_last=m[last:last + 1, :])


def _value_aug(v, h):
    lane = lax.broadcasted_iota(jnp.int32, (v.shape[0], HEAD_DIM), 1)
    return jnp.concatenate([v, jnp.where(lane == h, 1.0, 0.0)], axis=1).astype(BF16)


def _meta_kernel(x_ref, n1_ref, win_ref, wg_ref, bg_ref, wout_ref, wpool_ref,
                 u_ref, c_ref, m_ref, wout_bf_ref, wpool_bf_ref):
    wout_bf_ref[...] = wout_ref[...].astype(BF16)
    wpool_bf_ref[...] = wpool_ref[...].astype(BF16)
    x = jnp.concatenate([x_ref[...], jnp.zeros((LANES - N_META, D_MODEL), F32)], axis=0)
    xn = _rmsnorm(x, n1_ref[...]).astype(BF16)
    gc = _gate_cols(_dot(xn, wg_ref[...]) + bg_ref[...], jnp.zeros((1, LANES), F32), N_META)
    p = _dot(xn, win_ref[:, 0:IN_MAIN])
    u_ref[...] = p[0:N_META, 0:POOL_WIDTH]
    for h in range(N_HEADS):
        kw_t = (p[:, _hcols(COL_K, h)] * gc["ws"][:, h:h + 1]).T.astype(BF16)
        c_ref[h] = _dot(kw_t, _value_aug(p[:, _hcols(COL_V, h)], h))
    m_ref[...] = jnp.broadcast_to(gc["m_last"], (SUBLANES, LANES))


def _prompt_kernel(nt, xa_ref, xc_ref, meta_ref, n1_ref, win_ref, wg_ref, bg_ref,
                   wpool32_ref, pscale_ref, gain_ref, wout32_ref,
                   x1_ref, pool_ref, mfin_ref, cfin_ref, nfin_ref, wout_ref, wpool_ref,
                   uext_ref, xn_ref, p0_ref, p1_ref, g0_ref, g1_ref, mix0_ref, mix1_ref, c_ref, m_ref,
                   umeta_ref, cmeta_ref, mmeta_ref):
    s = pl.program_id(0)
    n_tiles = pl.num_programs(0) - 2

    @pl.when(s == 0)
    def _():
        _meta_kernel(meta_ref, n1_ref, win_ref, wg_ref, bg_ref, wout32_ref, wpool32_ref,
                     umeta_ref, cmeta_ref, mmeta_ref, wout_ref, wpool_ref)

    @pl.when((jnp.maximum(s - 1, 0) % nt == 0) & (s <= n_tiles))
    def _():
        c_ref[...] = cmeta_ref[...]
        m_ref[...] = mmeta_ref[...]
        uext_ref[0:CTX_PAD, :] = umeta_ref[...]

    stage = functools.partial(_prompt_step, xa_ref, xc_ref, n1_ref, win_ref, wg_ref, bg_ref, wpool_ref,
                              pscale_ref, gain_ref, wout_ref, x1_ref, c_ref, m_ref, uext_ref, xn_ref)
    even = (p0_ref, p1_ref, g0_ref, g1_ref, mix0_ref, mix1_ref)
    odd = (p1_ref, p0_ref, g1_ref, g0_ref, mix1_ref, mix0_ref)
    steady = (s >= 2) & (s < n_tiles)
    pl.when(s == 0)(lambda: stage(*even, project=True, mix=False, emit=False))
    pl.when(s == 1)(lambda: stage(*odd, project=True, mix=True, emit=False))
    pl.when(steady & (s % 2 == 0))(lambda: stage(*even, project=True, mix=True, emit=True))
    pl.when(steady & (s % 2 == 1))(lambda: stage(*odd, project=True, mix=True, emit=True))
    pl.when(s == n_tiles)(lambda: stage(*even, project=False, mix=True, emit=True))
    pl.when(s == n_tiles + 1)(lambda: stage(*odd, project=False, mix=False, emit=True))

    @pl.when((s >= 1) & (s <= n_tiles) & ((s - 1) % nt == nt - 1))
    def _():
        for h in range(N_HEADS):
            cfin_ref[0, h] = c_ref[h, :, 0:HEAD_DIM]
            nfin_ref[0, h:h + 1, :] = c_ref[h, :, HEAD_DIM:].T[h:h + 1, :]
        mfin_ref[0] = m_ref[...]
        pool_ref[0] = uext_ref[0:CTX_PAD, :]


def _prompt_step(xa_ref, xc_ref, n1_ref, win_ref, wg_ref, bg_ref, wpool_ref, pscale_ref, gain_ref, wout_ref,
                 x1_ref, c_ref, m_ref, uext_ref, xn_ref,
                 p_new, pb, g_new, g_old, mx, mix_old, *, project, mix, emit):
    dense = []
    if emit:
        def out_proj():
            x1_ref[0] = xc_ref[0] + _dot(mix_old[...], wout_ref[...])
        dense.append(out_proj)
    if project:
        def in_norm():
            xn_ref[...] = _rmsnorm(xa_ref[0], n1_ref[...]).astype(BF16)
            g_new[...] = _dot(xn_ref[...], wg_ref[...]) + bg_ref[...]

        def in_chunk(c):
            cs = slice(c * IN_CHUNK, (c + 1) * IN_CHUNK)
            p_new[:, cs] = _dot(xn_ref[...], win_ref[:, cs])
        dense.append(in_norm)
        dense += [functools.partial(in_chunk, c) for c in range(IN_MAIN // IN_CHUNK)]

    pieces = []
    if mix:
        for ci in range(PROMPT_TILE // CHUNK):
            pieces.append(_mix_chunk(pb, g_old, mx, slice(ci * CHUNK, (ci + 1) * CHUNK),
                                     wpool_ref, pscale_ref, gain_ref, c_ref, m_ref, uext_ref))
    n_mix = MIX_PIECES * len(pieces)
    done = 0
    for k, piece in enumerate(p for chunk in pieces for p in chunk):
        while done < len(dense) and done * n_mix < k * len(dense):
            dense[done]()
            done += 1
    for job in dense[done:]:
        job()


def _mix_chunk(pb, g_old, mx, rows, wpool_ref, pscale_ref, gain_ref, c_ref, m_ref, uext_ref):
    R = CHUNK
    gc = _gate_cols(g_old[rows, :], m_ref[0:1, :], R)
    r_t = gc["r"].T

    uext_ref[CTX_PAD:CTX_PAD + R, :] = pb[rows, 0:POOL_WIDTH]
    for gi, w in enumerate(POOL_WINDOWS):
        cs = slice(gi * POOL_GC, (gi + 1) * POOL_GC)
        acc = uext_ref[:, cs]
        sh = 1
        while sh < w:
            acc = acc + pltpu.roll(acc, sh, axis=0)
            sh *= 2
        z = acc[CTX_PAD:, :] * (1.0 / w) - uext_ref[CTX_PAD:CTX_PAD + R, cs]
        mx[rows, cs] = (_dot(z.astype(BF16), wpool_ref[gi]) * pscale_ref[:, cs]).astype(BF16)
    tail = uext_ref[R:R + CTX_PAD, :]
    uext_ref[0:CTX_PAD, :] = tail
    m_ref[...] = jnp.broadcast_to(gc["m_last"], (SUBLANES, LANES))
    yield

    causal = (lax.broadcasted_iota(jnp.int32, (R, R), 0) >= lax.broadcasted_iota(jnp.int32, (R, R), 1))
    front = []
    for h in range(N_HEADS):
        q = (pb[rows, _hcols(COL_Q, h)] * (HEAD_DIM ** -0.5)).astype(BF16)
        k = pb[rows, _hcols(COL_K, h)]
        vaug = _value_aug(pb[rows, _hcols(COL_V, h)], h)
        sc = _dot_nt(q, k.astype(BF16))
        c0 = c_ref[h]
        qc = _dot(q, c0.astype(BF16))
        kw_t = (k * gc["ws"][:, h:h + 1]).T.astype(BF16)
        c_ref[h] = gc["decay"][:, h:h + 1] * c0 + _dot(kw_t, vaug)
        front.append((sc, qc, vaug))
        if h % 2 == 1:
            yield
    for h in range(N_HEADS):
        sc, qc, vaug = front[h]
        e = jnp.exp(jnp.where(causal, r_t[h:h + 1, :] - gc["big_m"][:, h:h + 1], NEG_BIG))
        num = gc["inter"][:, h:h + 1] * qc + _dot((sc * e).astype(BF16), vaug)
        inv = 1.0 / jnp.maximum(jnp.abs(num[:, HEAD_DIM:]), gc["eneg"])
        hh = num[:, 0:HEAD_DIM] * inv[:, h:h + 1]
        hh = hh * lax.rsqrt(jnp.mean(hh * hh, axis=-1, keepdims=True) + EPS) * gain_ref[:, _hcols(0, h)]
        mx[rows, _hcols(POOL_WIDTH, h)] = (hh * jax.nn.sigmoid(pb[rows, _hcols(COL_O, h)])).astype(BF16)
        yield


def _sample_gates(g_ref, m0):
    n = m0.shape[0]
    b = jnp.zeros((n, LANES), F32)
    big_m = m0
    r, big_ms, ms = [], [], []
    for t in range(DEC_SEQ):
        g = g_ref[t * n:(t + 1) * n, :]
        b = b + _log_sigmoid(g[:, LANES:])
        r.append(g[:, :LANES] - b)
        big_m = jnp.maximum(big_m, r[-1])
        big_ms.append(big_m)
        ms.append(b + big_m)
    return dict(r=r, big_m=big_ms, m=ms,
                ws=[jnp.exp(r[t] - big_m) for t in range(DEC_SEQ)],
                decay=jnp.exp(m0 - big_m))


def _sample_kernel(x4_ref, ctx_ref, cin_ref, n0_ref, m0_ref, n1_ref, win_ref, wg_ref, bg_ref, wpool_ref,
                   pscale_ref, gain_ref, wout_ref,
                   x1_ref, pool_ref, cout_ref, nout_ref, mout_ref,
                   xs_ref, p_ref, g_ref, tmp_ref, qsm_ref, qcsm_ref, kwt_ref, vsm_ref, dec_ref, mix_ref):
    i = pl.program_id(0)
    n_seq = m0_ref.shape[0]
    n_grp = DEC_SEQ * n_seq // LANES
    scale = HEAD_DIM ** -0.5

    def seq_major(dst_ref, slab_of_t):
        for t in range(DEC_SEQ):
            slab = slab_of_t(t)
            for h in range(N_HEADS):
                dst_ref[h, pl.ds(t, n_seq, stride=DEC_SEQ), :] = slab[:, _hcols(0, h)]

    @pl.when(i == 0)
    def _():
        for t in range(DEC_SEQ):
            for lt in range(D_MODEL // LANES):
                xs_ref[t * n_seq:(t + 1) * n_seq, lt * LANES:(lt + 1) * LANES] = x4_ref[:, lt, t, :]
        xn = _rmsnorm(xs_ref[...], n1_ref[...]).astype(BF16)
        g_ref[...] = _dot(xn, wg_ref[...]) + bg_ref[...]
        p_ref[...] = _dot(xn, win_ref[...])

        def urow(e, cs):
            if e < POOL_CTX:
                return ctx_ref[e, :, cs]
            return p_ref[(e - POOL_CTX) * n_seq:(e - POOL_CTX + 1) * n_seq, cs]

        for gi, w in enumerate(POOL_WINDOWS):
            cs = slice(gi * POOL_GC, (gi + 1) * POOL_GC)
            for t in range(DEC_SEQ):
                e = POOL_CTX + t
                wsum = urow(e, cs)
                for j in range(1, w):
                    wsum = wsum + urow(e - j, cs)
                z = wsum * (1.0 / w) - urow(e, cs)
                mix_ref[t * n_seq:(t + 1) * n_seq, cs] = (
                    _dot(z.astype(BF16), wpool_ref[gi]) * pscale_ref[:, cs]).astype(BF16)
        for e in range(DEC_SEQ, POOL_CTX + DEC_SEQ):
            pool_ref[e - DEC_SEQ] = urow(e, slice(0, POOL_WIDTH))

        gs = _sample_gates(g_ref, m0_ref[...])
        dec_ref[...] = gs["decay"]
        mout_ref[...] = gs["m"][DEC_SEQ - 1]
        def kw_slab(t):
            rows = slice(t * n_seq, (t + 1) * n_seq)
            return jnp.concatenate([p_ref[rows, _hcols(COL_K, h)] * gs["ws"][t][:, h:h + 1]
                                    for h in range(N_HEADS)], axis=1)

        kw = [kw_slab(t) for t in range(DEC_SEQ)]
        nout_ref[...] = (jnp.concatenate([jnp.broadcast_to(gs["decay"][:, h:h + 1], (n_seq, HEAD_DIM))
                                          for h in range(N_HEADS)], axis=1) * n0_ref[...]
                         + kw[0] + kw[1] + kw[2] + kw[3])
        seq_major(tmp_ref, lambda t: kw[t])
        for h in range(N_HEADS):
            for gi in range(n_grp):
                kwt_ref[h, gi] = tmp_ref[h, gi * LANES:(gi + 1) * LANES, :].T.astype(BF16)
        seq_major(tmp_ref, lambda t: p_ref[t * n_seq:(t + 1) * n_seq, COL_V:COL_O])
        for h in range(N_HEADS):
            for gi in range(n_grp):
                vsm_ref[h, gi] = tmp_ref[h, gi * LANES:(gi + 1) * LANES, :].astype(BF16)
        seq_major(qsm_ref, lambda t: p_ref[t * n_seq:(t + 1) * n_seq, COL_Q:COL_K] * scale)

    lane_sl = lax.broadcasted_iota(jnp.int32, (HEAD_DIM, LANES), 1) // DEC_SEQ
    first = lax.broadcasted_iota(jnp.int32, (SUBLANES, HEAD_DIM), 0) < DEC_SEQ
    grp = (i * SAMPLE_BLOCK) // (LANES // DEC_SEQ)
    for jp in range(SAMPLE_BLOCK // 2):
        rows = pl.ds(pl.multiple_of(i * (SAMPLE_BLOCK * DEC_SEQ) + jp * SUBLANES, SUBLANES), SUBLANES)
        for h in range(N_HEADS):
            hs = _hcols(0, h)
            q8 = qsm_ref[h, rows, :].astype(BF16)
            readout = []
            for a in range(2):
                j = 2 * jp + a
                seq = i * SAMPLE_BLOCK + j
                c0 = cin_ref[j, h]
                readout.append(_dot(q8, c0.astype(BF16)))
                lhs = jnp.where(lane_sl == seq % (LANES // DEC_SEQ), kwt_ref[h, grp], jnp.zeros((), BF16))
                cout_ref[j, h] = dec_ref[pl.ds(seq, 1), :][:, h:h + 1] * c0 + _dot(lhs, vsm_ref[h, grp])
            qcsm_ref[h, rows, :] = jnp.where(first, readout[0], readout[1])

    @pl.when(i == pl.num_programs(0) - 1)
    def _():
        gs = _sample_gates(g_ref, m0_ref[...])
        for h in range(N_HEADS):
            hs = _hcols(0, h)
            n0 = n0_ref[:, hs]
            qs = [p_ref[t * n_seq:(t + 1) * n_seq, _hcols(COL_Q, h)] * scale for t in range(DEC_SEQ)]
            ks = [p_ref[t * n_seq:(t + 1) * n_seq, _hcols(COL_K, h)] for t in range(DEC_SEQ)]
            for t in range(DEC_SEQ):
                rows = slice(t * n_seq, (t + 1) * n_seq)
                inter = jnp.exp(m0_ref[...] - gs["big_m"][t])[:, h:h + 1]
                num = inter * qcsm_ref[h, pl.ds(t, n_seq, stride=DEC_SEQ), :]
                den = inter * jnp.sum(qs[t] * n0, axis=-1, keepdims=True)
                for s in range(t + 1):
                    pts = (jnp.sum(qs[t] * ks[s], axis=-1, keepdims=True)
                           * jnp.exp(gs["r"][s] - gs["big_m"][t])[:, h:h + 1])
                    num = num + pts * p_ref[s * n_seq:(s + 1) * n_seq, _hcols(COL_V, h)]
                    den = den + pts
                hh = num / jnp.maximum(jnp.abs(den), jnp.exp(-gs["m"][t])[:, h:h + 1])
                hh = hh * lax.rsqrt(jnp.mean(hh * hh, axis=-1, keepdims=True) + EPS) * gain_ref[:, hs]
                mix_ref[rows, _hcols(POOL_WIDTH, h)] = (
                    hh * jax.nn.sigmoid(p_ref[rows, _hcols(COL_O, h)])).astype(BF16)
        x1_ref[...] = xs_ref[...] + _dot(mix_ref[...], wout_ref[...])


def _ffn_kernel(xs_ref, xp_ref, n2_ref, nf_ref, wup_hbm, wdown_hbm, yp_ref, ys_ref,
                wup_ref, wdown_ref, stage_ref, sem):
    i = pl.program_id(0)
    last = pl.num_programs(0) - 1

    def fetch(c, up):
        cols = pl.ds(c * FFN_CHUNK, FFN_CHUNK)
        src = wup_hbm.at[:, cols] if up else wdown_hbm.at[cols, :]
        slot = 0 if up else 1
        return pltpu.make_async_copy(src, stage_ref.at[slot], sem.at[slot])

    def store_rows(rows, y):
        yp_ref[rows, :] = y

    def store_tiles(rows, y):
        n_seq = ys_ref.shape[0]
        for k in range(FFN_SUB // n_seq):
            t = rows.start // n_seq + k
            for lt in range(D_MODEL // LANES):
                ys_ref[:, lt, t, :] = y[k * n_seq:(k + 1) * n_seq, lt * LANES:(lt + 1) * LANES]

    def mlp(x_ref, store, load_weights):
        subs = [slice(j * FFN_SUB, (j + 1) * FFN_SUB) for j in range(x_ref.shape[0] // FFN_SUB)]
        n_chunk = D_FF // FFN_CHUNK
        if load_weights:
            fetch(0, True).start()
            fetch(0, False).start()
        xn = [_rmsnorm(x_ref[rows, :], n2_ref[...]).astype(BF16) for rows in subs]
        acc = [x_ref[rows, :] for rows in subs]
        for c in range(n_chunk):
            if load_weights:
                fetch(c, True).wait()
                wup_ref[c] = stage_ref[0].astype(BF16)
                if c + 1 < n_chunk:
                    fetch(c + 1, True).start()
                fetch(c, False).wait()
                wdown_ref[c] = stage_ref[1].astype(BF16)
                if c + 1 < n_chunk:
                    fetch(c + 1, False).start()
            for j in range(len(subs)):
                a = jnp.square(jnp.maximum(_dot(xn[j], wup_ref[c]), 0.0)).astype(BF16)
                acc[j] = acc[j] + _dot(a, wdown_ref[c])
        for j, rows in enumerate(subs):
            store(rows, _rmsnorm(acc[j], nf_ref[...]))

    pl.when(i == 0)(lambda: mlp(xp_ref, store_rows, True))
    pl.when((i > 0) & (i < last))(lambda: mlp(xp_ref, store_rows, False))
    pl.when(i == last)(lambda: mlp(xs_ref, store_tiles, False))


def _full(shape):
    return pl.BlockSpec(shape, lambda *_: (0,) * len(shape))


def _ffn(xs, xp, n2, nf, w_up, w_down, n_seq):
    assert D_MODEL == FFN_CHUNK
    assert FFN_SUB % n_seq == 0
    n_prompt = xp.shape[0] // FFN_TILE
    n_chunk = D_FF // FFN_CHUNK
    ys_shape = (n_seq, D_MODEL // LANES, xs.shape[0] // n_seq, LANES)
    prompt_tile = pl.BlockSpec((FFN_TILE, D_MODEL), lambda i: (jnp.minimum(i, n_prompt - 1), 0))
    return pl.pallas_call(
        _ffn_kernel,
        grid=(n_prompt + 1,),
        in_specs=[_full(xs.shape), prompt_tile, _full((1, D_MODEL)), _full((1, D_MODEL)),
                  pl.BlockSpec(memory_space=pl.ANY), pl.BlockSpec(memory_space=pl.ANY)],
        out_specs=[prompt_tile, _full(ys_shape)],
        out_shape=(jax.ShapeDtypeStruct(xp.shape, F32), jax.ShapeDtypeStruct(ys_shape, F32)),
        scratch_shapes=[pltpu.VMEM((n_chunk, D_MODEL, FFN_CHUNK), BF16),
                        pltpu.VMEM((n_chunk, FFN_CHUNK, D_MODEL), BF16),
                        pltpu.VMEM((2, D_MODEL, FFN_CHUNK), F32),
                        pltpu.SemaphoreType.DMA((2,))],
        compiler_params=pltpu.CompilerParams(dimension_semantics=("arbitrary",), vmem_limit_bytes=VMEM_LIMIT),
        name="ffn",
    )(xs, xp, n2, nf, w_up, w_down)


def kernel(x_prompt, x_sample, state_pool, state_C, state_n, state_m, meta_tokens, norm1, w_in, b_gate,
           w_pool, pool_scale, head_gain, w_out, norm2, w_up, w_down, norm_f):
    B, S, _ = x_prompt.shape
    n_seq = x_sample.shape[0]
    rows_s = DEC_SEQ * n_seq
    assert w_in.shape[0] == 1 and S % PROMPT_TILE == 0 and n_seq % SAMPLE_BLOCK == 0
    assert (B * S) % FFN_TILE == 0 and rows_s % FFN_SUB == 0

    win = w_in[0].astype(BF16)
    gpad = ((0, 0), (0, LANES - N_HEADS))
    wg = jnp.concatenate([jnp.pad(w_in[0][:, IN_MAIN:IN_MAIN + N_HEADS], gpad),
                          jnp.pad(w_in[0][:, IN_MAIN + N_HEADS:], gpad)], axis=1).astype(BF16)
    bg = jnp.concatenate([jnp.pad(b_gate[0][:N_HEADS], (0, LANES - N_HEADS)),
                          jnp.pad(b_gate[0][N_HEADS:], (0, LANES - N_HEADS))]).reshape(1, GATE_W)
    n1 = norm1[0].reshape(1, D_MODEL)
    n2 = norm2[0].reshape(1, D_MODEL)
    nf = norm_f.reshape(1, D_MODEL)
    pscale = pool_scale[0].reshape(1, POOL_WIDTH)
    gain = head_gain[0].reshape(1, MLSTM_WIDTH)

    weight_specs = [_full((1, D_MODEL)), _full((D_MODEL, IN_MAIN)), _full((D_MODEL, GATE_W)), _full((1, GATE_W))]
    mixer_specs = [_full((len(POOL_WINDOWS), POOL_GC, POOL_GC)), _full((1, POOL_WIDTH)),
                   _full((1, MLSTM_WIDTH)), _full((D_MODEL, D_MODEL))]

    nt = S // PROMPT_TILE
    n_tiles = B * nt
    assert n_tiles % 2 == 0 and n_tiles >= 2
    assert D_MODEL % (n_tiles * 2 * SUBLANES) == 0

    def tile_at(lag):
        def index(s):
            j = jnp.clip(s - lag, 0, n_tiles - 1)
            return (j // nt, j % nt, 0)
        return index

    def mixed_seq(ndim):
        return lambda s: (jnp.clip(s - 1, 0, n_tiles - 1) // nt,) + (0,) * (ndim - 1)

    wpool_shape = (len(POOL_WINDOWS), POOL_GC, POOL_GC)
    x1_p, pool_p, m_p, c_p, n_p, wout, wpool = pl.pallas_call(
        functools.partial(_prompt_kernel, nt),
        grid=(n_tiles + 2,),
        in_specs=[pl.BlockSpec((1, PROMPT_TILE, D_MODEL), tile_at(0)),
                  pl.BlockSpec((1, PROMPT_TILE, D_MODEL), tile_at(2)),
                  _full((N_META, D_MODEL))]
        + weight_specs + mixer_specs,
        out_specs=[pl.BlockSpec((1, PROMPT_TILE, D_MODEL), tile_at(2)),
                   pl.BlockSpec((1, CTX_PAD, POOL_WIDTH), mixed_seq(3)),
                   pl.BlockSpec((1, SUBLANES, LANES), mixed_seq(3)),
                   pl.BlockSpec((1, N_HEADS, HEAD_DIM, HEAD_DIM), mixed_seq(4)),
                   pl.BlockSpec((1, N_HEADS, HEAD_DIM), mixed_seq(3)),
                   _full((D_MODEL, D_MODEL)), _full(wpool_shape)],
        out_shape=(jax.ShapeDtypeStruct((B, S, D_MODEL), F32),
                   jax.ShapeDtypeStruct((B, CTX_PAD, POOL_WIDTH), F32),
                   jax.ShapeDtypeStruct((B, SUBLANES, LANES), F32),
                   jax.ShapeDtypeStruct((B, N_HEADS, HEAD_DIM, HEAD_DIM), F32),
                   jax.ShapeDtypeStruct((B, N_HEADS, HEAD_DIM), F32),
                   jax.ShapeDtypeStruct((D_MODEL, D_MODEL), BF16),
                   jax.ShapeDtypeStruct(wpool_shape, BF16)),
        scratch_shapes=[pltpu.VMEM((CTX_PAD + CHUNK, POOL_WIDTH), F32),
                        pltpu.VMEM((PROMPT_TILE, D_MODEL), BF16),
                        pltpu.VMEM((PROMPT_TILE, IN_MAIN), F32), pltpu.VMEM((PROMPT_TILE, IN_MAIN), F32),
                        pltpu.VMEM((PROMPT_TILE, GATE_W), F32), pltpu.VMEM((PROMPT_TILE, GATE_W), F32),
                        pltpu.VMEM((PROMPT_TILE, D_MODEL), BF16), pltpu.VMEM((PROMPT_TILE, D_MODEL), BF16),
                        pltpu.VMEM((N_HEADS, HEAD_DIM, AUG), F32), pltpu.VMEM((SUBLANES, LANES), F32),
                        pltpu.VMEM((N_META, POOL_WIDTH), F32), pltpu.VMEM((N_HEADS, HEAD_DIM, AUG), F32),
                        pltpu.VMEM((SUBLANES, LANES), F32)],
        compiler_params=pltpu.CompilerParams(dimension_semantics=("arbitrary",), vmem_limit_bytes=VMEM_LIMIT),
        name="prompt_mixer",
    )(x_prompt, x_prompt, meta_tokens, n1, win, wg, bg, w_pool[0], pscale, gain, w_out[0])

    xs_tiles = x_sample.reshape(n_seq, DEC_SEQ, D_MODEL // LANES, LANES).transpose(0, 2, 1, 3)
    x1_p, xs_tiles = lax.optimization_barrier((x1_p, xs_tiles))
    ctx_t = state_pool[0].transpose(1, 0, 2)
    n0 = state_n[0].reshape(n_seq, MLSTM_WIDTH)
    m0 = jnp.pad(state_m[0], ((0, 0), (0, LANES - N_HEADS)))
    state_block = pl.BlockSpec((SAMPLE_BLOCK, N_HEADS, HEAD_DIM, HEAD_DIM), lambda i: (i, 0, 0, 0))
    x1_s, pool_s, c_s, n_s, m_s = pl.pallas_call(
        _sample_kernel,
        grid=(n_seq // SAMPLE_BLOCK,),
        in_specs=[_full(xs_tiles.shape), _full((POOL_CTX, n_seq, POOL_WIDTH)), state_block,
                  _full((n_seq, MLSTM_WIDTH)), _full((n_seq, LANES))] + weight_specs + mixer_specs,
        out_specs=[_full((rows_s, D_MODEL)), _full((POOL_CTX, n_seq, POOL_WIDTH)), state_block,
                   _full((n_seq, MLSTM_WIDTH)), _full((n_seq, LANES))],
        out_shape=(jax.ShapeDtypeStruct((rows_s, D_MODEL), F32),
                   jax.ShapeDtypeStruct((POOL_CTX, n_seq, POOL_WIDTH), F32),
                   jax.ShapeDtypeStruct((n_seq, N_HEADS, HEAD_DIM, HEAD_DIM), F32),
                   jax.ShapeDtypeStruct((n_seq, MLSTM_WIDTH), F32),
                   jax.ShapeDtypeStruct((n_seq, LANES), F32)),
        scratch_shapes=[pltpu.VMEM((rows_s, D_MODEL), F32),
                        pltpu.VMEM((rows_s, IN_MAIN), F32),
                        pltpu.VMEM((rows_s, GATE_W), F32),
                        pltpu.VMEM((N_HEADS, rows_s, HEAD_DIM), F32),
                        pltpu.VMEM((N_HEADS, rows_s, HEAD_DIM), F32),
                        pltpu.VMEM((N_HEADS, rows_s, HEAD_DIM), F32),
                        pltpu.VMEM((N_HEADS, rows_s // LANES, HEAD_DIM, LANES), BF16),
                        pltpu.VMEM((N_HEADS, rows_s // LANES, LANES, HEAD_DIM), BF16),
                        pltpu.VMEM((n_seq, LANES), F32),
                        pltpu.VMEM((rows_s, D_MODEL), BF16)],
        compiler_params=pltpu.CompilerParams(dimension_semantics=("arbitrary",), vmem_limit_bytes=VMEM_LIMIT),
        name="sample_mixer",
    )(xs_tiles, ctx_t, state_C[0], n0, m0, n1, win, wg, bg, wpool, pscale, gain, wout)

    y_p, y_s = _ffn(x1_s, x1_p.reshape(B * S, D_MODEL), n2, nf, w_up[0], w_down[0], n_seq)
    y_p = y_p.reshape(B, S, D_MODEL)
    y_s = y_s.transpose(0, 2, 1, 3).reshape(n_seq, DEC_SEQ, D_MODEL)

    return (y_p, y_s,
            pool_p[:, CTX_PAD - POOL_CTX:, :][None],
            c_p[None],
            n_p[None],
            m_p[:, 0, :N_HEADS][None],
            pool_s.transpose(1, 0, 2)[None],
            c_s[None],
            n_s.reshape(n_seq, N_HEADS, HEAD_DIM)[None],
            m_s[:, :N_HEADS][None])
```

```python
import functools

import jax
import jax.numpy as jnp
from jax import lax
from jax.experimental import pallas as pl
from jax.experimental.pallas import tpu as pltpu

D_MODEL = 1024
N_META = 16
POOL_WIDTH = 512
POOL_WINDOWS = (2, 4, 8, 16)
POOL_GC = 128
POOL_CTX = 15
MLSTM_WIDTH = 512
N_HEADS = 4
HEAD_DIM = 128
D_FF = 4096
EPS = 1e-6
DEC_SEQ = 4

LANES = 128
SUBLANES = 8
COL_Q = POOL_WIDTH
COL_K = COL_Q + MLSTM_WIDTH
COL_V = COL_K + MLSTM_WIDTH
COL_O = COL_V + MLSTM_WIDTH
IN_MAIN = COL_O + MLSTM_WIDTH
GATE_W = 2 * LANES
AUG = 2 * HEAD_DIM
CTX_PAD = 16
CHUNK = 256
PROMPT_TILE = 512
IN_CHUNK = 512
MIX_PIECES = 1 + N_HEADS // 2 + N_HEADS
FFN_TILE = 512
FFN_SUB = 256
FFN_CHUNK = 1024
SAMPLE_BLOCK = 16
NEG_BIG = -1e30
VMEM_LIMIT = 56 * 1024 * 1024

F32 = jnp.float32
BF16 = jnp.bfloat16


def _rmsnorm(x, g):
    return x * lax.rsqrt(jnp.mean(x * x, axis=-1, keepdims=True) + EPS) * g


def _log_sigmoid(x):
    return jnp.minimum(x, 0.0) - jnp.log1p(jnp.exp(-jnp.abs(x)))


def _dot(a, b):
    return jnp.dot(a, b, preferred_element_type=F32)


def _dot_nt(a, b):
    return lax.dot_general(a, b, (((1,), (1,)), ((), ())), preferred_element_type=F32)


def _hcols(base, h):
    return slice(base + h * HEAD_DIM, base + (h + 1) * HEAD_DIM)


def _row_scan(x, op):
    n = x.shape[0]
    row = lax.broadcasted_iota(jnp.int32, x.shape, 0)
    sh = 1
    while sh < n:
        x = jnp.where(row >= sh, op(x, pltpu.roll(x, sh, axis=0)), x)
        sh *= 2
    return x


def _gate_cols(g, m0, n_valid):
    R = g.shape[0]
    b = _row_scan(_log_sigmoid(g[:, LANES:]), jnp.add)
    r = g[:, :LANES] - b
    big_m = jnp.maximum(m0, _row_scan(r, jnp.maximum))
    m = b + big_m
    last = n_valid - 1
    m_top = big_m[last:last + 1, :]
    ws = jnp.exp(r - m_top)
    if n_valid < R:
        ws = jnp.where(lax.broadcasted_iota(jnp.int32, ws.shape, 0) < n_valid, ws, 0.0)
    return dict(r=r, big_m=big_m, inter=jnp.exp(m0 - big_m), eneg=jnp.exp(-m), ws=ws,
                decay=jnp.exp(m0 - m_top), m_last=m[last:last + 1, :])


def _value_aug(v, h):
    lane = lax.broadcasted_iota(jnp.int32, (v.shape[0], HEAD_DIM), 1)
    return jnp.concatenate([v, jnp.where(lane == h, 1.0, 0.0)], axis=1).astype(BF16)


def _meta_prefix(x_ref, n1_ref, win_ref, wg_ref, bg_ref, wout_ref, wpool_ref,
                 u_ref, c_ref, m_ref, wout_bf_ref, wpool_bf_ref):
    wout_bf_ref[...] = wout_ref[...].astype(BF16)
    wpool_bf_ref[...] = wpool_ref[...].astype(BF16)
    x = jnp.concatenate([x_ref[...], jnp.zeros((LANES - N_META, D_MODEL), F32)], axis=0)
    xn = _rmsnorm(x, n1_ref[...]).astype(BF16)
    gc = _gate_cols(_dot(xn, wg_ref[...]) + bg_ref[...], jnp.zeros((1, LANES), F32), N_META)
    p = _dot(xn, win_ref[:, 0:IN_MAIN])
    u_ref[...] = p[0:N_META, 0:POOL_WIDTH]
    for h in range(N_HEADS):
        kw_t = (p[:, _hcols(COL_K, h)] * gc["ws"][:, h:h + 1]).T.astype(BF16)
        c_ref[h] = _dot(kw_t, _value_aug(p[:, _hcols(COL_V, h)], h))
    m_ref[...] = jnp.broadcast_to(gc["m_last"], (SUBLANES, LANES))


def _prompt_kernel(nt, xa_ref, xc_ref, meta_ref, n1_ref, win_ref, wg_ref, bg_ref,
                   wpool32_ref, pscale_ref, gain_ref, wout32_ref,
                   x1_ref, pool_ref, mfin_ref, cfin_ref, nfin_ref, wout_ref, wpool_ref,
                   uext_ref, xn_ref, p0_ref, p1_ref, g0_ref, g1_ref, mix0_ref, mix1_ref, c_ref, m_ref,
                   umeta_ref, cmeta_ref, mmeta_ref):
    s = pl.program_id(0)
    n_tiles = pl.num_programs(0) - 2

    @pl.when(s == 0)
    def _():
        _meta_prefix(meta_ref, n1_ref, win_ref, wg_ref, bg_ref, wout32_ref, wpool32_ref,
                     umeta_ref, cmeta_ref, mmeta_ref, wout_ref, wpool_ref)

    @pl.when((jnp.maximum(s - 1, 0) % nt == 0) & (s <= n_tiles))
    def _():
        c_ref[...] = cmeta_ref[...]
        m_ref[...] = mmeta_ref[...]
        uext_ref[0:CTX_PAD, :] = umeta_ref[...]

    stage = functools.partial(_prompt_step, xa_ref, xc_ref, n1_ref, win_ref, wg_ref, bg_ref, wpool_ref,
                              pscale_ref, gain_ref, wout_ref, x1_ref, c_ref, m_ref, uext_ref, xn_ref)
    even = (p0_ref, p1_ref, g0_ref, g1_ref, mix0_ref, mix1_ref)
    odd = (p1_ref, p0_ref, g1_ref, g0_ref, mix1_ref, mix0_ref)
    steady = (s >= 2) & (s < n_tiles)
    pl.when(s == 0)(lambda: stage(*even, project=True, mix=False, emit=False))
    pl.when(s == 1)(lambda: stage(*odd, project=True, mix=True, emit=False))
    pl.when(steady & (s % 2 == 0))(lambda: stage(*even, project=True, mix=True, emit=True))
    pl.when(steady & (s % 2 == 1))(lambda: stage(*odd, project=True, mix=True, emit=True))
    pl.when(s == n_tiles)(lambda: stage(*even, project=False, mix=True, emit=True))
    pl.when(s == n_tiles + 1)(lambda: stage(*odd, project=False, mix=False, emit=True))

    @pl.when((s >= 1) & (s <= n_tiles) & ((s - 1) % nt == nt - 1))
    def _():
        for h in range(N_HEADS):
            cfin_ref[0, h] = c_ref[h, :, 0:HEAD_DIM]
            nfin_ref[0, h:h + 1, :] = c_ref[h, :, HEAD_DIM:].T[h:h + 1, :]
        mfin_ref[0] = m_ref[...]
        pool_ref[0] = uext_ref[0:CTX_PAD, :]


def _prompt_step(xa_ref, xc_ref, n1_ref, win_ref, wg_ref, bg_ref, wpool_ref, pscale_ref, gain_ref, wout_ref,
                 x1_ref, c_ref, m_ref, uext_ref, xn_ref,
                 p_new, pb, g_new, g_old, mx, mix_old, *, project, mix, emit):
    dense = []
    if emit:
        def out_proj():
            x1_ref[0] = xc_ref[0] + _dot(mix_old[...], wout_ref[...])
        dense.append(out_proj)
    if project:
        def in_norm():
            xn_ref[...] = _rmsnorm(xa_ref[0], n1_ref[...]).astype(BF16)
            g_new[...] = _dot(xn_ref[...], wg_ref[...]) + bg_ref[...]

        def in_chunk(c):
            cs = slice(c * IN_CHUNK, (c + 1) * IN_CHUNK)
            p_new[:, cs] = _dot(xn_ref[...], win_ref[:, cs])
        dense.append(in_norm)
        dense += [functools.partial(in_chunk, c) for c in range(IN_MAIN // IN_CHUNK)]

    pieces = []
    if mix:
        for ci in range(PROMPT_TILE // CHUNK):
            pieces.append(_mix_chunk(pb, g_old, mx, slice(ci * CHUNK, (ci + 1) * CHUNK),
                                     wpool_ref, pscale_ref, gain_ref, c_ref, m_ref, uext_ref))
    n_mix = MIX_PIECES * len(pieces)
    done = 0
    for k, piece in enumerate(p for chunk in pieces for p in chunk):
        while done < len(dense) and done * n_mix < k * len(dense):
            dense[done]()
            done += 1
    for job in dense[done:]:
        job()


def _mix_chunk(pb, g_old, mx, rows, wpool_ref, pscale_ref, gain_ref, c_ref, m_ref, uext_ref):
    R = CHUNK
    gc = _gate_cols(g_old[rows, :], m_ref[0:1, :], R)
    r_t = gc["r"].T

    uext_ref[CTX_PAD:CTX_PAD + R, :] = pb[rows, 0:POOL_WIDTH]
    for gi, w in enumerate(POOL_WINDOWS):
        cs = slice(gi * POOL_GC, (gi + 1) * POOL_GC)
        acc = uext_ref[:, cs]
        sh = 1
        while sh < w:
            acc = acc + pltpu.roll(acc, sh, axis=0)
            sh *= 2
        z = acc[CTX_PAD:, :] * (1.0 / w) - uext_ref[CTX_PAD:CTX_PAD + R, cs]
        mx[rows, cs] = (_dot(z.astype(BF16), wpool_ref[gi]) * pscale_ref[:, cs]).astype(BF16)
    tail = uext_ref[R:R + CTX_PAD, :]
    uext_ref[0:CTX_PAD, :] = tail
    m_ref[...] = jnp.broadcast_to(gc["m_last"], (SUBLANES, LANES))
    yield

    causal = (lax.broadcasted_iota(jnp.int32, (R, R), 0) >= lax.broadcasted_iota(jnp.int32, (R, R), 1))
    front = []
    for h in range(N_HEADS):
        q = (pb[rows, _hcols(COL_Q, h)] * (HEAD_DIM ** -0.5)).astype(BF16)
        k = pb[rows, _hcols(COL_K, h)]
        vaug = _value_aug(pb[rows, _hcols(COL_V, h)], h)
        sc = _dot_nt(q, k.astype(BF16))
        c0 = c_ref[h]
        qc = _dot(q, c0.astype(BF16))
        kw_t = (k * gc["ws"][:, h:h + 1]).T.astype(BF16)
        c_ref[h] = gc["decay"][:, h:h + 1] * c0 + _dot(kw_t, vaug)
        front.append((sc, qc, vaug))
        if h % 2 == 1:
            yield
    for h in range(N_HEADS):
        sc, qc, vaug = front[h]
        e = jnp.exp(jnp.where(causal, r_t[h:h + 1, :] - gc["big_m"][:, h:h + 1], NEG_BIG))
        num = gc["inter"][:, h:h + 1] * qc + _dot((sc * e).astype(BF16), vaug)
        inv = 1.0 / jnp.maximum(jnp.abs(num[:, HEAD_DIM:]), gc["eneg"])
        hh = num[:, 0:HEAD_DIM] * inv[:, h:h + 1]
        hh = hh * lax.rsqrt(jnp.mean(hh * hh, axis=-1, keepdims=True) + EPS) * gain_ref[:, _hcols(0, h)]
        mx[rows, _hcols(POOL_WIDTH, h)] = (hh * jax.nn.sigmoid(pb[rows, _hcols(COL_O, h)])).astype(BF16)
        yield


def _sample_gates(g_ref, m0):
    n = m0.shape[0]
    b = jnp.zeros((n, LANES), F32)
    big_m = m0
    r, big_ms, ms = [], [], []
    for t in range(DEC_SEQ):
        g = g_ref[t * n:(t + 1) * n, :]
        b = b + _log_sigmoid(g[:, LANES:])
        r.append(g[:, :LANES] - b)
        big_m = jnp.maximum(big_m, r[-1])
        big_ms.append(big_m)
        ms.append(b + big_m)
    return dict(r=r, big_m=big_ms, m=ms,
                ws=[jnp.exp(r[t] - big_m) for t in range(DEC_SEQ)],
                decay=jnp.exp(m0 - big_m))


def _sample_kernel(x4_ref, ctx_ref, cin_ref, n0_ref, m0_ref, n1_ref, win_ref, wg_ref, bg_ref, wpool_ref,
                   pscale_ref, gain_ref, wout_ref,
                   x1_ref, pool_ref, cout_ref, nout_ref, mout_ref,
                   xs_ref, p_ref, g_ref, tmp_ref, qsm_ref, qcsm_ref, kwt_ref, vsm_ref, dec_ref, mix_ref):
    i = pl.program_id(0)
    n_seq = m0_ref.shape[0]
    n_grp = DEC_SEQ * n_seq // LANES
    scale = HEAD_DIM ** -0.5

    def seq_major(dst_ref, slab_of_t):
        for t in range(DEC_SEQ):
            slab = slab_of_t(t)
            for h in range(N_HEADS):
                dst_ref[h, pl.ds(t, n_seq, stride=DEC_SEQ), :] = slab[:, _hcols(0, h)]

    @pl.when(i == 0)
    def _():
        for t in range(DEC_SEQ):
            for lt in range(D_MODEL // LANES):
                xs_ref[t * n_seq:(t + 1) * n_seq, lt * LANES:(lt + 1) * LANES] = x4_ref[:, lt, t, :]
        xn = _rmsnorm(xs_ref[...], n1_ref[...]).astype(BF16)
        g_ref[...] = _dot(xn, wg_ref[...]) + bg_ref[...]
        p_ref[...] = _dot(xn, win_ref[...])

        def urow(e, cs):
            if e < POOL_CTX:
                return ctx_ref[e, :, cs]
            return p_ref[(e - POOL_CTX) * n_seq:(e - POOL_CTX + 1) * n_seq, cs]

        for gi, w in enumerate(POOL_WINDOWS):
            cs = slice(gi * POOL_GC, (gi + 1) * POOL_GC)
            for t in range(DEC_SEQ):
                e = POOL_CTX + t
                wsum = urow(e, cs)
                for j in range(1, w):
                    wsum = wsum + urow(e - j, cs)
                z = wsum * (1.0 / w) - urow(e, cs)
                mix_ref[t * n_seq:(t + 1) * n_seq, cs] = (
                    _dot(z.astype(BF16), wpool_ref[gi]) * pscale_ref[:, cs]).astype(BF16)
        for e in range(DEC_SEQ, POOL_CTX + DEC_SEQ):
            pool_ref[e - DEC_SEQ] = urow(e, slice(0, POOL_WIDTH))

        gs = _sample_gates(g_ref, m0_ref[...])
        dec_ref[...] = gs["decay"]
        mout_ref[...] = gs["m"][DEC_SEQ - 1]
        def kw_slab(t):
            rows = slice(t * n_seq, (t + 1) * n_seq)
            return jnp.concatenate([p_ref[rows, _hcols(COL_K, h)] * gs["ws"][t][:, h:h + 1]
                                    for h in range(N_HEADS)], axis=1)

        kw = [kw_slab(t) for t in range(DEC_SEQ)]
        nout_ref[...] = (jnp.concatenate([jnp.broadcast_to(gs["decay"][:, h:h + 1], (n_seq, HEAD_DIM))
                                          for h in range(N_HEADS)], axis=1) * n0_ref[...]
                         + kw[0] + kw[1] + kw[2] + kw[3])
        seq_major(tmp_ref, lambda t: kw[t])
        for h in range(N_HEADS):
            for gi in range(n_grp):
                kwt_ref[h, gi] = tmp_ref[h, gi * LANES:(gi + 1) * LANES, :].T.astype(BF16)
        seq_major(tmp_ref, lambda t: p_ref[t * n_seq:(t + 1) * n_seq, COL_V:COL_O])
        for h in range(N_HEADS):
            for gi in range(n_grp):
                vsm_ref[h, gi] = tmp_ref[h, gi * LANES:(gi + 1) * LANES, :].astype(BF16)
        seq_major(qsm_ref, lambda t: p_ref[t * n_seq:(t + 1) * n_seq, COL_Q:COL_K] * scale)

    lane_sl = lax.broadcasted_iota(jnp.int32, (HEAD_DIM, LANES), 1) // DEC_SEQ
    first = lax.broadcasted_iota(jnp.int32, (SUBLANES, HEAD_DIM), 0) < DEC_SEQ
    grp = (i * SAMPLE_BLOCK) // (LANES // DEC_SEQ)
    for jp in range(SAMPLE_BLOCK // 2):
        rows = pl.ds(pl.multiple_of(i * (SAMPLE_BLOCK * DEC_SEQ) + jp * SUBLANES, SUBLANES), SUBLANES)
        for h in range(N_HEADS):
            hs = _hcols(0, h)
            q8 = qsm_ref[h, rows, :].astype(BF16)
            readout = []
            for a in range(2):
                j = 2 * jp + a
                seq = i * SAMPLE_BLOCK + j
                c0 = cin_ref[j, h]
                readout.append(_dot(q8, c0.astype(BF16)))
                lhs = jnp.where(lane_sl == seq % (LANES // DEC_SEQ), kwt_ref[h, grp], jnp.zeros((), BF16))
                cout_ref[j, h] = dec_ref[pl.ds(seq, 1), :][:, h:h + 1] * c0 + _dot(lhs, vsm_ref[h, grp])
            qcsm_ref[h, rows, :] = jnp.where(first, readout[0], readout[1])

    @pl.when(i == pl.num_programs(0) - 1)
    def _():
        gs = _sample_gates(g_ref, m0_ref[...])
        for h in range(N_HEADS):
            hs = _hcols(0, h)
            n0 = n0_ref[:, hs]
            qs = [p_ref[t * n_seq:(t + 1) * n_seq, _hcols(COL_Q, h)] * scale for t in range(DEC_SEQ)]
            ks = [p_ref[t * n_seq:(t + 1) * n_seq, _hcols(COL_K, h)] for t in range(DEC_SEQ)]
            for t in range(DEC_SEQ):
                rows = slice(t * n_seq, (t + 1) * n_seq)
                inter = jnp.exp(m0_ref[...] - gs["big_m"][t])[:, h:h + 1]
                num = inter * qcsm_ref[h, pl.ds(t, n_seq, stride=DEC_SEQ), :]
                den = inter * jnp.sum(qs[t] * n0, axis=-1, keepdims=True)
                for s in range(t + 1):
                    pts = (jnp.sum(qs[t] * ks[s], axis=-1, keepdims=True)
                           * jnp.exp(gs["r"][s] - gs["big_m"][t])[:, h:h + 1])
                    num = num + pts * p_ref[s * n_seq:(s + 1) * n_seq, _hcols(COL_V, h)]
                    den = den + pts
                hh = num / jnp.maximum(jnp.abs(den), jnp.exp(-gs["m"][t])[:, h:h + 1])
                hh = hh * lax.rsqrt(jnp.mean(hh * hh, axis=-1, keepdims=True) + EPS) * gain_ref[:, hs]
                mix_ref[rows, _hcols(POOL_WIDTH, h)] = (
                    hh * jax.nn.sigmoid(p_ref[rows, _hcols(COL_O, h)])).astype(BF16)
        x1_ref[...] = xs_ref[...] + _dot(mix_ref[...], wout_ref[...])


def _ffn_kernel(xs_ref, xp_ref, n2_ref, nf_ref, wup_hbm, wdown_hbm, yp_ref, ys_ref,
                wup_ref, wdown_ref, stage_ref, sem):
    i = pl.program_id(0)
    last = pl.num_programs(0) - 1

    def fetch(c, up):
        cols = pl.ds(c * FFN_CHUNK, FFN_CHUNK)
        src = wup_hbm.at[:, cols] if up else wdown_hbm.at[cols, :]
        slot = 0 if up else 1
        return pltpu.make_async_copy(src, stage_ref.at[slot], sem.at[slot])

    def store_rows(rows, y):
        yp_ref[rows, :] = y

    def store_tiles(rows, y):
        n_seq = ys_ref.shape[0]
        for k in range(FFN_SUB // n_seq):
            t = rows.start // n_seq + k
            for lt in range(D_MODEL // LANES):
                ys_ref[:, lt, t, :] = y[k * n_seq:(k + 1) * n_seq, lt * LANES:(lt + 1) * LANES]

    def mlp(x_ref, store, load_weights):
        subs = [slice(j * FFN_SUB, (j + 1) * FFN_SUB) for j in range(x_ref.shape[0] // FFN_SUB)]
        n_chunk = D_FF // FFN_CHUNK
        if load_weights:
            fetch(0, True).start()
            fetch(0, False).start()
        xn = [_rmsnorm(x_ref[rows, :], n2_ref[...]).astype(BF16) for rows in subs]
        acc = [x_ref[rows, :] for rows in subs]
        for c in range(n_chunk):
            if load_weights:
                fetch(c, True).wait()
                wup_ref[c] = stage_ref[0].astype(BF16)
                if c + 1 < n_chunk:
                    fetch(c + 1, True).start()
                fetch(c, False).wait()
                wdown_ref[c] = stage_ref[1].astype(BF16)
                if c + 1 < n_chunk:
                    fetch(c + 1, False).start()
            for j in range(len(subs)):
                a = jnp.square(jnp.maximum(_dot(xn[j], wup_ref[c]), 0.0)).astype(BF16)
                acc[j] = acc[j] + _dot(a, wdown_ref[c])
        for j, rows in enumerate(subs):
            store(rows, _rmsnorm(acc[j], nf_ref[...]))

    pl.when(i == 0)(lambda: mlp(xp_ref, store_rows, True))
    pl.when((i > 0) & (i < last))(lambda: mlp(xp_ref, store_rows, False))
    pl.when(i == last)(lambda: mlp(xs_ref, store_tiles, False))


def _full(shape):
    return pl.BlockSpec(shape, lambda *_: (0,) * len(shape))


def _ffn(xs, xp, n2, nf, w_up, w_down, n_seq):
    assert D_MODEL == FFN_CHUNK
    assert FFN_SUB % n_seq == 0
    n_prompt = xp.shape[0] // FFN_TILE
    n_chunk = D_FF // FFN_CHUNK
    ys_shape = (n_seq, D_MODEL // LANES, xs.shape[0] // n_seq, LANES)
    prompt_tile = pl.BlockSpec((FFN_TILE, D_MODEL), lambda i: (jnp.minimum(i, n_prompt - 1), 0))
    return pl.pallas_call(
        _ffn_kernel,
        grid=(n_prompt + 1,),
        in_specs=[_full(xs.shape), prompt_tile, _full((1, D_MODEL)), _full((1, D_MODEL)),
                  pl.BlockSpec(memory_space=pl.ANY), pl.BlockSpec(memory_space=pl.ANY)],
        out_specs=[prompt_tile, _full(ys_shape)],
        out_shape=(jax.ShapeDtypeStruct(xp.shape, F32), jax.ShapeDtypeStruct(ys_shape, F32)),
        scratch_shapes=[pltpu.VMEM((n_chunk, D_MODEL, FFN_CHUNK), BF16),
                        pltpu.VMEM((n_chunk, FFN_CHUNK, D_MODEL), BF16),
                        pltpu.VMEM((2, D_MODEL, FFN_CHUNK), F32),
                        pltpu.SemaphoreType.DMA((2,))],
        compiler_params=pltpu.CompilerParams(dimension_semantics=("arbitrary",), vmem_limit_bytes=VMEM_LIMIT),
        name="ffn",
    )(xs, xp, n2, nf, w_up, w_down)


def kernel(x_prompt, x_sample, state_pool, state_C, state_n, state_m, meta_tokens, norm1, w_in, b_gate,
           w_pool, pool_scale, head_gain, w_out, norm2, w_up, w_down, norm_f):
    B, S, _ = x_prompt.shape
    n_seq = x_sample.shape[0]
    rows_s = DEC_SEQ * n_seq
    assert w_in.shape[0] == 1 and S % PROMPT_TILE == 0 and n_seq % SAMPLE_BLOCK == 0
    assert (B * S) % FFN_TILE == 0 and rows_s % FFN_SUB == 0

    win = w_in[0].astype(BF16)
    gpad = ((0, 0), (0, LANES - N_HEADS))
    wg = jnp.concatenate([jnp.pad(w_in[0][:, IN_MAIN:IN_MAIN + N_HEADS], gpad),
                          jnp.pad(w_in[0][:, IN_MAIN + N_HEADS:], gpad)], axis=1).astype(BF16)
    bg = jnp.concatenate([jnp.pad(b_gate[0][:N_HEADS], (0, LANES - N_HEADS)),
                          jnp.pad(b_gate[0][N_HEADS:], (0, LANES - N_HEADS))]).reshape(1, GATE_W)
    n1 = norm1[0].reshape(1, D_MODEL)
    n2 = norm2[0].reshape(1, D_MODEL)
    nf = norm_f.reshape(1, D_MODEL)
    pscale = pool_scale[0].reshape(1, POOL_WIDTH)
    gain = head_gain[0].reshape(1, MLSTM_WIDTH)

    weight_specs = [_full((1, D_MODEL)), _full((D_MODEL, IN_MAIN)), _full((D_MODEL, GATE_W)), _full((1, GATE_W))]
    mixer_specs = [_full((len(POOL_WINDOWS), POOL_GC, POOL_GC)), _full((1, POOL_WIDTH)),
                   _full((1, MLSTM_WIDTH)), _full((D_MODEL, D_MODEL))]

    nt = S // PROMPT_TILE
    n_tiles = B * nt
    assert n_tiles % 2 == 0 and n_tiles >= 2

    def tile_at(lag):
        def index(s):
            j = jnp.clip(s - lag, 0, n_tiles - 1)
            return (j // nt, j % nt, 0)
        return index

    def mixed_seq(ndim):
        return lambda s: (jnp.clip(s - 1, 0, n_tiles - 1) // nt,) + (0,) * (ndim - 1)

    wpool_shape = (len(POOL_WINDOWS), POOL_GC, POOL_GC)
    x1_p, pool_p, m_p, c_p, n_p, wout, wpool = pl.pallas_call(
        functools.partial(_prompt_kernel, nt),
        grid=(n_tiles + 2,),
        in_specs=[pl.BlockSpec((1, PROMPT_TILE, D_MODEL), tile_at(0)),
                  pl.BlockSpec((1, PROMPT_TILE, D_MODEL), tile_at(2)),
                  _full((N_META, D_MODEL))]
        + weight_specs + mixer_specs,
        out_specs=[pl.BlockSpec((1, PROMPT_TILE, D_MODEL), tile_at(2)),
                   pl.BlockSpec((1, CTX_PAD, POOL_WIDTH), mixed_seq(3)),
                   pl.BlockSpec((1, SUBLANES, LANES), mixed_seq(3)),
                   pl.BlockSpec((1, N_HEADS, HEAD_DIM, HEAD_DIM), mixed_seq(4)),
                   pl.BlockSpec((1, N_HEADS, HEAD_DIM), mixed_seq(3)),
                   _full((D_MODEL, D_MODEL)), _full(wpool_shape)],
        out_shape=(jax.ShapeDtypeStruct((B, S, D_MODEL), F32),
                   jax.ShapeDtypeStruct((B, CTX_PAD, POOL_WIDTH), F32),
                   jax.ShapeDtypeStruct((B, SUBLANES, LANES), F32),
                   jax.ShapeDtypeStruct((B, N_HEADS, HEAD_DIM, HEAD_DIM), F32),
                   jax.ShapeDtypeStruct((B, N_HEADS, HEAD_DIM), F32),
                   jax.ShapeDtypeStruct((D_MODEL, D_MODEL), BF16),
                   jax.ShapeDtypeStruct(wpool_shape, BF16)),
        scratch_shapes=[pltpu.VMEM((CTX_PAD + CHUNK, POOL_WIDTH), F32),
                        pltpu.VMEM((PROMPT_TILE, D_MODEL), BF16),
                        pltpu.VMEM((PROMPT_TILE, IN_MAIN), F32), pltpu.VMEM((PROMPT_TILE, IN_MAIN), F32),
                        pltpu.VMEM((PROMPT_TILE, GATE_W), F32), pltpu.VMEM((PROMPT_TILE, GATE_W), F32),
                        pltpu.VMEM((PROMPT_TILE, D_MODEL), BF16), pltpu.VMEM((PROMPT_TILE, D_MODEL), BF16),
                        pltpu.VMEM((N_HEADS, HEAD_DIM, AUG), F32), pltpu.VMEM((SUBLANES, LANES), F32),
                        pltpu.VMEM((N_META, POOL_WIDTH), F32), pltpu.VMEM((N_HEADS, HEAD_DIM, AUG), F32),
                        pltpu.VMEM((SUBLANES, LANES), F32)],
        compiler_params=pltpu.CompilerParams(dimension_semantics=("arbitrary",), vmem_limit_bytes=VMEM_LIMIT),
        name="prompt_mixer",
    )(x_prompt, x_prompt, meta_tokens, n1, win, wg, bg, w_pool[0], pscale, gain, w_out[0])

    xs_tiles = x_sample.reshape(n_seq, DEC_SEQ, D_MODEL // LANES, LANES).transpose(0, 2, 1, 3)
    x1_p, xs_tiles = lax.optimization_barrier((x1_p, xs_tiles))
    ctx_t = state_pool[0].transpose(1, 0, 2)
    n0 = state_n[0].reshape(n_seq, MLSTM_WIDTH)
    m0 = jnp.pad(state_m[0], ((0, 0), (0, LANES - N_HEADS)))
    state_block = pl.BlockSpec((SAMPLE_BLOCK, N_HEADS, HEAD_DIM, HEAD_DIM), lambda i: (i, 0, 0, 0))
    x1_s, pool_s, c_s, n_s, m_s = pl.pallas_call(
        _sample_kernel,
        grid=(n_seq // SAMPLE_BLOCK,),
        in_specs=[_full(xs_tiles.shape), _full((POOL_CTX, n_seq, POOL_WIDTH)), state_block,
                  _full((n_seq, MLSTM_WIDTH)), _full((n_seq, LANES))] + weight_specs + mixer_specs,
        out_specs=[_full((rows_s, D_MODEL)), _full((POOL_CTX, n_seq, POOL_WIDTH)), state_block,
                   _full((n_seq, MLSTM_WIDTH)), _full((n_seq, LANES))],
        out_shape=(jax.ShapeDtypeStruct((rows_s, D_MODEL), F32),
                   jax.ShapeDtypeStruct((POOL_CTX, n_seq, POOL_WIDTH), F32),
                   jax.ShapeDtypeStruct((n_seq, N_HEADS, HEAD_DIM, HEAD_DIM), F32),
                   jax.ShapeDtypeStruct((n_seq, MLSTM_WIDTH), F32),
                   jax.ShapeDtypeStruct((n_seq, LANES), F32)),
        scratch_shapes=[pltpu.VMEM((rows_s, D_MODEL), F32),
                        pltpu.VMEM((rows_s, IN_MAIN), F32),
                        pltpu.VMEM((rows_s, GATE_W), F32),
                        pltpu.VMEM((N_HEADS, rows_s, HEAD_DIM), F32),
                        pltpu.VMEM((N_HEADS, rows_s, HEAD_DIM), F32),
                        pltpu.VMEM((N_HEADS, rows_s, HEAD_DIM), F32),
                        pltpu.VMEM((N_HEADS, rows_s // LANES, HEAD_DIM, LANES), BF16),
                        pltpu.VMEM((N_HEADS, rows_s // LANES, LANES, HEAD_DIM), BF16),
                        pltpu.VMEM((n_seq, LANES), F32),
                        pltpu.VMEM((rows_s, D_MODEL), BF16)],
        compiler_params=pltpu.CompilerParams(dimension_semantics=("arbitrary",), vmem_limit_bytes=VMEM_LIMIT),
        name="sample_mixer",
    )(xs_tiles, ctx_t, state_C[0], n0, m0, n1, win, wg, bg, wpool, pscale, gain, wout)

    y_p, y_s = _ffn(x1_s, x1_p.reshape(B * S, D_MODEL), n2, nf, w_up[0], w_down[0], n_seq)
    y_p = y_p.reshape(B, S, D_MODEL)
    y_s = y_s.transpose(0, 2, 1, 3).reshape(n_seq, DEC_SEQ, D_MODEL)

    return (y_p, y_s,
            pool_p[:, CTX_PAD - POOL_CTX:, :][None],
            c_p[None],
            n_p[None],
            m_p[:, 0, :N_HEADS][None],
            pool_s.transpose(1, 0, 2)[None],
            c_s[None],
            n_s.reshape(n_seq, N_HEADS, HEAD_DIM)[None],
            m_s[:, :N_HEADS][None])
```

```python
import functools

import jax
import jax.numpy as jnp
from jax import lax
from jax.experimental import pallas as pl
from jax.experimental.pallas import tpu as pltpu

D_MODEL = 1024
N_META = 16
POOL_WIDTH = 512
POOL_WINDOWS = (2, 4, 8, 16)
POOL_GC = 128
POOL_CTX = 15
MLSTM_WIDTH = 512
N_HEADS = 4
HEAD_DIM = 128
D_FF = 4096
EPS = 1e-6
DEC_SEQ = 4

LANES = 128
SUBLANES = 8
COL_Q = POOL_WIDTH
COL_K = COL_Q + MLSTM_WIDTH
COL_V = COL_K + MLSTM_WIDTH
COL_O = COL_V + MLSTM_WIDTH
IN_MAIN = COL_O + MLSTM_WIDTH
GATE_W = 2 * LANES
AUG = 2 * HEAD_DIM
CTX_PAD = 16
CHUNK = 256
PROMPT_TILE = 512
IN_CHUNK = 512
MIX_PIECES = 1 + N_HEADS // 2 + N_HEADS
FFN_TILE = 512
FFN_SUB = 256
FFN_CHUNK = 1024
SAMPLE_BLOCK = 16
NEG_BIG = -1e30
VMEM_LIMIT = 56 * 1024 * 1024

F32 = jnp.float32
BF16 = jnp.bfloat16


def _rmsnorm(x, g):
    return x * lax.rsqrt(jnp.mean(x * x, axis=-1, keepdims=True) + EPS) * g


def _log_sigmoid(x):
    return jnp.minimum(x, 0.0) - jnp.log1p(jnp.exp(-jnp.abs(x)))


def _dot(a, b):
    return jnp.dot(a, b, preferred_element_type=F32)


def _dot_nt(a, b):
    return lax.dot_general(a, b, (((1,), (1,)), ((), ())), preferred_element_type=F32)


def _hcols(base, h):
    return slice(base + h * HEAD_DIM, base + (h + 1) * HEAD_DIM)


def _row_scan(x, op):
    n = x.shape[0]
    row = lax.broadcasted_iota(jnp.int32, x.shape, 0)
    sh = 1
    while sh < n:
        x = jnp.where(row >= sh, op(x, pltpu.roll(x, sh, axis=0)), x)
        sh *= 2
    return x


def _gate_cols(g, m0, n_valid):
    R = g.shape[0]
    b = _row_scan(_log_sigmoid(g[:, LANES:]), jnp.add)
    r = g[:, :LANES] - b
    big_m = jnp.maximum(m0, _row_scan(r, jnp.maximum))
    m = b + big_m
    last = n_valid - 1
    m_top = big_m[last:last + 1, :]
    ws = jnp.exp(r - m_top)
    if n_valid < R:
        ws = jnp.where(lax.broadcasted_iota(jnp.int32, ws.shape, 0) < n_valid, ws, 0.0)
    return dict(r=r, big_m=big_m, inter=jnp.exp(m0 - big_m), eneg=jnp.exp(-m), ws=ws,
                decay=jnp.exp(m0 - m_top), m_last=m[last:last + 1, :])


def _value_aug(v, h):
    lane = lax.broadcasted_iota(jnp.int32, (v.shape[0], HEAD_DIM), 1)
    return jnp.concatenate([v, jnp.where(lane == h, 1.0, 0.0)], axis=1).astype(BF16)


def _meta_prefix(x_ref, n1_ref, win_ref, wg_ref, bg_ref, wout_ref, wpool_ref,
                 u_ref, c_ref, m_ref, wout_bf_ref, wpool_bf_ref):
    wout_bf_ref[...] = wout_ref[...].astype(BF16)
    wpool_bf_ref[...] = wpool_ref[...].astype(BF16)
    x = jnp.concatenate([x_ref[...], jnp.zeros((LANES - N_META, D_MODEL), F32)], axis=0)
    xn = _rmsnorm(x, n1_ref[...]).astype(BF16)
    gc = _gate_cols(_dot(xn, wg_ref[...]) + bg_ref[...], jnp.zeros((1, LANES), F32), N_META)
    p = _dot_nt(xn, win_ref[0:IN_MAIN, :])
    u_ref[...] = p[0:N_META, 0:POOL_WIDTH]
    for h in range(N_HEADS):
        kw_t = (p[:, _hcols(COL_K, h)] * gc["ws"][:, h:h + 1]).T.astype(BF16)
        c_ref[h] = _dot(kw_t, _value_aug(p[:, _hcols(COL_V, h)], h))
    m_ref[...] = jnp.broadcast_to(gc["m_last"], (SUBLANES, LANES))


def _prompt_kernel(nt, xa_ref, xc_ref, meta_ref, n1_ref, win_ref, wg_ref, bg_ref,
                   wpool32_ref, pscale_ref, gain_ref, wout32_ref,
                   x1_ref, pool_ref, mfin_ref, cfin_ref, nfin_ref, wout_ref, wpool_ref,
                   uext_ref, xn_ref, p0_ref, p1_ref, g0_ref, g1_ref, mix0_ref, mix1_ref, c_ref, m_ref,
                   umeta_ref, cmeta_ref, mmeta_ref):
    s = pl.program_id(0)
    n_tiles = pl.num_programs(0) - 2

    @pl.when(s == 0)
    def _():
        _meta_prefix(meta_ref, n1_ref, win_ref, wg_ref, bg_ref, wout32_ref, wpool32_ref,
                     umeta_ref, cmeta_ref, mmeta_ref, wout_ref, wpool_ref)

    @pl.when((jnp.maximum(s - 1, 0) % nt == 0) & (s <= n_tiles))
    def _():
        c_ref[...] = cmeta_ref[...]
        m_ref[...] = mmeta_ref[...]
        uext_ref[0:CTX_PAD, :] = umeta_ref[...]

    stage = functools.partial(_prompt_step, xa_ref, xc_ref, n1_ref, win_ref, wg_ref, bg_ref, wpool_ref,
                              pscale_ref, gain_ref, wout_ref, x1_ref, c_ref, m_ref, uext_ref, xn_ref)
    even = (p0_ref, p1_ref, g0_ref, g1_ref, mix0_ref, mix1_ref)
    odd = (p1_ref, p0_ref, g1_ref, g0_ref, mix1_ref, mix0_ref)
    steady = (s >= 2) & (s < n_tiles)
    pl.when(s == 0)(lambda: stage(*even, project=True, mix=False, emit=False))
    pl.when(s == 1)(lambda: stage(*odd, project=True, mix=True, emit=False))
    pl.when(steady & (s % 2 == 0))(lambda: stage(*even, project=True, mix=True, emit=True))
    pl.when(steady & (s % 2 == 1))(lambda: stage(*odd, project=True, mix=True, emit=True))
    pl.when(s == n_tiles)(lambda: stage(*even, project=False, mix=True, emit=True))
    pl.when(s == n_tiles + 1)(lambda: stage(*odd, project=False, mix=False, emit=True))

    @pl.when((s >= 1) & (s <= n_tiles) & ((s - 1) % nt == nt - 1))
    def _():
        for h in range(N_HEADS):
            cfin_ref[0, h] = c_ref[h, :, 0:HEAD_DIM]
            nfin_ref[0, h:h + 1, :] = c_ref[h, :, HEAD_DIM:].T[h:h + 1, :]
        mfin_ref[0] = m_ref[...]
        pool_ref[0] = uext_ref[0:CTX_PAD, :]


def _prompt_step(xa_ref, xc_ref, n1_ref, win_ref, wg_ref, bg_ref, wpool_ref, pscale_ref, gain_ref, wout_ref,
                 x1_ref, c_ref, m_ref, uext_ref, xn_ref,
                 p_new, pb, g_new, g_old, mx, mix_old, *, project, mix, emit):
    dense = []
    if emit:
        def out_proj():
            x1_ref[0] = xc_ref[0] + _dot(mix_old[...], wout_ref[...])
        dense.append(out_proj)
    if project:
        def in_norm():
            xn_ref[...] = _rmsnorm(xa_ref[0], n1_ref[...]).astype(BF16)
            g_new[...] = _dot(xn_ref[...], wg_ref[...]) + bg_ref[...]

        def in_chunk(c):
            cs = slice(c * IN_CHUNK, (c + 1) * IN_CHUNK)
            p_new[:, cs] = _dot_nt(xn_ref[...], win_ref[cs, :])
        dense.append(in_norm)
        dense += [functools.partial(in_chunk, c) for c in range(IN_MAIN // IN_CHUNK)]

    pieces = []
    if mix:
        for ci in range(PROMPT_TILE // CHUNK):
            pieces.append(_mix_chunk(pb, g_old, mx, slice(ci * CHUNK, (ci + 1) * CHUNK),
                                     wpool_ref, pscale_ref, gain_ref, c_ref, m_ref, uext_ref))
    n_mix = MIX_PIECES * len(pieces)
    done = 0
    for k, piece in enumerate(p for chunk in pieces for p in chunk):
        while done < len(dense) and done * n_mix < k * len(dense):
            dense[done]()
            done += 1
    for job in dense[done:]:
        job()


def _mix_chunk(pb, g_old, mx, rows, wpool_ref, pscale_ref, gain_ref, c_ref, m_ref, uext_ref):
    R = CHUNK
    gc = _gate_cols(g_old[rows, :], m_ref[0:1, :], R)
    r_t = gc["r"].T

    uext_ref[CTX_PAD:CTX_PAD + R, :] = pb[rows, 0:POOL_WIDTH]
    for gi, w in enumerate(POOL_WINDOWS):
        cs = slice(gi * POOL_GC, (gi + 1) * POOL_GC)
        acc = uext_ref[:, cs]
        sh = 1
        while sh < w:
            acc = acc + pltpu.roll(acc, sh, axis=0)
            sh *= 2
        z = acc[CTX_PAD:, :] * (1.0 / w) - uext_ref[CTX_PAD:CTX_PAD + R, cs]
        mx[rows, cs] = (_dot(z.astype(BF16), wpool_ref[gi]) * pscale_ref[:, cs]).astype(BF16)
    tail = uext_ref[R:R + CTX_PAD, :]
    uext_ref[0:CTX_PAD, :] = tail
    m_ref[...] = jnp.broadcast_to(gc["m_last"], (SUBLANES, LANES))
    yield

    causal = (lax.broadcasted_iota(jnp.int32, (R, R), 0) >= lax.broadcasted_iota(jnp.int32, (R, R), 1))
    front = []
    for h in range(N_HEADS):
        q = (pb[rows, _hcols(COL_Q, h)] * (HEAD_DIM ** -0.5)).astype(BF16)
        k = pb[rows, _hcols(COL_K, h)]
        vaug = _value_aug(pb[rows, _hcols(COL_V, h)], h)
        sc = _dot_nt(q, k.astype(BF16))
        c0 = c_ref[h]
        qc = _dot(q, c0.astype(BF16))
        kw_t = (k * gc["ws"][:, h:h + 1]).T.astype(BF16)
        c_ref[h] = gc["decay"][:, h:h + 1] * c0 + _dot(kw_t, vaug)
        front.append((sc, qc, vaug))
        if h % 2 == 1:
            yield
    for h in range(N_HEADS):
        sc, qc, vaug = front[h]
        e = jnp.exp(jnp.where(causal, r_t[h:h + 1, :] - gc["big_m"][:, h:h + 1], NEG_BIG))
        num = gc["inter"][:, h:h + 1] * qc + _dot((sc * e).astype(BF16), vaug)
        inv = 1.0 / jnp.maximum(jnp.abs(num[:, HEAD_DIM:]), gc["eneg"])
        hh = num[:, 0:HEAD_DIM] * inv[:, h:h + 1]
        hh = hh * lax.rsqrt(jnp.mean(hh * hh, axis=-1, keepdims=True) + EPS) * gain_ref[:, _hcols(0, h)]
        mx[rows, _hcols(POOL_WIDTH, h)] = (hh * jax.nn.sigmoid(pb[rows, _hcols(COL_O, h)])).astype(BF16)
        yield


def _sample_gates(g_ref, m0):
    n = m0.shape[0]
    b = jnp.zeros((n, LANES), F32)
    big_m = m0
    r, big_ms, ms = [], [], []
    for t in range(DEC_SEQ):
        g = g_ref[t * n:(t + 1) * n, :]
        b = b + _log_sigmoid(g[:, LANES:])
        r.append(g[:, :LANES] - b)
        big_m = jnp.maximum(big_m, r[-1])
        big_ms.append(big_m)
        ms.append(b + big_m)
    return dict(r=r, big_m=big_ms, m=ms,
                ws=[jnp.exp(r[t] - big_m) for t in range(DEC_SEQ)],
                decay=jnp.exp(m0 - big_m))


def _sample_kernel(x4_ref, ctx_ref, cin_ref, n0_ref, m0_ref, n1_ref, win_ref, wg_ref, bg_ref, wpool_ref,
                   pscale_ref, gain_ref, wout_ref,
                   x1_ref, pool_ref, cout_ref, nout_ref, mout_ref,
                   xs_ref, p_ref, g_ref, tmp_ref, qsm_ref, qcsm_ref, kwt_ref, vsm_ref, dec_ref, mix_ref):
    i = pl.program_id(0)
    n_seq = m0_ref.shape[0]
    n_grp = DEC_SEQ * n_seq // LANES
    scale = HEAD_DIM ** -0.5

    def seq_major(dst_ref, slab_of_t):
        for t in range(DEC_SEQ):
            slab = slab_of_t(t)
            for h in range(N_HEADS):
                dst_ref[h, pl.ds(t, n_seq, stride=DEC_SEQ), :] = slab[:, _hcols(0, h)]

    @pl.when(i == 0)
    def _():
        for t in range(DEC_SEQ):
            for lt in range(D_MODEL // LANES):
                xs_ref[t * n_seq:(t + 1) * n_seq, lt * LANES:(lt + 1) * LANES] = x4_ref[:, lt, t, :]
        xn = _rmsnorm(xs_ref[...], n1_ref[...]).astype(BF16)
        g_ref[...] = _dot(xn, wg_ref[...]) + bg_ref[...]
        p_ref[...] = _dot_nt(xn, win_ref[...])

        def urow(e, cs):
            if e < POOL_CTX:
                return ctx_ref[e, :, cs]
            return p_ref[(e - POOL_CTX) * n_seq:(e - POOL_CTX + 1) * n_seq, cs]

        for gi, w in enumerate(POOL_WINDOWS):
            cs = slice(gi * POOL_GC, (gi + 1) * POOL_GC)
            for t in range(DEC_SEQ):
                e = POOL_CTX + t
                wsum = urow(e, cs)
                for j in range(1, w):
                    wsum = wsum + urow(e - j, cs)
                z = wsum * (1.0 / w) - urow(e, cs)
                mix_ref[t * n_seq:(t + 1) * n_seq, cs] = (
                    _dot(z.astype(BF16), wpool_ref[gi]) * pscale_ref[:, cs]).astype(BF16)
        for e in range(DEC_SEQ, POOL_CTX + DEC_SEQ):
            pool_ref[e - DEC_SEQ] = urow(e, slice(0, POOL_WIDTH))

        gs = _sample_gates(g_ref, m0_ref[...])
        dec_ref[...] = gs["decay"]
        mout_ref[...] = gs["m"][DEC_SEQ - 1]
        def kw_slab(t):
            rows = slice(t * n_seq, (t + 1) * n_seq)
            return jnp.concatenate([p_ref[rows, _hcols(COL_K, h)] * gs["ws"][t][:, h:h + 1]
                                    for h in range(N_HEADS)], axis=1)

        kw = [kw_slab(t) for t in range(DEC_SEQ)]
        nout_ref[...] = (jnp.concatenate([jnp.broadcast_to(gs["decay"][:, h:h + 1], (n_seq, HEAD_DIM))
                                          for h in range(N_HEADS)], axis=1) * n0_ref[...]
                         + kw[0] + kw[1] + kw[2] + kw[3])
        seq_major(tmp_ref, lambda t: kw[t])
        for h in range(N_HEADS):
            for gi in range(n_grp):
                kwt_ref[h, gi] = tmp_ref[h, gi * LANES:(gi + 1) * LANES, :].T.astype(BF16)
        seq_major(tmp_ref, lambda t: p_ref[t * n_seq:(t + 1) * n_seq, COL_V:COL_O])
        for h in range(N_HEADS):
            for gi in range(n_grp):
                vsm_ref[h, gi] = tmp_ref[h, gi * LANES:(gi + 1) * LANES, :].astype(BF16)
        seq_major(qsm_ref, lambda t: p_ref[t * n_seq:(t + 1) * n_seq, COL_Q:COL_K] * scale)

    lane_sl = lax.broadcasted_iota(jnp.int32, (HEAD_DIM, LANES), 1) // DEC_SEQ
    first = lax.broadcasted_iota(jnp.int32, (SUBLANES, HEAD_DIM), 0) < DEC_SEQ
    grp = (i * SAMPLE_BLOCK) // (LANES // DEC_SEQ)
    for jp in range(SAMPLE_BLOCK // 2):
        rows = pl.ds(pl.multiple_of(i * (SAMPLE_BLOCK * DEC_SEQ) + jp * SUBLANES, SUBLANES), SUBLANES)
        for h in range(N_HEADS):
            hs = _hcols(0, h)
            q8 = qsm_ref[h, rows, :].astype(BF16)
            readout = []
            for a in range(2):
                j = 2 * jp + a
                seq = i * SAMPLE_BLOCK + j
                c0 = cin_ref[j, h]
                readout.append(_dot(q8, c0.astype(BF16)))
                lhs = jnp.where(lane_sl == seq % (LANES // DEC_SEQ), kwt_ref[h, grp], jnp.zeros((), BF16))
                cout_ref[j, h] = dec_ref[pl.ds(seq, 1), :][:, h:h + 1] * c0 + _dot(lhs, vsm_ref[h, grp])
            qcsm_ref[h, rows, :] = jnp.where(first, readout[0], readout[1])

    @pl.when(i == pl.num_programs(0) - 1)
    def _():
        gs = _sample_gates(g_ref, m0_ref[...])
        for h in range(N_HEADS):
            hs = _hcols(0, h)
            n0 = n0_ref[:, hs]
            qs = [p_ref[t * n_seq:(t + 1) * n_seq, _hcols(COL_Q, h)] * scale for t in range(DEC_SEQ)]
            ks = [p_ref[t * n_seq:(t + 1) * n_seq, _hcols(COL_K, h)] for t in range(DEC_SEQ)]
            for t in range(DEC_SEQ):
                rows = slice(t * n_seq, (t + 1) * n_seq)
                inter = jnp.exp(m0_ref[...] - gs["big_m"][t])[:, h:h + 1]
                num = inter * qcsm_ref[h, pl.ds(t, n_seq, stride=DEC_SEQ), :]
                den = inter * jnp.sum(qs[t] * n0, axis=-1, keepdims=True)
                for s in range(t + 1):
                    pts = (jnp.sum(qs[t] * ks[s], axis=-1, keepdims=True)
                           * jnp.exp(gs["r"][s] - gs["big_m"][t])[:, h:h + 1])
                    num = num + pts * p_ref[s * n_seq:(s + 1) * n_seq, _hcols(COL_V, h)]
                    den = den + pts
                hh = num / jnp.maximum(jnp.abs(den), jnp.exp(-gs["m"][t])[:, h:h + 1])
                hh = hh * lax.rsqrt(jnp.mean(hh * hh, axis=-1, keepdims=True) + EPS) * gain_ref[:, hs]
                mix_ref[rows, _hcols(POOL_WIDTH, h)] = (
                    hh * jax.nn.sigmoid(p_ref[rows, _hcols(COL_O, h)])).astype(BF16)
        x1_ref[...] = xs_ref[...] + _dot(mix_ref[...], wout_ref[...])


def _ffn_kernel(xs_ref, xp_ref, n2_ref, nf_ref, wup_hbm, wdown_hbm, yp_ref, ys_ref,
                wup_ref, wdown_ref, stage_ref, sem):
    i = pl.program_id(0)
    last = pl.num_programs(0) - 1

    def fetch(c, up):
        cols = pl.ds(c * FFN_CHUNK, FFN_CHUNK)
        src = wup_hbm.at[:, cols] if up else wdown_hbm.at[cols, :]
        slot = 0 if up else 1
        return pltpu.make_async_copy(src, stage_ref.at[slot], sem.at[slot])

    def store_rows(rows, y):
        yp_ref[rows, :] = y

    def store_tiles(rows, y):
        n_seq = ys_ref.shape[0]
        for k in range(FFN_SUB // n_seq):
            t = rows.start // n_seq + k
            for lt in range(D_MODEL // LANES):
                ys_ref[:, lt, t, :] = y[k * n_seq:(k + 1) * n_seq, lt * LANES:(lt + 1) * LANES]

    def mlp(x_ref, store, load_weights):
        subs = [slice(j * FFN_SUB, (j + 1) * FFN_SUB) for j in range(x_ref.shape[0] // FFN_SUB)]
        n_chunk = D_FF // FFN_CHUNK
        if load_weights:
            fetch(0, True).start()
            fetch(0, False).start()
        xn = [_rmsnorm(x_ref[rows, :], n2_ref[...]).astype(BF16) for rows in subs]
        acc = [x_ref[rows, :] for rows in subs]
        for c in range(n_chunk):
            if load_weights:
                fetch(c, True).wait()
                wup_ref[c] = stage_ref[0].astype(BF16)
                if c + 1 < n_chunk:
                    fetch(c + 1, True).start()
                fetch(c, False).wait()
                wdown_ref[c] = stage_ref[1].astype(BF16)
                if c + 1 < n_chunk:
                    fetch(c + 1, False).start()
            for j in range(len(subs)):
                a = jnp.square(jnp.maximum(_dot(xn[j], wup_ref[c]), 0.0)).astype(BF16)
                acc[j] = acc[j] + _dot(a, wdown_ref[c])
        for j, rows in enumerate(subs):
            store(rows, _rmsnorm(acc[j], nf_ref[...]))

    pl.when(i == 0)(lambda: mlp(xp_ref, store_rows, True))
    pl.when((i > 0) & (i < last))(lambda: mlp(xp_ref, store_rows, False))
    pl.when(i == last)(lambda: mlp(xs_ref, store_tiles, False))


def _full(shape):
    return pl.BlockSpec(shape, lambda *_: (0,) * len(shape))


def _ffn(xs, xp, n2, nf, w_up, w_down, n_seq):
    assert D_MODEL == FFN_CHUNK
    assert FFN_SUB % n_seq == 0
    n_prompt = xp.shape[0] // FFN_TILE
    n_chunk = D_FF // FFN_CHUNK
    ys_shape = (n_seq, D_MODEL // LANES, xs.shape[0] // n_seq, LANES)
    prompt_tile = pl.BlockSpec((FFN_TILE, D_MODEL), lambda i: (jnp.minimum(i, n_prompt - 1), 0))
    return pl.pallas_call(
        _ffn_kernel,
        grid=(n_prompt + 1,),
        in_specs=[_full(xs.shape), prompt_tile, _full((1, D_MODEL)), _full((1, D_MODEL)),
                  pl.BlockSpec(memory_space=pl.ANY), pl.BlockSpec(memory_space=pl.ANY)],
        out_specs=[prompt_tile, _full(ys_shape)],
        out_shape=(jax.ShapeDtypeStruct(xp.shape, F32), jax.ShapeDtypeStruct(ys_shape, F32)),
        scratch_shapes=[pltpu.VMEM((n_chunk, D_MODEL, FFN_CHUNK), BF16),
                        pltpu.VMEM((n_chunk, FFN_CHUNK, D_MODEL), BF16),
                        pltpu.VMEM((2, D_MODEL, FFN_CHUNK), F32),
                        pltpu.SemaphoreType.DMA((2,))],
        compiler_params=pltpu.CompilerParams(dimension_semantics=("arbitrary",), vmem_limit_bytes=VMEM_LIMIT),
        name="ffn",
    )(xs, xp, n2, nf, w_up, w_down)


def kernel(x_prompt, x_sample, state_pool, state_C, state_n, state_m, meta_tokens, norm1, w_in, b_gate,
           w_pool, pool_scale, head_gain, w_out, norm2, w_up, w_down, norm_f):
    B, S, _ = x_prompt.shape
    n_seq = x_sample.shape[0]
    rows_s = DEC_SEQ * n_seq
    assert w_in.shape[0] == 1 and S % PROMPT_TILE == 0 and n_seq % SAMPLE_BLOCK == 0
    assert (B * S) % FFN_TILE == 0 and rows_s % FFN_SUB == 0

    win = w_in[0].T.astype(BF16)
    gpad = ((0, 0), (0, LANES - N_HEADS))
    wg = jnp.concatenate([jnp.pad(w_in[0][:, IN_MAIN:IN_MAIN + N_HEADS], gpad),
                          jnp.pad(w_in[0][:, IN_MAIN + N_HEADS:], gpad)], axis=1).astype(BF16)
    bg = jnp.concatenate([jnp.pad(b_gate[0][:N_HEADS], (0, LANES - N_HEADS)),
                          jnp.pad(b_gate[0][N_HEADS:], (0, LANES - N_HEADS))]).reshape(1, GATE_W)
    n1 = norm1[0].reshape(1, D_MODEL)
    n2 = norm2[0].reshape(1, D_MODEL)
    nf = norm_f.reshape(1, D_MODEL)
    pscale = pool_scale[0].reshape(1, POOL_WIDTH)
    gain = head_gain[0].reshape(1, MLSTM_WIDTH)

    weight_specs = [_full((1, D_MODEL)), _full((IN_MAIN, D_MODEL)), _full((D_MODEL, GATE_W)), _full((1, GATE_W))]
    mixer_specs = [_full((len(POOL_WINDOWS), POOL_GC, POOL_GC)), _full((1, POOL_WIDTH)),
                   _full((1, MLSTM_WIDTH)), _full((D_MODEL, D_MODEL))]

    nt = S // PROMPT_TILE
    n_tiles = B * nt
    assert n_tiles % 2 == 0 and n_tiles >= 2

    def tile_at(lag):
        def index(s):
            j = jnp.clip(s - lag, 0, n_tiles - 1)
            return (j // nt, j % nt, 0)
        return index

    def mixed_seq(ndim):
        return lambda s: (jnp.clip(s - 1, 0, n_tiles - 1) // nt,) + (0,) * (ndim - 1)

    wpool_shape = (len(POOL_WINDOWS), POOL_GC, POOL_GC)
    x1_p, pool_p, m_p, c_p, n_p, wout, wpool = pl.pallas_call(
        functools.partial(_prompt_kernel, nt),
        grid=(n_tiles + 2,),
        in_specs=[pl.BlockSpec((1, PROMPT_TILE, D_MODEL), tile_at(0)),
                  pl.BlockSpec((1, PROMPT_TILE, D_MODEL), tile_at(2)),
                  _full((N_META, D_MODEL))]
        + weight_specs + mixer_specs,
        out_specs=[pl.BlockSpec((1, PROMPT_TILE, D_MODEL), tile_at(2)),
                   pl.BlockSpec((1, CTX_PAD, POOL_WIDTH), mixed_seq(3)),
                   pl.BlockSpec((1, SUBLANES, LANES), mixed_seq(3)),
                   pl.BlockSpec((1, N_HEADS, HEAD_DIM, HEAD_DIM), mixed_seq(4)),
                   pl.BlockSpec((1, N_HEADS, HEAD_DIM), mixed_seq(3)),
                   _full((D_MODEL, D_MODEL)), _full(wpool_shape)],
        out_shape=(jax.ShapeDtypeStruct((B, S, D_MODEL), F32),
                   jax.ShapeDtypeStruct((B, CTX_PAD, POOL_WIDTH), F32),
                   jax.ShapeDtypeStruct((B, SUBLANES, LANES), F32),
                   jax.ShapeDtypeStruct((B, N_HEADS, HEAD_DIM, HEAD_DIM), F32),
                   jax.ShapeDtypeStruct((B, N_HEADS, HEAD_DIM), F32),
                   jax.ShapeDtypeStruct((D_MODEL, D_MODEL), BF16),
                   jax.ShapeDtypeStruct(wpool_shape, BF16)),
        scratch_shapes=[pltpu.VMEM((CTX_PAD + CHUNK, POOL_WIDTH), F32),
                        pltpu.VMEM((PROMPT_TILE, D_MODEL), BF16),
                        pltpu.VMEM((PROMPT_TILE, IN_MAIN), F32), pltpu.VMEM((PROMPT_TILE, IN_MAIN), F32),
                        pltpu.VMEM((PROMPT_TILE, GATE_W), F32), pltpu.VMEM((PROMPT_TILE, GATE_W), F32),
                        pltpu.VMEM((PROMPT_TILE, D_MODEL), BF16), pltpu.VMEM((PROMPT_TILE, D_MODEL), BF16),
                        pltpu.VMEM((N_HEADS, HEAD_DIM, AUG), F32), pltpu.VMEM((SUBLANES, LANES), F32),
                        pltpu.VMEM((N_META, POOL_WIDTH), F32), pltpu.VMEM((N_HEADS, HEAD_DIM, AUG), F32),
                        pltpu.VMEM((SUBLANES, LANES), F32)],
        compiler_params=pltpu.CompilerParams(dimension_semantics=("arbitrary",), vmem_limit_bytes=VMEM_LIMIT),
        name="prompt_mixer",
    )(x_prompt, x_prompt, meta_tokens, n1, win, wg, bg, w_pool[0], pscale, gain, w_out[0])

    xs_tiles = x_sample.reshape(n_seq, DEC_SEQ, D_MODEL // LANES, LANES).transpose(0, 2, 1, 3)
    x1_p, xs_tiles = lax.optimization_barrier((x1_p, xs_tiles))
    ctx_t = state_pool[0].transpose(1, 0, 2)
    n0 = state_n[0].reshape(n_seq, MLSTM_WIDTH)
    m0 = jnp.pad(state_m[0], ((0, 0), (0, LANES - N_HEADS)))
    state_block = pl.BlockSpec((SAMPLE_BLOCK, N_HEADS, HEAD_DIM, HEAD_DIM), lambda i: (i, 0, 0, 0))
    x1_s, pool_s, c_s, n_s, m_s = pl.pallas_call(
        _sample_kernel,
        grid=(n_seq // SAMPLE_BLOCK,),
        in_specs=[_full(xs_tiles.shape), _full((POOL_CTX, n_seq, POOL_WIDTH)), state_block,
                  _full((n_seq, MLSTM_WIDTH)), _full((n_seq, LANES))] + weight_specs + mixer_specs,
        out_specs=[_full((rows_s, D_MODEL)), _full((POOL_CTX, n_seq, POOL_WIDTH)), state_block,
                   _full((n_seq, MLSTM_WIDTH)), _full((n_seq, LANES))],
        out_shape=(jax.ShapeDtypeStruct((rows_s, D_MODEL), F32),
                   jax.ShapeDtypeStruct((POOL_CTX, n_seq, POOL_WIDTH), F32),
                   jax.ShapeDtypeStruct((n_seq, N_HEADS, HEAD_DIM, HEAD_DIM), F32),
                   jax.ShapeDtypeStruct((n_seq, MLSTM_WIDTH), F32),
                   jax.ShapeDtypeStruct((n_seq, LANES), F32)),
        scratch_shapes=[pltpu.VMEM((rows_s, D_MODEL), F32),
                        pltpu.VMEM((rows_s, IN_MAIN), F32),
                        pltpu.VMEM((rows_s, GATE_W), F32),
                        pltpu.VMEM((N_HEADS, rows_s, HEAD_DIM), F32),
                        pltpu.VMEM((N_HEADS, rows_s, HEAD_DIM), F32),
                        pltpu.VMEM((N_HEADS, rows_s, HEAD_DIM), F32),
                        pltpu.VMEM((N_HEADS, rows_s // LANES, HEAD_DIM, LANES), BF16),
                        pltpu.VMEM((N_HEADS, rows_s // LANES, LANES, HEAD_DIM), BF16),
                        pltpu.VMEM((n_seq, LANES), F32),
                        pltpu.VMEM((rows_s, D_MODEL), BF16)],
        compiler_params=pltpu.CompilerParams(dimension_semantics=("arbitrary",), vmem_limit_bytes=VMEM_LIMIT),
        name="sample_mixer",
    )(xs_tiles, ctx_t, state_C[0], n0, m0, n1, win, wg, bg, wpool, pscale, gain, wout)

    y_p, y_s = _ffn(x1_s, x1_p.reshape(B * S, D_MODEL), n2, nf, w_up[0], w_down[0], n_seq)
    y_p = y_p.reshape(B, S, D_MODEL)
    y_s = y_s.transpose(0, 2, 1, 3).reshape(n_seq, DEC_SEQ, D_MODEL)

    return (y_p, y_s,
            pool_p[:, CTX_PAD - POOL_CTX:, :][None],
            c_p[None],
            n_p[None],
            m_p[:, 0, :N_HEADS][None],
            pool_s.transpose(1, 0, 2)[None],
            c_s[None],
            n_s.reshape(n_seq, N_HEADS, HEAD_DIM)[None],
            m_s[:, :N_HEADS][None])
```

```python
import functools

import jax
import jax.numpy as jnp
from jax import lax
from jax.experimental import pallas as pl
from jax.experimental.pallas import tpu as pltpu

D_MODEL = 1024
N_META = 16
POOL_WIDTH = 512
POOL_WINDOWS = (2, 4, 8, 16)
POOL_GC = 128
POOL_CTX = 15
MLSTM_WIDTH = 512
N_HEADS = 4
HEAD_DIM = 128
D_FF = 4096
EPS = 1e-6
DEC_SEQ = 4

LANES = 128
SUBLANES = 8
COL_Q = POOL_WIDTH
COL_K = COL_Q + MLSTM_WIDTH
COL_V = COL_K + MLSTM_WIDTH
COL_O = COL_V + MLSTM_WIDTH
IN_MAIN = COL_O + MLSTM_WIDTH
GATE_W = 2 * LANES
AUG = 2 * HEAD_DIM
CTX_PAD = 16
CHUNK = 256
PROMPT_TILE = 512
IN_CHUNK = 512
MIX_PIECES = 1 + N_HEADS // 2 + N_HEADS
FFN_TILE = 512
FFN_SUB = 256
FFN_CHUNK = 1024
SAMPLE_BLOCK = 8
STATE_BUFFERS = 4
NEG_BIG = -1e30
VMEM_LIMIT = 56 * 1024 * 1024

F32 = jnp.float32
BF16 = jnp.bfloat16


def _rmsnorm(x, g):
    return x * lax.rsqrt(jnp.mean(x * x, axis=-1, keepdims=True) + EPS) * g


def _log_sigmoid(x):
    return jnp.minimum(x, 0.0) - jnp.log1p(jnp.exp(-jnp.abs(x)))


def _dot(a, b):
    return jnp.dot(a, b, preferred_element_type=F32)


def _dot_nt(a, b):
    return lax.dot_general(a, b, (((1,), (1,)), ((), ())), preferred_element_type=F32)


def _hcols(base, h):
    return slice(base + h * HEAD_DIM, base + (h + 1) * HEAD_DIM)


def _row_scan(x, op):
    n = x.shape[0]
    row = lax.broadcasted_iota(jnp.int32, x.shape, 0)
    sh = 1
    while sh < n:
        x = jnp.where(row >= sh, op(x, pltpu.roll(x, sh, axis=0)), x)
        sh *= 2
    return x


def _gate_cols(g, m0, n_valid):
    R = g.shape[0]
    b = _row_scan(_log_sigmoid(g[:, LANES:]), jnp.add)
    r = g[:, :LANES] - b
    big_m = jnp.maximum(m0, _row_scan(r, jnp.maximum))
    m = b + big_m
    last = n_valid - 1
    m_top = big_m[last:last + 1, :]
    ws = jnp.exp(r - m_top)
    if n_valid < R:
        ws = jnp.where(lax.broadcasted_iota(jnp.int32, ws.shape, 0) < n_valid, ws, 0.0)
    return dict(r=r, big_m=big_m, inter=jnp.exp(m0 - big_m), eneg=jnp.exp(-m), ws=ws,
                decay=jnp.exp(m0 - m_top), m_last=m[last:last + 1, :])


def _value_aug(v, h):
    lane = lax.broadcasted_iota(jnp.int32, (v.shape[0], HEAD_DIM), 1)
    return jnp.concatenate([v, jnp.where(lane == h, 1.0, 0.0)], axis=1).astype(BF16)


def _meta_prefix(x_ref, n1_ref, win_ref, wg_ref, bg_ref, wout_ref, wpool_ref,
                 u_ref, c_ref, m_ref, wout_bf_ref, wpool_bf_ref):
    wout_bf_ref[...] = wout_ref[...].astype(BF16)
    wpool_bf_ref[...] = wpool_ref[...].astype(BF16)
    x = jnp.concatenate([x_ref[...], jnp.zeros((LANES - N_META, D_MODEL), F32)], axis=0)
    xn = _rmsnorm(x, n1_ref[...]).astype(BF16)
    gc = _gate_cols(_dot(xn, wg_ref[...]) + bg_ref[...], jnp.zeros((1, LANES), F32), N_META)
    p = _dot(xn, win_ref[:, 0:IN_MAIN])
    u_ref[...] = p[0:N_META, 0:POOL_WIDTH]
    for h in range(N_HEADS):
        kw_t = (p[:, _hcols(COL_K, h)] * gc["ws"][:, h:h + 1]).T.astype(BF16)
        c_ref[h] = _dot(kw_t, _value_aug(p[:, _hcols(COL_V, h)], h))
    m_ref[...] = jnp.broadcast_to(gc["m_last"], (SUBLANES, LANES))


def _prompt_kernel(nt, xa_ref, xc_ref, meta_ref, n1_ref, win_ref, wg_ref, bg_ref,
                   wpool32_ref, pscale_ref, gain_ref, wout32_ref,
                   x1_ref, pool_ref, mfin_ref, cfin_ref, nfin_ref, wout_ref, wpool_ref,
                   uext_ref, xn_ref, p0_ref, p1_ref, g0_ref, g1_ref, mix0_ref, mix1_ref, c_ref, m_ref,
                   umeta_ref, cmeta_ref, mmeta_ref):
    s = pl.program_id(0)
    n_tiles = pl.num_programs(0) - 2

    @pl.when(s == 0)
    def _():
        _meta_prefix(meta_ref, n1_ref, win_ref, wg_ref, bg_ref, wout32_ref, wpool32_ref,
                     umeta_ref, cmeta_ref, mmeta_ref, wout_ref, wpool_ref)

    @pl.when((jnp.maximum(s - 1, 0) % nt == 0) & (s <= n_tiles))
    def _():
        c_ref[...] = cmeta_ref[...]
        m_ref[...] = mmeta_ref[...]
        uext_ref[0:CTX_PAD, :] = umeta_ref[...]

    stage = functools.partial(_prompt_step, xa_ref, xc_ref, n1_ref, win_ref, wg_ref, bg_ref, wpool_ref,
                              pscale_ref, gain_ref, wout_ref, x1_ref, c_ref, m_ref, uext_ref, xn_ref)
    even = (p0_ref, p1_ref, g0_ref, g1_ref, mix0_ref, mix1_ref)
    odd = (p1_ref, p0_ref, g1_ref, g0_ref, mix1_ref, mix0_ref)
    steady = (s >= 2) & (s < n_tiles)
    pl.when(s == 0)(lambda: stage(*even, project=True, mix=False, emit=False))
    pl.when(s == 1)(lambda: stage(*odd, project=True, mix=True, emit=False))
    pl.when(steady & (s % 2 == 0))(lambda: stage(*even, project=True, mix=True, emit=True))
    pl.when(steady & (s % 2 == 1))(lambda: stage(*odd, project=True, mix=True, emit=True))
    pl.when(s == n_tiles)(lambda: stage(*even, project=False, mix=True, emit=True))
    pl.when(s == n_tiles + 1)(lambda: stage(*odd, project=False, mix=False, emit=True))

    @pl.when((s >= 1) & (s <= n_tiles) & ((s - 1) % nt == nt - 1))
    def _():
        for h in range(N_HEADS):
            cfin_ref[0, h] = c_ref[h, :, 0:HEAD_DIM]
            nfin_ref[0, h:h + 1, :] = c_ref[h, :, HEAD_DIM:].T[h:h + 1, :]
        mfin_ref[0] = m_ref[...]
        pool_ref[0] = uext_ref[0:CTX_PAD, :]


def _prompt_step(xa_ref, xc_ref, n1_ref, win_ref, wg_ref, bg_ref, wpool_ref, pscale_ref, gain_ref, wout_ref,
                 x1_ref, c_ref, m_ref, uext_ref, xn_ref,
                 p_new, pb, g_new, g_old, mx, mix_old, *, project, mix, emit):
    dense = []
    if emit:
        def out_proj():
            x1_ref[0] = xc_ref[0] + _dot(mix_old[...], wout_ref[...])
        dense.append(out_proj)
    if project:
        def in_norm():
            xn_ref[...] = _rmsnorm(xa_ref[0], n1_ref[...]).astype(BF16)
            g_new[...] = _dot(xn_ref[...], wg_ref[...]) + bg_ref[...]

        def in_chunk(c):
            cs = slice(c * IN_CHUNK, (c + 1) * IN_CHUNK)
            p_new[:, cs] = _dot(xn_ref[...], win_ref[:, cs])
        dense.append(in_norm)
        dense += [functools.partial(in_chunk, c) for c in range(IN_MAIN // IN_CHUNK)]

    pieces = []
    if mix:
        for ci in range(PROMPT_TILE // CHUNK):
            pieces.append(_mix_chunk(pb, g_old, mx, slice(ci * CHUNK, (ci + 1) * CHUNK),
                                     wpool_ref, pscale_ref, gain_ref, c_ref, m_ref, uext_ref))
    n_mix = MIX_PIECES * len(pieces)
    done = 0
    for k, piece in enumerate(p for chunk in pieces for p in chunk):
        while done < len(dense) and done * n_mix < k * len(dense):
            dense[done]()
            done += 1
    for job in dense[done:]:
        job()


def _mix_chunk(pb, g_old, mx, rows, wpool_ref, pscale_ref, gain_ref, c_ref, m_ref, uext_ref):
    R = CHUNK
    gc = _gate_cols(g_old[rows, :], m_ref[0:1, :], R)
    r_t = gc["r"].T

    uext_ref[CTX_PAD:CTX_PAD + R, :] = pb[rows, 0:POOL_WIDTH]
    for gi, w in enumerate(POOL_WINDOWS):
        cs = slice(gi * POOL_GC, (gi + 1) * POOL_GC)
        acc = uext_ref[:, cs]
        sh = 1
        while sh < w:
            acc = acc + pltpu.roll(acc, sh, axis=0)
            sh *= 2
        z = acc[CTX_PAD:, :] * (1.0 / w) - uext_ref[CTX_PAD:CTX_PAD + R, cs]
        mx[rows, cs] = (_dot(z.astype(BF16), wpool_ref[gi]) * pscale_ref[:, cs]).astype(BF16)
    tail = uext_ref[R:R + CTX_PAD, :]
    uext_ref[0:CTX_PAD, :] = tail
    m_ref[...] = jnp.broadcast_to(gc["m_last"], (SUBLANES, LANES))
    yield

    causal = (lax.broadcasted_iota(jnp.int32, (R, R), 0) >= lax.broadcasted_iota(jnp.int32, (R, R), 1))
    front = []
    for h in range(N_HEADS):
        q = (pb[rows, _hcols(COL_Q, h)] * (HEAD_DIM ** -0.5)).astype(BF16)
        k = pb[rows, _hcols(COL_K, h)]
        vaug = _value_aug(pb[rows, _hcols(COL_V, h)], h)
        sc = _dot_nt(q, k.astype(BF16))
        c0 = c_ref[h]
        qc = _dot(q, c0.astype(BF16))
        kw_t = (k * gc["ws"][:, h:h + 1]).T.astype(BF16)
        c_ref[h] = gc["decay"][:, h:h + 1] * c0 + _dot(kw_t, vaug)
        front.append((sc, qc, vaug))
        if h % 2 == 1:
            yield
    for h in range(N_HEADS):
        sc, qc, vaug = front[h]
        e = jnp.exp(jnp.where(causal, r_t[h:h + 1, :] - gc["big_m"][:, h:h + 1], NEG_BIG))
        num = gc["inter"][:, h:h + 1] * qc + _dot((sc * e).astype(BF16), vaug)
        inv = 1.0 / jnp.maximum(jnp.abs(num[:, HEAD_DIM:]), gc["eneg"])
        hh = num[:, 0:HEAD_DIM] * inv[:, h:h + 1]
        hh = hh * lax.rsqrt(jnp.mean(hh * hh, axis=-1, keepdims=True) + EPS) * gain_ref[:, _hcols(0, h)]
        mx[rows, _hcols(POOL_WIDTH, h)] = (hh * jax.nn.sigmoid(pb[rows, _hcols(COL_O, h)])).astype(BF16)
        yield


def _sample_gates(g_ref, m0):
    n = m0.shape[0]
    b = jnp.zeros((n, LANES), F32)
    big_m = m0
    r, big_ms, ms = [], [], []
    for t in range(DEC_SEQ):
        g = g_ref[t * n:(t + 1) * n, :]
        b = b + _log_sigmoid(g[:, LANES:])
        r.append(g[:, :LANES] - b)
        big_m = jnp.maximum(big_m, r[-1])
        big_ms.append(big_m)
        ms.append(b + big_m)
    return dict(r=r, big_m=big_ms, m=ms,
                ws=[jnp.exp(r[t] - big_m) for t in range(DEC_SEQ)],
                decay=jnp.exp(m0 - big_m))


def _sample_kernel(x4_ref, ctx_ref, cin_hbm, n0_ref, m0_ref, n1_ref, win_ref, wg_ref, bg_ref, wpool_ref,
                   pscale_ref, gain_ref, wout_ref,
                   x1_ref, pool_ref, cout_ref, nout_ref, mout_ref,
                   xs_ref, p_ref, g_ref, tmp_ref, qsm_ref, qcsm_ref, kwt_ref, vsm_ref, dec_ref, mix_ref,
                   cin_ref, csem):
    i = pl.program_id(0)
    n_seq = m0_ref.shape[0]
    n_blk = n_seq // SAMPLE_BLOCK
    n_grp = DEC_SEQ * n_seq // LANES
    scale = HEAD_DIM ** -0.5
    slot = i % STATE_BUFFERS

    def state_copy(blk, buf):
        src = cin_hbm.at[pl.ds(blk * SAMPLE_BLOCK, SAMPLE_BLOCK)]
        return pltpu.make_async_copy(src, cin_ref.at[buf], csem.at[buf])

    @pl.when(i == 0)
    def _():
        for b in range(min(STATE_BUFFERS, n_blk)):
            state_copy(b, b).start()

    def seq_major(dst_ref, slab_of_t):
        for t in range(DEC_SEQ):
            slab = slab_of_t(t)
            for h in range(N_HEADS):
                dst_ref[h, pl.ds(t, n_seq, stride=DEC_SEQ), :] = slab[:, _hcols(0, h)]

    @pl.when(i == 0)
    def _():
        for t in range(DEC_SEQ):
            for lt in range(D_MODEL // LANES):
                xs_ref[t * n_seq:(t + 1) * n_seq, lt * LANES:(lt + 1) * LANES] = x4_ref[:, lt, t, :]
        xn = _rmsnorm(xs_ref[...], n1_ref[...]).astype(BF16)
        g_ref[...] = _dot(xn, wg_ref[...]) + bg_ref[...]
        p_ref[...] = _dot(xn, win_ref[...])

        def urow(e, cs):
            if e < POOL_CTX:
                return ctx_ref[e, :, cs]
            return p_ref[(e - POOL_CTX) * n_seq:(e - POOL_CTX + 1) * n_seq, cs]

        for gi, w in enumerate(POOL_WINDOWS):
            cs = slice(gi * POOL_GC, (gi + 1) * POOL_GC)
            for t in range(DEC_SEQ):
                e = POOL_CTX + t
                wsum = urow(e, cs)
                for j in range(1, w):
                    wsum = wsum + urow(e - j, cs)
                z = wsum * (1.0 / w) - urow(e, cs)
                mix_ref[t * n_seq:(t + 1) * n_seq, cs] = (
                    _dot(z.astype(BF16), wpool_ref[gi]) * pscale_ref[:, cs]).astype(BF16)
        for e in range(DEC_SEQ, POOL_CTX + DEC_SEQ):
            pool_ref[e - DEC_SEQ] = urow(e, slice(0, POOL_WIDTH))

        gs = _sample_gates(g_ref, m0_ref[...])
        dec_ref[...] = gs["decay"]
        mout_ref[...] = gs["m"][DEC_SEQ - 1]
        def kw_slab(t):
            rows = slice(t * n_seq, (t + 1) * n_seq)
            return jnp.concatenate([p_ref[rows, _hcols(COL_K, h)] * gs["ws"][t][:, h:h + 1]
                                    for h in range(N_HEADS)], axis=1)

        kw = [kw_slab(t) for t in range(DEC_SEQ)]
        nout_ref[...] = (jnp.concatenate([jnp.broadcast_to(gs["decay"][:, h:h + 1], (n_seq, HEAD_DIM))
                                          for h in range(N_HEADS)], axis=1) * n0_ref[...]
                         + kw[0] + kw[1] + kw[2] + kw[3])
        seq_major(tmp_ref, lambda t: kw[t])
        for h in range(N_HEADS):
            for gi in range(n_grp):
                kwt_ref[h, gi] = tmp_ref[h, gi * LANES:(gi + 1) * LANES, :].T.astype(BF16)
        seq_major(tmp_ref, lambda t: p_ref[t * n_seq:(t + 1) * n_seq, COL_V:COL_O])
        for h in range(N_HEADS):
            for gi in range(n_grp):
                vsm_ref[h, gi] = tmp_ref[h, gi * LANES:(gi + 1) * LANES, :].astype(BF16)
        seq_major(qsm_ref, lambda t: p_ref[t * n_seq:(t + 1) * n_seq, COL_Q:COL_K] * scale)

    lane_sl = lax.broadcasted_iota(jnp.int32, (HEAD_DIM, LANES), 1) // DEC_SEQ
    first = lax.broadcasted_iota(jnp.int32, (SUBLANES, HEAD_DIM), 0) < DEC_SEQ
    grp = (i * SAMPLE_BLOCK) // (LANES // DEC_SEQ)
    state_copy(i, slot).wait()
    for jp in range(SAMPLE_BLOCK // 2):
        rows = pl.ds(pl.multiple_of(i * (SAMPLE_BLOCK * DEC_SEQ) + jp * SUBLANES, SUBLANES), SUBLANES)
        for h in range(N_HEADS):
            hs = _hcols(0, h)
            q8 = qsm_ref[h, rows, :].astype(BF16)
            readout = []
            for a in range(2):
                j = 2 * jp + a
                seq = i * SAMPLE_BLOCK + j
                c0 = cin_ref[slot, j, h]
                readout.append(_dot(q8, c0.astype(BF16)))
                lhs = jnp.where(lane_sl == seq % (LANES // DEC_SEQ), kwt_ref[h, grp], jnp.zeros((), BF16))
                cout_ref[j, h] = dec_ref[pl.ds(seq, 1), :][:, h:h + 1] * c0 + _dot(lhs, vsm_ref[h, grp])
            qcsm_ref[h, rows, :] = jnp.where(first, readout[0], readout[1])

    @pl.when(i + STATE_BUFFERS < n_blk)
    def _():
        state_copy(i + STATE_BUFFERS, slot).start()

    @pl.when(i == pl.num_programs(0) - 1)
    def _():
        gs = _sample_gates(g_ref, m0_ref[...])
        for h in range(N_HEADS):
            hs = _hcols(0, h)
            n0 = n0_ref[:, hs]
            qs = [p_ref[t * n_seq:(t + 1) * n_seq, _hcols(COL_Q, h)] * scale for t in range(DEC_SEQ)]
            ks = [p_ref[t * n_seq:(t + 1) * n_seq, _hcols(COL_K, h)] for t in range(DEC_SEQ)]
            for t in range(DEC_SEQ):
                rows = slice(t * n_seq, (t + 1) * n_seq)
                inter = jnp.exp(m0_ref[...] - gs["big_m"][t])[:, h:h + 1]
                num = inter * qcsm_ref[h, pl.ds(t, n_seq, stride=DEC_SEQ), :]
                den = inter * jnp.sum(qs[t] * n0, axis=-1, keepdims=True)
                for s in range(t + 1):
                    pts = (jnp.sum(qs[t] * ks[s], axis=-1, keepdims=True)
                           * jnp.exp(gs["r"][s] - gs["big_m"][t])[:, h:h + 1])
                    num = num + pts * p_ref[s * n_seq:(s + 1) * n_seq, _hcols(COL_V, h)]
                    den = den + pts
                hh = num / jnp.maximum(jnp.abs(den), jnp.exp(-gs["m"][t])[:, h:h + 1])
                hh = hh * lax.rsqrt(jnp.mean(hh * hh, axis=-1, keepdims=True) + EPS) * gain_ref[:, hs]
                mix_ref[rows, _hcols(POOL_WIDTH, h)] = (
                    hh * jax.nn.sigmoid(p_ref[rows, _hcols(COL_O, h)])).astype(BF16)
        x1_ref[...] = xs_ref[...] + _dot(mix_ref[...], wout_ref[...])


def _ffn_kernel(xs_ref, xp_ref, n2_ref, nf_ref, wup_hbm, wdown_hbm, yp_ref, ys_ref,
                wup_ref, wdown_ref, stage_ref, sem):
    i = pl.program_id(0)
    last = pl.num_programs(0) - 1

    def fetch(c, up):
        cols = pl.ds(c * FFN_CHUNK, FFN_CHUNK)
        src = wup_hbm.at[:, cols] if up else wdown_hbm.at[cols, :]
        slot = 0 if up else 1
        return pltpu.make_async_copy(src, stage_ref.at[slot], sem.at[slot])

    def store_rows(rows, y):
        yp_ref[rows, :] = y

    def store_tiles(rows, y):
        n_seq = ys_ref.shape[0]
        for k in range(FFN_SUB // n_seq):
            t = rows.start // n_seq + k
            for lt in range(D_MODEL // LANES):
                ys_ref[:, lt, t, :] = y[k * n_seq:(k + 1) * n_seq, lt * LANES:(lt + 1) * LANES]

    def mlp(x_ref, store, load_weights):
        subs = [slice(j * FFN_SUB, (j + 1) * FFN_SUB) for j in range(x_ref.shape[0] // FFN_SUB)]
        n_chunk = D_FF // FFN_CHUNK
        if load_weights:
            fetch(0, True).start()
            fetch(0, False).start()
        xn = [_rmsnorm(x_ref[rows, :], n2_ref[...]).astype(BF16) for rows in subs]
        acc = [x_ref[rows, :] for rows in subs]
        for c in range(n_chunk):
            if load_weights:
                fetch(c, True).wait()
                wup_ref[c] = stage_ref[0].astype(BF16)
                if c + 1 < n_chunk:
                    fetch(c + 1, True).start()
                fetch(c, False).wait()
                wdown_ref[c] = stage_ref[1].astype(BF16)
                if c + 1 < n_chunk:
                    fetch(c + 1, False).start()
            for j in range(len(subs)):
                a = jnp.square(jnp.maximum(_dot(xn[j], wup_ref[c]), 0.0)).astype(BF16)
                acc[j] = acc[j] + _dot(a, wdown_ref[c])
        for j, rows in enumerate(subs):
            store(rows, _rmsnorm(acc[j], nf_ref[...]))

    pl.when(i == 0)(lambda: mlp(xp_ref, store_rows, True))
    pl.when((i > 0) & (i < last))(lambda: mlp(xp_ref, store_rows, False))
    pl.when(i == last)(lambda: mlp(xs_ref, store_tiles, False))


def _full(shape):
    return pl.BlockSpec(shape, lambda *_: (0,) * len(shape))


def _ffn(xs, xp, n2, nf, w_up, w_down, n_seq):
    assert D_MODEL == FFN_CHUNK
    assert FFN_SUB % n_seq == 0
    n_prompt = xp.shape[0] // FFN_TILE
    n_chunk = D_FF // FFN_CHUNK
    ys_shape = (n_seq, D_MODEL // LANES, xs.shape[0] // n_seq, LANES)
    prompt_tile = pl.BlockSpec((FFN_TILE, D_MODEL), lambda i: (jnp.minimum(i, n_prompt - 1), 0))
    return pl.pallas_call(
        _ffn_kernel,
        grid=(n_prompt + 1,),
        in_specs=[_full(xs.shape), prompt_tile, _full((1, D_MODEL)), _full((1, D_MODEL)),
                  pl.BlockSpec(memory_space=pl.ANY), pl.BlockSpec(memory_space=pl.ANY)],
        out_specs=[prompt_tile, _full(ys_shape)],
        out_shape=(jax.ShapeDtypeStruct(xp.shape, F32), jax.ShapeDtypeStruct(ys_shape, F32)),
        scratch_shapes=[pltpu.VMEM((n_chunk, D_MODEL, FFN_CHUNK), BF16),
                        pltpu.VMEM((n_chunk, FFN_CHUNK, D_MODEL), BF16),
                        pltpu.VMEM((2, D_MODEL, FFN_CHUNK), F32),
                        pltpu.SemaphoreType.DMA((2,))],
        compiler_params=pltpu.CompilerParams(dimension_semantics=("arbitrary",), vmem_limit_bytes=VMEM_LIMIT),
        name="ffn",
    )(xs, xp, n2, nf, w_up, w_down)


def kernel(x_prompt, x_sample, state_pool, state_C, state_n, state_m, meta_tokens, norm1, w_in, b_gate,
           w_pool, pool_scale, head_gain, w_out, norm2, w_up, w_down, norm_f):
    B, S, _ = x_prompt.shape
    n_seq = x_sample.shape[0]
    rows_s = DEC_SEQ * n_seq
    assert w_in.shape[0] == 1 and S % PROMPT_TILE == 0 and n_seq % SAMPLE_BLOCK == 0
    assert (B * S) % FFN_TILE == 0 and rows_s % FFN_SUB == 0

    win = w_in[0].astype(BF16)
    gpad = ((0, 0), (0, LANES - N_HEADS))
    wg = jnp.concatenate([jnp.pad(w_in[0][:, IN_MAIN:IN_MAIN + N_HEADS], gpad),
                          jnp.pad(w_in[0][:, IN_MAIN + N_HEADS:], gpad)], axis=1).astype(BF16)
    bg = jnp.concatenate([jnp.pad(b_gate[0][:N_HEADS], (0, LANES - N_HEADS)),
                          jnp.pad(b_gate[0][N_HEADS:], (0, LANES - N_HEADS))]).reshape(1, GATE_W)
    n1 = norm1[0].reshape(1, D_MODEL)
    n2 = norm2[0].reshape(1, D_MODEL)
    nf = norm_f.reshape(1, D_MODEL)
    pscale = pool_scale[0].reshape(1, POOL_WIDTH)
    gain = head_gain[0].reshape(1, MLSTM_WIDTH)

    weight_specs = [_full((1, D_MODEL)), _full((D_MODEL, IN_MAIN)), _full((D_MODEL, GATE_W)), _full((1, GATE_W))]
    mixer_specs = [_full((len(POOL_WINDOWS), POOL_GC, POOL_GC)), _full((1, POOL_WIDTH)),
                   _full((1, MLSTM_WIDTH)), _full((D_MODEL, D_MODEL))]

    nt = S // PROMPT_TILE
    n_tiles = B * nt
    assert n_tiles % 2 == 0 and n_tiles >= 2

    def tile_at(lag):
        def index(s):
            j = jnp.clip(s - lag, 0, n_tiles - 1)
            return (j // nt, j % nt, 0)
        return index

    def mixed_seq(ndim):
        return lambda s: (jnp.clip(s - 1, 0, n_tiles - 1) // nt,) + (0,) * (ndim - 1)

    wpool_shape = (len(POOL_WINDOWS), POOL_GC, POOL_GC)
    x1_p, pool_p, m_p, c_p, n_p, wout, wpool = pl.pallas_call(
        functools.partial(_prompt_kernel, nt),
        grid=(n_tiles + 2,),
        in_specs=[pl.BlockSpec((1, PROMPT_TILE, D_MODEL), tile_at(0)),
                  pl.BlockSpec((1, PROMPT_TILE, D_MODEL), tile_at(2)),
                  _full((N_META, D_MODEL))]
        + weight_specs + mixer_specs,
        out_specs=[pl.BlockSpec((1, PROMPT_TILE, D_MODEL), tile_at(2)),
                   pl.BlockSpec((1, CTX_PAD, POOL_WIDTH), mixed_seq(3)),
                   pl.BlockSpec((1, SUBLANES, LANES), mixed_seq(3)),
                   pl.BlockSpec((1, N_HEADS, HEAD_DIM, HEAD_DIM), mixed_seq(4)),
                   pl.BlockSpec((1, N_HEADS, HEAD_DIM), mixed_seq(3)),
                   _full((D_MODEL, D_MODEL)), _full(wpool_shape)],
        out_shape=(jax.ShapeDtypeStruct((B, S, D_MODEL), F32),
                   jax.ShapeDtypeStruct((B, CTX_PAD, POOL_WIDTH), F32),
                   jax.ShapeDtypeStruct((B, SUBLANES, LANES), F32),
                   jax.ShapeDtypeStruct((B, N_HEADS, HEAD_DIM, HEAD_DIM), F32),
                   jax.ShapeDtypeStruct((B, N_HEADS, HEAD_DIM), F32),
                   jax.ShapeDtypeStruct((D_MODEL, D_MODEL), BF16),
                   jax.ShapeDtypeStruct(wpool_shape, BF16)),
        scratch_shapes=[pltpu.VMEM((CTX_PAD + CHUNK, POOL_WIDTH), F32),
                        pltpu.VMEM((PROMPT_TILE, D_MODEL), BF16),
                        pltpu.VMEM((PROMPT_TILE, IN_MAIN), F32), pltpu.VMEM((PROMPT_TILE, IN_MAIN), F32),
                        pltpu.VMEM((PROMPT_TILE, GATE_W), F32), pltpu.VMEM((PROMPT_TILE, GATE_W), F32),
                        pltpu.VMEM((PROMPT_TILE, D_MODEL), BF16), pltpu.VMEM((PROMPT_TILE, D_MODEL), BF16),
                        pltpu.VMEM((N_HEADS, HEAD_DIM, AUG), F32), pltpu.VMEM((SUBLANES, LANES), F32),
                        pltpu.VMEM((N_META, POOL_WIDTH), F32), pltpu.VMEM((N_HEADS, HEAD_DIM, AUG), F32),
                        pltpu.VMEM((SUBLANES, LANES), F32)],
        compiler_params=pltpu.CompilerParams(dimension_semantics=("arbitrary",), vmem_limit_bytes=VMEM_LIMIT),
        name="prompt_mixer",
    )(x_prompt, x_prompt, meta_tokens, n1, win, wg, bg, w_pool[0], pscale, gain, w_out[0])

    xs_tiles = x_sample.reshape(n_seq, DEC_SEQ, D_MODEL // LANES, LANES).transpose(0, 2, 1, 3)
    x1_p, xs_tiles = lax.optimization_barrier((x1_p, xs_tiles))
    ctx_t = state_pool[0].transpose(1, 0, 2)
    n0 = state_n[0].reshape(n_seq, MLSTM_WIDTH)
    m0 = jnp.pad(state_m[0], ((0, 0), (0, LANES - N_HEADS)))
    state_shape = (SAMPLE_BLOCK, N_HEADS, HEAD_DIM, HEAD_DIM)
    state_block = pl.BlockSpec(state_shape, lambda i: (i, 0, 0, 0))
    state_in = pl.BlockSpec(memory_space=pl.ANY)
    x1_s, pool_s, c_s, n_s, m_s = pl.pallas_call(
        _sample_kernel,
        grid=(n_seq // SAMPLE_BLOCK,),
        in_specs=[_full(xs_tiles.shape), _full((POOL_CTX, n_seq, POOL_WIDTH)), state_in,
                  _full((n_seq, MLSTM_WIDTH)), _full((n_seq, LANES))] + weight_specs + mixer_specs,
        out_specs=[_full((rows_s, D_MODEL)), _full((POOL_CTX, n_seq, POOL_WIDTH)), state_block,
                   _full((n_seq, MLSTM_WIDTH)), _full((n_seq, LANES))],
        out_shape=(jax.ShapeDtypeStruct((rows_s, D_MODEL), F32),
                   jax.ShapeDtypeStruct((POOL_CTX, n_seq, POOL_WIDTH), F32),
                   jax.ShapeDtypeStruct((n_seq, N_HEADS, HEAD_DIM, HEAD_DIM), F32),
                   jax.ShapeDtypeStruct((n_seq, MLSTM_WIDTH), F32),
                   jax.ShapeDtypeStruct((n_seq, LANES), F32)),
        scratch_shapes=[pltpu.VMEM((rows_s, D_MODEL), F32),
                        pltpu.VMEM((rows_s, IN_MAIN), F32),
                        pltpu.VMEM((rows_s, GATE_W), F32),
                        pltpu.VMEM((N_HEADS, rows_s, HEAD_DIM), F32),
                        pltpu.VMEM((N_HEADS, rows_s, HEAD_DIM), F32),
                        pltpu.VMEM((N_HEADS, rows_s, HEAD_DIM), F32),
                        pltpu.VMEM((N_HEADS, rows_s // LANES, HEAD_DIM, LANES), BF16),
                        pltpu.VMEM((N_HEADS, rows_s // LANES, LANES, HEAD_DIM), BF16),
                        pltpu.VMEM((n_seq, LANES), F32),
                        pltpu.VMEM((rows_s, D_MODEL), BF16),
                        pltpu.VMEM((STATE_BUFFERS,) + state_shape, F32),
                        pltpu.SemaphoreType.DMA((STATE_BUFFERS,))],
        compiler_params=pltpu.CompilerParams(dimension_semantics=("arbitrary",), vmem_limit_bytes=VMEM_LIMIT),
        name="sample_mixer",
    )(xs_tiles, ctx_t, state_C[0], n0, m0, n1, win, wg, bg, wpool, pscale, gain, wout)

    y_p, y_s = _ffn(x1_s, x1_p.reshape(B * S, D_MODEL), n2, nf, w_up[0], w_down[0], n_seq)
    y_p = y_p.reshape(B, S, D_MODEL)
    y_s = y_s.transpose(0, 2, 1, 3).reshape(n_seq, DEC_SEQ, D_MODEL)

    return (y_p, y_s,
            pool_p[:, CTX_PAD - POOL_CTX:, :][None],
            c_p[None],
            n_p[None],
            m_p[:, 0, :N_HEADS][None],
            pool_s.transpose(1, 0, 2)[None],
            c_s[None],
            n_s.reshape(n_seq, N_HEADS, HEAD_DIM)[None],
            m_s[:, :N_HEADS][None])
```

```python
import functools

import jax
import jax.numpy as jnp
from jax import lax
from jax.experimental import pallas as pl
from jax.experimental.pallas import tpu as pltpu

D_MODEL = 1024
N_META = 16
POOL_WIDTH = 512
POOL_WINDOWS = (2, 4, 8, 16)
POOL_GC = 128
POOL_CTX = 15
MLSTM_WIDTH = 512
N_HEADS = 4
HEAD_DIM = 128
D_FF = 4096
EPS = 1e-6
DEC_SEQ = 4

LANES = 128
SUBLANES = 8
COL_Q = POOL_WIDTH
COL_K = COL_Q + MLSTM_WIDTH
COL_V = COL_K + MLSTM_WIDTH
COL_O = COL_V + MLSTM_WIDTH
IN_MAIN = COL_O + MLSTM_WIDTH
GATE_W = 2 * LANES
AUG = 2 * HEAD_DIM
CTX_PAD = 16
CHUNK = 256
PROMPT_TILE = 512
IN_CHUNK = 512
MIX_PIECES = 1 + N_HEADS // 2 + N_HEADS
FFN_TILE = 512
FFN_SUB = 256
FFN_CHUNK = 1024
SAMPLE_BLOCK = 8
STATE_BUFFERS = 4
NEG_BIG = -1e30
VMEM_LIMIT = 56 * 1024 * 1024

F32 = jnp.float32
BF16 = jnp.bfloat16


def _rmsnorm(x, g):
    return x * lax.rsqrt(jnp.mean(x * x, axis=-1, keepdims=True) + EPS) * g


def _log_sigmoid(x):
    return jnp.minimum(x, 0.0) - jnp.log1p(jnp.exp(-jnp.abs(x)))


def _dot(a, b):
    return jnp.dot(a, b, preferred_element_type=F32)


def _dot_nt(a, b):
    return lax.dot_general(a, b, (((1,), (1,)), ((), ())), preferred_element_type=F32)


def _hcols(base, h):
    return slice(base + h * HEAD_DIM, base + (h + 1) * HEAD_DIM)


def _row_scan(x, op):
    n = x.shape[0]
    row = lax.broadcasted_iota(jnp.int32, x.shape, 0)
    sh = 1
    while sh < n:
        x = jnp.where(row >= sh, op(x, pltpu.roll(x, sh, axis=0)), x)
        sh *= 2
    return x


def _gate_cols(g, m0, n_valid):
    R = g.shape[0]
    b = _row_scan(_log_sigmoid(g[:, LANES:]), jnp.add)
    r = g[:, :LANES] - b
    big_m = jnp.maximum(m0, _row_scan(r, jnp.maximum))
    m = b + big_m
    last = n_valid - 1
    m_top = big_m[last:last + 1, :]
    ws = jnp.exp(r - m_top)
    if n_valid < R:
        ws = jnp.where(lax.broadcasted_iota(jnp.int32, ws.shape, 0) < n_valid, ws, 0.0)
    return dict(r=r, big_m=big_m, inter=jnp.exp(m0 - big_m), eneg=jnp.exp(-m), ws=ws,
                decay=jnp.exp(m0 - m_top), m_last=m[last:last + 1, :])


def _value_aug(v, h):
    lane = lax.broadcasted_iota(jnp.int32, (v.shape[0], HEAD_DIM), 1)
    return jnp.concatenate([v, jnp.where(lane == h, 1.0, 0.0)], axis=1).astype(BF16)


def _meta_prefix(x_ref, n1_ref, win_ref, wg_ref, bg_ref, wout_ref, wpool_ref,
                 u_ref, c_ref, m_ref, wout_bf_ref, wpool_bf_ref):
    wout_bf_ref[...] = wout_ref[...].astype(BF16)
    wpool_bf_ref[...] = wpool_ref[...].astype(BF16)
    x = jnp.concatenate([x_ref[...], jnp.zeros((LANES - N_META, D_MODEL), F32)], axis=0)
    xn = _rmsnorm(x, n1_ref[...]).astype(BF16)
    gc = _gate_cols(_dot(xn, wg_ref[...]) + bg_ref[...], jnp.zeros((1, LANES), F32), N_META)
    p = _dot(xn, win_ref[:, 0:IN_MAIN])
    u_ref[...] = p[0:N_META, 0:POOL_WIDTH]
    for h in range(N_HEADS):
        kw_t = (p[:, _hcols(COL_K, h)] * gc["ws"][:, h:h + 1]).T.astype(BF16)
        c_ref[h] = _dot(kw_t, _value_aug(p[:, _hcols(COL_V, h)], h))
    m_ref[...] = jnp.broadcast_to(gc["m_last"], (SUBLANES, LANES))


def _prompt_kernel(nt, xa_ref, xc_ref, meta_ref, n1_ref, win_ref, wg_ref, bg_ref,
                   wpool32_ref, pscale_ref, gain_ref, wout32_ref,
                   x1_ref, pool_ref, mfin_ref, cfin_ref, nfin_ref, wout_ref, wpool_ref,
                   uext_ref, xn_ref, p0_ref, p1_ref, g0_ref, g1_ref, mix0_ref, mix1_ref, c_ref, m_ref,
                   umeta_ref, cmeta_ref, mmeta_ref):
    s = pl.program_id(0)
    n_tiles = pl.num_programs(0) - 2

    @pl.when(s == 0)
    def _():
        _meta_prefix(meta_ref, n1_ref, win_ref, wg_ref, bg_ref, wout32_ref, wpool32_ref,
                     umeta_ref, cmeta_ref, mmeta_ref, wout_ref, wpool_ref)

    @pl.when((jnp.maximum(s - 1, 0) % nt == 0) & (s <= n_tiles))
    def _():
        c_ref[...] = cmeta_ref[...]
        m_ref[...] = mmeta_ref[...]
        uext_ref[0:CTX_PAD, :] = umeta_ref[...]

    stage = functools.partial(_prompt_step, xa_ref, xc_ref, n1_ref, win_ref, wg_ref, bg_ref, wpool_ref,
                              pscale_ref, gain_ref, wout_ref, x1_ref, c_ref, m_ref, uext_ref, xn_ref)
    even = (p0_ref, p1_ref, g0_ref, g1_ref, mix0_ref, mix1_ref)
    odd = (p1_ref, p0_ref, g1_ref, g0_ref, mix1_ref, mix0_ref)
    steady = (s >= 2) & (s < n_tiles)
    pl.when(s == 0)(lambda: stage(*even, project=True, mix=False, emit=False))
    pl.when(s == 1)(lambda: stage(*odd, project=True, mix=True, emit=False))
    pl.when(steady & (s % 2 == 0))(lambda: stage(*even, project=True, mix=True, emit=True))
    pl.when(steady & (s % 2 == 1))(lambda: stage(*odd, project=True, mix=True, emit=True))
    pl.when(s == n_tiles)(lambda: stage(*even, project=False, mix=True, emit=True))
    pl.when(s == n_tiles + 1)(lambda: stage(*odd, project=False, mix=False, emit=True))

    @pl.when((s >= 1) & (s <= n_tiles) & ((s - 1) % nt == nt - 1))
    def _():
        for h in range(N_HEADS):
            cfin_ref[0, h] = c_ref[h, :, 0:HEAD_DIM]
            nfin_ref[0, h:h + 1, :] = c_ref[h, :, HEAD_DIM:].T[h:h + 1, :]
        mfin_ref[0] = m_ref[...]
        pool_ref[0] = uext_ref[0:CTX_PAD, :]


def _prompt_step(xa_ref, xc_ref, n1_ref, win_ref, wg_ref, bg_ref, wpool_ref, pscale_ref, gain_ref, wout_ref,
                 x1_ref, c_ref, m_ref, uext_ref, xn_ref,
                 p_new, pb, g_new, g_old, mx, mix_old, *, project, mix, emit):
    dense = []
    if emit:
        def out_proj():
            x1_ref[0] = xc_ref[0] + _dot(mix_old[...], wout_ref[...])
        dense.append(out_proj)
    if project:
        def in_norm():
            xn_ref[...] = _rmsnorm(xa_ref[0], n1_ref[...]).astype(BF16)
            g_new[...] = _dot(xn_ref[...], wg_ref[...]) + bg_ref[...]

        def in_chunk(c):
            cs = slice(c * IN_CHUNK, (c + 1) * IN_CHUNK)
            p_new[:, cs] = _dot(xn_ref[...], win_ref[:, cs])
        dense.append(in_norm)
        dense += [functools.partial(in_chunk, c) for c in range(IN_MAIN // IN_CHUNK)]

    pieces = []
    if mix:
        for ci in range(PROMPT_TILE // CHUNK):
            pieces.append(_mix_chunk(pb, g_old, mx, slice(ci * CHUNK, (ci + 1) * CHUNK),
                                     wpool_ref, pscale_ref, gain_ref, c_ref, m_ref, uext_ref))
    n_mix = MIX_PIECES * len(pieces)
    done = 0
    for k, piece in enumerate(p for chunk in pieces for p in chunk):
        while done < len(dense) and done * n_mix < k * len(dense):
            dense[done]()
            done += 1
    for job in dense[done:]:
        job()


def _mix_chunk(pb, g_old, mx, rows, wpool_ref, pscale_ref, gain_ref, c_ref, m_ref, uext_ref):
    R = CHUNK
    gc = _gate_cols(g_old[rows, :], m_ref[0:1, :], R)
    r_t = gc["r"].T

    uext_ref[CTX_PAD:CTX_PAD + R, :] = pb[rows, 0:POOL_WIDTH]
    for gi, w in enumerate(POOL_WINDOWS):
        cs = slice(gi * POOL_GC, (gi + 1) * POOL_GC)
        acc = uext_ref[:, cs]
        sh = 1
        while sh < w:
            acc = acc + pltpu.roll(acc, sh, axis=0)
            sh *= 2
        z = acc[CTX_PAD:, :] * (1.0 / w) - uext_ref[CTX_PAD:CTX_PAD + R, cs]
        mx[rows, cs] = (_dot(z.astype(BF16), wpool_ref[gi]) * pscale_ref[:, cs]).astype(BF16)
    tail = uext_ref[R:R + CTX_PAD, :]
    uext_ref[0:CTX_PAD, :] = tail
    m_ref[...] = jnp.broadcast_to(gc["m_last"], (SUBLANES, LANES))
    yield

    causal = (lax.broadcasted_iota(jnp.int32, (R, R), 0) >= lax.broadcasted_iota(jnp.int32, (R, R), 1))
    front = []
    for h in range(N_HEADS):
        q = (pb[rows, _hcols(COL_Q, h)] * (HEAD_DIM ** -0.5)).astype(BF16)
        k = pb[rows, _hcols(COL_K, h)]
        vaug = _value_aug(pb[rows, _hcols(COL_V, h)], h)
        sc = _dot_nt(q, k.astype(BF16))
        c0 = c_ref[h]
        qc = _dot(q, c0.astype(BF16))
        kw_t = (k * gc["ws"][:, h:h + 1]).T.astype(BF16)
        c_ref[h] = gc["decay"][:, h:h + 1] * c0 + _dot(kw_t, vaug)
        front.append((sc, qc, vaug))
        if h % 2 == 1:
            yield
    for h in range(N_HEADS):
        sc, qc, vaug = front[h]
        e = jnp.exp(jnp.where(causal, r_t[h:h + 1, :] - gc["big_m"][:, h:h + 1], NEG_BIG))
        num = gc["inter"][:, h:h + 1] * qc + _dot((sc * e).astype(BF16), vaug)
        inv = 1.0 / jnp.maximum(jnp.abs(num[:, HEAD_DIM:]), gc["eneg"])
        hh = num[:, 0:HEAD_DIM] * inv[:, h:h + 1]
        hh = hh * lax.rsqrt(jnp.mean(hh * hh, axis=-1, keepdims=True) + EPS) * gain_ref[:, _hcols(0, h)]
        mx[rows, _hcols(POOL_WIDTH, h)] = (hh * jax.nn.sigmoid(pb[rows, _hcols(COL_O, h)])).astype(BF16)
        yield


def _sample_gates(g_ref, m0):
    n = m0.shape[0]
    b = jnp.zeros((n, LANES), F32)
    big_m = m0
    r, big_ms, ms = [], [], []
    for t in range(DEC_SEQ):
        g = g_ref[t * n:(t + 1) * n, :]
        b = b + _log_sigmoid(g[:, LANES:])
        r.append(g[:, :LANES] - b)
        big_m = jnp.maximum(big_m, r[-1])
        big_ms.append(big_m)
        ms.append(b + big_m)
    return dict(r=r, big_m=big_ms, m=ms,
                ws=[jnp.exp(r[t] - big_m) for t in range(DEC_SEQ)],
                decay=jnp.exp(m0 - big_m))


def _sample_kernel(x4_ref, ctx_hbm, cin_hbm, n0_ref, m0_ref, n1_ref, win_ref, wg_ref, bg_ref, wpool_ref,
                   pscale_ref, gain_ref, wout_ref,
                   x1_ref, pool_ref, cout_ref, nout_ref, mout_ref,
                   xs_ref, p_ref, g_ref, tmp_ref, qsm_ref, qcsm_ref, kwt_ref, vsm_ref, dec_ref, mix_ref,
                   cin_ref, csem, ctx_ref, ctx_sem):
    i = pl.program_id(0)
    n_seq = m0_ref.shape[0]
    n_blk = n_seq // SAMPLE_BLOCK
    n_grp = DEC_SEQ * n_seq // LANES
    scale = HEAD_DIM ** -0.5
    slot = i % STATE_BUFFERS

    def state_copy(blk, buf):
        src = cin_hbm.at[pl.ds(blk * SAMPLE_BLOCK, SAMPLE_BLOCK)]
        return pltpu.make_async_copy(src, cin_ref.at[buf], csem.at[buf])

    def ctx_copy():
        return pltpu.make_async_copy(ctx_hbm, ctx_ref, ctx_sem)

    @pl.when(i == 0)
    def _():
        for b in range(min(STATE_BUFFERS, n_blk)):
            state_copy(b, b).start()
        ctx_copy().start()

    def seq_major(dst_ref, slab_of_t):
        for t in range(DEC_SEQ):
            slab = slab_of_t(t)
            for h in range(N_HEADS):
                dst_ref[h, pl.ds(t, n_seq, stride=DEC_SEQ), :] = slab[:, _hcols(0, h)]

    @pl.when(i == 0)
    def _():
        for t in range(DEC_SEQ):
            for lt in range(D_MODEL // LANES):
                xs_ref[t * n_seq:(t + 1) * n_seq, lt * LANES:(lt + 1) * LANES] = x4_ref[:, lt, t, :]
        xn = _rmsnorm(xs_ref[...], n1_ref[...]).astype(BF16)
        g_ref[...] = _dot(xn, wg_ref[...]) + bg_ref[...]
        p_ref[...] = _dot(xn, win_ref[...])
        ctx_copy().wait()

        def urow(e, cs):
            if e < POOL_CTX:
                return ctx_ref[e, :, cs]
            return p_ref[(e - POOL_CTX) * n_seq:(e - POOL_CTX + 1) * n_seq, cs]

        for gi, w in enumerate(POOL_WINDOWS):
            cs = slice(gi * POOL_GC, (gi + 1) * POOL_GC)
            for t in range(DEC_SEQ):
                e = POOL_CTX + t
                wsum = urow(e, cs)
                for j in range(1, w):
                    wsum = wsum + urow(e - j, cs)
                z = wsum * (1.0 / w) - urow(e, cs)
                mix_ref[t * n_seq:(t + 1) * n_seq, cs] = (
                    _dot(z.astype(BF16), wpool_ref[gi]) * pscale_ref[:, cs]).astype(BF16)
        for e in range(DEC_SEQ, POOL_CTX + DEC_SEQ):
            pool_ref[e - DEC_SEQ] = urow(e, slice(0, POOL_WIDTH))

        gs = _sample_gates(g_ref, m0_ref[...])
        dec_ref[...] = gs["decay"]
        mout_ref[...] = gs["m"][DEC_SEQ - 1]
        def kw_slab(t):
            rows = slice(t * n_seq, (t + 1) * n_seq)
            return jnp.concatenate([p_ref[rows, _hcols(COL_K, h)] * gs["ws"][t][:, h:h + 1]
                                    for h in range(N_HEADS)], axis=1)

        kw = [kw_slab(t) for t in range(DEC_SEQ)]
        nout_ref[...] = (jnp.concatenate([jnp.broadcast_to(gs["decay"][:, h:h + 1], (n_seq, HEAD_DIM))
                                          for h in range(N_HEADS)], axis=1) * n0_ref[...]
                         + kw[0] + kw[1] + kw[2] + kw[3])
        seq_major(tmp_ref, lambda t: kw[t])
        for h in range(N_HEADS):
            for gi in range(n_grp):
                kwt_ref[h, gi] = tmp_ref[h, gi * LANES:(gi + 1) * LANES, :].T.astype(BF16)
        seq_major(tmp_ref, lambda t: p_ref[t * n_seq:(t + 1) * n_seq, COL_V:COL_O])
        for h in range(N_HEADS):
            for gi in range(n_grp):
                vsm_ref[h, gi] = tmp_ref[h, gi * LANES:(gi + 1) * LANES, :].astype(BF16)
        seq_major(qsm_ref, lambda t: p_ref[t * n_seq:(t + 1) * n_seq, COL_Q:COL_K] * scale)

    lane_sl = lax.broadcasted_iota(jnp.int32, (HEAD_DIM, LANES), 1) // DEC_SEQ
    first = lax.broadcasted_iota(jnp.int32, (SUBLANES, HEAD_DIM), 0) < DEC_SEQ
    grp = (i * SAMPLE_BLOCK) // (LANES // DEC_SEQ)
    state_copy(i, slot).wait()
    for jp in range(SAMPLE_BLOCK // 2):
        rows = pl.ds(pl.multiple_of(i * (SAMPLE_BLOCK * DEC_SEQ) + jp * SUBLANES, SUBLANES), SUBLANES)
        for h in range(N_HEADS):
            hs = _hcols(0, h)
            q8 = qsm_ref[h, rows, :].astype(BF16)
            readout = []
            for a in range(2):
                j = 2 * jp + a
                seq = i * SAMPLE_BLOCK + j
                c0 = cin_ref[slot, j, h]
                readout.append(_dot(q8, c0.astype(BF16)))
                lhs = jnp.where(lane_sl == seq % (LANES // DEC_SEQ), kwt_ref[h, grp], jnp.zeros((), BF16))
                cout_ref[j, h] = dec_ref[pl.ds(seq, 1), :][:, h:h + 1] * c0 + _dot(lhs, vsm_ref[h, grp])
            qcsm_ref[h, rows, :] = jnp.where(first, readout[0], readout[1])

    @pl.when(i + STATE_BUFFERS < n_blk)
    def _():
        state_copy(i + STATE_BUFFERS, slot).start()

    @pl.when(i == pl.num_programs(0) - 1)
    def _():
        gs = _sample_gates(g_ref, m0_ref[...])
        for h in range(N_HEADS):
            hs = _hcols(0, h)
            n0 = n0_ref[:, hs]
            qs = [p_ref[t * n_seq:(t + 1) * n_seq, _hcols(COL_Q, h)] * scale for t in range(DEC_SEQ)]
            ks = [p_ref[t * n_seq:(t + 1) * n_seq, _hcols(COL_K, h)] for t in range(DEC_SEQ)]
            for t in range(DEC_SEQ):
                rows = slice(t * n_seq, (t + 1) * n_seq)
                inter = jnp.exp(m0_ref[...] - gs["big_m"][t])[:, h:h + 1]
                num = inter * qcsm_ref[h, pl.ds(t, n_seq, stride=DEC_SEQ), :]
                den = inter * jnp.sum(qs[t] * n0, axis=-1, keepdims=True)
                for s in range(t + 1):
                    pts = (jnp.sum(qs[t] * ks[s], axis=-1, keepdims=True)
                           * jnp.exp(gs["r"][s] - gs["big_m"][t])[:, h:h + 1])
                    num = num + pts * p_ref[s * n_seq:(s + 1) * n_seq, _hcols(COL_V, h)]
                    den = den + pts
                hh = num / jnp.maximum(jnp.abs(den), jnp.exp(-gs["m"][t])[:, h:h + 1])
                hh = hh * lax.rsqrt(jnp.mean(hh * hh, axis=-1, keepdims=True) + EPS) * gain_ref[:, hs]
                mix_ref[rows, _hcols(POOL_WIDTH, h)] = (
                    hh * jax.nn.sigmoid(p_ref[rows, _hcols(COL_O, h)])).astype(BF16)
        x1_ref[...] = xs_ref[...] + _dot(mix_ref[...], wout_ref[...])


def _ffn_kernel(xs_ref, xp_ref, n2_ref, nf_ref, wup_hbm, wdown_hbm, yp_ref, ys_ref,
                wup_ref, wdown_ref, stage_ref, sem):
    i = pl.program_id(0)
    last = pl.num_programs(0) - 1

    def fetch(c, up):
        cols = pl.ds(c * FFN_CHUNK, FFN_CHUNK)
        src = wup_hbm.at[:, cols] if up else wdown_hbm.at[cols, :]
        slot = 0 if up else 1
        return pltpu.make_async_copy(src, stage_ref.at[slot], sem.at[slot])

    def store_rows(rows, y):
        yp_ref[rows, :] = y

    def store_tiles(rows, y):
        n_seq = ys_ref.shape[0]
        for k in range(FFN_SUB // n_seq):
            t = rows.start // n_seq + k
            for lt in range(D_MODEL // LANES):
                ys_ref[:, lt, t, :] = y[k * n_seq:(k + 1) * n_seq, lt * LANES:(lt + 1) * LANES]

    def mlp(x_ref, store, load_weights):
        subs = [slice(j * FFN_SUB, (j + 1) * FFN_SUB) for j in range(x_ref.shape[0] // FFN_SUB)]
        n_chunk = D_FF // FFN_CHUNK
        if load_weights:
            fetch(0, True).start()
            fetch(0, False).start()
        xn = [_rmsnorm(x_ref[rows, :], n2_ref[...]).astype(BF16) for rows in subs]
        acc = [x_ref[rows, :] for rows in subs]
        for c in range(n_chunk):
            if load_weights:
                fetch(c, True).wait()
                wup_ref[c] = stage_ref[0].astype(BF16)
                if c + 1 < n_chunk:
                    fetch(c + 1, True).start()
                fetch(c, False).wait()
                wdown_ref[c] = stage_ref[1].astype(BF16)
                if c + 1 < n_chunk:
                    fetch(c + 1, False).start()
            for j in range(len(subs)):
                a = jnp.square(jnp.maximum(_dot(xn[j], wup_ref[c]), 0.0)).astype(BF16)
                acc[j] = acc[j] + _dot(a, wdown_ref[c])
        for j, rows in enumerate(subs):
            store(rows, _rmsnorm(acc[j], nf_ref[...]))

    pl.when(i == 0)(lambda: mlp(xp_ref, store_rows, True))
    pl.when((i > 0) & (i < last))(lambda: mlp(xp_ref, store_rows, False))
    pl.when(i == last)(lambda: mlp(xs_ref, store_tiles, False))


def _full(shape):
    return pl.BlockSpec(shape, lambda *_: (0,) * len(shape))


def _ffn(xs, xp, n2, nf, w_up, w_down, n_seq):
    assert D_MODEL == FFN_CHUNK
    assert FFN_SUB % n_seq == 0
    n_prompt = xp.shape[0] // FFN_TILE
    n_chunk = D_FF // FFN_CHUNK
    ys_shape = (n_seq, D_MODEL // LANES, xs.shape[0] // n_seq, LANES)
    prompt_tile = pl.BlockSpec((FFN_TILE, D_MODEL), lambda i: (jnp.minimum(i, n_prompt - 1), 0))
    return pl.pallas_call(
        _ffn_kernel,
        grid=(n_prompt + 1,),
        in_specs=[_full(xs.shape), prompt_tile, _full((1, D_MODEL)), _full((1, D_MODEL)),
                  pl.BlockSpec(memory_space=pl.ANY), pl.BlockSpec(memory_space=pl.ANY)],
        out_specs=[prompt_tile, _full(ys_shape)],
        out_shape=(jax.ShapeDtypeStruct(xp.shape, F32), jax.ShapeDtypeStruct(ys_shape, F32)),
        scratch_shapes=[pltpu.VMEM((n_chunk, D_MODEL, FFN_CHUNK), BF16),
                        pltpu.VMEM((n_chunk, FFN_CHUNK, D_MODEL), BF16),
                        pltpu.VMEM((2, D_MODEL, FFN_CHUNK), F32),
                        pltpu.SemaphoreType.DMA((2,))],
        compiler_params=pltpu.CompilerParams(dimension_semantics=("arbitrary",), vmem_limit_bytes=VMEM_LIMIT),
        name="ffn",
    )(xs, xp, n2, nf, w_up, w_down)


def kernel(x_prompt, x_sample, state_pool, state_C, state_n, state_m, meta_tokens, norm1, w_in, b_gate,
           w_pool, pool_scale, head_gain, w_out, norm2, w_up, w_down, norm_f):
    B, S, _ = x_prompt.shape
    n_seq = x_sample.shape[0]
    rows_s = DEC_SEQ * n_seq
    assert w_in.shape[0] == 1 and S % PROMPT_TILE == 0 and n_seq % SAMPLE_BLOCK == 0
    assert (B * S) % FFN_TILE == 0 and rows_s % FFN_SUB == 0

    win = w_in[0].astype(BF16)
    gpad = ((0, 0), (0, LANES - N_HEADS))
    wg = jnp.concatenate([jnp.pad(w_in[0][:, IN_MAIN:IN_MAIN + N_HEADS], gpad),
                          jnp.pad(w_in[0][:, IN_MAIN + N_HEADS:], gpad)], axis=1).astype(BF16)
    bg = jnp.concatenate([jnp.pad(b_gate[0][:N_HEADS], (0, LANES - N_HEADS)),
                          jnp.pad(b_gate[0][N_HEADS:], (0, LANES - N_HEADS))]).reshape(1, GATE_W)
    n1 = norm1[0].reshape(1, D_MODEL)
    n2 = norm2[0].reshape(1, D_MODEL)
    nf = norm_f.reshape(1, D_MODEL)
    pscale = pool_scale[0].reshape(1, POOL_WIDTH)
    gain = head_gain[0].reshape(1, MLSTM_WIDTH)

    weight_specs = [_full((1, D_MODEL)), _full((D_MODEL, IN_MAIN)), _full((D_MODEL, GATE_W)), _full((1, GATE_W))]
    mixer_specs = [_full((len(POOL_WINDOWS), POOL_GC, POOL_GC)), _full((1, POOL_WIDTH)),
                   _full((1, MLSTM_WIDTH)), _full((D_MODEL, D_MODEL))]

    nt = S // PROMPT_TILE
    n_tiles = B * nt
    assert n_tiles % 2 == 0 and n_tiles >= 2

    def tile_at(lag):
        def index(s):
            j = jnp.clip(s - lag, 0, n_tiles - 1)
            return (j // nt, j % nt, 0)
        return index

    def mixed_seq(ndim):
        return lambda s: (jnp.clip(s - 1, 0, n_tiles - 1) // nt,) + (0,) * (ndim - 1)

    wpool_shape = (len(POOL_WINDOWS), POOL_GC, POOL_GC)
    x1_p, pool_p, m_p, c_p, n_p, wout, wpool = pl.pallas_call(
        functools.partial(_prompt_kernel, nt),
        grid=(n_tiles + 2,),
        in_specs=[pl.BlockSpec((1, PROMPT_TILE, D_MODEL), tile_at(0)),
                  pl.BlockSpec((1, PROMPT_TILE, D_MODEL), tile_at(2)),
                  _full((N_META, D_MODEL))]
        + weight_specs + mixer_specs,
        out_specs=[pl.BlockSpec((1, PROMPT_TILE, D_MODEL), tile_at(2)),
                   pl.BlockSpec((1, CTX_PAD, POOL_WIDTH), mixed_seq(3)),
                   pl.BlockSpec((1, SUBLANES, LANES), mixed_seq(3)),
                   pl.BlockSpec((1, N_HEADS, HEAD_DIM, HEAD_DIM), mixed_seq(4)),
                   pl.BlockSpec((1, N_HEADS, HEAD_DIM), mixed_seq(3)),
                   _full((D_MODEL, D_MODEL)), _full(wpool_shape)],
        out_shape=(jax.ShapeDtypeStruct((B, S, D_MODEL), F32),
                   jax.ShapeDtypeStruct((B, CTX_PAD, POOL_WIDTH), F32),
                   jax.ShapeDtypeStruct((B, SUBLANES, LANES), F32),
                   jax.ShapeDtypeStruct((B, N_HEADS, HEAD_DIM, HEAD_DIM), F32),
                   jax.ShapeDtypeStruct((B, N_HEADS, HEAD_DIM), F32),
                   jax.ShapeDtypeStruct((D_MODEL, D_MODEL), BF16),
                   jax.ShapeDtypeStruct(wpool_shape, BF16)),
        scratch_shapes=[pltpu.VMEM((CTX_PAD + CHUNK, POOL_WIDTH), F32),
                        pltpu.VMEM((PROMPT_TILE, D_MODEL), BF16),
                        pltpu.VMEM((PROMPT_TILE, IN_MAIN), F32), pltpu.VMEM((PROMPT_TILE, IN_MAIN), F32),
                        pltpu.VMEM((PROMPT_TILE, GATE_W), F32), pltpu.VMEM((PROMPT_TILE, GATE_W), F32),
                        pltpu.VMEM((PROMPT_TILE, D_MODEL), BF16), pltpu.VMEM((PROMPT_TILE, D_MODEL), BF16),
                        pltpu.VMEM((N_HEADS, HEAD_DIM, AUG), F32), pltpu.VMEM((SUBLANES, LANES), F32),
                        pltpu.VMEM((N_META, POOL_WIDTH), F32), pltpu.VMEM((N_HEADS, HEAD_DIM, AUG), F32),
                        pltpu.VMEM((SUBLANES, LANES), F32)],
        compiler_params=pltpu.CompilerParams(dimension_semantics=("arbitrary",), vmem_limit_bytes=VMEM_LIMIT),
        name="prompt_mixer",
    )(x_prompt, x_prompt, meta_tokens, n1, win, wg, bg, w_pool[0], pscale, gain, w_out[0])

    xs_tiles = x_sample.reshape(n_seq, DEC_SEQ, D_MODEL // LANES, LANES).transpose(0, 2, 1, 3)
    x1_p, xs_tiles = lax.optimization_barrier((x1_p, xs_tiles))
    ctx_t = state_pool[0].transpose(1, 0, 2)
    n0 = state_n[0].reshape(n_seq, MLSTM_WIDTH)
    m0 = jnp.pad(state_m[0], ((0, 0), (0, LANES - N_HEADS)))
    state_shape = (SAMPLE_BLOCK, N_HEADS, HEAD_DIM, HEAD_DIM)
    state_block = pl.BlockSpec(state_shape, lambda i: (i, 0, 0, 0))
    state_in = pl.BlockSpec(memory_space=pl.ANY)
    x1_s, pool_s, c_s, n_s, m_s = pl.pallas_call(
        _sample_kernel,
        grid=(n_seq // SAMPLE_BLOCK,),
        in_specs=[_full(xs_tiles.shape), pl.BlockSpec(memory_space=pl.ANY), state_in,
                  _full((n_seq, MLSTM_WIDTH)), _full((n_seq, LANES))] + weight_specs + mixer_specs,
        out_specs=[_full((rows_s, D_MODEL)), _full((POOL_CTX, n_seq, POOL_WIDTH)), state_block,
                   _full((n_seq, MLSTM_WIDTH)), _full((n_seq, LANES))],
        out_shape=(jax.ShapeDtypeStruct((rows_s, D_MODEL), F32),
                   jax.ShapeDtypeStruct((POOL_CTX, n_seq, POOL_WIDTH), F32),
                   jax.ShapeDtypeStruct((n_seq, N_HEADS, HEAD_DIM, HEAD_DIM), F32),
                   jax.ShapeDtypeStruct((n_seq, MLSTM_WIDTH), F32),
                   jax.ShapeDtypeStruct((n_seq, LANES), F32)),
        scratch_shapes=[pltpu.VMEM((rows_s, D_MODEL), F32),
                        pltpu.VMEM((rows_s, IN_MAIN), F32),
                        pltpu.VMEM((rows_s, GATE_W), F32),
                        pltpu.VMEM((N_HEADS, rows_s, HEAD_DIM), F32),
                        pltpu.VMEM((N_HEADS, rows_s, HEAD_DIM), F32),
                        pltpu.VMEM((N_HEADS, rows_s, HEAD_DIM), F32),
                        pltpu.VMEM((N_HEADS, rows_s // LANES, HEAD_DIM, LANES), BF16),
                        pltpu.VMEM((N_HEADS, rows_s // LANES, LANES, HEAD_DIM), BF16),
                        pltpu.VMEM((n_seq, LANES), F32),
                        pltpu.VMEM((rows_s, D_MODEL), BF16),
                        pltpu.VMEM((STATE_BUFFERS,) + state_shape, F32),
                        pltpu.SemaphoreType.DMA((STATE_BUFFERS,)),
                        pltpu.VMEM((POOL_CTX, n_seq, POOL_WIDTH), F32),
                        pltpu.SemaphoreType.DMA(())],
        compiler_params=pltpu.CompilerParams(dimension_semantics=("arbitrary",), vmem_limit_bytes=VMEM_LIMIT),
        name="sample_mixer",
    )(xs_tiles, ctx_t, state_C[0], n0, m0, n1, win, wg, bg, wpool, pscale, gain, wout)

    y_p, y_s = _ffn(x1_s, x1_p.reshape(B * S, D_MODEL), n2, nf, w_up[0], w_down[0], n_seq)
    y_p = y_p.reshape(B, S, D_MODEL)
    y_s = y_s.transpose(0, 2, 1, 3).reshape(n_seq, DEC_SEQ, D_MODEL)

    return (y_p, y_s,
            pool_p[:, CTX_PAD - POOL_CTX:, :][None],
            c_p[None],
            n_p[None],
            m_p[:, 0, :N_HEADS][None],
            pool_s.transpose(1, 0, 2)[None],
            c_s[None],
            n_s.reshape(n_seq, N_HEADS, HEAD_DIM)[None],
            m_s[:, :N_HEADS][None])
```

```python
import functools

import jax
import jax.numpy as jnp
from jax import lax
from jax.experimental import pallas as pl
from jax.experimental.pallas import tpu as pltpu

D_MODEL = 1024
N_META = 16
POOL_WIDTH = 512
POOL_WINDOWS = (2, 4, 8, 16)
POOL_GC = 128
POOL_CTX = 15
MLSTM_WIDTH = 512
N_HEADS = 4
HEAD_DIM = 128
D_FF = 4096
EPS = 1e-6
DEC_SEQ = 4

LANES = 128
SUBLANES = 8
COL_Q = POOL_WIDTH
COL_K = COL_Q + MLSTM_WIDTH
COL_V = COL_K + MLSTM_WIDTH
COL_O = COL_V + MLSTM_WIDTH
IN_MAIN = COL_O + MLSTM_WIDTH
GATE_W = 2 * LANES
AUG = 2 * HEAD_DIM
CTX_PAD = 16
CHUNK = 256
PROMPT_TILE = 512
IN_CHUNK = 512
MIX_PIECES = 1 + N_HEADS // 2 + N_HEADS
FFN_TILE = 512
FFN_SUB = 256
FFN_CHUNK = 1024
SAMPLE_BLOCK = 8
STATE_BUFFERS = 4
NEG_BIG = -1e30
VMEM_LIMIT = 56 * 1024 * 1024

F32 = jnp.float32
BF16 = jnp.bfloat16


def _rmsnorm(x, g):
    return x * lax.rsqrt(jnp.mean(x * x, axis=-1, keepdims=True) + EPS) * g


def _log_sigmoid(x):
    return jnp.minimum(x, 0.0) - jnp.log1p(jnp.exp(-jnp.abs(x)))


def _dot(a, b):
    return jnp.dot(a, b, preferred_element_type=F32)


def _dot_nt(a, b):
    return lax.dot_general(a, b, (((1,), (1,)), ((), ())), preferred_element_type=F32)


def _hcols(base, h):
    return slice(base + h * HEAD_DIM, base + (h + 1) * HEAD_DIM)


def _row_scan(x, op):
    n = x.shape[0]
    row = lax.broadcasted_iota(jnp.int32, x.shape, 0)
    sh = 1
    while sh < n:
        x = jnp.where(row >= sh, op(x, pltpu.roll(x, sh, axis=0)), x)
        sh *= 2
    return x


def _gate_cols(g, m0, n_valid):
    R = g.shape[0]
    b = _row_scan(_log_sigmoid(g[:, LANES:]), jnp.add)
    r = g[:, :LANES] - b
    big_m = jnp.maximum(m0, _row_scan(r, jnp.maximum))
    m = b + big_m
    last = n_valid - 1
    m_top = big_m[last:last + 1, :]
    ws = jnp.exp(r - m_top)
    if n_valid < R:
        ws = jnp.where(lax.broadcasted_iota(jnp.int32, ws.shape, 0) < n_valid, ws, 0.0)
    return dict(r=r, big_m=big_m, inter=jnp.exp(m0 - big_m), eneg=jnp.exp(-m), ws=ws,
                decay=jnp.exp(m0 - m_top), m_last=m[last:last + 1, :])


def _value_aug(v, h):
    lane = lax.broadcasted_iota(jnp.int32, (v.shape[0], HEAD_DIM), 1)
    return jnp.concatenate([v, jnp.where(lane == h, 1.0, 0.0)], axis=1).astype(BF16)


def _meta_prefix(x_ref, n1_ref, win_ref, wg_ref, bg_ref, wout_ref, wpool_ref,
                 u_ref, c_ref, m_ref, wout_bf_ref, wpool_bf_ref):
    wout_bf_ref[...] = wout_ref[...].astype(BF16)
    wpool_bf_ref[...] = wpool_ref[...].astype(BF16)
    x = jnp.concatenate([x_ref[...], jnp.zeros((LANES - N_META, D_MODEL), F32)], axis=0)
    xn = _rmsnorm(x, n1_ref[...]).astype(BF16)
    gc = _gate_cols(_dot(xn, wg_ref[...]) + bg_ref[...], jnp.zeros((1, LANES), F32), N_META)
    p = _dot(xn, win_ref[:, 0:IN_MAIN])
    u_ref[...] = p[0:N_META, 0:POOL_WIDTH]
    for h in range(N_HEADS):
        kw_t = (p[:, _hcols(COL_K, h)] * gc["ws"][:, h:h + 1]).T.astype(BF16)
        c_ref[h] = _dot(kw_t, _value_aug(p[:, _hcols(COL_V, h)], h))
    m_ref[...] = jnp.broadcast_to(gc["m_last"], (SUBLANES, LANES))


def _prompt_kernel(nt, xa_ref, xc_ref, meta_ref, n1_ref, win_ref, wg_ref, bg_ref,
                   wpool32_ref, pscale_ref, gain_ref, wout32_ref,
                   x1_ref, pool_ref, mfin_ref, cfin_ref, nfin_ref, wout_ref, wpool_ref,
                   uext_ref, xn_ref, p0_ref, p1_ref, g0_ref, g1_ref, mix0_ref, mix1_ref, c_ref, m_ref,
                   umeta_ref, cmeta_ref, mmeta_ref):
    s = pl.program_id(0)
    n_tiles = pl.num_programs(0) - 2

    @pl.when(s == 0)
    def _():
        _meta_prefix(meta_ref, n1_ref, win_ref, wg_ref, bg_ref, wout32_ref, wpool32_ref,
                     umeta_ref, cmeta_ref, mmeta_ref, wout_ref, wpool_ref)

    @pl.when((jnp.maximum(s - 1, 0) % nt == 0) & (s <= n_tiles))
    def _():
        c_ref[...] = cmeta_ref[...]
        m_ref[...] = mmeta_ref[...]
        uext_ref[0:CTX_PAD, :] = umeta_ref[...]

    stage = functools.partial(_prompt_step, xa_ref, xc_ref, n1_ref, win_ref, wg_ref, bg_ref, wpool_ref,
                              pscale_ref, gain_ref, wout_ref, x1_ref, c_ref, m_ref, uext_ref, xn_ref)
    even = (p0_ref, p1_ref, g0_ref, g1_ref, mix0_ref, mix1_ref)
    odd = (p1_ref, p0_ref, g1_ref, g0_ref, mix1_ref, mix0_ref)
    steady = (s >= 2) & (s < n_tiles)
    pl.when(s == 0)(lambda: stage(*even, project=True, mix=False, emit=False))
    pl.when(s == 1)(lambda: stage(*odd, project=True, mix=True, emit=False))
    pl.when(steady & (s % 2 == 0))(lambda: stage(*even, project=True, mix=True, emit=True))
    pl.when(steady & (s % 2 == 1))(lambda: stage(*odd, project=True, mix=True, emit=True))
    pl.when(s == n_tiles)(lambda: stage(*even, project=False, mix=True, emit=True))
    pl.when(s == n_tiles + 1)(lambda: stage(*odd, project=False, mix=False, emit=True))

    @pl.when((s >= 1) & (s <= n_tiles) & ((s - 1) % nt == nt - 1))
    def _():
        for h in range(N_HEADS):
            cfin_ref[0, h] = c_ref[h, :, 0:HEAD_DIM]
            nfin_ref[0, h:h + 1, :] = c_ref[h, :, HEAD_DIM:].T[h:h + 1, :]
        mfin_ref[0] = m_ref[...]
        pool_ref[0] = uext_ref[0:CTX_PAD, :]


def _prompt_step(xa_ref, xc_ref, n1_ref, win_ref, wg_ref, bg_ref, wpool_ref, pscale_ref, gain_ref, wout_ref,
                 x1_ref, c_ref, m_ref, uext_ref, xn_ref,
                 p_new, pb, g_new, g_old, mx, mix_old, *, project, mix, emit):
    dense = []
    if emit:
        def out_proj():
            x1_ref[0] = xc_ref[0] + _dot(mix_old[...], wout_ref[...])
        dense.append(out_proj)
    if project:
        def in_norm():
            xn_ref[...] = _rmsnorm(xa_ref[0], n1_ref[...]).astype(BF16)
            g_new[...] = _dot(xn_ref[...], wg_ref[...]) + bg_ref[...]

        def in_chunk(c):
            cs = slice(c * IN_CHUNK, (c + 1) * IN_CHUNK)
            p_new[:, cs] = _dot(xn_ref[...], win_ref[:, cs])
        dense.append(in_norm)
        dense += [functools.partial(in_chunk, c) for c in range(IN_MAIN // IN_CHUNK)]

    pieces = []
    if mix:
        for ci in range(PROMPT_TILE // CHUNK):
            pieces.append(_mix_chunk(pb, g_old, mx, slice(ci * CHUNK, (ci + 1) * CHUNK),
                                     wpool_ref, pscale_ref, gain_ref, c_ref, m_ref, uext_ref))
    n_mix = MIX_PIECES * len(pieces)
    done = 0
    for k, piece in enumerate(p for chunk in pieces for p in chunk):
        while done < len(dense) and done * n_mix < k * len(dense):
            dense[done]()
            done += 1
    for job in dense[done:]:
        job()


def _mix_chunk(pb, g_old, mx, rows, wpool_ref, pscale_ref, gain_ref, c_ref, m_ref, uext_ref):
    R = CHUNK
    gc = _gate_cols(g_old[rows, :], m_ref[0:1, :], R)
    r_t = gc["r"].T

    uext_ref[CTX_PAD:CTX_PAD + R, :] = pb[rows, 0:POOL_WIDTH]
    for gi, w in enumerate(POOL_WINDOWS):
        cs = slice(gi * POOL_GC, (gi + 1) * POOL_GC)
        acc = uext_ref[:, cs]
        sh = 1
        while sh < w:
            acc = acc + pltpu.roll(acc, sh, axis=0)
            sh *= 2
        z = acc[CTX_PAD:, :] * (1.0 / w) - uext_ref[CTX_PAD:CTX_PAD + R, cs]
        mx[rows, cs] = (_dot(z.astype(BF16), wpool_ref[gi]) * pscale_ref[:, cs]).astype(BF16)
    tail = uext_ref[R:R + CTX_PAD, :]
    uext_ref[0:CTX_PAD, :] = tail
    m_ref[...] = jnp.broadcast_to(gc["m_last"], (SUBLANES, LANES))
    yield

    causal = (lax.broadcasted_iota(jnp.int32, (R, R), 0) >= lax.broadcasted_iota(jnp.int32, (R, R), 1))
    front = []
    for h in range(N_HEADS):
        q = (pb[rows, _hcols(COL_Q, h)] * (HEAD_DIM ** -0.5)).astype(BF16)
        k = pb[rows, _hcols(COL_K, h)]
        vaug = _value_aug(pb[rows, _hcols(COL_V, h)], h)
        sc = _dot_nt(q, k.astype(BF16))
        c0 = c_ref[h]
        qc = _dot(q, c0.astype(BF16))
        kw_t = (k * gc["ws"][:, h:h + 1]).T.astype(BF16)
        c_ref[h] = gc["decay"][:, h:h + 1] * c0 + _dot(kw_t, vaug)
        front.append((sc, qc, vaug))
        if h % 2 == 1:
            yield
    for h in range(N_HEADS):
        sc, qc, vaug = front[h]
        e = jnp.exp(jnp.where(causal, r_t[h:h + 1, :] - gc["big_m"][:, h:h + 1], NEG_BIG))
        num = gc["inter"][:, h:h + 1] * qc + _dot((sc * e).astype(BF16), vaug)
        inv = 1.0 / jnp.maximum(jnp.abs(num[:, HEAD_DIM:]), gc["eneg"])
        hh = num[:, 0:HEAD_DIM] * inv[:, h:h + 1]
        hh = hh * lax.rsqrt(jnp.mean(hh * hh, axis=-1, keepdims=True) + EPS) * gain_ref[:, _hcols(0, h)]
        mx[rows, _hcols(POOL_WIDTH, h)] = (hh * jax.nn.sigmoid(pb[rows, _hcols(COL_O, h)])).astype(BF16)
        yield


def _sample_gates(g_ref, m0):
    n = m0.shape[0]
    b = jnp.zeros((n, LANES), F32)
    big_m = m0
    r, big_ms, ms = [], [], []
    for t in range(DEC_SEQ):
        g = g_ref[t * n:(t + 1) * n, :]
        b = b + _log_sigmoid(g[:, LANES:])
        r.append(g[:, :LANES] - b)
        big_m = jnp.maximum(big_m, r[-1])
        big_ms.append(big_m)
        ms.append(b + big_m)
    return dict(r=r, big_m=big_ms, m=ms,
                ws=[jnp.exp(r[t] - big_m) for t in range(DEC_SEQ)],
                decay=jnp.exp(m0 - big_m))


def _sample_kernel(x4_ref, ctx_ref, cin_hbm, n0_ref, m0_ref, n1_ref, win_ref, wg_ref, bg_ref, wpool_ref,
                   pscale_ref, gain_ref, wout_ref,
                   x1_ref, pool_ref, cout_ref, nout_ref, mout_ref,
                   xs_ref, p_ref, g_ref, tmp_ref, qsm_ref, qcsm_ref, kwt_ref, vsm_ref, dec_ref, mix_ref,
                   cin_ref, csem):
    i = pl.program_id(0)
    n_seq = m0_ref.shape[0]
    n_blk = n_seq // SAMPLE_BLOCK
    n_grp = DEC_SEQ * n_seq // LANES
    scale = HEAD_DIM ** -0.5
    slot = i % STATE_BUFFERS

    def state_copy(blk, buf):
        src = cin_hbm.at[pl.ds(blk * SAMPLE_BLOCK, SAMPLE_BLOCK)]
        return pltpu.make_async_copy(src, cin_ref.at[buf], csem.at[buf])

    @pl.when(i == 0)
    def _():
        for b in range(min(STATE_BUFFERS, n_blk)):
            state_copy(b, b).start(priority=b % 2)

    def seq_major(dst_ref, slab_of_t):
        for t in range(DEC_SEQ):
            slab = slab_of_t(t)
            for h in range(N_HEADS):
                dst_ref[h, pl.ds(t, n_seq, stride=DEC_SEQ), :] = slab[:, _hcols(0, h)]

    @pl.when(i == 0)
    def _():
        for t in range(DEC_SEQ):
            for lt in range(D_MODEL // LANES):
                xs_ref[t * n_seq:(t + 1) * n_seq, lt * LANES:(lt + 1) * LANES] = x4_ref[:, lt, t, :]
        xn = _rmsnorm(xs_ref[...], n1_ref[...]).astype(BF16)
        g_ref[...] = _dot(xn, wg_ref[...]) + bg_ref[...]
        p_ref[...] = _dot(xn, win_ref[...])

        def urow(e, cs):
            if e < POOL_CTX:
                return ctx_ref[e, :, cs]
            return p_ref[(e - POOL_CTX) * n_seq:(e - POOL_CTX + 1) * n_seq, cs]

        for gi, w in enumerate(POOL_WINDOWS):
            cs = slice(gi * POOL_GC, (gi + 1) * POOL_GC)
            for t in range(DEC_SEQ):
                e = POOL_CTX + t
                wsum = urow(e, cs)
                for j in range(1, w):
                    wsum = wsum + urow(e - j, cs)
                z = wsum * (1.0 / w) - urow(e, cs)
                mix_ref[t * n_seq:(t + 1) * n_seq, cs] = (
                    _dot(z.astype(BF16), wpool_ref[gi]) * pscale_ref[:, cs]).astype(BF16)
        for e in range(DEC_SEQ, POOL_CTX + DEC_SEQ):
            pool_ref[e - DEC_SEQ] = urow(e, slice(0, POOL_WIDTH))

        gs = _sample_gates(g_ref, m0_ref[...])
        dec_ref[...] = gs["decay"]
        mout_ref[...] = gs["m"][DEC_SEQ - 1]
        def kw_slab(t):
            rows = slice(t * n_seq, (t + 1) * n_seq)
            return jnp.concatenate([p_ref[rows, _hcols(COL_K, h)] * gs["ws"][t][:, h:h + 1]
                                    for h in range(N_HEADS)], axis=1)

        kw = [kw_slab(t) for t in range(DEC_SEQ)]
        nout_ref[...] = (jnp.concatenate([jnp.broadcast_to(gs["decay"][:, h:h + 1], (n_seq, HEAD_DIM))
                                          for h in range(N_HEADS)], axis=1) * n0_ref[...]
                         + kw[0] + kw[1] + kw[2] + kw[3])
        seq_major(tmp_ref, lambda t: kw[t])
        for h in range(N_HEADS):
            for gi in range(n_grp):
                kwt_ref[h, gi] = tmp_ref[h, gi * LANES:(gi + 1) * LANES, :].T.astype(BF16)
        seq_major(tmp_ref, lambda t: p_ref[t * n_seq:(t + 1) * n_seq, COL_V:COL_O])
        for h in range(N_HEADS):
            for gi in range(n_grp):
                vsm_ref[h, gi] = tmp_ref[h, gi * LANES:(gi + 1) * LANES, :].astype(BF16)
        seq_major(qsm_ref, lambda t: p_ref[t * n_seq:(t + 1) * n_seq, COL_Q:COL_K] * scale)

    lane_sl = lax.broadcasted_iota(jnp.int32, (HEAD_DIM, LANES), 1) // DEC_SEQ
    first = lax.broadcasted_iota(jnp.int32, (SUBLANES, HEAD_DIM), 0) < DEC_SEQ
    grp = (i * SAMPLE_BLOCK) // (LANES // DEC_SEQ)
    state_copy(i, slot).wait()
    for jp in range(SAMPLE_BLOCK // 2):
        rows = pl.ds(pl.multiple_of(i * (SAMPLE_BLOCK * DEC_SEQ) + jp * SUBLANES, SUBLANES), SUBLANES)
        for h in range(N_HEADS):
            hs = _hcols(0, h)
            q8 = qsm_ref[h, rows, :].astype(BF16)
            readout = []
            for a in range(2):
                j = 2 * jp + a
                seq = i * SAMPLE_BLOCK + j
                c0 = cin_ref[slot, j, h]
                readout.append(_dot(q8, c0.astype(BF16)))
                lhs = jnp.where(lane_sl == seq % (LANES // DEC_SEQ), kwt_ref[h, grp], jnp.zeros((), BF16))
                cout_ref[j, h] = dec_ref[pl.ds(seq, 1), :][:, h:h + 1] * c0 + _dot(lhs, vsm_ref[h, grp])
            qcsm_ref[h, rows, :] = jnp.where(first, readout[0], readout[1])

    @pl.when(i + STATE_BUFFERS < n_blk)
    def _():
        state_copy(i + STATE_BUFFERS, slot).start(priority=1)

    @pl.when(i == pl.num_programs(0) - 1)
    def _():
        gs = _sample_gates(g_ref, m0_ref[...])
        for h in range(N_HEADS):
            hs = _hcols(0, h)
            n0 = n0_ref[:, hs]
            qs = [p_ref[t * n_seq:(t + 1) * n_seq, _hcols(COL_Q, h)] * scale for t in range(DEC_SEQ)]
            ks = [p_ref[t * n_seq:(t + 1) * n_seq, _hcols(COL_K, h)] for t in range(DEC_SEQ)]
            for t in range(DEC_SEQ):
                rows = slice(t * n_seq, (t + 1) * n_seq)
                inter = jnp.exp(m0_ref[...] - gs["big_m"][t])[:, h:h + 1]
                num = inter * qcsm_ref[h, pl.ds(t, n_seq, stride=DEC_SEQ), :]
                den = inter * jnp.sum(qs[t] * n0, axis=-1, keepdims=True)
                for s in range(t + 1):
                    pts = (jnp.sum(qs[t] * ks[s], axis=-1, keepdims=True)
                           * jnp.exp(gs["r"][s] - gs["big_m"][t])[:, h:h + 1])
                    num = num + pts * p_ref[s * n_seq:(s + 1) * n_seq, _hcols(COL_V, h)]
                    den = den + pts
                hh = num / jnp.maximum(jnp.abs(den), jnp.exp(-gs["m"][t])[:, h:h + 1])
                hh = hh * lax.rsqrt(jnp.mean(hh * hh, axis=-1, keepdims=True) + EPS) * gain_ref[:, hs]
                mix_ref[rows, _hcols(POOL_WIDTH, h)] = (
                    hh * jax.nn.sigmoid(p_ref[rows, _hcols(COL_O, h)])).astype(BF16)
        x1_ref[...] = xs_ref[...] + _dot(mix_ref[...], wout_ref[...])


def _ffn_kernel(xs_ref, xp_ref, n2_ref, nf_ref, wup_hbm, wdown_hbm, yp_ref, ys_ref,
                wup_ref, wdown_ref, stage_ref, sem):
    i = pl.program_id(0)
    last = pl.num_programs(0) - 1

    def fetch(c, up):
        cols = pl.ds(c * FFN_CHUNK, FFN_CHUNK)
        src = wup_hbm.at[:, cols] if up else wdown_hbm.at[cols, :]
        slot = 0 if up else 1
        return pltpu.make_async_copy(src, stage_ref.at[slot], sem.at[slot])

    def store_rows(rows, y):
        yp_ref[rows, :] = y

    def store_tiles(rows, y):
        n_seq = ys_ref.shape[0]
        for k in range(FFN_SUB // n_seq):
            t = rows.start // n_seq + k
            for lt in range(D_MODEL // LANES):
                ys_ref[:, lt, t, :] = y[k * n_seq:(k + 1) * n_seq, lt * LANES:(lt + 1) * LANES]

    def mlp(x_ref, store, load_weights):
        subs = [slice(j * FFN_SUB, (j + 1) * FFN_SUB) for j in range(x_ref.shape[0] // FFN_SUB)]
        n_chunk = D_FF // FFN_CHUNK
        if load_weights:
            fetch(0, True).start()
            fetch(0, False).start(priority=1)
        xn = [_rmsnorm(x_ref[rows, :], n2_ref[...]).astype(BF16) for rows in subs]
        acc = [x_ref[rows, :] for rows in subs]
        for c in range(n_chunk):
            if load_weights:
                fetch(c, True).wait()
                wup_ref[c] = stage_ref[0].astype(BF16)
                if c + 1 < n_chunk:
                    fetch(c + 1, True).start()
                fetch(c, False).wait()
                wdown_ref[c] = stage_ref[1].astype(BF16)
                if c + 1 < n_chunk:
                    fetch(c + 1, False).start(priority=1)
            for j in range(len(subs)):
                a = jnp.square(jnp.maximum(_dot(xn[j], wup_ref[c]), 0.0)).astype(BF16)
                acc[j] = acc[j] + _dot(a, wdown_ref[c])
        for j, rows in enumerate(subs):
            store(rows, _rmsnorm(acc[j], nf_ref[...]))

    pl.when(i == 0)(lambda: mlp(xp_ref, store_rows, True))
    pl.when((i > 0) & (i < last))(lambda: mlp(xp_ref, store_rows, False))
    pl.when(i == last)(lambda: mlp(xs_ref, store_tiles, False))


def _full(shape):
    return pl.BlockSpec(shape, lambda *_: (0,) * len(shape))


def _ffn(xs, xp, n2, nf, w_up, w_down, n_seq):
    assert D_MODEL == FFN_CHUNK
    assert FFN_SUB % n_seq == 0
    n_prompt = xp.shape[0] // FFN_TILE
    n_chunk = D_FF // FFN_CHUNK
    ys_shape = (n_seq, D_MODEL // LANES, xs.shape[0] // n_seq, LANES)
    prompt_tile = pl.BlockSpec((FFN_TILE, D_MODEL), lambda i: (jnp.minimum(i, n_prompt - 1), 0))
    return pl.pallas_call(
        _ffn_kernel,
        grid=(n_prompt + 1,),
        in_specs=[_full(xs.shape), prompt_tile, _full((1, D_MODEL)), _full((1, D_MODEL)),
                  pl.BlockSpec(memory_space=pl.ANY), pl.BlockSpec(memory_space=pl.ANY)],
        out_specs=[prompt_tile, _full(ys_shape)],
        out_shape=(jax.ShapeDtypeStruct(xp.shape, F32), jax.ShapeDtypeStruct(ys_shape, F32)),
        scratch_shapes=[pltpu.VMEM((n_chunk, D_MODEL, FFN_CHUNK), BF16),
                        pltpu.VMEM((n_chunk, FFN_CHUNK, D_MODEL), BF16),
                        pltpu.VMEM((2, D_MODEL, FFN_CHUNK), F32),
                        pltpu.SemaphoreType.DMA((2,))],
        compiler_params=pltpu.CompilerParams(dimension_semantics=("arbitrary",), vmem_limit_bytes=VMEM_LIMIT),
        name="ffn",
    )(xs, xp, n2, nf, w_up, w_down)


def kernel(x_prompt, x_sample, state_pool, state_C, state_n, state_m, meta_tokens, norm1, w_in, b_gate,
           w_pool, pool_scale, head_gain, w_out, norm2, w_up, w_down, norm_f):
    B, S, _ = x_prompt.shape
    n_seq = x_sample.shape[0]
    rows_s = DEC_SEQ * n_seq
    assert w_in.shape[0] == 1 and S % PROMPT_TILE == 0 and n_seq % SAMPLE_BLOCK == 0
    assert (B * S) % FFN_TILE == 0 and rows_s % FFN_SUB == 0

    win = w_in[0].astype(BF16)
    gpad = ((0, 0), (0, LANES - N_HEADS))
    wg = jnp.concatenate([jnp.pad(w_in[0][:, IN_MAIN:IN_MAIN + N_HEADS], gpad),
                          jnp.pad(w_in[0][:, IN_MAIN + N_HEADS:], gpad)], axis=1).astype(BF16)
    bg = jnp.concatenate([jnp.pad(b_gate[0][:N_HEADS], (0, LANES - N_HEADS)),
                          jnp.pad(b_gate[0][N_HEADS:], (0, LANES - N_HEADS))]).reshape(1, GATE_W)
    n1 = norm1[0].reshape(1, D_MODEL)
    n2 = norm2[0].reshape(1, D_MODEL)
    nf = norm_f.reshape(1, D_MODEL)
    pscale = pool_scale[0].reshape(1, POOL_WIDTH)
    gain = head_gain[0].reshape(1, MLSTM_WIDTH)

    weight_specs = [_full((1, D_MODEL)), _full((D_MODEL, IN_MAIN)), _full((D_MODEL, GATE_W)), _full((1, GATE_W))]
    mixer_specs = [_full((len(POOL_WINDOWS), POOL_GC, POOL_GC)), _full((1, POOL_WIDTH)),
                   _full((1, MLSTM_WIDTH)), _full((D_MODEL, D_MODEL))]

    nt = S // PROMPT_TILE
    n_tiles = B * nt
    assert n_tiles % 2 == 0 and n_tiles >= 2

    def tile_at(lag):
        def index(s):
            j = jnp.clip(s - lag, 0, n_tiles - 1)
            return (j // nt, j % nt, 0)
        return index

    def mixed_seq(ndim):
        return lambda s: (jnp.clip(s - 1, 0, n_tiles - 1) // nt,) + (0,) * (ndim - 1)

    wpool_shape = (len(POOL_WINDOWS), POOL_GC, POOL_GC)
    x1_p, pool_p, m_p, c_p, n_p, wout, wpool = pl.pallas_call(
        functools.partial(_prompt_kernel, nt),
        grid=(n_tiles + 2,),
        in_specs=[pl.BlockSpec((1, PROMPT_TILE, D_MODEL), tile_at(0)),
                  pl.BlockSpec((1, PROMPT_TILE, D_MODEL), tile_at(2)),
                  _full((N_META, D_MODEL))]
        + weight_specs + mixer_specs,
        out_specs=[pl.BlockSpec((1, PROMPT_TILE, D_MODEL), tile_at(2)),
                   pl.BlockSpec((1, CTX_PAD, POOL_WIDTH), mixed_seq(3)),
                   pl.BlockSpec((1, SUBLANES, LANES), mixed_seq(3)),
                   pl.BlockSpec((1, N_HEADS, HEAD_DIM, HEAD_DIM), mixed_seq(4)),
                   pl.BlockSpec((1, N_HEADS, HEAD_DIM), mixed_seq(3)),
                   _full((D_MODEL, D_MODEL)), _full(wpool_shape)],
        out_shape=(jax.ShapeDtypeStruct((B, S, D_MODEL), F32),
                   jax.ShapeDtypeStruct((B, CTX_PAD, POOL_WIDTH), F32),
                   jax.ShapeDtypeStruct((B, SUBLANES, LANES), F32),
                   jax.ShapeDtypeStruct((B, N_HEADS, HEAD_DIM, HEAD_DIM), F32),
                   jax.ShapeDtypeStruct((B, N_HEADS, HEAD_DIM), F32),
                   jax.ShapeDtypeStruct((D_MODEL, D_MODEL), BF16),
                   jax.ShapeDtypeStruct(wpool_shape, BF16)),
        scratch_shapes=[pltpu.VMEM((CTX_PAD + CHUNK, POOL_WIDTH), F32),
                        pltpu.VMEM((PROMPT_TILE, D_MODEL), BF16),
                        pltpu.VMEM((PROMPT_TILE, IN_MAIN), F32), pltpu.VMEM((PROMPT_TILE, IN_MAIN), F32),
                        pltpu.VMEM((PROMPT_TILE, GATE_W), F32), pltpu.VMEM((PROMPT_TILE, GATE_W), F32),
                        pltpu.VMEM((PROMPT_TILE, D_MODEL), BF16), pltpu.VMEM((PROMPT_TILE, D_MODEL), BF16),
                        pltpu.VMEM((N_HEADS, HEAD_DIM, AUG), F32), pltpu.VMEM((SUBLANES, LANES), F32),
                        pltpu.VMEM((N_META, POOL_WIDTH), F32), pltpu.VMEM((N_HEADS, HEAD_DIM, AUG), F32),
                        pltpu.VMEM((SUBLANES, LANES), F32)],
        compiler_params=pltpu.CompilerParams(dimension_semantics=("arbitrary",), vmem_limit_bytes=VMEM_LIMIT),
        name="prompt_mixer",
    )(x_prompt, x_prompt, meta_tokens, n1, win, wg, bg, w_pool[0], pscale, gain, w_out[0])

    xs_tiles = x_sample.reshape(n_seq, DEC_SEQ, D_MODEL // LANES, LANES).transpose(0, 2, 1, 3)
    x1_p, xs_tiles = lax.optimization_barrier((x1_p, xs_tiles))
    ctx_t = state_pool[0].transpose(1, 0, 2)
    n0 = state_n[0].reshape(n_seq, MLSTM_WIDTH)
    m0 = jnp.pad(state_m[0], ((0, 0), (0, LANES - N_HEADS)))
    state_shape = (SAMPLE_BLOCK, N_HEADS, HEAD_DIM, HEAD_DIM)
    state_block = pl.BlockSpec(state_shape, lambda i: (i, 0, 0, 0))
    state_in = pl.BlockSpec(memory_space=pl.ANY)
    x1_s, pool_s, c_s, n_s, m_s = pl.pallas_call(
        _sample_kernel,
        grid=(n_seq // SAMPLE_BLOCK,),
        in_specs=[_full(xs_tiles.shape), _full((POOL_CTX, n_seq, POOL_WIDTH)), state_in,
                  _full((n_seq, MLSTM_WIDTH)), _full((n_seq, LANES))] + weight_specs + mixer_specs,
        out_specs=[_full((rows_s, D_MODEL)), _full((POOL_CTX, n_seq, POOL_WIDTH)), state_block,
                   _full((n_seq, MLSTM_WIDTH)), _full((n_seq, LANES))],
        out_shape=(jax.ShapeDtypeStruct((rows_s, D_MODEL), F32),
                   jax.ShapeDtypeStruct((POOL_CTX, n_seq, POOL_WIDTH), F32),
                   jax.ShapeDtypeStruct((n_seq, N_HEADS, HEAD_DIM, HEAD_DIM), F32),
                   jax.ShapeDtypeStruct((n_seq, MLSTM_WIDTH), F32),
                   jax.ShapeDtypeStruct((n_seq, LANES), F32)),
        scratch_shapes=[pltpu.VMEM((rows_s, D_MODEL), F32),
                        pltpu.VMEM((rows_s, IN_MAIN), F32),
                        pltpu.VMEM((rows_s, GATE_W), F32),
                        pltpu.VMEM((N_HEADS, rows_s, HEAD_DIM), F32),
                        pltpu.VMEM((N_HEADS, rows_s, HEAD_DIM), F32),
                        pltpu.VMEM((N_HEADS, rows_s, HEAD_DIM), F32),
                        pltpu.VMEM((N_HEADS, rows_s // LANES, HEAD_DIM, LANES), BF16),
                        pltpu.VMEM((N_HEADS, rows_s // LANES, LANES, HEAD_DIM), BF16),
                        pltpu.VMEM((n_seq, LANES), F32),
                        pltpu.VMEM((rows_s, D_MODEL), BF16),
                        pltpu.VMEM((STATE_BUFFERS,) + state_shape, F32),
                        pltpu.SemaphoreType.DMA((STATE_BUFFERS,))],
        compiler_params=pltpu.CompilerParams(dimension_semantics=("arbitrary",), vmem_limit_bytes=VMEM_LIMIT),
        name="sample_mixer",
    )(xs_tiles, ctx_t, state_C[0], n0, m0, n1, win, wg, bg, wpool, pscale, gain, wout)

    y_p, y_s = _ffn(x1_s, x1_p.reshape(B * S, D_MODEL), n2, nf, w_up[0], w_down[0], n_seq)
    y_p = y_p.reshape(B, S, D_MODEL)
    y_s = y_s.transpose(0, 2, 1, 3).reshape(n_seq, DEC_SEQ, D_MODEL)

    return (y_p, y_s,
            pool_p[:, CTX_PAD - POOL_CTX:, :][None],
            c_p[None],
            n_p[None],
            m_p[:, 0, :N_HEADS][None],
            pool_s.transpose(1, 0, 2)[None],
            c_s[None],
            n_s.reshape(n_seq, N_HEADS, HEAD_DIM)[None],
            m_s[:, :N_HEADS][None])
```
